```python
import jax, jax.numpy as jnp
from jax import lax
import numpy as np

D_MODEL = 1024
BATCH = 8
SEQ = 4096
DEPTH = 2

GRID_W = 64
CTX_LEN = 256
ROPE_BASE = 10000.0
EPS = 1e-6
N_MOD = 6

MLA_HEADS = 8
MLA_NOPE = 64
MLA_ROPE = 32
MLA_QK = MLA_NOPE + MLA_ROPE
MLA_V = 64
MLA_Q_RANK = 384
MLA_KV_RANK = 256
Q_BLOCK = 128

SWA_HEADS = 8
SWA_KV_HEADS = 2
SWA_GROUP = SWA_HEADS // SWA_KV_HEADS
SWA_DIM = 64
WINDOW = 128
WIN_BLOCK = 128
N_NEIGH = 2 * (WINDOW // WIN_BLOCK) + 1

D_MIX = MLA_HEADS * MLA_V + SWA_HEADS * SWA_DIM
Q_COLS = MLA_Q_RANK + SWA_HEADS * SWA_DIM
KV_COLS = MLA_KV_RANK + MLA_ROPE + 2 * SWA_KV_HEADS * SWA_DIM
D_IN = Q_COLS + KV_COLS

PEER_HEADS = 8
PEER_NKEYS = 128
PEER_EXPERTS = PEER_NKEYS * PEER_NKEYS
PEER_DQ = 256
PEER_DHALF = PEER_DQ // 2
PEER_TOPK = 16
PEER_CHUNK = 128

kernel_name = "hymba_mla_swa_peer_dit_block"


def rms_norm(x, g):
    x32 = x.astype(jnp.float32)
    y = x32 * lax.rsqrt(jnp.mean(x32 * x32, axis=-1, keepdims=True) + EPS)
    return (y * g.astype(jnp.float32)).astype(x.dtype)


def modulate(x, g, shift, scale):
    return rms_norm(x, g) * (1 + scale) + shift


def axial_cos_sin(rows, cols, rot_dim):
    quarter = rot_dim // 4
    inv = ROPE_BASE ** (-jnp.arange(quarter, dtype=jnp.float32) / quarter)
    ar = rows[:, None] * inv
    ac = cols[:, None] * inv
    ang = jnp.concatenate([ar, ar, ac, ac], axis=-1)
    return jnp.cos(ang), jnp.sin(ang)


def apply_rope(x, cs):
    cos, sin = cs
    x32 = x.astype(jnp.float32)
    a, b, c, d = jnp.split(x32, 4, axis=-1)
    rot = jnp.concatenate([-b, a, -d, c], axis=-1)
    return (x32 * cos[:, None, :] + rot * sin[:, None, :]).astype(x.dtype)


def rope_tail(x, cs):
    return jnp.concatenate([x[..., :MLA_NOPE], apply_rope(x[..., MLA_NOPE:], cs)], axis=-1)


def split_kv_cols(p):
    b0 = MLA_KV_RANK
    b1 = b0 + MLA_ROPE
    b2 = b1 + SWA_KV_HEADS * SWA_DIM
    return p[..., :b0], p[..., b0:b1], p[..., b1:b2], p[..., b2:]


def mla_queries(qa, qa_g, wuq, qn_g, cs):
    B, T = qa.shape[:2]
    q = (rms_norm(qa, qa_g) @ wuq).reshape(B, T, MLA_HEADS, MLA_QK)
    q = rms_norm(q, qn_g)
    return q if cs is None else rope_tail(q, cs)


def mla_keys_values(kva, kr, kva_g, wukv, kn_g, cs):
    B, T = kva.shape[:2]
    kv = (rms_norm(kva, kva_g) @ wukv).reshape(B, T, MLA_HEADS, MLA_NOPE + MLA_V)
    k_rope = jnp.broadcast_to(kr[:, :, None, :], (B, T, MLA_HEADS, MLA_ROPE))
    k = rms_norm(jnp.concatenate([kv[..., :MLA_NOPE], k_rope], axis=-1), kn_g)
    if cs is not None:
        k = rope_tail(k, cs)
    return k, kv[..., MLA_NOPE:]


def mla_latent_attend(q, k, v, kc, vc):
    B, S = q.shape[:2]
    nb = S // Q_BLOCK
    k_all = jnp.concatenate([kc, k], axis=1)
    v_all = jnp.concatenate([vc, v], axis=1)
    qb = q.reshape(B, nb, Q_BLOCK, MLA_HEADS, MLA_QK).transpose(1, 0, 2, 3, 4)
    scale = MLA_QK ** -0.5

    def block(qblk):
        s = jnp.einsum('bqhd,bkhd->bhqk', qblk, k_all).astype(jnp.float32) * scale
        p = jax.nn.softmax(s, axis=-1).astype(v_all.dtype)
        return jnp.einsum('bhqk,bkhd->bqhd', p, v_all)

    o = lax.map(block, qb)
    return o.transpose(1, 0, 2, 3, 4).reshape(B, S, MLA_HEADS * MLA_V)


def mla_ctx_attend(q, kc, vc):
    B, C = q.shape[:2]
    s = jnp.einsum('bqhd,bkhd->bhqk', q, kc).astype(jnp.float32) * (MLA_QK ** -0.5)
    p = jax.nn.softmax(s, axis=-1).astype(vc.dtype)
    return jnp.einsum('bhqk,bkhd->bqhd', p, vc).reshape(B, C, MLA_HEADS * MLA_V)


def swa_queries(sq, qn_g, cs):
    B, T = sq.shape[:2]
    q = rms_norm(sq.reshape(B, T, SWA_HEADS, SWA_DIM), qn_g)
    return q if cs is None else apply_rope(q, cs)


def swa_keys_values(sk, sv, kn_g, cs):
    B, T = sk.shape[:2]
    k = rms_norm(sk.reshape(B, T, SWA_KV_HEADS, SWA_DIM), kn_g)
    if cs is not None:
        k = apply_rope(k, cs)
    return k, sv.reshape(B, T, SWA_KV_HEADS, SWA_DIM)


def window_attend(q, k, v, kc, vc, sink):
    B, S = q.shape[:2]
    nb = S // WIN_BLOCK
    nk = N_NEIGH * WIN_BLOCK
    C = kc.shape[1]
    scale = SWA_DIM ** -0.5
    qb = q.reshape(B, nb, WIN_BLOCK, SWA_KV_HEADS, SWA_GROUP, SWA_DIM).transpose(1, 0, 2, 3, 4, 5)
    pad = ((0, 0), (WINDOW, WINDOW), (0, 0), (0, 0))
    kp, vp = jnp.pad(k, pad), jnp.pad(v, pad)

    def band(xp):
        parts = [xp[:, i * WIN_BLOCK: i * WIN_BLOCK + S].reshape(B, nb, WIN_BLOCK, SWA_KV_HEADS, SWA_DIM)
                 for i in range(N_NEIGH)]
        return jnp.concatenate(parts, axis=2).transpose(1, 0, 2, 3, 4)

    kb, vb = band(kp), band(vp)
    qi = jnp.arange(WIN_BLOCK)[:, None]
    kj = jnp.arange(nk)[None, :]
    k_abs = jnp.arange(nb)[:, None, None] * WIN_BLOCK - WINDOW + kj
    mask = (jnp.abs(kj - WINDOW - qi) <= WINDOW) & (k_abs >= 0) & (k_abs < S)
    sink_col = sink.reshape(SWA_KV_HEADS, SWA_GROUP, 1, 1).astype(jnp.float32)

    def block(args):
        qblk, kblk, vblk, m = args
        s_ctx = jnp.einsum('bqgrd,bcgd->bgrqc', qblk, kc).astype(jnp.float32) * scale
        s_loc = jnp.einsum('bqgrd,bkgd->bgrqk', qblk, kblk).astype(jnp.float32) * scale
        s_loc = jnp.where(m, s_loc, -jnp.inf)
        s_sink = jnp.broadcast_to(sink_col, s_ctx.shape[:-1] + (1,))
        p = jax.nn.softmax(jnp.concatenate([s_ctx, s_loc, s_sink], axis=-1), axis=-1).astype(vblk.dtype)
        return (jnp.einsum('bgrqc,bcgd->bqgrd', p[..., :C], vc)
                + jnp.einsum('bgrqk,bkgd->bqgrd', p[..., C:C + nk], vblk))

    o = lax.map(block, (qb, kb, vb, mask))
    return o.transpose(1, 0, 2, 3, 4, 5).reshape(B, S, SWA_HEADS * SWA_DIM)


def swa_ctx_attend(q, kc, vc, sink):
    B, C = q.shape[:2]
    qg = q.reshape(B, C, SWA_KV_HEADS, SWA_GROUP, SWA_DIM)
    s = jnp.einsum('bqgrd,bkgd->bgrqk', qg, kc).astype(jnp.float32) * (SWA_DIM ** -0.5)
    s_sink = jnp.broadcast_to(sink.reshape(SWA_KV_HEADS, SWA_GROUP, 1, 1).astype(jnp.float32),
                              s.shape[:-1] + (1,))
    p = jax.nn.softmax(jnp.concatenate([s, s_sink], axis=-1), axis=-1)[..., :C].astype(vc.dtype)
    return jnp.einsum('bgrqk,bkgd->bqgrd', p, vc).reshape(B, C, SWA_HEADS * SWA_DIM)


def peer(h, wq, sub_keys, u, v):
    shp = h.shape
    chunks = h.reshape(-1, PEER_CHUNK, shp[-1])

    def chunk(xc):
        q = (xc @ wq).reshape(PEER_CHUNK, PEER_HEADS, 2, PEER_DHALF)
        s = jnp.einsum('thpd,hpkd->thpk', q, sub_keys).astype(jnp.float32)
        sv, si = lax.top_k(s, PEER_TOPK)
        cand = sv[:, :, 0, :, None] + sv[:, :, 1, None, :]
        cv, ci = lax.top_k(cand.reshape(PEER_CHUNK, PEER_HEADS, PEER_TOPK * PEER_TOPK), PEER_TOPK)
        i1 = jnp.take_along_axis(si[:, :, 0], ci // PEER_TOPK, axis=-1)
        i2 = jnp.take_along_axis(si[:, :, 1], ci % PEER_TOPK, axis=-1)
        idx = i1 * PEER_NKEYS + i2
        g = jax.nn.softmax(cv, axis=-1)
        u_sel = u[idx]
        v_sel = v[idx]
        a = jax.nn.gelu(jnp.einsum('td,thkd->thk', xc, u_sel).astype(jnp.float32), approximate=False)
        return jnp.einsum('thk,thkd->td', (g * a).astype(xc.dtype), v_sel)

    return lax.map(chunk, chunks).reshape(shp)


def setup_inputs(seed: int = 0) -> dict:
    key = jax.random.key(seed)
    ks = jax.random.split(key, 24)
    f32 = jnp.float32

    def nrm(k, shape, s):
        return s * jax.random.normal(k, shape, f32)

    L, D = DEPTH, D_MODEL
    return {
        "x": nrm(ks[0], (BATCH, SEQ, D), 1.0),
        "c": nrm(ks[1], (BATCH, D), 1.0),
        "ctx": nrm(ks[2], (BATCH, CTX_LEN, D), 1.0),
        "c_ctx": nrm(ks[3], (D,), 1.0),
        "ada_w": nrm(ks[4], (L, D, N_MOD * D), D ** -0.5),
        "ada_b": nrm(ks[5], (L, N_MOD * D), 0.01),
        "norm1_g": 1.0 + nrm(ks[6], (L, D), 0.02),
        "norm2_g": 1.0 + nrm(ks[7], (L, D), 0.02),
        "w_in": nrm(ks[8], (L, D, D_IN), D ** -0.5),
        "mla_qa_g": 1.0 + nrm(ks[9], (L, MLA_Q_RANK), 0.02),
        "mla_wuq": nrm(ks[10], (L, MLA_Q_RANK, MLA_HEADS * MLA_QK), MLA_Q_RANK ** -0.5),
        "mla_kva_g": 1.0 + nrm(ks[11], (L, MLA_KV_RANK), 0.02),
        "mla_wukv": nrm(ks[12], (L, MLA_KV_RANK, MLA_HEADS * (MLA_NOPE + MLA_V)), MLA_KV_RANK ** -0.5),
        "mla_qn_g": 1.0 + nrm(ks[13], (L, MLA_QK), 0.02),
        "mla_kn_g": 1.0 + nrm(ks[14], (L, MLA_QK), 0.02),
        "swa_qn_g": 1.0 + nrm(ks[15], (L, SWA_DIM), 0.02),
        "swa_kn_g": 1.0 + nrm(ks[16], (L, SWA_DIM), 0.02),
        "swa_sink": nrm(ks[17], (L, SWA_HEADS), 0.5),
        "w_out": nrm(ks[18], (L, D_MIX, D), D_MIX ** -0.5),
        "peer_wq": nrm(ks[19], (L, D, PEER_HEADS * PEER_DQ), D ** -0.5),
        "peer_keys": nrm(ks[20], (L, PEER_HEADS, 2, PEER_NKEYS, PEER_DHALF), PEER_DHALF ** -0.5),
        "peer_u": nrm(ks[21], (L, PEER_EXPERTS, D), D ** -0.5),
        "peer_v": nrm(ks[22], (L, PEER_EXPERTS, D), PEER_HEADS ** -0.5),
    }


def reference(x, c, ctx, c_ctx, ada_w, ada_b, norm1_g, norm2_g, w_in, mla_qa_g, mla_wuq,
              mla_kva_g, mla_wukv, mla_qn_g, mla_kn_g, swa_qn_g, swa_kn_g, swa_sink, w_out,
              peer_wq, peer_keys, peer_u, peer_v):
    B, S, D = x.shape
    ROWS = S // GRID_W
    rows = jnp.repeat(jnp.arange(ROWS, dtype=jnp.float32), GRID_W)
    cols = jnp.tile(jnp.arange(GRID_W, dtype=jnp.float32), ROWS)
    cs_mla = axial_cos_sin(rows, cols, MLA_ROPE)
    cs_swa = axial_cos_sin(rows, cols, SWA_DIM)
    s_c = jax.nn.silu(c)
    s_cctx = jax.nn.silu(c_ctx)

    h_lat, h_ctx = x, ctx
    for l in range(DEPTH):
        last = l == DEPTH - 1
        mod = (s_c @ ada_w[l] + ada_b[l]).reshape(B, N_MOD, D)
        sh1, sc1, g1, sh2, sc2, g2 = [mod[:, i][:, None, :] for i in range(N_MOD)]
        mod_c = (s_cctx @ ada_w[l] + ada_b[l]).reshape(N_MOD, D)

        a_lat = modulate(h_lat, norm1_g[l], sh1, sc1)
        a_ctx = modulate(h_ctx, norm1_g[l], mod_c[0], mod_c[1])
        p_lat = a_lat @ w_in[l]
        p_ctx = a_ctx @ (w_in[l][:, Q_COLS:] if last else w_in[l])

        qa, sq = p_lat[..., :MLA_Q_RANK], p_lat[..., MLA_Q_RANK:Q_COLS]
        kva, kr, sk, sv = split_kv_cols(p_lat[..., Q_COLS:])
        kva_c, kr_c, sk_c, sv_c = split_kv_cols(p_ctx[..., -KV_COLS:])

        q_m = mla_queries(qa, mla_qa_g[l], mla_wuq[l], mla_qn_g[l], cs_mla)
        k_m, v_m = mla_keys_values(kva, kr, mla_kva_g[l], mla_wukv[l], mla_kn_g[l], cs_mla)
        kc_m, vc_m = mla_keys_values(kva_c, kr_c, mla_kva_g[l], mla_wukv[l], mla_kn_g[l], None)
        o_mla = mla_latent_attend(q_m, k_m, v_m, kc_m, vc_m)

        q_s = swa_queries(sq, swa_qn_g[l], cs_swa)
        k_s, v_s = swa_keys_values(sk, sv, swa_kn_g[l], cs_swa)
        kc_s, vc_s = swa_keys_values(sk_c, sv_c, swa_kn_g[l], None)
        o_swa = window_attend(q_s, k_s, v_s, kc_s, vc_s, swa_sink[l])

        h_lat = h_lat + g1 * (jnp.concatenate([o_mla, o_swa], axis=-1) @ w_out[l])

        if not last:
            qa_c, sq_c = p_ctx[..., :MLA_Q_RANK], p_ctx[..., MLA_Q_RANK:Q_COLS]
            qc_m = mla_queries(qa_c, mla_qa_g[l], mla_wuq[l], mla_qn_g[l], None)
            qc_s = swa_queries(sq_c, swa_qn_g[l], None)
            oc = jnp.concatenate([mla_ctx_attend(qc_m, kc_m, vc_m),
                                  swa_ctx_attend(qc_s, kc_s, vc_s, swa_sink[l])], axis=-1)
            h_ctx = h_ctx + mod_c[2] * (oc @ w_out[l])

        b_lat = modulate(h_lat, norm2_g[l], sh2, sc2)
        h_lat = h_lat + g2 * peer(b_lat, peer_wq[l], peer_keys[l], peer_u[l], peer_v[l])
        if not last:
            b_ctx = modulate(h_ctx, norm2_g[l], mod_c[3], mod_c[4])
            h_ctx = h_ctx + mod_c[5] * peer(b_ctx, peer_wq[l], peer_keys[l], peer_u[l], peer_v[l])

    return h_lat
```

```python
import functools
import jax
import jax.numpy as jnp
from jax import lax
from jax.experimental import pallas as pl
from jax.experimental.pallas import tpu as pltpu

LANES = 128
EPS = 1e-6
ROPE_BASE = 10000.0
GRID_W = 64
N_MOD = 6

MLA_HEADS = 8
MLA_NOPE = 64
MLA_ROPE = 32
MLA_QK = MLA_NOPE + MLA_ROPE
MLA_V = 64
MLA_Q_RANK = 384
MLA_KV_RANK = 256
SWA_HEADS = 8
SWA_KV_HEADS = 2
SWA_GROUP = SWA_HEADS // SWA_KV_HEADS
SWA_DIM = 64
WINDOW = 128
WIN_KEYS = 3 * WINDOW
Q_COLS = MLA_Q_RANK + SWA_HEADS * SWA_DIM

PEER_HEADS = 8
PEER_NKEYS = 128
PEER_DHALF = 128
PEER_TOPK = 16
PEER_ROWS = PEER_HEADS * PEER_TOPK
N_CHUNK = 4
TILE_STRIDE = PEER_ROWS + 1
TOK_BLK = 64

OFF_QA = 0
OFF_KVA = OFF_QA + MLA_Q_RANK
OFF_KR = OFF_KVA + MLA_KV_RANK
OFF_SQ = OFF_KR + LANES
OFF_SK = OFF_SQ + SWA_HEADS * LANES
OFF_SV = OFF_SK + SWA_KV_HEADS * LANES
N_IN = OFF_SV + SWA_KV_HEADS * LANES

VMEM_LIMIT = 56 * 1024 * 1024
NEG_INF = float("-inf")


def _params(sem, vmem=VMEM_LIMIT):
    return pltpu.CompilerParams(dimension_semantics=sem, vmem_limit_bytes=vmem)


def _rms(x, g, n):
    ms = jnp.sum(x * x, axis=-1, keepdims=True) * (1.0 / n)
    return x * lax.rsqrt(ms + EPS) * g


def _rope(x, cos, sin_up, sin_dn, shift):
    return (x * cos + pltpu.roll(x, LANES - shift, axis=1) * sin_up
            + pltpu.roll(x, shift, axis=1) * sin_dn)


def _mod_kernel(c_ref, w_ref, b_ref, o_ref):
    c = c_ref[...]
    s = c * (1.0 / (1.0 + jnp.exp(-c)))
    o_ref[0] = jnp.dot(s, w_ref[0], preferred_element_type=jnp.float32,
                       precision=lax.Precision.HIGHEST) + b_ref[0]


def modulation(cond, ada_w, ada_b):
    nl, d, n6 = ada_w.shape
    tn = 1536
    return pl.pallas_call(
        _mod_kernel,
        grid=(nl, n6 // tn),
        in_specs=[
            pl.BlockSpec(cond.shape, lambda l, j: (0, 0)),
            pl.BlockSpec((1, d, tn), lambda l, j: (l, 0, j)),
            pl.BlockSpec((1, 1, tn), lambda l, j: (l, 0, j)),
        ],
        out_specs=pl.BlockSpec((1, cond.shape[0], tn), lambda l, j: (l, 0, j)),
        out_shape=jax.ShapeDtypeStruct((nl, cond.shape[0], n6), jnp.float32),
        compiler_params=_params(("arbitrary", "arbitrary")),
        name="modulation",
    )(cond, ada_w, ada_b.reshape(nl, 1, n6))


def _proj_kernel(h_ref, mod_ref, n1_ref, win_ref, qag_ref, wuq_ref, kvag_ref, wuk_ref, wuv_ref,
                 qn_ref, kn_ref, sqn_ref, skn_ref, rm_ref, rs_ref,
                 qm_ref, km_ref, vm_ref, qs_ref, ks_ref, vlo_ref, vhi_ref):
    h = h_ref[0]
    d = h.shape[-1]
    mod = mod_ref[0]
    a = _rms(h, n1_ref[...], d) * (1.0 + mod[1:2]) + mod[0:1]
    p = jnp.dot(a.astype(jnp.bfloat16), win_ref[...], preferred_element_type=jnp.float32)

    cm, sm_up, sm_dn = rm_ref[0], rm_ref[1], rm_ref[2]
    cs, ss_up, ss_dn = rs_ref[0], rs_ref[1], rs_ref[2]

    qa = _rms(p[:, OFF_QA:OFF_QA + MLA_Q_RANK], qag_ref[...], MLA_Q_RANK)
    q = jnp.dot(qa.astype(jnp.bfloat16), wuq_ref[...], preferred_element_type=jnp.float32)
    kva = _rms(p[:, OFF_KVA:OFF_KVA + MLA_KV_RANK], kvag_ref[...], MLA_KV_RANK).astype(jnp.bfloat16)
    kn = jnp.dot(kva, wuk_ref[...], preferred_element_type=jnp.float32)
    vm = jnp.dot(kva, wuv_ref[...], preferred_element_type=jnp.float32)
    kr = p[:, OFF_KR:OFF_KR + LANES]
    for hd in range(MLA_HEADS):
        sl = slice(hd * LANES, (hd + 1) * LANES)
        qh = _rope(_rms(q[:, sl], qn_ref[...], MLA_QK), cm, sm_up, sm_dn, MLA_ROPE // 4)
        qm_ref[0, hd] = (qh * (MLA_QK ** -0.5)).astype(jnp.bfloat16)
        kh = _rope(_rms(kn[:, sl] + kr, kn_ref[...], MLA_QK), cm, sm_up, sm_dn, MLA_ROPE // 4)
        km_ref[0, hd] = kh.astype(jnp.bfloat16)
        vm_ref[0, hd] = vm[:, sl].astype(jnp.bfloat16)
    for hd in range(SWA_HEADS):
        x = p[:, OFF_SQ + hd * LANES:OFF_SQ + (hd + 1) * LANES]
        qh = _rope(_rms(x, sqn_ref[...], SWA_DIM), cs, ss_up, ss_dn, SWA_DIM // 4)
        qs_ref[0, hd] = (qh * (SWA_DIM ** -0.5)).astype(jnp.bfloat16)
    lane = lax.broadcasted_iota(jnp.int32, (h.shape[0], LANES), 1)
    for g in range(SWA_KV_HEADS):
        x = p[:, OFF_SK + g * LANES:OFF_SK + (g + 1) * LANES]
        kh = _rope(_rms(x, skn_ref[...], SWA_DIM), cs, ss_up, ss_dn, SWA_DIM // 4)
        ks_ref[0, g] = kh.astype(jnp.bfloat16)
        v = p[:, OFF_SV + g * LANES:OFF_SV + (g + 1) * LANES]
        vlo_ref[0, g] = jnp.where(lane < SWA_DIM, v, 0.0).astype(jnp.bfloat16)
        vhi_ref[0, g] = jnp.where(lane >= SWA_DIM, v, 0.0).astype(jnp.bfloat16)


def projections(h, mod, n1, win, qag, wuq, kvag, wuk, wuv, qn, kn, sqn, skn, rope_m, rope_s, r, c):
    b, t, d = h.shape
    nctx = c // r
    full = lambda a: pl.BlockSpec(a.shape, lambda i, j: (0,) * a.ndim)
    head_out = lambda nh: pl.BlockSpec((1, nh, r, LANES), lambda i, j: (i, 0, j, 0))
    head_shape = lambda nh: jax.ShapeDtypeStruct((b, nh, t, LANES), jnp.bfloat16)
    return pl.pallas_call(
        _proj_kernel,
        grid=(b, t // r),
        in_specs=[
            pl.BlockSpec((1, r, d), lambda i, j: (i, j, 0)),
            pl.BlockSpec((1, N_MOD, d), lambda i, j: (jnp.where(j < nctx, b, i), 0, 0)),
            full(n1), full(win), full(qag), full(wuq), full(kvag), full(wuk), full(wuv),
            full(qn), full(kn), full(sqn), full(skn),
            pl.BlockSpec((3, r, LANES), lambda i, j: (0, j, 0)),
            pl.BlockSpec((3, r, LANES), lambda i, j: (0, j, 0)),
        ],
        out_specs=[head_out(MLA_HEADS), head_out(MLA_HEADS), head_out(MLA_HEADS),
                   head_out(SWA_HEADS), head_out(SWA_KV_HEADS), head_out(SWA_KV_HEADS),
                   head_out(SWA_KV_HEADS)],
        out_shape=[head_shape(MLA_HEADS), head_shape(MLA_HEADS), head_shape(MLA_HEADS),
                   head_shape(SWA_HEADS), head_shape(SWA_KV_HEADS), head_shape(SWA_KV_HEADS),
                   head_shape(SWA_KV_HEADS)],
        compiler_params=_params(("parallel", "arbitrary")),
        name="projections",
    )(h, mod, n1, win, qag, wuq, kvag, wuk, wuv, qn, kn, sqn, skn, rope_m, rope_s)


def _mla_kernel(q_ref, k_ref, v_ref, o_ref, *, nctx_blocks, c):
    qi = pl.program_id(2)

    def attend(nk):
        acc = None
        for i in range(2):
            q = q_ref[0, i]
            s = lax.dot_general(q, k_ref[0, i, :nk], (((1,), (1,)), ((), ())),
                                preferred_element_type=jnp.float32)
            m = jnp.max(s, axis=-1, keepdims=True)
            p = jnp.exp(s - m)
            l = jnp.sum(p, axis=-1, keepdims=True)
            o = jnp.dot(p.astype(jnp.bfloat16), v_ref[0, i, :nk], preferred_element_type=jnp.float32)
            o = o * (1.0 / l)
            acc = o if acc is None else acc + o
        o_ref[0] = acc.astype(o_ref.dtype)

    if nctx_blocks:
        @pl.when(qi < nctx_blocks)
        def _():
            attend(c)

        @pl.when(qi >= nctx_blocks)
        def _():
            attend(k_ref.shape[2])
    else:
        attend(k_ref.shape[2])


def mla_attention(qm, km, vm, tq, c, with_ctx):
    b, nh, t, _ = qm.shape
    off = 0 if with_ctx else c // tq
    nq = t // tq - off
    return pl.pallas_call(
        functools.partial(_mla_kernel, nctx_blocks=(c // tq if with_ctx else 0), c=c),
        grid=(b, nh // 2, nq),
        in_specs=[
            pl.BlockSpec((1, 2, tq, LANES), lambda i, hp, j: (i, hp, j + off, 0)),
            pl.BlockSpec((1, 2, t, LANES), lambda i, hp, j: (i, hp, 0, 0)),
            pl.BlockSpec((1, 2, t, LANES), lambda i, hp, j: (i, hp, 0, 0)),
        ],
        out_specs=pl.BlockSpec((1, tq, LANES), lambda i, hp, j: (i, j, hp)),
        out_shape=jax.ShapeDtypeStruct((b, nq * tq, nh // 2 * LANES), jnp.bfloat16),
        compiler_params=_params(("parallel", "arbitrary", "arbitrary")),
        name="mla_attention",
    )(qm, km, vm)


def _swa_kernel(sink_ref, q_ref, k_ref, vlo_ref, vhi_ref, o_ref, *, off, c):
    g = pl.program_id(1)
    qi = pl.program_id(2) + off
    t = k_ref.shape[2]
    rows = SWA_GROUP * WINDOW
    q = q_ref[0].reshape(rows, LANES)
    r_idx = lax.broadcasted_iota(jnp.int32, (rows, 1), 0)
    sink = jnp.zeros((rows, 1), jnp.float32)
    for i in range(SWA_GROUP):
        sink = jnp.where(r_idx // WINDOW == i, sink_ref[g * SWA_GROUP + i], sink)

    ws = pl.multiple_of(jnp.clip(qi * WINDOW - WINDOW, c, t - WIN_KEYS), WINDOW)
    nt = (((1,), (1,)), ((), ()))
    s_ctx = lax.dot_general(q, k_ref[0, 0, :c], nt, preferred_element_type=jnp.float32)
    s_loc = lax.dot_general(q, k_ref[0, 0, pl.ds(ws, WIN_KEYS)], nt, preferred_element_type=jnp.float32)
    qpos = qi * WINDOW + (r_idx % WINDOW)
    kpos = ws + lax.broadcasted_iota(jnp.int32, (1, WIN_KEYS), 1)
    reach = jnp.where(qi * WINDOW >= c, WINDOW, -1)
    s_loc = jnp.where(jnp.abs(qpos - kpos) <= reach, s_loc, NEG_INF)
    m = jnp.maximum(jnp.maximum(jnp.max(s_ctx, axis=-1, keepdims=True),
                                jnp.max(s_loc, axis=-1, keepdims=True)), sink)
    p_ctx = jnp.exp(s_ctx - m)
    p_loc = jnp.exp(s_loc - m)
    l = (jnp.sum(p_ctx, axis=-1, keepdims=True) + jnp.sum(p_loc, axis=-1, keepdims=True)
         + jnp.exp(sink - m))
    inv = 1.0 / l
    p_ctx = p_ctx.astype(jnp.bfloat16)
    p_loc = p_loc.astype(jnp.bfloat16)
    outs = []
    for pair in range(SWA_GROUP // 2):
        acc = None
        for i, v_ref in enumerate((vlo_ref, vhi_ref)):
            rs = slice((2 * pair + i) * WINDOW, (2 * pair + i + 1) * WINDOW)
            o = (jnp.dot(p_ctx[rs], v_ref[0, 0, :c], preferred_element_type=jnp.float32)
                 + jnp.dot(p_loc[rs], v_ref[0, 0, pl.ds(ws, WIN_KEYS)], preferred_element_type=jnp.float32))
            o = o * inv[rs]
            acc = o if acc is None else acc + o
        outs.append(acc)
    o_ref[0] = jnp.concatenate(outs, axis=-1).astype(o_ref.dtype)


def swa_attention(sink, qs, ks, vlo, vhi, c, with_ctx):
    b, nh, t, _ = qs.shape
    off = 0 if with_ctx else c // WINDOW
    nq = t // WINDOW - off
    kv_spec = pl.BlockSpec((1, 1, t, LANES), lambda i, g, j: (i, g, 0, 0))
    return pl.pallas_call(
        functools.partial(_swa_kernel, off=off, c=c),
        grid=(b, SWA_KV_HEADS, nq),
        in_specs=[
            pl.BlockSpec(memory_space=pltpu.SMEM),
            pl.BlockSpec((1, SWA_GROUP, WINDOW, LANES), lambda i, g, j: (i, g, j + off, 0)),
            kv_spec, kv_spec, kv_spec,
        ],
        out_specs=pl.BlockSpec((1, WINDOW, SWA_GROUP // 2 * LANES), lambda i, g, j: (i, j, g)),
        out_shape=jax.ShapeDtypeStruct((b, nq * WINDOW, SWA_HEADS // 2 * LANES), jnp.bfloat16),
        compiler_params=_params(("parallel", "arbitrary", "arbitrary")),
        name="swa_attention",
    )(sink, qs, ks, vlo, vhi)


def _top16(s, n):
    rows = lax.broadcasted_iota(jnp.int32, s.shape, 0)
    vals, idxs = [], []
    for _ in range(PEER_TOPK):
        m = jnp.max(s, axis=0, keepdims=True)
        am = jnp.min(jnp.where(s == m, rows, n), axis=0, keepdims=True)
        s = jnp.where(rows == am, NEG_INF, s)
        vals.append(m)
        idxs.append(am)
    return jnp.concatenate(vals, axis=0), jnp.concatenate(idxs, axis=0)


def _route_kernel(h_ref, om_ref, os_ref, wo_ref, mod_ref, n2_ref, wq_ref, keys_ref,
                  h1_ref, xhi_ref, xlo_ref, idx_ref, g_ref, q_scr, idxt_scr, gt_scr):
    h = h_ref[0]
    d = h.shape[-1]
    mod = mod_ref[0]
    half = om_ref.shape[-1]
    mix = (jnp.dot(om_ref[0], wo_ref[:half], preferred_element_type=jnp.float32)
           + jnp.dot(os_ref[0], wo_ref[half:], preferred_element_type=jnp.float32))
    h1 = h + mod[2:3] * mix
    h1_ref[0] = h1
    x = _rms(h1, n2_ref[...], d) * (1.0 + mod[4:5]) + mod[3:4]
    xhi = x.astype(jnp.bfloat16)
    xhi_ref[0] = xhi
    xlo_ref[0] = (x - xhi.astype(jnp.float32)).astype(jnp.bfloat16)
    q = jnp.dot(xhi, wq_ref[...], preferred_element_type=jnp.float32)
    for k in range(2 * PEER_HEADS):
        q_scr[k] = q[:, k * PEER_DHALF:(k + 1) * PEER_DHALF].astype(jnp.bfloat16)

    def head(hh, carry):
        sv, si = [], []
        for part in range(2):
            qhp = q_scr[2 * hh + part]
            st = lax.dot_general(keys_ref[hh, part], qhp, (((1,), (1,)), ((), ())),
                                 preferred_element_type=jnp.float32)
            v, i = _top16(st, PEER_NKEYS)
            sv.append(v)
            si.append(i)
        cand = jnp.concatenate([sv[0][a:a + 1] + sv[1] for a in range(PEER_TOPK)], axis=0)
        cv, ci = _top16(cand, PEER_TOPK * PEER_TOPK)
        ia = ci >> 4
        ib = ci & (PEER_TOPK - 1)
        i1 = jnp.zeros_like(ci)
        i2 = jnp.zeros_like(ci)
        for a in range(PEER_TOPK):
            i1 = jnp.where(ia == a, si[0][a:a + 1], i1)
            i2 = jnp.where(ib == a, si[1][a:a + 1], i2)
        e = jnp.exp(cv - cv[0:1])
        gate = e * (1.0 / jnp.sum(e, axis=0, keepdims=True))
        row = pl.multiple_of(hh * PEER_TOPK, PEER_TOPK)
        idxt_scr[pl.ds(row, PEER_TOPK), :] = (i1 * PEER_NKEYS + i2).astype(jnp.float32)
        gt_scr[pl.ds(row, PEER_TOPK), :] = gate
        return carry

    lax.fori_loop(0, PEER_HEADS, head, 0)
    idx_ref[0] = idxt_scr[...].T.astype(jnp.int32)
    g_ref[0] = gt_scr[...].T


def route(h, om, osw, wo, mod, n2, wq, keys, r, c, with_ctx):
    b, t, d = h.shape
    off = 0 if with_ctx else c // r
    nblk = t // r - off
    nctx = c // r
    tq = nblk * r
    full = lambda a: pl.BlockSpec(a.shape, lambda i, j: (0,) * a.ndim)
    row_spec = lambda w: pl.BlockSpec((1, r, w), lambda i, j: (i, j, 0))
    return pl.pallas_call(
        _route_kernel,
        grid=(b, nblk),
        in_specs=[
            pl.BlockSpec((1, r, d), lambda i, j: (i, j + off, 0)),
            row_spec(om.shape[-1]), row_spec(osw.shape[-1]),
            full(wo),
            pl.BlockSpec((1, N_MOD, d), lambda i, j: (jnp.where(j + off < nctx, b, i), 0, 0)),
            full(n2), full(wq), full(keys),
        ],
        out_specs=[row_spec(d), row_spec(d), row_spec(d), row_spec(PEER_ROWS), row_spec(PEER_ROWS)],
        out_shape=[
            jax.ShapeDtypeStruct((b, tq, d), jnp.float32),
            jax.ShapeDtypeStruct((b, tq, d), jnp.bfloat16),
            jax.ShapeDtypeStruct((b, tq, d), jnp.bfloat16),
            jax.ShapeDtypeStruct((b, tq, PEER_ROWS), jnp.int32),
            jax.ShapeDtypeStruct((b, tq, PEER_ROWS), jnp.float32),
        ],
        scratch_shapes=[
            pltpu.VMEM((2 * PEER_HEADS, r, PEER_DHALF), jnp.bfloat16),
            pltpu.VMEM((PEER_ROWS, r), jnp.float32),
            pltpu.VMEM((PEER_ROWS, r), jnp.float32),
        ],
        compiler_params=_params(("parallel", "arbitrary")),
        name="peer_route",
    )(h, om, osw, wo, mod, n2, wq, keys)


def pack_table(tab):
    e, d = tab.shape
    bits = lax.bitcast_convert_type(tab.astype(jnp.bfloat16), jnp.uint16).astype(jnp.uint32)
    w = (bits[:, : d // 2] << 16) | bits[:, d // 2:]
    return w.reshape(e, N_CHUNK, LANES)


def _gather_rows(idx_ref, tab_ref, tile_ref, t):
    for j in range(PEER_ROWS):
        tile_ref[pl.ds(j, N_CHUNK, stride=TILE_STRIDE), :] = tab_ref[idx_ref[t, j]]


def _chunk(tile_ref, c):
    return pltpu.bitcast(tile_ref[pl.ds(c * TILE_STRIDE, PEER_ROWS), :], jnp.bfloat16)


def _u_kernel(idx_ref, x_ref, g_ref, tab_ref, w_ref, tile_a, tile_b):
    row = lax.broadcasted_iota(jnp.int32, (16, 2 * PEER_ROWS), 0) & 7
    last = TOK_BLK - 1

    def scores(tile_ref, t):
        x16 = x_ref[t]
        top = jnp.zeros((16, 2 * PEER_ROWS), jnp.float32)
        bot = jnp.zeros((16, 2 * PEER_ROWS), jnp.float32)
        for c in range(N_CHUNK):
            y = lax.dot_general(x16, _chunk(tile_ref, c), (((1,), (1,)), ((), ())),
                                preferred_element_type=jnp.float32)
            top = top + jnp.where(row == c, y, 0.0)
            bot = bot + jnp.where(row == c + N_CHUNK, y, 0.0)
        tot = top + pltpu.roll(bot, 1, axis=1)
        a = jnp.sum(tot, axis=0, keepdims=True)
        gelu = 0.5 * a * (1.0 + lax.erf(a * (2.0 ** -0.5)))
        w_ref[t] = g_ref[t] * gelu

    _gather_rows(idx_ref, tab_ref, tile_a, 0)

    def body(i, carry):
        t = 2 * i
        _gather_rows(idx_ref, tab_ref, tile_b, t + 1)
        scores(tile_a, t)
        _gather_rows(idx_ref, tab_ref, tile_a, jnp.minimum(t + 2, last))
        scores(tile_b, t + 1)
        return carry

    lax.fori_loop(0, TOK_BLK // 2, body, 0)


def peer_scores(idx, x16, g, tab):
    n = idx.shape[0]
    tok = lambda *s: pl.BlockSpec((TOK_BLK,) + s, lambda i: (i,) + (0,) * len(s))
    return pl.pallas_call(
        _u_kernel,
        grid=(n // TOK_BLK,),
        in_specs=[
            pl.BlockSpec((TOK_BLK, PEER_ROWS), lambda i: (i, 0), memory_space=pltpu.SMEM),
            tok(16, LANES), tok(1, 2 * PEER_ROWS),
            pl.BlockSpec(tab.shape, lambda i: (0, 0, 0), pipeline_mode=pl.Buffered(1)),
        ],
        out_specs=tok(1, 2 * PEER_ROWS),
        out_shape=jax.ShapeDtypeStruct((n, 1, 2 * PEER_ROWS), jnp.float32),
        scratch_shapes=[pltpu.VMEM((N_CHUNK * TILE_STRIDE, LANES), jnp.uint32)] * 2,
        compiler_params=_params(("arbitrary",)),
        name="peer_scores",
    )(idx, x16, g, tab)


def _v_kernel(idx_ref, w_ref, h_ref, g2_ref, tab_ref, o_ref, tile_a, tile_b):
    row = lax.broadcasted_iota(jnp.int32, (16, 2 * PEER_ROWS), 0)
    last = TOK_BLK - 1

    def combine(tile_ref, t):
        w = w_ref[t]
        w_hi = w.astype(jnp.bfloat16).astype(jnp.float32)
        w_lo = w - w_hi
        parts = (w_hi, pltpu.roll(w_hi, 2 * PEER_ROWS - 1, axis=1),
                 w_lo, pltpu.roll(w_lo, 2 * PEER_ROWS - 1, axis=1))
        acc = jnp.zeros((16, LANES), jnp.float32)
        for c in range(N_CHUNK):
            lhs = jnp.zeros((16, 2 * PEER_ROWS), jnp.float32)
            for k, part in enumerate(parts):
                lhs = jnp.where(row == c + N_CHUNK * k, part, lhs)
            acc = acc + jnp.dot(lhs.astype(jnp.bfloat16), _chunk(tile_ref, c),
                                preferred_element_type=jnp.float32)
        o_ref[t] = h_ref[t] + g2_ref[0, 0] * (acc[:8] + acc[8:])

    _gather_rows(idx_ref, tab_ref, tile_a, 0)

    def body(i, carry):
        t = 2 * i
        _gather_rows(idx_ref, tab_ref, tile_b, t + 1)
        combine(tile_a, t)
        _gather_rows(idx_ref, tab_ref, tile_a, jnp.minimum(t + 2, last))
        combine(tile_b, t + 1)
        return carry

    lax.fori_loop(0, TOK_BLK // 2, body, 0)


def peer_combine(idx, w, h8, g2, tab, blocks_per_sample, ctx_blocks):
    n = idx.shape[0]
    tok = lambda *s: pl.BlockSpec((TOK_BLK,) + s, lambda i: (i,) + (0,) * len(s))
    g2_map = lambda i: (i // blocks_per_sample,
                        jnp.where(i % blocks_per_sample < ctx_blocks, 0, 1), 0, 0)
    return pl.pallas_call(
        _v_kernel,
        grid=(n // TOK_BLK,),
        in_specs=[
            pl.BlockSpec((TOK_BLK, PEER_ROWS), lambda i: (i, 0), memory_space=pltpu.SMEM),
            tok(1, 2 * PEER_ROWS), tok(8, LANES),
            pl.BlockSpec((1, 1, 8, LANES), g2_map),
            pl.BlockSpec(tab.shape, lambda i: (0, 0, 0), pipeline_mode=pl.Buffered(1)),
        ],
        out_specs=tok(8, LANES),
        out_shape=jax.ShapeDtypeStruct((n, 8, LANES), jnp.float32),
        scratch_shapes=[pltpu.VMEM((N_CHUNK * TILE_STRIDE, LANES), jnp.uint32)] * 2,
        compiler_params=_params(("arbitrary",)),
        name="peer_combine",
    )(idx, w, h8, g2, tab)


def _slots(w, width, offset=0):
    lead = w.shape[:-1]
    n = w.shape[-1] // width
    w = w.reshape(lead + (n, width))
    w = jnp.pad(w, [(0, 0)] * len(lead) + [(0, 0), (offset, LANES - width - offset)])
    return w.reshape(lead + (n * LANES,))


def _rope_tables(rot_dim, lane0, t, c):
    s = t - c
    q = rot_dim // 4
    pos = jnp.arange(s, dtype=jnp.float32)
    rows = jnp.floor(pos / GRID_W)
    cols = pos - rows * GRID_W
    inv = ROPE_BASE ** (-jnp.arange(q, dtype=jnp.float32) / q)
    ar = rows[:, None] * inv
    ac = cols[:, None] * inv
    zero = jnp.zeros_like(ar)
    cos = jnp.cos(jnp.concatenate([ar, ar, ac, ac], axis=-1))
    up = jnp.concatenate([-jnp.sin(ar), zero, -jnp.sin(ac), zero], axis=-1)
    dn = jnp.concatenate([zero, jnp.sin(ar), zero, jnp.sin(ac)], axis=-1)
    pad = lambda a, fill: jnp.pad(
        jnp.pad(a, ((0, 0), (lane0, LANES - lane0 - rot_dim)), constant_values=fill),
        ((c, 0), (0, 0)), constant_values=fill)
    cos = jnp.pad(jnp.pad(cos, ((0, 0), (lane0, LANES - lane0 - rot_dim)), constant_values=1.0),
                  ((c, 0), (0, 0)), constant_values=1.0)
    return jnp.stack([cos, pad(up, 0.0), pad(dn, 0.0)])


def kernel(x, c, ctx, c_ctx, ada_w, ada_b, norm1_g, norm2_g, w_in, mla_qa_g, mla_wuq, mla_kva_g, mla_wukv, mla_qn_g, mla_kn_g, swa_qn_g, swa_kn_g, swa_sink, w_out, peer_wq, peer_keys, peer_u, peer_v):
    b, s, d = x.shape
    nctx = ctx.shape[1]
    t = nctx + s
    depth = ada_w.shape[0]
    r = min(256, nctx)
    bf = jnp.bfloat16

    cond = jnp.zeros((16, d), jnp.float32).at[:b].set(c).at[b].set(c_ctx)
    mod_all = modulation(cond, ada_w, ada_b).reshape(depth, 16, N_MOD, d)
    rope_m = _rope_tables(MLA_ROPE, MLA_NOPE, t, nctx)
    rope_s = _rope_tables(SWA_DIM, 0, t, nctx)

    h = jnp.concatenate([ctx, x], axis=1)
    for l in range(depth):
        last = l == depth - 1
        with_ctx = not last
        mod = mod_all[l]
        wi = w_in[l]
        kv0 = Q_COLS + MLA_KV_RANK
        sk0 = kv0 + MLA_ROPE
        sv0 = sk0 + SWA_KV_HEADS * SWA_DIM
        sv = wi[:, sv0:].reshape(d, SWA_KV_HEADS, 1, SWA_DIM)
        win = jnp.concatenate([
            wi[:, :MLA_Q_RANK], wi[:, Q_COLS:kv0],
            _slots(wi[:, kv0:sk0], MLA_ROPE, MLA_NOPE),
            _slots(wi[:, MLA_Q_RANK:Q_COLS], SWA_DIM),
            _slots(wi[:, sk0:sv0], SWA_DIM),
            jnp.broadcast_to(sv, (d, SWA_KV_HEADS, 2, SWA_DIM)).reshape(d, SWA_KV_HEADS * LANES),
        ], axis=1).astype(bf)
        wuq = _slots(mla_wuq[l], MLA_QK).astype(bf)
        wukv = mla_wukv[l].reshape(MLA_KV_RANK, MLA_HEADS, MLA_NOPE + MLA_V)
        wuk = _slots(wukv[:, :, :MLA_NOPE].reshape(MLA_KV_RANK, -1), MLA_NOPE).astype(bf)
        wv = wukv[:, :, MLA_NOPE:].reshape(MLA_KV_RANK, MLA_HEADS // 2, 2, MLA_V)
        zero = jnp.zeros_like(wv[:, :, 0])
        wuv = jnp.stack([jnp.concatenate([wv[:, :, 0], zero], axis=-1),
                         jnp.concatenate([zero, wv[:, :, 1]], axis=-1)], axis=2)
        wuv = wuv.reshape(MLA_KV_RANK, MLA_HEADS * LANES).astype(bf)
        row = lambda g: g.reshape(1, -1)

        qm, km, vm, qs, ks, vlo, vhi = projections(
            h, mod, row(norm1_g[l]), win, row(mla_qa_g[l]), wuq, row(mla_kva_g[l]), wuk, wuv,
            row(_slots(mla_qn_g[l], MLA_QK)), row(_slots(mla_kn_g[l], MLA_QK)),
            row(_slots(swa_qn_g[l], SWA_DIM)), row(_slots(swa_kn_g[l], SWA_DIM)),
            rope_m, rope_s, r, nctx)
        om = mla_attention(qm, km, vm, r, nctx, with_ctx)
        osw = swa_attention(swa_sink[l], qs, ks, vlo, vhi, nctx, with_ctx)
        h1, xhi, xlo, idx, gate = route(
            h, om, osw, w_out[l].astype(bf), mod, row(norm2_g[l]), peer_wq[l].astype(bf),
            peer_keys[l].astype(bf), r, nctx, with_ctx)

        tl = h1.shape[1]
        n = b * tl
        idx = idx.reshape(n, PEER_ROWS)
        x16 = jnp.concatenate([xhi.reshape(n, 8, LANES), xlo.reshape(n, 8, LANES)], axis=1)
        gate = jnp.stack([jnp.zeros_like(gate), gate], axis=-1).reshape(n, 1, 2 * PEER_ROWS)
        w = peer_scores(idx, x16, gate, pack_table(peer_u[l]))
        g2 = jnp.stack([jnp.broadcast_to(mod[b, 5], (b, d)), mod[:b, 5]], axis=1)
        h = peer_combine(idx, w, h1.reshape(n, 8, LANES), g2.reshape(b, 2, 8, LANES),
                         pack_table(peer_v[l]), tl // TOK_BLK,
                         nctx // TOK_BLK if with_ctx else 0).reshape(b, tl, d)
    return h
```

```python
import functools
import jax
import jax.numpy as jnp
from jax import lax
from jax.experimental import pallas as pl
from jax.experimental.pallas import tpu as pltpu

LANES = 128
EPS = 1e-6
ROPE_BASE = 10000.0
GRID_W = 64
N_MOD = 6

MLA_HEADS = 8
MLA_NOPE = 64
MLA_ROPE = 32
MLA_QK = MLA_NOPE + MLA_ROPE
MLA_V = 64
MLA_Q_RANK = 384
MLA_KV_RANK = 256
SWA_HEADS = 8
SWA_KV_HEADS = 2
SWA_GROUP = SWA_HEADS // SWA_KV_HEADS
SWA_DIM = 64
WINDOW = 128
WIN_KEYS = 3 * WINDOW
Q_COLS = MLA_Q_RANK + SWA_HEADS * SWA_DIM

PEER_HEADS = 8
PEER_NKEYS = 128
PEER_DHALF = 128
PEER_TOPK = 16
PEER_ROWS = PEER_HEADS * PEER_TOPK
N_CHUNK = 4
TILE_STRIDE = PEER_ROWS + 1
TOK_BLK = 64
TOK_SET = 4

OFF_QA = 0
OFF_KVA = OFF_QA + MLA_Q_RANK
OFF_KR = OFF_KVA + MLA_KV_RANK
OFF_SQ = OFF_KR + LANES
OFF_SK = OFF_SQ + SWA_HEADS * LANES
OFF_SV = OFF_SK + SWA_KV_HEADS * LANES
N_IN = OFF_SV + SWA_KV_HEADS * LANES

VMEM_LIMIT = 56 * 1024 * 1024
NEG_INF = float("-inf")


def _params(sem, vmem=VMEM_LIMIT):
    return pltpu.CompilerParams(dimension_semantics=sem, vmem_limit_bytes=vmem)


def _rms(x, g, n):
    ms = jnp.sum(x * x, axis=-1, keepdims=True) * (1.0 / n)
    return x * lax.rsqrt(ms + EPS) * g


def _rope(x, cos, sin_up, sin_dn, shift):
    return (x * cos + pltpu.roll(x, LANES - shift, axis=1) * sin_up
            + pltpu.roll(x, shift, axis=1) * sin_dn)


def _mod_kernel(c_ref, w_ref, b_ref, o_ref):
    c = c_ref[...]
    s = c * (1.0 / (1.0 + jnp.exp(-c)))
    o_ref[0] = jnp.dot(s, w_ref[0], preferred_element_type=jnp.float32,
                       precision=lax.Precision.HIGHEST) + b_ref[0]


def modulation(cond, ada_w, ada_b):
    nl, d, n6 = ada_w.shape
    tn = 1536
    return pl.pallas_call(
        _mod_kernel,
        grid=(nl, n6 // tn),
        in_specs=[
            pl.BlockSpec(cond.shape, lambda l, j: (0, 0)),
            pl.BlockSpec((1, d, tn), lambda l, j: (l, 0, j)),
            pl.BlockSpec((1, 1, tn), lambda l, j: (l, 0, j)),
        ],
        out_specs=pl.BlockSpec((1, cond.shape[0], tn), lambda l, j: (l, 0, j)),
        out_shape=jax.ShapeDtypeStruct((nl, cond.shape[0], n6), jnp.float32),
        compiler_params=_params(("arbitrary", "arbitrary")),
        name="modulation",
    )(cond, ada_w, ada_b.reshape(nl, 1, n6))


def _proj_kernel(h_ref, mod_ref, n1_ref, win_ref, qag_ref, wuq_ref, kvag_ref, wuk_ref, wuv_ref,
                 qn_ref, kn_ref, sqn_ref, skn_ref, rm_ref, rs_ref,
                 qm_ref, km_ref, vm_ref, qs_ref, ks_ref, vlo_ref, vhi_ref):
    h = h_ref[0]
    d = h.shape[-1]
    mod = mod_ref[0]
    a = _rms(h, n1_ref[...], d) * (1.0 + mod[1:2]) + mod[0:1]
    p = jnp.dot(a.astype(jnp.bfloat16), win_ref[...], preferred_element_type=jnp.float32)

    cm, sm_up, sm_dn = rm_ref[0], rm_ref[1], rm_ref[2]
    cs, ss_up, ss_dn = rs_ref[0], rs_ref[1], rs_ref[2]

    qa = _rms(p[:, OFF_QA:OFF_QA + MLA_Q_RANK], qag_ref[...], MLA_Q_RANK)
    q = jnp.dot(qa.astype(jnp.bfloat16), wuq_ref[...], preferred_element_type=jnp.float32)
    kva = _rms(p[:, OFF_KVA:OFF_KVA + MLA_KV_RANK], kvag_ref[...], MLA_KV_RANK).astype(jnp.bfloat16)
    kn = jnp.dot(kva, wuk_ref[...], preferred_element_type=jnp.float32)
    vm = jnp.dot(kva, wuv_ref[...], preferred_element_type=jnp.float32)
    kr = p[:, OFF_KR:OFF_KR + LANES]
    for hd in range(MLA_HEADS):
        sl = slice(hd * LANES, (hd + 1) * LANES)
        qh = _rope(_rms(q[:, sl], qn_ref[...], MLA_QK), cm, sm_up, sm_dn, MLA_ROPE // 4)
        qm_ref[0, hd] = (qh * (MLA_QK ** -0.5)).astype(jnp.bfloat16)
        kh = _rope(_rms(kn[:, sl] + kr, kn_ref[...], MLA_QK), cm, sm_up, sm_dn, MLA_ROPE // 4)
        km_ref[0, hd] = kh.astype(jnp.bfloat16)
        vm_ref[0, hd] = vm[:, sl].astype(jnp.bfloat16)
    for hd in range(SWA_HEADS):
        x = p[:, OFF_SQ + hd * LANES:OFF_SQ + (hd + 1) * LANES]
        qh = _rope(_rms(x, sqn_ref[...], SWA_DIM), cs, ss_up, ss_dn, SWA_DIM // 4)
        qs_ref[0, hd] = (qh * (SWA_DIM ** -0.5)).astype(jnp.bfloat16)
    lane = lax.broadcasted_iota(jnp.int32, (h.shape[0], LANES), 1)
    for g in range(SWA_KV_HEADS):
        x = p[:, OFF_SK + g * LANES:OFF_SK + (g + 1) * LANES]
        kh = _rope(_rms(x, skn_ref[...], SWA_DIM), cs, ss_up, ss_dn, SWA_DIM // 4)
        ks_ref[0, g] = kh.astype(jnp.bfloat16)
        v = p[:, OFF_SV + g * LANES:OFF_SV + (g + 1) * LANES]
        vlo_ref[0, g] = jnp.where(lane < SWA_DIM, v, 0.0).astype(jnp.bfloat16)
        vhi_ref[0, g] = jnp.where(lane >= SWA_DIM, v, 0.0).astype(jnp.bfloat16)


def projections(h, mod, n1, win, qag, wuq, kvag, wuk, wuv, qn, kn, sqn, skn, rope_m, rope_s, r, c):
    b, t, d = h.shape
    nctx = c // r
    full = lambda a: pl.BlockSpec(a.shape, lambda i, j: (0,) * a.ndim)
    head_out = lambda nh: pl.BlockSpec((1, nh, r, LANES), lambda i, j: (i, 0, j, 0))
    head_shape = lambda nh: jax.ShapeDtypeStruct((b, nh, t, LANES), jnp.bfloat16)
    return pl.pallas_call(
        _proj_kernel,
        grid=(b, t // r),
        in_specs=[
            pl.BlockSpec((1, r, d), lambda i, j: (i, j, 0)),
            pl.BlockSpec((1, N_MOD, d), lambda i, j: (jnp.where(j < nctx, b, i), 0, 0)),
            full(n1), full(win), full(qag), full(wuq), full(kvag), full(wuk), full(wuv),
            full(qn), full(kn), full(sqn), full(skn),
            pl.BlockSpec((3, r, LANES), lambda i, j: (0, j, 0)),
            pl.BlockSpec((3, r, LANES), lambda i, j: (0, j, 0)),
        ],
        out_specs=[head_out(MLA_HEADS), head_out(MLA_HEADS), head_out(MLA_HEADS),
                   head_out(SWA_HEADS), head_out(SWA_KV_HEADS), head_out(SWA_KV_HEADS),
                   head_out(SWA_KV_HEADS)],
        out_shape=[head_shape(MLA_HEADS), head_shape(MLA_HEADS), head_shape(MLA_HEADS),
                   head_shape(SWA_HEADS), head_shape(SWA_KV_HEADS), head_shape(SWA_KV_HEADS),
                   head_shape(SWA_KV_HEADS)],
        compiler_params=_params(("parallel", "arbitrary")),
        name="projections",
    )(h, mod, n1, win, qag, wuq, kvag, wuk, wuv, qn, kn, sqn, skn, rope_m, rope_s)


def _mla_kernel(q_ref, k_ref, v_ref, o_ref, *, nctx_blocks, c):
    qi = pl.program_id(2)

    def attend(nk):
        acc = None
        for i in range(2):
            q = q_ref[0, i]
            s = lax.dot_general(q, k_ref[0, i, :nk], (((1,), (1,)), ((), ())),
                                preferred_element_type=jnp.float32)
            m = jnp.max(s, axis=-1, keepdims=True)
            p = jnp.exp(s - m)
            l = jnp.sum(p, axis=-1, keepdims=True)
            o = jnp.dot(p.astype(jnp.bfloat16), v_ref[0, i, :nk], preferred_element_type=jnp.float32)
            o = o * (1.0 / l)
            acc = o if acc is None else acc + o
        o_ref[0] = acc.astype(o_ref.dtype)

    if nctx_blocks:
        @pl.when(qi < nctx_blocks)
        def _():
            attend(c)

        @pl.when(qi >= nctx_blocks)
        def _():
            attend(k_ref.shape[2])
    else:
        attend(k_ref.shape[2])


def mla_attention(qm, km, vm, tq, c, with_ctx):
    b, nh, t, _ = qm.shape
    off = 0 if with_ctx else c // tq
    nq = t // tq - off
    return pl.pallas_call(
        functools.partial(_mla_kernel, nctx_blocks=(c // tq if with_ctx else 0), c=c),
        grid=(b, nh // 2, nq),
        in_specs=[
            pl.BlockSpec((1, 2, tq, LANES), lambda i, hp, j: (i, hp, j + off, 0)),
            pl.BlockSpec((1, 2, t, LANES), lambda i, hp, j: (i, hp, 0, 0)),
            pl.BlockSpec((1, 2, t, LANES), lambda i, hp, j: (i, hp, 0, 0)),
        ],
        out_specs=pl.BlockSpec((1, tq, LANES), lambda i, hp, j: (i, j, hp)),
        out_shape=jax.ShapeDtypeStruct((b, nq * tq, nh // 2 * LANES), jnp.bfloat16),
        compiler_params=_params(("parallel", "arbitrary", "arbitrary")),
        name="mla_attention",
    )(qm, km, vm)


def _swa_kernel(sink_ref, q_ref, k_ref, vlo_ref, vhi_ref, o_ref, *, off, c):
    g = pl.program_id(1)
    qi = pl.program_id(2) + off
    t = k_ref.shape[2]
    rows = SWA_GROUP * WINDOW
    q = q_ref[0].reshape(rows, LANES)
    r_idx = lax.broadcasted_iota(jnp.int32, (rows, 1), 0)
    sink = jnp.zeros((rows, 1), jnp.float32)
    for i in range(SWA_GROUP):
        sink = jnp.where(r_idx // WINDOW == i, sink_ref[g * SWA_GROUP + i], sink)

    ws = pl.multiple_of(jnp.clip(qi * WINDOW - WINDOW, c, t - WIN_KEYS), WINDOW)
    nt = (((1,), (1,)), ((), ()))
    s_ctx = lax.dot_general(q, k_ref[0, 0, :c], nt, preferred_element_type=jnp.float32)
    s_loc = lax.dot_general(q, k_ref[0, 0, pl.ds(ws, WIN_KEYS)], nt, preferred_element_type=jnp.float32)
    qpos = qi * WINDOW + (r_idx % WINDOW)
    kpos = ws + lax.broadcasted_iota(jnp.int32, (1, WIN_KEYS), 1)
    reach = jnp.where(qi * WINDOW >= c, WINDOW, -1)
    s_loc = jnp.where(jnp.abs(qpos - kpos) <= reach, s_loc, NEG_INF)
    m = jnp.maximum(jnp.maximum(jnp.max(s_ctx, axis=-1, keepdims=True),
                                jnp.max(s_loc, axis=-1, keepdims=True)), sink)
    p_ctx = jnp.exp(s_ctx - m)
    p_loc = jnp.exp(s_loc - m)
    l = (jnp.sum(p_ctx, axis=-1, keepdims=True) + jnp.sum(p_loc, axis=-1, keepdims=True)
         + jnp.exp(sink - m))
    inv = 1.0 / l
    p_ctx = p_ctx.astype(jnp.bfloat16)
    p_loc = p_loc.astype(jnp.bfloat16)
    outs = []
    for pair in range(SWA_GROUP // 2):
        acc = None
        for i, v_ref in enumerate((vlo_ref, vhi_ref)):
            rs = slice((2 * pair + i) * WINDOW, (2 * pair + i + 1) * WINDOW)
            o = (jnp.dot(p_ctx[rs], v_ref[0, 0, :c], preferred_element_type=jnp.float32)
                 + jnp.dot(p_loc[rs], v_ref[0, 0, pl.ds(ws, WIN_KEYS)], preferred_element_type=jnp.float32))
            o = o * inv[rs]
            acc = o if acc is None else acc + o
        outs.append(acc)
    o_ref[0] = jnp.concatenate(outs, axis=-1).astype(o_ref.dtype)


def swa_attention(sink, qs, ks, vlo, vhi, c, with_ctx):
    b, nh, t, _ = qs.shape
    off = 0 if with_ctx else c // WINDOW
    nq = t // WINDOW - off
    kv_spec = pl.BlockSpec((1, 1, t, LANES), lambda i, g, j: (i, g, 0, 0))
    return pl.pallas_call(
        functools.partial(_swa_kernel, off=off, c=c),
        grid=(b, SWA_KV_HEADS, nq),
        in_specs=[
            pl.BlockSpec(memory_space=pltpu.SMEM),
            pl.BlockSpec((1, SWA_GROUP, WINDOW, LANES), lambda i, g, j: (i, g, j + off, 0)),
            kv_spec, kv_spec, kv_spec,
        ],
        out_specs=pl.BlockSpec((1, WINDOW, SWA_GROUP // 2 * LANES), lambda i, g, j: (i, j, g)),
        out_shape=jax.ShapeDtypeStruct((b, nq * WINDOW, SWA_HEADS // 2 * LANES), jnp.bfloat16),
        compiler_params=_params(("parallel", "arbitrary", "arbitrary")),
        name="swa_attention",
    )(sink, qs, ks, vlo, vhi)


def _top16(s, n):
    rows = lax.broadcasted_iota(jnp.int32, s.shape, 0)
    vals, idxs = [], []
    for _ in range(PEER_TOPK):
        m = jnp.max(s, axis=0, keepdims=True)
        am = jnp.min(jnp.where(s == m, rows, n), axis=0, keepdims=True)
        s = jnp.where(rows == am, NEG_INF, s)
        vals.append(m)
        idxs.append(am)
    return jnp.concatenate(vals, axis=0), jnp.concatenate(idxs, axis=0)


def _route_kernel(h_ref, om_ref, os_ref, wo_ref, mod_ref, n2_ref, wq_ref, keys_ref,
                  h1_ref, xhi_ref, xlo_ref, idx_ref, g_ref, q_scr, idxt_scr, gt_scr):
    h = h_ref[0]
    d = h.shape[-1]
    mod = mod_ref[0]
    half = om_ref.shape[-1]
    mix = (jnp.dot(om_ref[0], wo_ref[:half], preferred_element_type=jnp.float32)
           + jnp.dot(os_ref[0], wo_ref[half:], preferred_element_type=jnp.float32))
    h1 = h + mod[2:3] * mix
    h1_ref[0] = h1
    x = _rms(h1, n2_ref[...], d) * (1.0 + mod[4:5]) + mod[3:4]
    xhi = x.astype(jnp.bfloat16)
    xhi_ref[0] = xhi
    xlo_ref[0] = (x - xhi.astype(jnp.float32)).astype(jnp.bfloat16)
    q = jnp.dot(xhi, wq_ref[...], preferred_element_type=jnp.float32)
    for k in range(2 * PEER_HEADS):
        q_scr[k] = q[:, k * PEER_DHALF:(k + 1) * PEER_DHALF].astype(jnp.bfloat16)

    def head(hh, carry):
        sv, si = [], []
        for part in range(2):
            qhp = q_scr[2 * hh + part]
            st = lax.dot_general(keys_ref[hh, part], qhp, (((1,), (1,)), ((), ())),
                                 preferred_element_type=jnp.float32)
            v, i = _top16(st, PEER_NKEYS)
            sv.append(v)
            si.append(i)
        cand = jnp.concatenate([sv[0][a:a + 1] + sv[1] for a in range(PEER_TOPK)], axis=0)
        cv, ci = _top16(cand, PEER_TOPK * PEER_TOPK)
        ia = ci >> 4
        ib = ci & (PEER_TOPK - 1)
        i1 = jnp.zeros_like(ci)
        i2 = jnp.zeros_like(ci)
        for a in range(PEER_TOPK):
            i1 = jnp.where(ia == a, si[0][a:a + 1], i1)
            i2 = jnp.where(ib == a, si[1][a:a + 1], i2)
        e = jnp.exp(cv - cv[0:1])
        gate = e * (1.0 / jnp.sum(e, axis=0, keepdims=True))
        row = pl.multiple_of(hh * PEER_TOPK, PEER_TOPK)
        idxt_scr[pl.ds(row, PEER_TOPK), :] = ((i1 * PEER_NKEYS + i2) * N_CHUNK).astype(jnp.float32)
        gt_scr[pl.ds(row, PEER_TOPK), :] = gate
        return carry

    lax.fori_loop(0, PEER_HEADS, head, 0)
    idx_ref[0] = idxt_scr[...].T.astype(jnp.int32)
    g_ref[0] = gt_scr[...].T


def route(h, om, osw, wo, mod, n2, wq, keys, r, c, with_ctx):
    b, t, d = h.shape
    off = 0 if with_ctx else c // r
    nblk = t // r - off
    nctx = c // r
    tq = nblk * r
    full = lambda a: pl.BlockSpec(a.shape, lambda i, j: (0,) * a.ndim)
    row_spec = lambda w: pl.BlockSpec((1, r, w), lambda i, j: (i, j, 0))
    return pl.pallas_call(
        _route_kernel,
        grid=(b, nblk),
        in_specs=[
            pl.BlockSpec((1, r, d), lambda i, j: (i, j + off, 0)),
            row_spec(om.shape[-1]), row_spec(osw.shape[-1]),
            full(wo),
            pl.BlockSpec((1, N_MOD, d), lambda i, j: (jnp.where(j + off < nctx, b, i), 0, 0)),
            full(n2), full(wq), full(keys),
        ],
        out_specs=[row_spec(d), row_spec(d), row_spec(d), row_spec(PEER_ROWS), row_spec(PEER_ROWS)],
        out_shape=[
            jax.ShapeDtypeStruct((b, tq, d), jnp.float32),
            jax.ShapeDtypeStruct((b, tq, d), jnp.bfloat16),
            jax.ShapeDtypeStruct((b, tq, d), jnp.bfloat16),
            jax.ShapeDtypeStruct((b, tq, PEER_ROWS), jnp.int32),
            jax.ShapeDtypeStruct((b, tq, PEER_ROWS), jnp.float32),
        ],
        scratch_shapes=[
            pltpu.VMEM((2 * PEER_HEADS, r, PEER_DHALF), jnp.bfloat16),
            pltpu.VMEM((PEER_ROWS, r), jnp.float32),
            pltpu.VMEM((PEER_ROWS, r), jnp.float32),
        ],
        compiler_params=_params(("parallel", "arbitrary")),
        name="peer_route",
    )(h, om, osw, wo, mod, n2, wq, keys)


def pack_table(tab):
    e, d = tab.shape
    bits = lax.bitcast_convert_type(tab.astype(jnp.bfloat16), jnp.uint16).astype(jnp.uint32)
    w = (bits[:, : d // 2] << 16) | bits[:, d // 2:]
    return w.reshape(e * N_CHUNK, LANES)


def _gather_rows(idx_ref, tab_ref, tile_ref, t):
    for j in range(PEER_ROWS):
        r = pl.multiple_of(idx_ref[t, j], N_CHUNK)
        tile_ref[pl.ds(j, N_CHUNK, stride=TILE_STRIDE), :] = tab_ref[pl.ds(r, N_CHUNK), :]


def _token_pipeline(gather, compute, tiles):
    last = TOK_BLK - 1
    for k in range(TOK_SET):
        gather(tiles[0].at[k], k)

    def step(t, cur, nxt):
        for k in range(TOK_SET):
            compute(cur.at[k], t + k)
            gather(nxt.at[k], jnp.minimum(t + TOK_SET + k, last))

    def body(i, carry):
        t = TOK_SET * i

        @pl.when(i % 2 == 0)
        def _():
            step(t, tiles[0], tiles[1])

        @pl.when(i % 2 == 1)
        def _():
            step(t, tiles[1], tiles[0])

        return carry

    lax.fori_loop(0, TOK_BLK // TOK_SET, body, 0)


def _chunk(tile_ref, c):
    return pltpu.bitcast(tile_ref[pl.ds(c * TILE_STRIDE, PEER_ROWS), :], jnp.bfloat16)


def _u_kernel(idx_ref, x_ref, g_ref, tab_ref, w_ref, *tiles):
    row = lax.broadcasted_iota(jnp.int32, (16, 2 * PEER_ROWS), 0) & 7

    def scores(tile_ref, t):
        x16 = x_ref[t]
        top = jnp.zeros((16, 2 * PEER_ROWS), jnp.float32)
        bot = jnp.zeros((16, 2 * PEER_ROWS), jnp.float32)
        for c in range(N_CHUNK):
            y = lax.dot_general(x16, _chunk(tile_ref, c), (((1,), (1,)), ((), ())),
                                preferred_element_type=jnp.float32)
            top = top + jnp.where(row == c, y, 0.0)
            bot = bot + jnp.where(row == c + N_CHUNK, y, 0.0)
        tot = top + pltpu.roll(bot, 1, axis=1)
        a = jnp.sum(tot, axis=0, keepdims=True)
        gelu = 0.5 * a * (1.0 + lax.erf(a * (2.0 ** -0.5)))
        w_ref[t] = g_ref[t] * gelu

    _token_pipeline(functools.partial(_gather_rows, idx_ref, tab_ref), scores, tiles)


def peer_scores(idx, x16, g, tab):
    n = idx.shape[0]
    tok = lambda *s: pl.BlockSpec((TOK_BLK,) + s, lambda i: (i,) + (0,) * len(s))
    return pl.pallas_call(
        _u_kernel,
        grid=(n // TOK_BLK,),
        in_specs=[
            pl.BlockSpec((TOK_BLK, PEER_ROWS), lambda i: (i, 0), memory_space=pltpu.SMEM),
            tok(16, LANES), tok(1, 2 * PEER_ROWS),
            pl.BlockSpec(tab.shape, lambda i: (0, 0), pipeline_mode=pl.Buffered(1)),
        ],
        out_specs=tok(1, 2 * PEER_ROWS),
        out_shape=jax.ShapeDtypeStruct((n, 1, 2 * PEER_ROWS), jnp.float32),
        scratch_shapes=[pltpu.VMEM((TOK_SET, N_CHUNK * TILE_STRIDE, LANES), jnp.uint32)] * 2,
        compiler_params=_params(("arbitrary",)),
        name="peer_scores",
    )(idx, x16, g, tab)


def _v_kernel(idx_ref, w_ref, h_ref, g2_ref, tab_ref, o_ref, *tiles):
    row = lax.broadcasted_iota(jnp.int32, (16, 2 * PEER_ROWS), 0)

    def combine(tile_ref, t):
        w = w_ref[t]
        w_hi = w.astype(jnp.bfloat16).astype(jnp.float32)
        w_lo = w - w_hi
        parts = (w_hi, pltpu.roll(w_hi, 2 * PEER_ROWS - 1, axis=1),
                 w_lo, pltpu.roll(w_lo, 2 * PEER_ROWS - 1, axis=1))
        acc = jnp.zeros((16, LANES), jnp.float32)
        for c in range(N_CHUNK):
            lhs = jnp.zeros((16, 2 * PEER_ROWS), jnp.float32)
            for k, part in enumerate(parts):
                lhs = jnp.where(row == c + N_CHUNK * k, part, lhs)
            acc = acc + jnp.dot(lhs.astype(jnp.bfloat16), _chunk(tile_ref, c),
                                preferred_element_type=jnp.float32)
        o_ref[t] = h_ref[t] + g2_ref[0, 0] * (acc[:8] + acc[8:])

    _token_pipeline(functools.partial(_gather_rows, idx_ref, tab_ref), combine, tiles)


def peer_combine(idx, w, h8, g2, tab, blocks_per_sample, ctx_blocks):
    n = idx.shape[0]
    tok = lambda *s: pl.BlockSpec((TOK_BLK,) + s, lambda i: (i,) + (0,) * len(s))
    g2_map = lambda i: (i // blocks_per_sample,
                        jnp.where(i % blocks_per_sample < ctx_blocks, 0, 1), 0, 0)
    return pl.pallas_call(
        _v_kernel,
        grid=(n // TOK_BLK,),
        in_specs=[
            pl.BlockSpec((TOK_BLK, PEER_ROWS), lambda i: (i, 0), memory_space=pltpu.SMEM),
            tok(1, 2 * PEER_ROWS), tok(8, LANES),
            pl.BlockSpec((1, 1, 8, LANES), g2_map),
            pl.BlockSpec(tab.shape, lambda i: (0, 0), pipeline_mode=pl.Buffered(1)),
        ],
        out_specs=tok(8, LANES),
        out_shape=jax.ShapeDtypeStruct((n, 8, LANES), jnp.float32),
        scratch_shapes=[pltpu.VMEM((TOK_SET, N_CHUNK * TILE_STRIDE, LANES), jnp.uint32)] * 2,
        compiler_params=_params(("arbitrary",)),
        name="peer_combine",
    )(idx, w, h8, g2, tab)


def _slots(w, width, offset=0):
    lead = w.shape[:-1]
    n = w.shape[-1] // width
    w = w.reshape(lead + (n, width))
    w = jnp.pad(w, [(0, 0)] * len(lead) + [(0, 0), (offset, LANES - width - offset)])
    return w.reshape(lead + (n * LANES,))


def _rope_tables(rot_dim, lane0, t, c):
    s = t - c
    q = rot_dim // 4
    pos = jnp.arange(s, dtype=jnp.float32)
    rows = jnp.floor(pos / GRID_W)
    cols = pos - rows * GRID_W
    inv = ROPE_BASE ** (-jnp.arange(q, dtype=jnp.float32) / q)
    ar = rows[:, None] * inv
    ac = cols[:, None] * inv
    zero = jnp.zeros_like(ar)
    cos = jnp.cos(jnp.concatenate([ar, ar, ac, ac], axis=-1))
    up = jnp.concatenate([-jnp.sin(ar), zero, -jnp.sin(ac), zero], axis=-1)
    dn = jnp.concatenate([zero, jnp.sin(ar), zero, jnp.sin(ac)], axis=-1)
    pad = lambda a, fill: jnp.pad(
        jnp.pad(a, ((0, 0), (lane0, LANES - lane0 - rot_dim)), constant_values=fill),
        ((c, 0), (0, 0)), constant_values=fill)
    cos = jnp.pad(jnp.pad(cos, ((0, 0), (lane0, LANES - lane0 - rot_dim)), constant_values=1.0),
                  ((c, 0), (0, 0)), constant_values=1.0)
    return jnp.stack([cos, pad(up, 0.0), pad(dn, 0.0)])


def kernel(x, c, ctx, c_ctx, ada_w, ada_b, norm1_g, norm2_g, w_in, mla_qa_g, mla_wuq, mla_kva_g, mla_wukv, mla_qn_g, mla_kn_g, swa_qn_g, swa_kn_g, swa_sink, w_out, peer_wq, peer_keys, peer_u, peer_v):
    b, s, d = x.shape
    nctx = ctx.shape[1]
    t = nctx + s
    depth = ada_w.shape[0]
    r = min(256, nctx)
    bf = jnp.bfloat16

    cond = jnp.zeros((16, d), jnp.float32).at[:b].set(c).at[b].set(c_ctx)
    mod_all = modulation(cond, ada_w, ada_b).reshape(depth, 16, N_MOD, d)
    rope_m = _rope_tables(MLA_ROPE, MLA_NOPE, t, nctx)
    rope_s = _rope_tables(SWA_DIM, 0, t, nctx)

    h = jnp.concatenate([ctx, x], axis=1)
    for l in range(depth):
        last = l == depth - 1
        with_ctx = not last
        mod = mod_all[l]
        wi = w_in[l]
        kv0 = Q_COLS + MLA_KV_RANK
        sk0 = kv0 + MLA_ROPE
        sv0 = sk0 + SWA_KV_HEADS * SWA_DIM
        sv = wi[:, sv0:].reshape(d, SWA_KV_HEADS, 1, SWA_DIM)
        win = jnp.concatenate([
            wi[:, :MLA_Q_RANK], wi[:, Q_COLS:kv0],
            _slots(wi[:, kv0:sk0], MLA_ROPE, MLA_NOPE),
            _slots(wi[:, MLA_Q_RANK:Q_COLS], SWA_DIM),
            _slots(wi[:, sk0:sv0], SWA_DIM),
            jnp.broadcast_to(sv, (d, SWA_KV_HEADS, 2, SWA_DIM)).reshape(d, SWA_KV_HEADS * LANES),
        ], axis=1).astype(bf)
        wuq = _slots(mla_wuq[l], MLA_QK).astype(bf)
        wukv = mla_wukv[l].reshape(MLA_KV_RANK, MLA_HEADS, MLA_NOPE + MLA_V)
        wuk = _slots(wukv[:, :, :MLA_NOPE].reshape(MLA_KV_RANK, -1), MLA_NOPE).astype(bf)
        wv = wukv[:, :, MLA_NOPE:].reshape(MLA_KV_RANK, MLA_HEADS // 2, 2, MLA_V)
        zero = jnp.zeros_like(wv[:, :, 0])
        wuv = jnp.stack([jnp.concatenate([wv[:, :, 0], zero], axis=-1),
                         jnp.concatenate([zero, wv[:, :, 1]], axis=-1)], axis=2)
        wuv = wuv.reshape(MLA_KV_RANK, MLA_HEADS * LANES).astype(bf)
        row = lambda g: g.reshape(1, -1)

        qm, km, vm, qs, ks, vlo, vhi = projections(
            h, mod, row(norm1_g[l]), win, row(mla_qa_g[l]), wuq, row(mla_kva_g[l]), wuk, wuv,
            row(_slots(mla_qn_g[l], MLA_QK)), row(_slots(mla_kn_g[l], MLA_QK)),
            row(_slots(swa_qn_g[l], SWA_DIM)), row(_slots(swa_kn_g[l], SWA_DIM)),
            rope_m, rope_s, r, nctx)
        om = mla_attention(qm, km, vm, r, nctx, with_ctx)
        osw = swa_attention(swa_sink[l], qs, ks, vlo, vhi, nctx, with_ctx)
        h1, xhi, xlo, idx, gate = route(
            h, om, osw, w_out[l].astype(bf), mod, row(norm2_g[l]), peer_wq[l].astype(bf),
            peer_keys[l].astype(bf), r, nctx, with_ctx)

        tl = h1.shape[1]
        n = b * tl
        idx = idx.reshape(n, PEER_ROWS)
        x16 = jnp.concatenate([xhi.reshape(n, 8, LANES), xlo.reshape(n, 8, LANES)], axis=1)
        gate = jnp.stack([jnp.zeros_like(gate), gate], axis=-1).reshape(n, 1, 2 * PEER_ROWS)
        w = peer_scores(idx, x16, gate, pack_table(peer_u[l]))
        g2 = jnp.stack([jnp.broadcast_to(mod[b, 5], (b, d)), mod[:b, 5]], axis=1)
        h = peer_combine(idx, w, h1.reshape(n, 8, LANES), g2.reshape(b, 2, 8, LANES),
                         pack_table(peer_v[l]), tl // TOK_BLK,
                         nctx // TOK_BLK if with_ctx else 0).reshape(b, tl, d)
    return h
```

```python
import functools
import jax
import jax.numpy as jnp
from jax import lax
from jax.experimental import pallas as pl
from jax.experimental.pallas import tpu as pltpu

LANES = 128
EPS = 1e-6
ROPE_BASE = 10000.0
GRID_W = 64
N_MOD = 6

MLA_HEADS = 8
MLA_NOPE = 64
MLA_ROPE = 32
MLA_QK = MLA_NOPE + MLA_ROPE
MLA_V = 64
MLA_Q_RANK = 384
MLA_KV_RANK = 256
SWA_HEADS = 8
SWA_KV_HEADS = 2
SWA_GROUP = SWA_HEADS // SWA_KV_HEADS
SWA_DIM = 64
WINDOW = 128
WIN_KEYS = 3 * WINDOW
Q_COLS = MLA_Q_RANK + SWA_HEADS * SWA_DIM

PEER_HEADS = 8
PEER_NKEYS = 128
PEER_DHALF = 128
PEER_TOPK = 16
PEER_ROWS = PEER_HEADS * PEER_TOPK
N_CHUNK = 4
TILE_STRIDE = PEER_ROWS + 1
TOK_BLK = 128
TOK_SET = 4

OFF_QA = 0
OFF_KVA = OFF_QA + MLA_Q_RANK
OFF_KR = OFF_KVA + MLA_KV_RANK
OFF_SQ = OFF_KR + LANES
OFF_SK = OFF_SQ + SWA_HEADS * LANES
OFF_SV = OFF_SK + SWA_KV_HEADS * LANES
N_IN = OFF_SV + SWA_KV_HEADS * LANES

VMEM_LIMIT = 56 * 1024 * 1024
NEG_INF = float("-inf")


def _params(sem, vmem=VMEM_LIMIT):
    return pltpu.CompilerParams(dimension_semantics=sem, vmem_limit_bytes=vmem)


def _rms(x, g, n):
    ms = jnp.sum(x * x, axis=-1, keepdims=True) * (1.0 / n)
    return x * lax.rsqrt(ms + EPS) * g


def _rope(x, cos, sin_up, sin_dn, shift):
    return (x * cos + pltpu.roll(x, LANES - shift, axis=1) * sin_up
            + pltpu.roll(x, shift, axis=1) * sin_dn)


def _mod_kernel(c_ref, w_ref, b_ref, o_ref):
    c = c_ref[...]
    s = c * (1.0 / (1.0 + jnp.exp(-c)))
    o_ref[0] = jnp.dot(s, w_ref[0], preferred_element_type=jnp.float32,
                       precision=lax.Precision.HIGHEST) + b_ref[0]


def modulation(cond, ada_w, ada_b):
    nl, d, n6 = ada_w.shape
    tn = 1536
    return pl.pallas_call(
        _mod_kernel,
        grid=(nl, n6 // tn),
        in_specs=[
            pl.BlockSpec(cond.shape, lambda l, j: (0, 0)),
            pl.BlockSpec((1, d, tn), lambda l, j: (l, 0, j)),
            pl.BlockSpec((1, 1, tn), lambda l, j: (l, 0, j)),
        ],
        out_specs=pl.BlockSpec((1, cond.shape[0], tn), lambda l, j: (l, 0, j)),
        out_shape=jax.ShapeDtypeStruct((nl, cond.shape[0], n6), jnp.float32),
        compiler_params=_params(("arbitrary", "arbitrary")),
        name="modulation",
    )(cond, ada_w, ada_b.reshape(nl, 1, n6))


def _proj_kernel(h_ref, mod_ref, n1_ref, win_ref, qag_ref, wuq_ref, kvag_ref, wuk_ref, wuv_ref,
                 qn_ref, kn_ref, sqn_ref, skn_ref, rm_ref, rs_ref,
                 qm_ref, km_ref, vm_ref, qs_ref, ks_ref, vlo_ref, vhi_ref):
    h = h_ref[0]
    d = h.shape[-1]
    mod = mod_ref[0]
    a = _rms(h, n1_ref[...], d) * (1.0 + mod[1:2]) + mod[0:1]
    p = jnp.dot(a.astype(jnp.bfloat16), win_ref[...], preferred_element_type=jnp.float32)

    cm, sm_up, sm_dn = rm_ref[0], rm_ref[1], rm_ref[2]
    cs, ss_up, ss_dn = rs_ref[0], rs_ref[1], rs_ref[2]

    qa = _rms(p[:, OFF_QA:OFF_QA + MLA_Q_RANK], qag_ref[...], MLA_Q_RANK)
    q = jnp.dot(qa.astype(jnp.bfloat16), wuq_ref[...], preferred_element_type=jnp.float32)
    kva = _rms(p[:, OFF_KVA:OFF_KVA + MLA_KV_RANK], kvag_ref[...], MLA_KV_RANK).astype(jnp.bfloat16)
    kn = jnp.dot(kva, wuk_ref[...], preferred_element_type=jnp.float32)
    vm = jnp.dot(kva, wuv_ref[...], preferred_element_type=jnp.float32)
    kr = p[:, OFF_KR:OFF_KR + LANES]
    for hd in range(MLA_HEADS):
        sl = slice(hd * LANES, (hd + 1) * LANES)
        qh = _rope(_rms(q[:, sl], qn_ref[...], MLA_QK), cm, sm_up, sm_dn, MLA_ROPE // 4)
        qm_ref[0, hd] = (qh * (MLA_QK ** -0.5)).astype(jnp.bfloat16)
        kh = _rope(_rms(kn[:, sl] + kr, kn_ref[...], MLA_QK), cm, sm_up, sm_dn, MLA_ROPE // 4)
        km_ref[0, hd] = kh.astype(jnp.bfloat16)
        vm_ref[0, hd] = vm[:, sl].astype(jnp.bfloat16)
    for hd in range(SWA_HEADS):
        x = p[:, OFF_SQ + hd * LANES:OFF_SQ + (hd + 1) * LANES]
        qh = _rope(_rms(x, sqn_ref[...], SWA_DIM), cs, ss_up, ss_dn, SWA_DIM // 4)
        qs_ref[0, hd] = (qh * (SWA_DIM ** -0.5)).astype(jnp.bfloat16)
    lane = lax.broadcasted_iota(jnp.int32, (h.shape[0], LANES), 1)
    for g in range(SWA_KV_HEADS):
        x = p[:, OFF_SK + g * LANES:OFF_SK + (g + 1) * LANES]
        kh = _rope(_rms(x, skn_ref[...], SWA_DIM), cs, ss_up, ss_dn, SWA_DIM // 4)
        ks_ref[0, g] = kh.astype(jnp.bfloat16)
        v = p[:, OFF_SV + g * LANES:OFF_SV + (g + 1) * LANES]
        vlo_ref[0, g] = jnp.where(lane < SWA_DIM, v, 0.0).astype(jnp.bfloat16)
        vhi_ref[0, g] = jnp.where(lane >= SWA_DIM, v, 0.0).astype(jnp.bfloat16)


def projections(h, mod, n1, win, qag, wuq, kvag, wuk, wuv, qn, kn, sqn, skn, rope_m, rope_s, r, c):
    b, t, d = h.shape
    nctx = c // r
    full = lambda a: pl.BlockSpec(a.shape, lambda i, j: (0,) * a.ndim)
    head_out = lambda nh: pl.BlockSpec((1, nh, r, LANES), lambda i, j: (i, 0, j, 0))
    head_shape = lambda nh: jax.ShapeDtypeStruct((b, nh, t, LANES), jnp.bfloat16)
    return pl.pallas_call(
        _proj_kernel,
        grid=(b, t // r),
        in_specs=[
            pl.BlockSpec((1, r, d), lambda i, j: (i, j, 0)),
            pl.BlockSpec((1, N_MOD, d), lambda i, j: (jnp.where(j < nctx, b, i), 0, 0)),
            full(n1), full(win), full(qag), full(wuq), full(kvag), full(wuk), full(wuv),
            full(qn), full(kn), full(sqn), full(skn),
            pl.BlockSpec((3, r, LANES), lambda i, j: (0, j, 0)),
            pl.BlockSpec((3, r, LANES), lambda i, j: (0, j, 0)),
        ],
        out_specs=[head_out(MLA_HEADS), head_out(MLA_HEADS), head_out(MLA_HEADS),
                   head_out(SWA_HEADS), head_out(SWA_KV_HEADS), head_out(SWA_KV_HEADS),
                   head_out(SWA_KV_HEADS)],
        out_shape=[head_shape(MLA_HEADS), head_shape(MLA_HEADS), head_shape(MLA_HEADS),
                   head_shape(SWA_HEADS), head_shape(SWA_KV_HEADS), head_shape(SWA_KV_HEADS),
                   head_shape(SWA_KV_HEADS)],
        compiler_params=_params(("parallel", "arbitrary")),
        name="projections",
    )(h, mod, n1, win, qag, wuq, kvag, wuk, wuv, qn, kn, sqn, skn, rope_m, rope_s)


def _mla_kernel(q_ref, k_ref, v_ref, o_ref, *, nctx_blocks, c):
    qi = pl.program_id(2)

    def attend(nk):
        acc = None
        for i in range(2):
            q = q_ref[0, i]
            s = lax.dot_general(q, k_ref[0, i, :nk], (((1,), (1,)), ((), ())),
                                preferred_element_type=jnp.float32)
            m = jnp.max(s, axis=-1, keepdims=True)
            p = jnp.exp(s - m)
            l = jnp.sum(p, axis=-1, keepdims=True)
            o = jnp.dot(p.astype(jnp.bfloat16), v_ref[0, i, :nk], preferred_element_type=jnp.float32)
            o = o * (1.0 / l)
            acc = o if acc is None else acc + o
        o_ref[0] = acc.astype(o_ref.dtype)

    if nctx_blocks:
        @pl.when(qi < nctx_blocks)
        def _():
            attend(c)

        @pl.when(qi >= nctx_blocks)
        def _():
            attend(k_ref.shape[2])
    else:
        attend(k_ref.shape[2])


def mla_attention(qm, km, vm, tq, c, with_ctx):
    b, nh, t, _ = qm.shape
    off = 0 if with_ctx else c // tq
    nq = t // tq - off
    return pl.pallas_call(
        functools.partial(_mla_kernel, nctx_blocks=(c // tq if with_ctx else 0), c=c),
        grid=(b, nh // 2, nq),
        in_specs=[
            pl.BlockSpec((1, 2, tq, LANES), lambda i, hp, j: (i, hp, j + off, 0)),
            pl.BlockSpec((1, 2, t, LANES), lambda i, hp, j: (i, hp, 0, 0)),
            pl.BlockSpec((1, 2, t, LANES), lambda i, hp, j: (i, hp, 0, 0)),
        ],
        out_specs=pl.BlockSpec((1, tq, LANES), lambda i, hp, j: (i, j, hp)),
        out_shape=jax.ShapeDtypeStruct((b, nq * tq, nh // 2 * LANES), jnp.bfloat16),
        compiler_params=_params(("parallel", "arbitrary", "arbitrary")),
        name="mla_attention",
    )(qm, km, vm)


def _swa_kernel(sink_ref, q_ref, k_ref, vlo_ref, vhi_ref, o_ref, *, off, c):
    g = pl.program_id(1)
    qi = pl.program_id(2) + off
    t = k_ref.shape[2]
    rows = SWA_GROUP * WINDOW
    q = q_ref[0].reshape(rows, LANES)
    r_idx = lax.broadcasted_iota(jnp.int32, (rows, 1), 0)
    sink = jnp.zeros((rows, 1), jnp.float32)
    for i in range(SWA_GROUP):
        sink = jnp.where(r_idx // WINDOW == i, sink_ref[g * SWA_GROUP + i], sink)

    ws = pl.multiple_of(jnp.clip(qi * WINDOW - WINDOW, c, t - WIN_KEYS), WINDOW)
    nt = (((1,), (1,)), ((), ()))
    s_ctx = lax.dot_general(q, k_ref[0, 0, :c], nt, preferred_element_type=jnp.float32)
    s_loc = lax.dot_general(q, k_ref[0, 0, pl.ds(ws, WIN_KEYS)], nt, preferred_element_type=jnp.float32)
    qpos = qi * WINDOW + (r_idx % WINDOW)
    kpos = ws + lax.broadcasted_iota(jnp.int32, (1, WIN_KEYS), 1)
    reach = jnp.where(qi * WINDOW >= c, WINDOW, -1)
    s_loc = jnp.where(jnp.abs(qpos - kpos) <= reach, s_loc, NEG_INF)
    m = jnp.maximum(jnp.maximum(jnp.max(s_ctx, axis=-1, keepdims=True),
                                jnp.max(s_loc, axis=-1, keepdims=True)), sink)
    p_ctx = jnp.exp(s_ctx - m)
    p_loc = jnp.exp(s_loc - m)
    l = (jnp.sum(p_ctx, axis=-1, keepdims=True) + jnp.sum(p_loc, axis=-1, keepdims=True)
         + jnp.exp(sink - m))
    inv = 1.0 / l
    p_ctx = p_ctx.astype(jnp.bfloat16)
    p_loc = p_loc.astype(jnp.bfloat16)
    outs = []
    for pair in range(SWA_GROUP // 2):
        acc = None
        for i, v_ref in enumerate((vlo_ref, vhi_ref)):
            rs = slice((2 * pair + i) * WINDOW, (2 * pair + i + 1) * WINDOW)
            o = (jnp.dot(p_ctx[rs], v_ref[0, 0, :c], preferred_element_type=jnp.float32)
                 + jnp.dot(p_loc[rs], v_ref[0, 0, pl.ds(ws, WIN_KEYS)], preferred_element_type=jnp.float32))
            o = o * inv[rs]
            acc = o if acc is None else acc + o
        outs.append(acc)
    o_ref[0] = jnp.concatenate(outs, axis=-1).astype(o_ref.dtype)


def swa_attention(sink, qs, ks, vlo, vhi, c, with_ctx):
    b, nh, t, _ = qs.shape
    off = 0 if with_ctx else c // WINDOW
    nq = t // WINDOW - off
    kv_spec = pl.BlockSpec((1, 1, t, LANES), lambda i, g, j: (i, g, 0, 0))
    return pl.pallas_call(
        functools.partial(_swa_kernel, off=off, c=c),
        grid=(b, SWA_KV_HEADS, nq),
        in_specs=[
            pl.BlockSpec(memory_space=pltpu.SMEM),
            pl.BlockSpec((1, SWA_GROUP, WINDOW, LANES), lambda i, g, j: (i, g, j + off, 0)),
            kv_spec, kv_spec, kv_spec,
        ],
        out_specs=pl.BlockSpec((1, WINDOW, SWA_GROUP // 2 * LANES), lambda i, g, j: (i, j, g)),
        out_shape=jax.ShapeDtypeStruct((b, nq * WINDOW, SWA_HEADS // 2 * LANES), jnp.bfloat16),
        compiler_params=_params(("parallel", "arbitrary", "arbitrary")),
        name="swa_attention",
    )(sink, qs, ks, vlo, vhi)


def _tree(op, xs):
    xs = list(xs)
    while len(xs) > 1:
        xs = [op(xs[i], xs[i + 1]) for i in range(0, len(xs) - 1, 2)] + (xs[-1:] if len(xs) % 2 else [])
    return xs[0]


def _top16(vals, ids, big):
    out_v, out_i = [], []
    for _ in range(PEER_TOPK):
        m = jnp.max(_tree(jnp.maximum, vals), axis=0, keepdims=True)
        hit = [jnp.where(v == m, i, big) for v, i in zip(vals, ids)]
        am = jnp.min(_tree(jnp.minimum, hit), axis=0, keepdims=True)
        vals = [jnp.where(h == am, NEG_INF, v) for h, v in zip(hit, vals)]
        out_v.append(m)
        out_i.append(am)
    return jnp.concatenate(out_v, axis=0), jnp.concatenate(out_i, axis=0)


def _pair_candidates(sv1, sv2, sub):
    lo, hi = sv1[:8], sv1[8:]
    vals = [lo + sv2[0:1], hi + sv2[0:1]]
    ids = [sub * PEER_TOPK, (sub + 8) * PEER_TOPK]
    for b in range(1, 8):
        keep = PEER_TOPK // (b + 1)
        v = lo + sv2[b:b + 1]
        vals.append(v if keep >= 8 else jnp.where(sub < keep, v, NEG_INF))
        ids.append(sub * PEER_TOPK + b)
    vals.append(sv1[0:1] + sv2[8:])
    ids.append(sub + 8)
    return vals, ids


def _route_kernel(h_ref, om_ref, os_ref, wo_ref, mod_ref, n2_ref, wq_ref, keys_ref,
                  h1_ref, xhi_ref, xlo_ref, idx_ref, g_ref, q_scr, idxt_scr, gt_scr):
    h = h_ref[0]
    d = h.shape[-1]
    mod = mod_ref[0]
    half = om_ref.shape[-1]
    mix = (jnp.dot(om_ref[0], wo_ref[:half], preferred_element_type=jnp.float32)
           + jnp.dot(os_ref[0], wo_ref[half:], preferred_element_type=jnp.float32))
    h1 = h + mod[2:3] * mix
    h1_ref[0] = h1
    x = _rms(h1, n2_ref[...], d) * (1.0 + mod[4:5]) + mod[3:4]
    xhi = x.astype(jnp.bfloat16)
    xhi_ref[0] = xhi
    xlo_ref[0] = (x - xhi.astype(jnp.float32)).astype(jnp.bfloat16)
    q = jnp.dot(xhi, wq_ref[...], preferred_element_type=jnp.float32)
    for k in range(2 * PEER_HEADS):
        q_scr[k] = q[:, k * PEER_DHALF:(k + 1) * PEER_DHALF].astype(jnp.bfloat16)

    def head(hh, carry):
        row = pl.multiple_of(hh * PEER_TOPK, PEER_TOPK)
        st = [lax.dot_general(keys_ref[hh, part], q_scr[2 * hh + part], (((1,), (1,)), ((), ())),
                              preferred_element_type=jnp.float32) for part in range(2)]
        for c0 in range(0, st[0].shape[1], LANES):
            sub = lax.broadcasted_iota(jnp.int32, (8, LANES), 0).astype(jnp.float32)
            sv, si = [], []
            for part in range(2):
                blocks = [st[part][8 * g:8 * g + 8, c0:c0 + LANES] for g in range(PEER_NKEYS // 8)]
                v, i = _top16(blocks, [sub + 8 * g for g in range(PEER_NKEYS // 8)], PEER_NKEYS)
                sv.append(v)
                si.append(i)
            cv, ci = _top16(*_pair_candidates(sv[0], sv[1], sub), PEER_TOPK * PEER_TOPK)
            ia = jnp.floor(ci * (1.0 / PEER_TOPK))
            ib = ci - ia * PEER_TOPK
            i1 = jnp.zeros_like(ci)
            i2 = jnp.zeros_like(ci)
            for a in range(PEER_TOPK):
                i1 = jnp.where(ia == a, si[0][a:a + 1], i1)
                i2 = jnp.where(ib == a, si[1][a:a + 1], i2)
            e = jnp.exp(cv - cv[0:1])
            gate = e * (1.0 / jnp.sum(e, axis=0, keepdims=True))
            idxt_scr[pl.ds(row, PEER_TOPK), c0:c0 + LANES] = (
                (i1 * PEER_NKEYS + i2) * N_CHUNK)
            gt_scr[pl.ds(row, PEER_TOPK), c0:c0 + LANES] = gate
        return carry

    lax.fori_loop(0, PEER_HEADS, head, 0)
    idx_ref[0] = idxt_scr[...].T.astype(jnp.int32)
    g_ref[0] = gt_scr[...].T


def route(h, om, osw, wo, mod, n2, wq, keys, r, c, with_ctx):
    b, t, d = h.shape
    off = 0 if with_ctx else c // r
    nblk = t // r - off
    nctx = c // r
    tq = nblk * r
    full = lambda a: pl.BlockSpec(a.shape, lambda i, j: (0,) * a.ndim)
    row_spec = lambda w: pl.BlockSpec((1, r, w), lambda i, j: (i, j, 0))
    return pl.pallas_call(
        _route_kernel,
        grid=(b, nblk),
        in_specs=[
            pl.BlockSpec((1, r, d), lambda i, j: (i, j + off, 0)),
            row_spec(om.shape[-1]), row_spec(osw.shape[-1]),
            full(wo),
            pl.BlockSpec((1, N_MOD, d), lambda i, j: (jnp.where(j + off < nctx, b, i), 0, 0)),
            full(n2), full(wq), full(keys),
        ],
        out_specs=[row_spec(d), row_spec(d), row_spec(d), row_spec(PEER_ROWS), row_spec(PEER_ROWS)],
        out_shape=[
            jax.ShapeDtypeStruct((b, tq, d), jnp.float32),
            jax.ShapeDtypeStruct((b, tq, d), jnp.bfloat16),
            jax.ShapeDtypeStruct((b, tq, d), jnp.bfloat16),
            jax.ShapeDtypeStruct((b, tq, PEER_ROWS), jnp.int32),
            jax.ShapeDtypeStruct((b, tq, PEER_ROWS), jnp.float32),
        ],
        scratch_shapes=[
            pltpu.VMEM((2 * PEER_HEADS, r, PEER_DHALF), jnp.bfloat16),
            pltpu.VMEM((PEER_ROWS, r), jnp.float32),
            pltpu.VMEM((PEER_ROWS, r), jnp.float32),
        ],
        compiler_params=_params(("parallel", "arbitrary")),
        name="peer_route",
    )(h, om, osw, wo, mod, n2, wq, keys)


def pack_table(tab):
    e, d = tab.shape
    bits = lax.bitcast_convert_type(tab.astype(jnp.bfloat16), jnp.uint16).astype(jnp.uint32)
    w = (bits[:, : d // 2] << 16) | bits[:, d // 2:]
    return w.reshape(e * N_CHUNK, LANES)


def _gather_rows(idx_ref, tab_ref, tile_ref, t):
    for j in range(PEER_ROWS):
        r = pl.multiple_of(idx_ref[t, j], N_CHUNK)
        tile_ref[pl.ds(j, N_CHUNK, stride=TILE_STRIDE), :] = tab_ref[pl.ds(r, N_CHUNK), :]


def _token_pipeline(gather, compute, tiles):
    last = TOK_BLK - 1
    for k in range(TOK_SET):
        gather(tiles[0].at[k], k)

    def step(t, cur, nxt):
        for k in range(TOK_SET):
            compute(cur.at[k], t + k)
            gather(nxt.at[k], jnp.minimum(t + TOK_SET + k, last))

    def body(i, carry):
        t = TOK_SET * i

        @pl.when(i % 2 == 0)
        def _():
            step(t, tiles[0], tiles[1])

        @pl.when(i % 2 == 1)
        def _():
            step(t, tiles[1], tiles[0])

        return carry

    lax.fori_loop(0, TOK_BLK // TOK_SET, body, 0)


def _chunk(tile_ref, c):
    return pltpu.bitcast(tile_ref[pl.ds(c * TILE_STRIDE, PEER_ROWS), :], jnp.bfloat16)


def _u_kernel(idx_ref, x_ref, g_ref, tab_ref, w_ref, *tiles):
    row = lax.broadcasted_iota(jnp.int32, (16, 2 * PEER_ROWS), 0) & 7

    def scores(tile_ref, t):
        x16 = x_ref[t]
        top = jnp.zeros((16, 2 * PEER_ROWS), jnp.float32)
        bot = jnp.zeros((16, 2 * PEER_ROWS), jnp.float32)
        for c in range(N_CHUNK):
            y = lax.dot_general(x16, _chunk(tile_ref, c), (((1,), (1,)), ((), ())),
                                preferred_element_type=jnp.float32)
            top = top + jnp.where(row == c, y, 0.0)
            bot = bot + jnp.where(row == c + N_CHUNK, y, 0.0)
        tot = top + pltpu.roll(bot, 1, axis=1)
        a = jnp.sum(tot, axis=0, keepdims=True)
        gelu = 0.5 * a * (1.0 + lax.erf(a * (2.0 ** -0.5)))
        w_ref[t] = g_ref[t] * gelu

    _token_pipeline(functools.partial(_gather_rows, idx_ref, tab_ref), scores, tiles)


def peer_scores(idx, x16, g, tab):
    n = idx.shape[0]
    tok = lambda *s: pl.BlockSpec((TOK_BLK,) + s, lambda i: (i,) + (0,) * len(s))
    return pl.pallas_call(
        _u_kernel,
        grid=(n // TOK_BLK,),
        in_specs=[
            pl.BlockSpec((TOK_BLK, PEER_ROWS), lambda i: (i, 0), memory_space=pltpu.SMEM),
            tok(16, LANES), tok(1, 2 * PEER_ROWS),
            pl.BlockSpec(tab.shape, lambda i: (0, 0), pipeline_mode=pl.Buffered(1)),
        ],
        out_specs=tok(1, 2 * PEER_ROWS),
        out_shape=jax.ShapeDtypeStruct((n, 1, 2 * PEER_ROWS), jnp.float32),
        scratch_shapes=[pltpu.VMEM((TOK_SET, N_CHUNK * TILE_STRIDE, LANES), jnp.uint32)] * 2,
        compiler_params=_params(("arbitrary",)),
        name="peer_scores",
    )(idx, x16, g, tab)


def _v_kernel(idx_ref, w_ref, h_ref, g2_ref, tab_ref, o_ref, *tiles):
    row = lax.broadcasted_iota(jnp.int32, (16, 2 * PEER_ROWS), 0)

    def combine(tile_ref, t):
        w = w_ref[t]
        w_hi = w.astype(jnp.bfloat16).astype(jnp.float32)
        w_lo = w - w_hi
        parts = (w_hi, pltpu.roll(w_hi, 2 * PEER_ROWS - 1, axis=1),
                 w_lo, pltpu.roll(w_lo, 2 * PEER_ROWS - 1, axis=1))
        acc = jnp.zeros((16, LANES), jnp.float32)
        for c in range(N_CHUNK):
            lhs = jnp.zeros((16, 2 * PEER_ROWS), jnp.float32)
            for k, part in enumerate(parts):
                lhs = jnp.where(row == c + N_CHUNK * k, part, lhs)
            acc = acc + jnp.dot(lhs.astype(jnp.bfloat16), _chunk(tile_ref, c),
                                preferred_element_type=jnp.float32)
        o_ref[t] = h_ref[t] + g2_ref[0, 0] * (acc[:8] + acc[8:])

    _token_pipeline(functools.partial(_gather_rows, idx_ref, tab_ref), combine, tiles)


def peer_combine(idx, w, h8, g2, tab, blocks_per_sample, ctx_blocks):
    n = idx.shape[0]
    tok = lambda *s: pl.BlockSpec((TOK_BLK,) + s, lambda i: (i,) + (0,) * len(s))
    g2_map = lambda i: (i // blocks_per_sample,
                        jnp.where(i % blocks_per_sample < ctx_blocks, 0, 1), 0, 0)
    return pl.pallas_call(
        _v_kernel,
        grid=(n // TOK_BLK,),
        in_specs=[
            pl.BlockSpec((TOK_BLK, PEER_ROWS), lambda i: (i, 0), memory_space=pltpu.SMEM),
            tok(1, 2 * PEER_ROWS), tok(8, LANES),
            pl.BlockSpec((1, 1, 8, LANES), g2_map),
            pl.BlockSpec(tab.shape, lambda i: (0, 0), pipeline_mode=pl.Buffered(1)),
        ],
        out_specs=tok(8, LANES),
        out_shape=jax.ShapeDtypeStruct((n, 8, LANES), jnp.float32),
        scratch_shapes=[pltpu.VMEM((TOK_SET, N_CHUNK * TILE_STRIDE, LANES), jnp.uint32)] * 2,
        compiler_params=_params(("arbitrary",)),
        name="peer_combine",
    )(idx, w, h8, g2, tab)


def _slots(w, width, offset=0):
    lead = w.shape[:-1]
    n = w.shape[-1] // width
    w = w.reshape(lead + (n, width))
    w = jnp.pad(w, [(0, 0)] * len(lead) + [(0, 0), (offset, LANES - width - offset)])
    return w.reshape(lead + (n * LANES,))


def _rope_tables(rot_dim, lane0, t, c):
    s = t - c
    q = rot_dim // 4
    pos = jnp.arange(s, dtype=jnp.float32)
    rows = jnp.floor(pos / GRID_W)
    cols = pos - rows * GRID_W
    inv = ROPE_BASE ** (-jnp.arange(q, dtype=jnp.float32) / q)
    ar = rows[:, None] * inv
    ac = cols[:, None] * inv
    zero = jnp.zeros_like(ar)
    cos = jnp.cos(jnp.concatenate([ar, ar, ac, ac], axis=-1))
    up = jnp.concatenate([-jnp.sin(ar), zero, -jnp.sin(ac), zero], axis=-1)
    dn = jnp.concatenate([zero, jnp.sin(ar), zero, jnp.sin(ac)], axis=-1)
    pad = lambda a, fill: jnp.pad(
        jnp.pad(a, ((0, 0), (lane0, LANES - lane0 - rot_dim)), constant_values=fill),
        ((c, 0), (0, 0)), constant_values=fill)
    cos = jnp.pad(jnp.pad(cos, ((0, 0), (lane0, LANES - lane0 - rot_dim)), constant_values=1.0),
                  ((c, 0), (0, 0)), constant_values=1.0)
    return jnp.stack([cos, pad(up, 0.0), pad(dn, 0.0)])


def kernel(x, c, ctx, c_ctx, ada_w, ada_b, norm1_g, norm2_g, w_in, mla_qa_g, mla_wuq, mla_kva_g, mla_wukv, mla_qn_g, mla_kn_g, swa_qn_g, swa_kn_g, swa_sink, w_out, peer_wq, peer_keys, peer_u, peer_v):
    b, s, d = x.shape
    nctx = ctx.shape[1]
    t = nctx + s
    depth = ada_w.shape[0]
    r = min(256, nctx)
    bf = jnp.bfloat16

    cond = jnp.zeros((16, d), jnp.float32).at[:b].set(c).at[b].set(c_ctx)
    mod_all = modulation(cond, ada_w, ada_b).reshape(depth, 16, N_MOD, d)
    rope_m = _rope_tables(MLA_ROPE, MLA_NOPE, t, nctx)
    rope_s = _rope_tables(SWA_DIM, 0, t, nctx)

    h = jnp.concatenate([ctx, x], axis=1)
    for l in range(depth):
        last = l == depth - 1
        with_ctx = not last
        mod = mod_all[l]
        wi = w_in[l]
        kv0 = Q_COLS + MLA_KV_RANK
        sk0 = kv0 + MLA_ROPE
        sv0 = sk0 + SWA_KV_HEADS * SWA_DIM
        sv = wi[:, sv0:].reshape(d, SWA_KV_HEADS, 1, SWA_DIM)
        win = jnp.concatenate([
            wi[:, :MLA_Q_RANK], wi[:, Q_COLS:kv0],
            _slots(wi[:, kv0:sk0], MLA_ROPE, MLA_NOPE),
            _slots(wi[:, MLA_Q_RANK:Q_COLS], SWA_DIM),
            _slots(wi[:, sk0:sv0], SWA_DIM),
            jnp.broadcast_to(sv, (d, SWA_KV_HEADS, 2, SWA_DIM)).reshape(d, SWA_KV_HEADS * LANES),
        ], axis=1).astype(bf)
        wuq = _slots(mla_wuq[l], MLA_QK).astype(bf)
        wukv = mla_wukv[l].reshape(MLA_KV_RANK, MLA_HEADS, MLA_NOPE + MLA_V)
        wuk = _slots(wukv[:, :, :MLA_NOPE].reshape(MLA_KV_RANK, -1), MLA_NOPE).astype(bf)
        wv = wukv[:, :, MLA_NOPE:].reshape(MLA_KV_RANK, MLA_HEADS // 2, 2, MLA_V)
        zero = jnp.zeros_like(wv[:, :, 0])
        wuv = jnp.stack([jnp.concatenate([wv[:, :, 0], zero], axis=-1),
                         jnp.concatenate([zero, wv[:, :, 1]], axis=-1)], axis=2)
        wuv = wuv.reshape(MLA_KV_RANK, MLA_HEADS * LANES).astype(bf)
        row = lambda g: g.reshape(1, -1)

        qm, km, vm, qs, ks, vlo, vhi = projections(
            h, mod, row(norm1_g[l]), win, row(mla_qa_g[l]), wuq, row(mla_kva_g[l]), wuk, wuv,
            row(_slots(mla_qn_g[l], MLA_QK)), row(_slots(mla_kn_g[l], MLA_QK)),
            row(_slots(swa_qn_g[l], SWA_DIM)), row(_slots(swa_kn_g[l], SWA_DIM)),
            rope_m, rope_s, r, nctx)
        om = mla_attention(qm, km, vm, r, nctx, with_ctx)
        osw = swa_attention(swa_sink[l], qs, ks, vlo, vhi, nctx, with_ctx)
        h1, xhi, xlo, idx, gate = route(
            h, om, osw, w_out[l].astype(bf), mod, row(norm2_g[l]), peer_wq[l].astype(bf),
            peer_keys[l].astype(bf), r, nctx, with_ctx)

        tl = h1.shape[1]
        n = b * tl
        idx = idx.reshape(n, PEER_ROWS)
        x16 = jnp.concatenate([xhi.reshape(n, 8, LANES), xlo.reshape(n, 8, LANES)], axis=1)
        gate = jnp.stack([jnp.zeros_like(gate), gate], axis=-1).reshape(n, 1, 2 * PEER_ROWS)
        w = peer_scores(idx, x16, gate, pack_table(peer_u[l]))
        g2 = jnp.stack([jnp.broadcast_to(mod[b, 5], (b, d)), mod[:b, 5]], axis=1)
        h = peer_combine(idx, w, h1.reshape(n, 8, LANES), g2.reshape(b, 2, 8, LANES),
                         pack_table(peer_v[l]), tl // TOK_BLK,
                         nctx // TOK_BLK if with_ctx else 0).reshape(b, tl, d)
    return h
```

```python
import functools
import jax
import jax.numpy as jnp
from jax import lax
from jax.experimental import pallas as pl
from jax.experimental.pallas import tpu as pltpu

LANES = 128
EPS = 1e-6
ROPE_BASE = 10000.0
GRID_W = 64
N_MOD = 6

MLA_HEADS = 8
MLA_NOPE = 64
MLA_ROPE = 32
MLA_QK = MLA_NOPE + MLA_ROPE
MLA_V = 64
MLA_Q_RANK = 384
MLA_KV_RANK = 256
SWA_HEADS = 8
SWA_KV_HEADS = 2
SWA_GROUP = SWA_HEADS // SWA_KV_HEADS
SWA_DIM = 64
WINDOW = 128
WIN_KEYS = 3 * WINDOW
Q_COLS = MLA_Q_RANK + SWA_HEADS * SWA_DIM

PEER_HEADS = 8
PEER_NKEYS = 128
PEER_DHALF = 128
PEER_TOPK = 16
PEER_ROWS = PEER_HEADS * PEER_TOPK
N_CHUNK = 4
TILE_STRIDE = PEER_ROWS + 1
TOPK_BLK = 1024
STAGE_STRIDE = PEER_NKEYS + 8
TOK_BLK = 128
TOK_SET = 8

OFF_QA = 0
OFF_KVA = OFF_QA + MLA_Q_RANK
OFF_KR = OFF_KVA + MLA_KV_RANK
OFF_SQ = OFF_KR + LANES
OFF_SK = OFF_SQ + SWA_HEADS * LANES
OFF_SV = OFF_SK + SWA_KV_HEADS * LANES
N_IN = OFF_SV + SWA_KV_HEADS * LANES

VMEM_LIMIT = 56 * 1024 * 1024
NEG_INF = float("-inf")


def _params(sem, vmem=VMEM_LIMIT):
    return pltpu.CompilerParams(dimension_semantics=sem, vmem_limit_bytes=vmem)


def _rms(x, g, n):
    ms = jnp.sum(x * x, axis=-1, keepdims=True) * (1.0 / n)
    return x * lax.rsqrt(ms + EPS) * g


def _rope(x, cos, sin_up, sin_dn, shift):
    return (x * cos + pltpu.roll(x, LANES - shift, axis=1) * sin_up
            + pltpu.roll(x, shift, axis=1) * sin_dn)


def _mod_kernel(c_ref, w_ref, b_ref, o_ref):
    c = c_ref[...]
    s = c * (1.0 / (1.0 + jnp.exp(-c)))
    o_ref[0] = jnp.dot(s, w_ref[0], preferred_element_type=jnp.float32,
                       precision=lax.Precision.HIGHEST) + b_ref[0]


def modulation(cond, ada_w, ada_b):
    nl, d, n6 = ada_w.shape
    tn = 1536
    return pl.pallas_call(
        _mod_kernel,
        grid=(nl, n6 // tn),
        in_specs=[
            pl.BlockSpec(cond.shape, lambda l, j: (0, 0)),
            pl.BlockSpec((1, d, tn), lambda l, j: (l, 0, j)),
            pl.BlockSpec((1, 1, tn), lambda l, j: (l, 0, j)),
        ],
        out_specs=pl.BlockSpec((1, cond.shape[0], tn), lambda l, j: (l, 0, j)),
        out_shape=jax.ShapeDtypeStruct((nl, cond.shape[0], n6), jnp.float32),
        compiler_params=_params(("arbitrary", "arbitrary")),
        name="modulation",
    )(cond, ada_w, ada_b.reshape(nl, 1, n6))


def _proj_kernel(h_ref, mod_ref, n1_ref, win_ref, qag_ref, wuq_ref, kvag_ref, wuk_ref, wuv_ref,
                 qn_ref, kn_ref, sqn_ref, skn_ref, rm_ref, rs_ref,
                 qm_ref, km_ref, vm_ref, qs_ref, ks_ref, vlo_ref, vhi_ref):
    h = h_ref[0]
    d = h.shape[-1]
    mod = mod_ref[0]
    a = _rms(h, n1_ref[...], d) * (1.0 + mod[1:2]) + mod[0:1]
    p = jnp.dot(a.astype(jnp.bfloat16), win_ref[...], preferred_element_type=jnp.float32)

    cm, sm_up, sm_dn = rm_ref[0], rm_ref[1], rm_ref[2]
    cs, ss_up, ss_dn = rs_ref[0], rs_ref[1], rs_ref[2]

    qa = _rms(p[:, OFF_QA:OFF_QA + MLA_Q_RANK], qag_ref[...], MLA_Q_RANK)
    q = jnp.dot(qa.astype(jnp.bfloat16), wuq_ref[...], preferred_element_type=jnp.float32)
    kva = _rms(p[:, OFF_KVA:OFF_KVA + MLA_KV_RANK], kvag_ref[...], MLA_KV_RANK).astype(jnp.bfloat16)
    kn = jnp.dot(kva, wuk_ref[...], preferred_element_type=jnp.float32)
    vm = jnp.dot(kva, wuv_ref[...], preferred_element_type=jnp.float32)
    kr = p[:, OFF_KR:OFF_KR + LANES]
    for hd in range(MLA_HEADS):
        sl = slice(hd * LANES, (hd + 1) * LANES)
        qh = _rope(_rms(q[:, sl], qn_ref[...], MLA_QK), cm, sm_up, sm_dn, MLA_ROPE // 4)
        qm_ref[0, hd] = (qh * (MLA_QK ** -0.5)).astype(jnp.bfloat16)
        kh = _rope(_rms(kn[:, sl] + kr, kn_ref[...], MLA_QK), cm, sm_up, sm_dn, MLA_ROPE // 4)
        km_ref[0, hd] = kh.astype(jnp.bfloat16)
        vm_ref[0, hd] = vm[:, sl].astype(jnp.bfloat16)
    for hd in range(SWA_HEADS):
        x = p[:, OFF_SQ + hd * LANES:OFF_SQ + (hd + 1) * LANES]
        qh = _rope(_rms(x, sqn_ref[...], SWA_DIM), cs, ss_up, ss_dn, SWA_DIM // 4)
        qs_ref[0, hd] = (qh * (SWA_DIM ** -0.5)).astype(jnp.bfloat16)
    lane = lax.broadcasted_iota(jnp.int32, (h.shape[0], LANES), 1)
    for g in range(SWA_KV_HEADS):
        x = p[:, OFF_SK + g * LANES:OFF_SK + (g + 1) * LANES]
        kh = _rope(_rms(x, skn_ref[...], SWA_DIM), cs, ss_up, ss_dn, SWA_DIM // 4)
        ks_ref[0, g] = kh.astype(jnp.bfloat16)
        v = p[:, OFF_SV + g * LANES:OFF_SV + (g + 1) * LANES]
        vlo_ref[0, g] = jnp.where(lane < SWA_DIM, v, 0.0).astype(jnp.bfloat16)
        vhi_ref[0, g] = jnp.where(lane >= SWA_DIM, v, 0.0).astype(jnp.bfloat16)


def projections(h, mod, n1, win, qag, wuq, kvag, wuk, wuv, qn, kn, sqn, skn, rope_m, rope_s, r, c):
    b, t, d = h.shape
    nctx = c // r
    full = lambda a: pl.BlockSpec(a.shape, lambda i, j: (0,) * a.ndim)
    head_out = lambda nh: pl.BlockSpec((1, nh, r, LANES), lambda i, j: (i, 0, j, 0))
    head_shape = lambda nh: jax.ShapeDtypeStruct((b, nh, t, LANES), jnp.bfloat16)
    return pl.pallas_call(
        _proj_kernel,
        grid=(b, t // r),
        in_specs=[
            pl.BlockSpec((1, r, d), lambda i, j: (i, j, 0)),
            pl.BlockSpec((1, N_MOD, d), lambda i, j: (jnp.where(j < nctx, b, i), 0, 0)),
            full(n1), full(win), full(qag), full(wuq), full(kvag), full(wuk), full(wuv),
            full(qn), full(kn), full(sqn), full(skn),
            pl.BlockSpec((3, r, LANES), lambda i, j: (0, j, 0)),
            pl.BlockSpec((3, r, LANES), lambda i, j: (0, j, 0)),
        ],
        out_specs=[head_out(MLA_HEADS), head_out(MLA_HEADS), head_out(MLA_HEADS),
                   head_out(SWA_HEADS), head_out(SWA_KV_HEADS), head_out(SWA_KV_HEADS),
                   head_out(SWA_KV_HEADS)],
        out_shape=[head_shape(MLA_HEADS), head_shape(MLA_HEADS), head_shape(MLA_HEADS),
                   head_shape(SWA_HEADS), head_shape(SWA_KV_HEADS), head_shape(SWA_KV_HEADS),
                   head_shape(SWA_KV_HEADS)],
        compiler_params=_params(("parallel", "arbitrary")),
        name="projections",
    )(h, mod, n1, win, qag, wuq, kvag, wuk, wuv, qn, kn, sqn, skn, rope_m, rope_s)


def _mla_kernel(q_ref, k_ref, v_ref, o_ref, *, nctx_blocks, c):
    qi = pl.program_id(2)

    def attend(nk):
        acc = None
        for i in range(2):
            q = q_ref[0, i]
            s = lax.dot_general(q, k_ref[0, i, :nk], (((1,), (1,)), ((), ())),
                                preferred_element_type=jnp.float32)
            m = jnp.max(s, axis=-1, keepdims=True)
            p = jnp.exp(s - m)
            l = jnp.sum(p, axis=-1, keepdims=True)
            o = jnp.dot(p.astype(jnp.bfloat16), v_ref[0, i, :nk], preferred_element_type=jnp.float32)
            o = o * (1.0 / l)
            acc = o if acc is None else acc + o
        o_ref[0] = acc.astype(o_ref.dtype)

    if nctx_blocks:
        @pl.when(qi < nctx_blocks)
        def _():
            attend(c)

        @pl.when(qi >= nctx_blocks)
        def _():
            attend(k_ref.shape[2])
    else:
        attend(k_ref.shape[2])


def mla_attention(qm, km, vm, tq, c, with_ctx):
    b, nh, t, _ = qm.shape
    off = 0 if with_ctx else c // tq
    nq = t // tq - off
    return pl.pallas_call(
        functools.partial(_mla_kernel, nctx_blocks=(c // tq if with_ctx else 0), c=c),
        grid=(b, nh // 2, nq),
        in_specs=[
            pl.BlockSpec((1, 2, tq, LANES), lambda i, hp, j: (i, hp, j + off, 0)),
            pl.BlockSpec((1, 2, t, LANES), lambda i, hp, j: (i, hp, 0, 0)),
            pl.BlockSpec((1, 2, t, LANES), lambda i, hp, j: (i, hp, 0, 0)),
        ],
        out_specs=pl.BlockSpec((1, tq, LANES), lambda i, hp, j: (i, j, hp)),
        out_shape=jax.ShapeDtypeStruct((b, nq * tq, nh // 2 * LANES), jnp.bfloat16),
        compiler_params=_params(("parallel", "arbitrary", "arbitrary")),
        name="mla_attention",
    )(qm, km, vm)


def _swa_kernel(sink_ref, q_ref, k_ref, vlo_ref, vhi_ref, o_ref, *, off, c):
    g = pl.program_id(1)
    qi = pl.program_id(2) + off
    t = k_ref.shape[2]
    rows = SWA_GROUP * WINDOW
    q = q_ref[0].reshape(rows, LANES)
    r_idx = lax.broadcasted_iota(jnp.int32, (rows, 1), 0)
    sink = jnp.zeros((rows, 1), jnp.float32)
    for i in range(SWA_GROUP):
        sink = jnp.where(r_idx // WINDOW == i, sink_ref[g * SWA_GROUP + i], sink)

    ws = pl.multiple_of(jnp.clip(qi * WINDOW - WINDOW, c, t - WIN_KEYS), WINDOW)
    nt = (((1,), (1,)), ((), ()))
    s_ctx = lax.dot_general(q, k_ref[0, 0, :c], nt, preferred_element_type=jnp.float32)
    s_loc = lax.dot_general(q, k_ref[0, 0, pl.ds(ws, WIN_KEYS)], nt, preferred_element_type=jnp.float32)
    qpos = qi * WINDOW + (r_idx % WINDOW)
    kpos = ws + lax.broadcasted_iota(jnp.int32, (1, WIN_KEYS), 1)
    reach = jnp.where(qi * WINDOW >= c, WINDOW, -1)
    s_loc = jnp.where(jnp.abs(qpos - kpos) <= reach, s_loc, NEG_INF)
    m = jnp.maximum(jnp.maximum(jnp.max(s_ctx, axis=-1, keepdims=True),
                                jnp.max(s_loc, axis=-1, keepdims=True)), sink)
    p_ctx = jnp.exp(s_ctx - m)
    p_loc = jnp.exp(s_loc - m)
    l = (jnp.sum(p_ctx, axis=-1, keepdims=True) + jnp.sum(p_loc, axis=-1, keepdims=True)
         + jnp.exp(sink - m))
    inv = 1.0 / l
    p_ctx = p_ctx.astype(jnp.bfloat16)
    p_loc = p_loc.astype(jnp.bfloat16)
    outs = []
    for pair in range(SWA_GROUP // 2):
        acc = None
        for i, v_ref in enumerate((vlo_ref, vhi_ref)):
            rs = slice((2 * pair + i) * WINDOW, (2 * pair + i + 1) * WINDOW)
            o = (jnp.dot(p_ctx[rs], v_ref[0, 0, :c], preferred_element_type=jnp.float32)
                 + jnp.dot(p_loc[rs], v_ref[0, 0, pl.ds(ws, WIN_KEYS)], preferred_element_type=jnp.float32))
            o = o * inv[rs]
            acc = o if acc is None else acc + o
        outs.append(acc)
    o_ref[0] = jnp.concatenate(outs, axis=-1).astype(o_ref.dtype)


def swa_attention(sink, qs, ks, vlo, vhi, c, with_ctx):
    b, nh, t, _ = qs.shape
    off = 0 if with_ctx else c // WINDOW
    nq = t // WINDOW - off
    kv_spec = pl.BlockSpec((1, 1, t, LANES), lambda i, g, j: (i, g, 0, 0))
    return pl.pallas_call(
        functools.partial(_swa_kernel, off=off, c=c),
        grid=(b, SWA_KV_HEADS, nq),
        in_specs=[
            pl.BlockSpec(memory_space=pltpu.SMEM),
            pl.BlockSpec((1, SWA_GROUP, WINDOW, LANES), lambda i, g, j: (i, g, j + off, 0)),
            kv_spec, kv_spec, kv_spec,
        ],
        out_specs=pl.BlockSpec((1, WINDOW, SWA_GROUP // 2 * LANES), lambda i, g, j: (i, j, g)),
        out_shape=jax.ShapeDtypeStruct((b, nq * WINDOW, SWA_HEADS // 2 * LANES), jnp.bfloat16),
        compiler_params=_params(("parallel", "arbitrary", "arbitrary")),
        name="swa_attention",
    )(sink, qs, ks, vlo, vhi)


def _tree(op, xs):
    xs = list(xs)
    while len(xs) > 1:
        xs = [op(xs[i], xs[i + 1]) for i in range(0, len(xs) - 1, 2)] + (xs[-1:] if len(xs) % 2 else [])
    return xs[0]


def _top16_sweeps(ids, *problems):
    n = len(ids)
    big = float(max(ids) + 1)

    def step(r, carry):
        for val_ref, out_v, out_i in problems:
            m = _tree(jnp.maximum, [val_ref[k] for k in range(n)])
            am = _tree(jnp.minimum, [jnp.where(val_ref[k] == m, float(ids[k]), big) for k in range(n)])
            for k in range(n):
                val_ref[k] = jnp.where(am == float(ids[k]), NEG_INF, val_ref[k])
            out_v[r] = m
            out_i[r] = am
        return carry

    lax.fori_loop(0, PEER_TOPK, step, 0)


PAIRS = [(a, b) for a in range(PEER_TOPK) for b in range(PEER_TOPK) if (a + 1) * (b + 1) <= PEER_TOPK]


def _route_kernel(h_ref, om_ref, os_ref, wo_ref, mod_ref, n2_ref, wq_ref,
                  h1_ref, xhi_ref, xlo_ref, q_ref):
    h = h_ref[0]
    d = h.shape[-1]
    mod = mod_ref[0]
    half = om_ref.shape[-1]
    mix = (jnp.dot(om_ref[0], wo_ref[:half], preferred_element_type=jnp.float32)
           + jnp.dot(os_ref[0], wo_ref[half:], preferred_element_type=jnp.float32))
    h1 = h + mod[2:3] * mix
    h1_ref[0] = h1
    x = _rms(h1, n2_ref[...], d) * (1.0 + mod[4:5]) + mod[3:4]
    xhi = x.astype(jnp.bfloat16)
    xhi_ref[0] = xhi
    xlo_ref[0] = (x - xhi.astype(jnp.float32)).astype(jnp.bfloat16)
    q = jnp.dot(xhi, wq_ref[...], preferred_element_type=jnp.float32)
    for k in range(2 * PEER_HEADS):
        q_ref[k] = q[:, k * PEER_DHALF:(k + 1) * PEER_DHALF].astype(jnp.bfloat16)


def _topk_kernel(q_ref, keys_ref, idx_ref, g_ref, stage, vals, sv, si, cand, cv, ci, out_i, out_g):
    groups = vals.shape[2]

    def head(hh, carry):
        for part in range(2):
            st = lax.dot_general(keys_ref[hh, part], q_ref[2 * hh + part], (((1,), (1,)), ((), ())),
                                 preferred_element_type=jnp.float32)
            for g in range(groups):
                stage[pl.ds(g * STAGE_STRIDE, PEER_NKEYS), :] = st[:, g * LANES:(g + 1) * LANES]
            for k in range(PEER_NKEYS):
                vals[part, k] = stage[pl.ds(k, groups, stride=STAGE_STRIDE), :]
        _top16_sweeps(list(range(PEER_NKEYS)), *[(vals.at[p], sv.at[p], si.at[p]) for p in range(2)])
        for n, (a, b) in enumerate(PAIRS):
            cand[n] = sv[0, a] + sv[1, b]
        _top16_sweeps([a * PEER_TOPK + b for a, b in PAIRS], (cand, cv, ci))
        top = cv[0]
        e = [jnp.exp(cv[r] - top) for r in range(PEER_TOPK)]
        inv = 1.0 / _tree(jnp.add, e)
        for r in range(PEER_TOPK):
            ia = jnp.floor(ci[r] * (1.0 / PEER_TOPK))
            ib = ci[r] - ia * PEER_TOPK
            i1 = jnp.zeros_like(ia)
            i2 = jnp.zeros_like(ia)
            for a in range(PEER_TOPK):
                i1 = jnp.where(ia == a, si[0, a], i1)
                i2 = jnp.where(ib == a, si[1, a], i2)
            out_i[hh * PEER_TOPK + r] = (i1 * PEER_NKEYS + i2) * N_CHUNK
            out_g[hh * PEER_TOPK + r] = e[r] * inv
        return carry

    lax.fori_loop(0, PEER_HEADS, head, 0)
    for g in range(groups):
        rows = slice(g * LANES, (g + 1) * LANES)
        idx_ref[rows, :] = out_i[:, g, :].T.astype(jnp.int32)
        g_ref[rows, :] = out_g[:, g, :].T


def peer_topk(q16, keys):
    n = q16.shape[1]
    tb = next(t for t in (TOPK_BLK, TOPK_BLK // 2, TOPK_BLK // 4, LANES) if n % t == 0)
    groups = tb // LANES
    blk = lambda dt: pltpu.VMEM((PEER_TOPK, groups, LANES), dt)
    return pl.pallas_call(
        _topk_kernel,
        grid=(n // tb,),
        in_specs=[
            pl.BlockSpec((2 * PEER_HEADS, tb, PEER_DHALF), lambda i: (0, i, 0)),
            pl.BlockSpec(keys.shape, lambda i: (0, 0, 0, 0)),
        ],
        out_specs=[pl.BlockSpec((tb, PEER_ROWS), lambda i: (i, 0))] * 2,
        out_shape=[jax.ShapeDtypeStruct((n, PEER_ROWS), jnp.int32),
                   jax.ShapeDtypeStruct((n, PEER_ROWS), jnp.float32)],
        scratch_shapes=[
            pltpu.VMEM((groups * STAGE_STRIDE, LANES), jnp.float32),
            pltpu.VMEM((2, PEER_NKEYS, groups, LANES), jnp.float32),
            pltpu.VMEM((2, PEER_TOPK, groups, LANES), jnp.float32),
            pltpu.VMEM((2, PEER_TOPK, groups, LANES), jnp.float32),
            pltpu.VMEM((len(PAIRS), groups, LANES), jnp.float32),
            blk(jnp.float32), blk(jnp.float32),
            pltpu.VMEM((PEER_ROWS, groups, LANES), jnp.float32),
            pltpu.VMEM((PEER_ROWS, groups, LANES), jnp.float32),
        ],
        compiler_params=_params(("arbitrary",)),
        name="peer_topk",
    )(q16, keys)


def route(h, om, osw, wo, mod, n2, wq, r, c, with_ctx):
    b, t, d = h.shape
    off = 0 if with_ctx else c // r
    nblk = t // r - off
    nctx = c // r
    tq = nblk * r
    full = lambda a: pl.BlockSpec(a.shape, lambda i, j: (0,) * a.ndim)
    row_spec = lambda w: pl.BlockSpec((1, r, w), lambda i, j: (i, j, 0))
    return pl.pallas_call(
        _route_kernel,
        grid=(b, nblk),
        in_specs=[
            pl.BlockSpec((1, r, d), lambda i, j: (i, j + off, 0)),
            row_spec(om.shape[-1]), row_spec(osw.shape[-1]),
            full(wo),
            pl.BlockSpec((1, N_MOD, d), lambda i, j: (jnp.where(j + off < nctx, b, i), 0, 0)),
            full(n2), full(wq),
        ],
        out_specs=[row_spec(d), row_spec(d), row_spec(d),
                   pl.BlockSpec((2 * PEER_HEADS, r, PEER_DHALF), lambda i, j: (0, i * nblk + j, 0))],
        out_shape=[
            jax.ShapeDtypeStruct((b, tq, d), jnp.float32),
            jax.ShapeDtypeStruct((b, tq, d), jnp.bfloat16),
            jax.ShapeDtypeStruct((b, tq, d), jnp.bfloat16),
            jax.ShapeDtypeStruct((2 * PEER_HEADS, b * tq, PEER_DHALF), jnp.bfloat16),
        ],
        compiler_params=_params(("parallel", "arbitrary")),
        name="peer_route",
    )(h, om, osw, wo, mod, n2, wq)


def pack_table(tab):
    e, d = tab.shape
    bits = lax.bitcast_convert_type(tab.astype(jnp.bfloat16), jnp.uint16).astype(jnp.uint32)
    w = (bits[:, : d // 2] << 16) | bits[:, d // 2:]
    return w.reshape(e * N_CHUNK, LANES)


def _gather_rows(idx_ref, tab_ref, tile_ref, t):
    for j in range(PEER_ROWS):
        r = pl.multiple_of(idx_ref[t, j], N_CHUNK)
        tile_ref[pl.ds(j, N_CHUNK, stride=TILE_STRIDE), :] = tab_ref[pl.ds(r, N_CHUNK), :]


def _token_pipeline(gather, compute, tiles):
    last = TOK_BLK - 1
    for k in range(TOK_SET):
        gather(tiles[0].at[k], k)

    def step(t, cur, nxt):
        for k in range(TOK_SET):
            compute(cur.at[k], t + k)
            gather(nxt.at[k], jnp.minimum(t + TOK_SET + k, last))

    def body(i, carry):
        t = TOK_SET * i

        @pl.when(i % 2 == 0)
        def _():
            step(t, tiles[0], tiles[1])

        @pl.when(i % 2 == 1)
        def _():
            step(t, tiles[1], tiles[0])

        return carry

    lax.fori_loop(0, TOK_BLK // TOK_SET, body, 0)


def _chunk(tile_ref, c):
    return pltpu.bitcast(tile_ref[pl.ds(c * TILE_STRIDE, PEER_ROWS), :], jnp.bfloat16)


def _u_kernel(idx_ref, x_ref, g_ref, tab_ref, w_ref, *tiles):
    row = lax.broadcasted_iota(jnp.int32, (16, 2 * PEER_ROWS), 0) & 7

    def scores(tile_ref, t):
        x16 = x_ref[t]
        top = jnp.zeros((16, 2 * PEER_ROWS), jnp.float32)
        bot = jnp.zeros((16, 2 * PEER_ROWS), jnp.float32)
        for c in range(N_CHUNK):
            y = lax.dot_general(x16, _chunk(tile_ref, c), (((1,), (1,)), ((), ())),
                                preferred_element_type=jnp.float32)
            top = top + jnp.where(row == c, y, 0.0)
            bot = bot + jnp.where(row == c + N_CHUNK, y, 0.0)
        tot = top + pltpu.roll(bot, 1, axis=1)
        a = jnp.sum(tot, axis=0, keepdims=True)
        gelu = 0.5 * a * (1.0 + lax.erf(a * (2.0 ** -0.5)))
        w_ref[t] = g_ref[t] * gelu

    _token_pipeline(functools.partial(_gather_rows, idx_ref, tab_ref), scores, tiles)


def peer_scores(idx, x16, g, tab):
    n = idx.shape[0]
    tok = lambda *s: pl.BlockSpec((TOK_BLK,) + s, lambda i: (i,) + (0,) * len(s))
    return pl.pallas_call(
        _u_kernel,
        grid=(n // TOK_BLK,),
        in_specs=[
            pl.BlockSpec((TOK_BLK, PEER_ROWS), lambda i: (i, 0), memory_space=pltpu.SMEM),
            tok(16, LANES), tok(1, 2 * PEER_ROWS),
            pl.BlockSpec(tab.shape, lambda i: (0, 0), pipeline_mode=pl.Buffered(1)),
        ],
        out_specs=tok(1, 2 * PEER_ROWS),
        out_shape=jax.ShapeDtypeStruct((n, 1, 2 * PEER_ROWS), jnp.float32),
        scratch_shapes=[pltpu.VMEM((TOK_SET, N_CHUNK * TILE_STRIDE, LANES), jnp.uint32)] * 2,
        compiler_params=_params(("arbitrary",)),
        name="peer_scores",
    )(idx, x16, g, tab)


def _v_kernel(idx_ref, w_ref, h_ref, g2_ref, tab_ref, o_ref, *tiles):
    row = lax.broadcasted_iota(jnp.int32, (16, 2 * PEER_ROWS), 0)

    def combine(tile_ref, t):
        w = w_ref[t]
        w_hi = w.astype(jnp.bfloat16).astype(jnp.float32)
        w_lo = w - w_hi
        parts = (w_hi, pltpu.roll(w_hi, 2 * PEER_ROWS - 1, axis=1),
                 w_lo, pltpu.roll(w_lo, 2 * PEER_ROWS - 1, axis=1))
        acc = jnp.zeros((16, LANES), jnp.float32)
        for c in range(N_CHUNK):
            lhs = jnp.zeros((16, 2 * PEER_ROWS), jnp.float32)
            for k, part in enumerate(parts):
                lhs = jnp.where(row == c + N_CHUNK * k, part, lhs)
            acc = acc + jnp.dot(lhs.astype(jnp.bfloat16), _chunk(tile_ref, c),
                                preferred_element_type=jnp.float32)
        o_ref[t] = h_ref[t] + g2_ref[0, 0] * (acc[:8] + acc[8:])

    _token_pipeline(functools.partial(_gather_rows, idx_ref, tab_ref), combine, tiles)


def peer_combine(idx, w, h8, g2, tab, blocks_per_sample, ctx_blocks):
    n = idx.shape[0]
    tok = lambda *s: pl.BlockSpec((TOK_BLK,) + s, lambda i: (i,) + (0,) * len(s))
    g2_map = lambda i: (i // blocks_per_sample,
                        jnp.where(i % blocks_per_sample < ctx_blocks, 0, 1), 0, 0)
    return pl.pallas_call(
        _v_kernel,
        grid=(n // TOK_BLK,),
        in_specs=[
            pl.BlockSpec((TOK_BLK, PEER_ROWS), lambda i: (i, 0), memory_space=pltpu.SMEM),
            tok(1, 2 * PEER_ROWS), tok(8, LANES),
            pl.BlockSpec((1, 1, 8, LANES), g2_map),
            pl.BlockSpec(tab.shape, lambda i: (0, 0), pipeline_mode=pl.Buffered(1)),
        ],
        out_specs=tok(8, LANES),
        out_shape=jax.ShapeDtypeStruct((n, 8, LANES), jnp.float32),
        scratch_shapes=[pltpu.VMEM((TOK_SET, N_CHUNK * TILE_STRIDE, LANES), jnp.uint32)] * 2,
        compiler_params=_params(("arbitrary",)),
        name="peer_combine",
    )(idx, w, h8, g2, tab)


def _slots(w, width, offset=0):
    lead = w.shape[:-1]
    n = w.shape[-1] // width
    w = w.reshape(lead + (n, width))
    w = jnp.pad(w, [(0, 0)] * len(lead) + [(0, 0), (offset, LANES - width - offset)])
    return w.reshape(lead + (n * LANES,))


def _rope_tables(rot_dim, lane0, t, c):
    s = t - c
    q = rot_dim // 4
    pos = jnp.arange(s, dtype=jnp.float32)
    rows = jnp.floor(pos / GRID_W)
    cols = pos - rows * GRID_W
    inv = ROPE_BASE ** (-jnp.arange(q, dtype=jnp.float32) / q)
    ar = rows[:, None] * inv
    ac = cols[:, None] * inv
    zero = jnp.zeros_like(ar)
    cos = jnp.cos(jnp.concatenate([ar, ar, ac, ac], axis=-1))
    up = jnp.concatenate([-jnp.sin(ar), zero, -jnp.sin(ac), zero], axis=-1)
    dn = jnp.concatenate([zero, jnp.sin(ar), zero, jnp.sin(ac)], axis=-1)
    pad = lambda a, fill: jnp.pad(
        jnp.pad(a, ((0, 0), (lane0, LANES - lane0 - rot_dim)), constant_values=fill),
        ((c, 0), (0, 0)), constant_values=fill)
    cos = jnp.pad(jnp.pad(cos, ((0, 0), (lane0, LANES - lane0 - rot_dim)), constant_values=1.0),
                  ((c, 0), (0, 0)), constant_values=1.0)
    return jnp.stack([cos, pad(up, 0.0), pad(dn, 0.0)])


def kernel(x, c, ctx, c_ctx, ada_w, ada_b, norm1_g, norm2_g, w_in, mla_qa_g, mla_wuq, mla_kva_g, mla_wukv, mla_qn_g, mla_kn_g, swa_qn_g, swa_kn_g, swa_sink, w_out, peer_wq, peer_keys, peer_u, peer_v):
    b, s, d = x.shape
    nctx = ctx.shape[1]
    t = nctx + s
    depth = ada_w.shape[0]
    r = min(256, nctx)
    bf = jnp.bfloat16

    cond = jnp.zeros((16, d), jnp.float32).at[:b].set(c).at[b].set(c_ctx)
    mod_all = modulation(cond, ada_w, ada_b).reshape(depth, 16, N_MOD, d)
    rope_m = _rope_tables(MLA_ROPE, MLA_NOPE, t, nctx)
    rope_s = _rope_tables(SWA_DIM, 0, t, nctx)

    h = jnp.concatenate([ctx, x], axis=1)
    for l in range(depth):
        last = l == depth - 1
        with_ctx = not last
        mod = mod_all[l]
        wi = w_in[l]
        kv0 = Q_COLS + MLA_KV_RANK
        sk0 = kv0 + MLA_ROPE
        sv0 = sk0 + SWA_KV_HEADS * SWA_DIM
        sv = wi[:, sv0:].reshape(d, SWA_KV_HEADS, 1, SWA_DIM)
        win = jnp.concatenate([
            wi[:, :MLA_Q_RANK], wi[:, Q_COLS:kv0],
            _slots(wi[:, kv0:sk0], MLA_ROPE, MLA_NOPE),
            _slots(wi[:, MLA_Q_RANK:Q_COLS], SWA_DIM),
            _slots(wi[:, sk0:sv0], SWA_DIM),
            jnp.broadcast_to(sv, (d, SWA_KV_HEADS, 2, SWA_DIM)).reshape(d, SWA_KV_HEADS * LANES),
        ], axis=1).astype(bf)
        wuq = _slots(mla_wuq[l], MLA_QK).astype(bf)
        wukv = mla_wukv[l].reshape(MLA_KV_RANK, MLA_HEADS, MLA_NOPE + MLA_V)
        wuk = _slots(wukv[:, :, :MLA_NOPE].reshape(MLA_KV_RANK, -1), MLA_NOPE).astype(bf)
        wv = wukv[:, :, MLA_NOPE:].reshape(MLA_KV_RANK, MLA_HEADS // 2, 2, MLA_V)
        zero = jnp.zeros_like(wv[:, :, 0])
        wuv = jnp.stack([jnp.concatenate([wv[:, :, 0], zero], axis=-1),
                         jnp.concatenate([zero, wv[:, :, 1]], axis=-1)], axis=2)
        wuv = wuv.reshape(MLA_KV_RANK, MLA_HEADS * LANES).astype(bf)
        row = lambda g: g.reshape(1, -1)

        qm, km, vm, qs, ks, vlo, vhi = projections(
            h, mod, row(norm1_g[l]), win, row(mla_qa_g[l]), wuq, row(mla_kva_g[l]), wuk, wuv,
            row(_slots(mla_qn_g[l], MLA_QK)), row(_slots(mla_kn_g[l], MLA_QK)),
            row(_slots(swa_qn_g[l], SWA_DIM)), row(_slots(swa_kn_g[l], SWA_DIM)),
            rope_m, rope_s, r, nctx)
        om = mla_attention(qm, km, vm, r, nctx, with_ctx)
        osw = swa_attention(swa_sink[l], qs, ks, vlo, vhi, nctx, with_ctx)
        h1, xhi, xlo, q16 = route(
            h, om, osw, w_out[l].astype(bf), mod, row(norm2_g[l]), peer_wq[l].astype(bf),
            r, nctx, with_ctx)
        idx, gate = peer_topk(q16, peer_keys[l].astype(bf))

        tl = h1.shape[1]
        n = b * tl
        x16 = jnp.concatenate([xhi.reshape(n, 8, LANES), xlo.reshape(n, 8, LANES)], axis=1)
        gate = jnp.stack([jnp.zeros_like(gate), gate], axis=-1).reshape(n, 1, 2 * PEER_ROWS)
        w = peer_scores(idx, x16, gate, pack_table(peer_u[l]))
        g2 = jnp.stack([jnp.broadcast_to(mod[b, 5], (b, d)), mod[:b, 5]], axis=1)
        h = peer_combine(idx, w, h1.reshape(n, 8, LANES), g2.reshape(b, 2, 8, LANES),
                         pack_table(peer_v[l]), tl // TOK_BLK,
                         nctx // TOK_BLK if with_ctx else 0).reshape(b, tl, d)
    return h
```

```python
import functools
import jax
import jax.numpy as jnp
from jax import lax
from jax.experimental import pallas as pl
from jax.experimental.pallas import tpu as pltpu

LANES = 128
EPS = 1e-6
ROPE_BASE = 10000.0
GRID_W = 64
N_MOD = 6

MLA_HEADS = 8
MLA_NOPE = 64
MLA_ROPE = 32
MLA_QK = MLA_NOPE + MLA_ROPE
MLA_V = 64
MLA_Q_RANK = 384
MLA_KV_RANK = 256
SWA_HEADS = 8
SWA_KV_HEADS = 2
SWA_GROUP = SWA_HEADS // SWA_KV_HEADS
SWA_DIM = 64
WINDOW = 128
WIN_KEYS = 3 * WINDOW
Q_COLS = MLA_Q_RANK + SWA_HEADS * SWA_DIM

PEER_HEADS = 8
PEER_NKEYS = 128
PEER_DHALF = 128
PEER_TOPK = 16
PEER_ROWS = PEER_HEADS * PEER_TOPK
N_CHUNK = 4
TILE_STRIDE = PEER_ROWS + 1
TOPK_BLK = 1024
STAGE_STRIDE = PEER_NKEYS + 8
TOK_BLK = 128
TOK_SET = 8

OFF_QA = 0
OFF_KVA = OFF_QA + MLA_Q_RANK
OFF_KR = OFF_KVA + MLA_KV_RANK
OFF_SQ = OFF_KR + LANES
OFF_SK = OFF_SQ + SWA_HEADS * LANES
OFF_SV = OFF_SK + SWA_KV_HEADS * LANES
N_IN = OFF_SV + SWA_KV_HEADS * LANES

VMEM_LIMIT = 56 * 1024 * 1024
NEG_INF = float("-inf")


def _params(sem, vmem=VMEM_LIMIT):
    return pltpu.CompilerParams(dimension_semantics=sem, vmem_limit_bytes=vmem)


def _rms(x, g, n):
    ms = jnp.sum(x * x, axis=-1, keepdims=True) * (1.0 / n)
    return x * lax.rsqrt(ms + EPS) * g


def _rope(x, cos, sin_up, sin_dn, shift):
    return (x * cos + pltpu.roll(x, LANES - shift, axis=1) * sin_up
            + pltpu.roll(x, shift, axis=1) * sin_dn)


def _mod_kernel(c_ref, w_ref, b_ref, o_ref):
    c = c_ref[...]
    s = c * (1.0 / (1.0 + jnp.exp(-c)))
    o_ref[0] = jnp.dot(s, w_ref[0], preferred_element_type=jnp.float32,
                       precision=lax.Precision.HIGHEST) + b_ref[0]


def modulation(cond, ada_w, ada_b):
    nl, d, n6 = ada_w.shape
    tn = 1536
    return pl.pallas_call(
        _mod_kernel,
        grid=(nl, n6 // tn),
        in_specs=[
            pl.BlockSpec(cond.shape, lambda l, j: (0, 0)),
            pl.BlockSpec((1, d, tn), lambda l, j: (l, 0, j)),
            pl.BlockSpec((1, 1, tn), lambda l, j: (l, 0, j)),
        ],
        out_specs=pl.BlockSpec((1, cond.shape[0], tn), lambda l, j: (l, 0, j)),
        out_shape=jax.ShapeDtypeStruct((nl, cond.shape[0], n6), jnp.float32),
        compiler_params=_params(("arbitrary", "arbitrary")),
        name="modulation",
    )(cond, ada_w, ada_b.reshape(nl, 1, n6))


def _proj_kernel(h_ref, mod_ref, n1_ref, win_ref, qag_ref, wuq_ref, kvag_ref, wuk_ref, wuv_ref,
                 qn_ref, kn_ref, sqn_ref, skn_ref, rm_ref, rs_ref,
                 qm_ref, km_ref, vm_ref, qs_ref, ks_ref, vlo_ref, vhi_ref):
    h = h_ref[0]
    d = h.shape[-1]
    mod = mod_ref[0]
    a = _rms(h, n1_ref[...], d) * (1.0 + mod[1:2]) + mod[0:1]
    p = jnp.dot(a.astype(jnp.bfloat16), win_ref[...], preferred_element_type=jnp.float32)

    cm, sm_up, sm_dn = rm_ref[0], rm_ref[1], rm_ref[2]
    cs, ss_up, ss_dn = rs_ref[0], rs_ref[1], rs_ref[2]

    qa = _rms(p[:, OFF_QA:OFF_QA + MLA_Q_RANK], qag_ref[...], MLA_Q_RANK)
    q = jnp.dot(qa.astype(jnp.bfloat16), wuq_ref[...], preferred_element_type=jnp.float32)
    kva = _rms(p[:, OFF_KVA:OFF_KVA + MLA_KV_RANK], kvag_ref[...], MLA_KV_RANK).astype(jnp.bfloat16)
    kn = jnp.dot(kva, wuk_ref[...], preferred_element_type=jnp.float32)
    vm = jnp.dot(kva, wuv_ref[...], preferred_element_type=jnp.float32)
    kr = p[:, OFF_KR:OFF_KR + LANES]
    for hd in range(MLA_HEADS):
        sl = slice(hd * LANES, (hd + 1) * LANES)
        qh = _rope(_rms(q[:, sl], qn_ref[...], MLA_QK), cm, sm_up, sm_dn, MLA_ROPE // 4)
        qm_ref[0, hd] = (qh * (MLA_QK ** -0.5)).astype(jnp.bfloat16)
        kh = _rope(_rms(kn[:, sl] + kr, kn_ref[...], MLA_QK), cm, sm_up, sm_dn, MLA_ROPE // 4)
        km_ref[0, hd] = kh.astype(jnp.bfloat16)
        vm_ref[0, hd] = vm[:, sl].astype(jnp.bfloat16)
    for hd in range(SWA_HEADS):
        x = p[:, OFF_SQ + hd * LANES:OFF_SQ + (hd + 1) * LANES]
        qh = _rope(_rms(x, sqn_ref[...], SWA_DIM), cs, ss_up, ss_dn, SWA_DIM // 4)
        qs_ref[0, hd] = (qh * (SWA_DIM ** -0.5)).astype(jnp.bfloat16)
    lane = lax.broadcasted_iota(jnp.int32, (h.shape[0], LANES), 1)
    for g in range(SWA_KV_HEADS):
        x = p[:, OFF_SK + g * LANES:OFF_SK + (g + 1) * LANES]
        kh = _rope(_rms(x, skn_ref[...], SWA_DIM), cs, ss_up, ss_dn, SWA_DIM // 4)
        ks_ref[0, g] = kh.astype(jnp.bfloat16)
        v = p[:, OFF_SV + g * LANES:OFF_SV + (g + 1) * LANES]
        vlo_ref[0, g] = jnp.where(lane < SWA_DIM, v, 0.0).astype(jnp.bfloat16)
        vhi_ref[0, g] = jnp.where(lane >= SWA_DIM, v, 0.0).astype(jnp.bfloat16)


def projections(h, mod, n1, win, qag, wuq, kvag, wuk, wuv, qn, kn, sqn, skn, rope_m, rope_s, r, c):
    b, t, d = h.shape
    nctx = c // r
    full = lambda a: pl.BlockSpec(a.shape, lambda i, j: (0,) * a.ndim)
    head_out = lambda nh: pl.BlockSpec((1, nh, r, LANES), lambda i, j: (i, 0, j, 0))
    head_shape = lambda nh: jax.ShapeDtypeStruct((b, nh, t, LANES), jnp.bfloat16)
    return pl.pallas_call(
        _proj_kernel,
        grid=(b, t // r),
        in_specs=[
            pl.BlockSpec((1, r, d), lambda i, j: (i, j, 0)),
            pl.BlockSpec((1, N_MOD, d), lambda i, j: (jnp.where(j < nctx, b, i), 0, 0)),
            full(n1), full(win), full(qag), full(wuq), full(kvag), full(wuk), full(wuv),
            full(qn), full(kn), full(sqn), full(skn),
            pl.BlockSpec((3, r, LANES), lambda i, j: (0, j, 0)),
            pl.BlockSpec((3, r, LANES), lambda i, j: (0, j, 0)),
        ],
        out_specs=[head_out(MLA_HEADS), head_out(MLA_HEADS), head_out(MLA_HEADS),
                   head_out(SWA_HEADS), head_out(SWA_KV_HEADS), head_out(SWA_KV_HEADS),
                   head_out(SWA_KV_HEADS)],
        out_shape=[head_shape(MLA_HEADS), head_shape(MLA_HEADS), head_shape(MLA_HEADS),
                   head_shape(SWA_HEADS), head_shape(SWA_KV_HEADS), head_shape(SWA_KV_HEADS),
                   head_shape(SWA_KV_HEADS)],
        compiler_params=_params(("parallel", "arbitrary")),
        name="projections",
    )(h, mod, n1, win, qag, wuq, kvag, wuk, wuv, qn, kn, sqn, skn, rope_m, rope_s)


def _mla_kernel(q_ref, k_ref, v_ref, o_ref, *, nctx_blocks, c):
    qi = pl.program_id(2)

    def attend(nk):
        acc = None
        for i in range(2):
            q = q_ref[0, i]
            s = lax.dot_general(q, k_ref[0, i, :nk], (((1,), (1,)), ((), ())),
                                preferred_element_type=jnp.float32)
            m = jnp.max(s, axis=-1, keepdims=True)
            p = jnp.exp(s - m)
            l = jnp.sum(p, axis=-1, keepdims=True)
            o = jnp.dot(p.astype(jnp.bfloat16), v_ref[0, i, :nk], preferred_element_type=jnp.float32)
            o = o * (1.0 / l)
            acc = o if acc is None else acc + o
        o_ref[0] = acc.astype(o_ref.dtype)

    if nctx_blocks:
        @pl.when(qi < nctx_blocks)
        def _():
            attend(c)

        @pl.when(qi >= nctx_blocks)
        def _():
            attend(k_ref.shape[2])
    else:
        attend(k_ref.shape[2])


def mla_attention(qm, km, vm, tq, c, with_ctx):
    b, nh, t, _ = qm.shape
    off = 0 if with_ctx else c // tq
    nq = t // tq - off
    return pl.pallas_call(
        functools.partial(_mla_kernel, nctx_blocks=(c // tq if with_ctx else 0), c=c),
        grid=(b, nh // 2, nq),
        in_specs=[
            pl.BlockSpec((1, 2, tq, LANES), lambda i, hp, j: (i, hp, j + off, 0)),
            pl.BlockSpec((1, 2, t, LANES), lambda i, hp, j: (i, hp, 0, 0)),
            pl.BlockSpec((1, 2, t, LANES), lambda i, hp, j: (i, hp, 0, 0)),
        ],
        out_specs=pl.BlockSpec((1, tq, LANES), lambda i, hp, j: (i, j, hp)),
        out_shape=jax.ShapeDtypeStruct((b, nq * tq, nh // 2 * LANES), jnp.bfloat16),
        compiler_params=_params(("parallel", "arbitrary", "arbitrary")),
        name="mla_attention",
    )(qm, km, vm)


def _swa_kernel(sink_ref, q_ref, k_ref, vlo_ref, vhi_ref, o_ref, *, off, c):
    g = pl.program_id(1)
    qi = pl.program_id(2) + off
    t = k_ref.shape[2]
    rows = SWA_GROUP * WINDOW
    q = q_ref[0].reshape(rows, LANES)
    r_idx = lax.broadcasted_iota(jnp.int32, (rows, 1), 0)
    sink = jnp.zeros((rows, 1), jnp.float32)
    for i in range(SWA_GROUP):
        sink = jnp.where(r_idx // WINDOW == i, sink_ref[g * SWA_GROUP + i], sink)

    ws = pl.multiple_of(jnp.clip(qi * WINDOW - WINDOW, c, t - WIN_KEYS), WINDOW)
    nt = (((1,), (1,)), ((), ()))
    s_ctx = lax.dot_general(q, k_ref[0, 0, :c], nt, preferred_element_type=jnp.float32)
    s_loc = lax.dot_general(q, k_ref[0, 0, pl.ds(ws, WIN_KEYS)], nt, preferred_element_type=jnp.float32)
    qpos = qi * WINDOW + (r_idx % WINDOW)
    kpos = ws + lax.broadcasted_iota(jnp.int32, (1, WIN_KEYS), 1)
    reach = jnp.where(qi * WINDOW >= c, WINDOW, -1)
    s_loc = jnp.where(jnp.abs(qpos - kpos) <= reach, s_loc, NEG_INF)
    m = jnp.maximum(jnp.maximum(jnp.max(s_ctx, axis=-1, keepdims=True),
                                jnp.max(s_loc, axis=-1, keepdims=True)), sink)
    p_ctx = jnp.exp(s_ctx - m)
    p_loc = jnp.exp(s_loc - m)
    l = (jnp.sum(p_ctx, axis=-1, keepdims=True) + jnp.sum(p_loc, axis=-1, keepdims=True)
         + jnp.exp(sink - m))
    inv = 1.0 / l
    p_ctx = p_ctx.astype(jnp.bfloat16)
    p_loc = p_loc.astype(jnp.bfloat16)
    outs = []
    for pair in range(SWA_GROUP // 2):
        acc = None
        for i, v_ref in enumerate((vlo_ref, vhi_ref)):
            rs = slice((2 * pair + i) * WINDOW, (2 * pair + i + 1) * WINDOW)
            o = (jnp.dot(p_ctx[rs], v_ref[0, 0, :c], preferred_element_type=jnp.float32)
                 + jnp.dot(p_loc[rs], v_ref[0, 0, pl.ds(ws, WIN_KEYS)], preferred_element_type=jnp.float32))
            o = o * inv[rs]
            acc = o if acc is None else acc + o
        outs.append(acc)
    o_ref[0] = jnp.concatenate(outs, axis=-1).astype(o_ref.dtype)


def swa_attention(sink, qs, ks, vlo, vhi, c, with_ctx):
    b, nh, t, _ = qs.shape
    off = 0 if with_ctx else c // WINDOW
    nq = t // WINDOW - off
    kv_spec = pl.BlockSpec((1, 1, t, LANES), lambda i, g, j: (i, g, 0, 0))
    return pl.pallas_call(
        functools.partial(_swa_kernel, off=off, c=c),
        grid=(b, SWA_KV_HEADS, nq),
        in_specs=[
            pl.BlockSpec(memory_space=pltpu.SMEM),
            pl.BlockSpec((1, SWA_GROUP, WINDOW, LANES), lambda i, g, j: (i, g, j + off, 0)),
            kv_spec, kv_spec, kv_spec,
        ],
        out_specs=pl.BlockSpec((1, WINDOW, SWA_GROUP // 2 * LANES), lambda i, g, j: (i, j, g)),
        out_shape=jax.ShapeDtypeStruct((b, nq * WINDOW, SWA_HEADS // 2 * LANES), jnp.bfloat16),
        compiler_params=_params(("parallel", "arbitrary", "arbitrary")),
        name="swa_attention",
    )(sink, qs, ks, vlo, vhi)


def _tree(op, xs):
    xs = list(xs)
    while len(xs) > 1:
        xs = [op(xs[i], xs[i + 1]) for i in range(0, len(xs) - 1, 2)] + (xs[-1:] if len(xs) % 2 else [])
    return xs[0]


def _top16_sweeps(ids, *problems):
    n = len(ids)
    big = float(max(ids) + 1)

    def step(r, carry):
        for val_ref, out_v, out_i in problems:
            m = _tree(jnp.maximum, [val_ref[k] for k in range(n)])
            am = _tree(jnp.minimum, [jnp.where(val_ref[k] == m, float(ids[k]), big) for k in range(n)])
            for k in range(n):
                val_ref[k] = jnp.where(am == float(ids[k]), NEG_INF, val_ref[k])
            out_v[r] = m
            out_i[r] = am
        return carry

    lax.fori_loop(0, PEER_TOPK, step, 0)


PAIRS = [(a, b) for a in range(PEER_TOPK) for b in range(PEER_TOPK) if (a + 1) * (b + 1) <= PEER_TOPK]


def _route_kernel(h_ref, om_ref, os_ref, wo_ref, mod_ref, n2_ref, wq_ref,
                  h1_ref, x2_ref, q_ref):
    h = h_ref[0]
    d = h.shape[-1]
    mod = mod_ref[0]
    half = om_ref.shape[-1]
    mix = (jnp.dot(om_ref[0], wo_ref[:half], preferred_element_type=jnp.float32)
           + jnp.dot(os_ref[0], wo_ref[half:], preferred_element_type=jnp.float32))
    h1 = h + mod[2:3] * mix
    h1_ref[0] = h1
    x = _rms(h1, n2_ref[...], d) * (1.0 + mod[4:5]) + mod[3:4]
    xhi = x.astype(jnp.bfloat16)
    x2_ref[0, :, :d] = xhi
    x2_ref[0, :, d:] = (x - xhi.astype(jnp.float32)).astype(jnp.bfloat16)
    q = jnp.dot(xhi, wq_ref[...], preferred_element_type=jnp.float32)
    for k in range(2 * PEER_HEADS):
        q_ref[k] = q[:, k * PEER_DHALF:(k + 1) * PEER_DHALF].astype(jnp.bfloat16)


def _topk_kernel(q_ref, keys_ref, idx_ref, g_ref, stage, vals, sv, si, cand, cv, ci, out_i, out_g):
    groups = vals.shape[2]

    def head(hh, carry):
        for part in range(2):
            st = lax.dot_general(keys_ref[hh, part], q_ref[2 * hh + part], (((1,), (1,)), ((), ())),
                                 preferred_element_type=jnp.float32)
            for g in range(groups):
                stage[pl.ds(g * STAGE_STRIDE, PEER_NKEYS), :] = st[:, g * LANES:(g + 1) * LANES]
            for k in range(PEER_NKEYS):
                vals[part, k] = stage[pl.ds(k, groups, stride=STAGE_STRIDE), :]
        _top16_sweeps(list(range(PEER_NKEYS)), *[(vals.at[p], sv.at[p], si.at[p]) for p in range(2)])
        for n, (a, b) in enumerate(PAIRS):
            cand[n] = sv[0, a] + sv[1, b]
        _top16_sweeps([a * PEER_TOPK + b for a, b in PAIRS], (cand, cv, ci))
        top = cv[0]
        e = [jnp.exp(cv[r] - top) for r in range(PEER_TOPK)]
        inv = 1.0 / _tree(jnp.add, e)
        for r in range(PEER_TOPK):
            ia = jnp.floor(ci[r] * (1.0 / PEER_TOPK))
            ib = ci[r] - ia * PEER_TOPK
            i1 = jnp.zeros_like(ia)
            i2 = jnp.zeros_like(ia)
            for a in range(PEER_TOPK):
                i1 = jnp.where(ia == a, si[0, a], i1)
                i2 = jnp.where(ib == a, si[1, a], i2)
            out_i[hh * PEER_TOPK + r] = (i1 * PEER_NKEYS + i2) * N_CHUNK
            out_g[hh * PEER_TOPK + r] = e[r] * inv
        return carry

    lax.fori_loop(0, PEER_HEADS, head, 0)
    for g in range(groups):
        rows = slice(g * LANES, (g + 1) * LANES)
        idx_ref[rows, :] = out_i[:, g, :].T.astype(jnp.int32)
        g_ref[rows, :] = out_g[:, g, :].T


def peer_topk(q16, keys):
    n = q16.shape[1]
    tb = next(t for t in (TOPK_BLK, TOPK_BLK // 2, TOPK_BLK // 4, LANES) if n % t == 0)
    groups = tb // LANES
    blk = lambda dt: pltpu.VMEM((PEER_TOPK, groups, LANES), dt)
    return pl.pallas_call(
        _topk_kernel,
        grid=(n // tb,),
        in_specs=[
            pl.BlockSpec((2 * PEER_HEADS, tb, PEER_DHALF), lambda i: (0, i, 0)),
            pl.BlockSpec(keys.shape, lambda i: (0, 0, 0, 0)),
        ],
        out_specs=[pl.BlockSpec((tb, PEER_ROWS), lambda i: (i, 0))] * 2,
        out_shape=[jax.ShapeDtypeStruct((n, PEER_ROWS), jnp.int32),
                   jax.ShapeDtypeStruct((n, PEER_ROWS), jnp.float32)],
        scratch_shapes=[
            pltpu.VMEM((groups * STAGE_STRIDE, LANES), jnp.float32),
            pltpu.VMEM((2, PEER_NKEYS, groups, LANES), jnp.float32),
            pltpu.VMEM((2, PEER_TOPK, groups, LANES), jnp.float32),
            pltpu.VMEM((2, PEER_TOPK, groups, LANES), jnp.float32),
            pltpu.VMEM((len(PAIRS), groups, LANES), jnp.float32),
            blk(jnp.float32), blk(jnp.float32),
            pltpu.VMEM((PEER_ROWS, groups, LANES), jnp.float32),
            pltpu.VMEM((PEER_ROWS, groups, LANES), jnp.float32),
        ],
        compiler_params=_params(("arbitrary",)),
        name="peer_topk",
    )(q16, keys)


def route(h, om, osw, wo, mod, n2, wq, r, c, with_ctx):
    b, t, d = h.shape
    off = 0 if with_ctx else c // r
    nblk = t // r - off
    nctx = c // r
    tq = nblk * r
    full = lambda a: pl.BlockSpec(a.shape, lambda i, j: (0,) * a.ndim)
    row_spec = lambda w: pl.BlockSpec((1, r, w), lambda i, j: (i, j, 0))
    return pl.pallas_call(
        _route_kernel,
        grid=(b, nblk),
        in_specs=[
            pl.BlockSpec((1, r, d), lambda i, j: (i, j + off, 0)),
            row_spec(om.shape[-1]), row_spec(osw.shape[-1]),
            full(wo),
            pl.BlockSpec((1, N_MOD, d), lambda i, j: (jnp.where(j + off < nctx, b, i), 0, 0)),
            full(n2), full(wq),
        ],
        out_specs=[row_spec(d), row_spec(2 * d),
                   pl.BlockSpec((2 * PEER_HEADS, r, PEER_DHALF), lambda i, j: (0, i * nblk + j, 0))],
        out_shape=[
            jax.ShapeDtypeStruct((b, tq, d), jnp.float32),
            jax.ShapeDtypeStruct((b, tq, 2 * d), jnp.bfloat16),
            jax.ShapeDtypeStruct((2 * PEER_HEADS, b * tq, PEER_DHALF), jnp.bfloat16),
        ],
        compiler_params=_params(("parallel", "arbitrary")),
        name="peer_route",
    )(h, om, osw, wo, mod, n2, wq)


def pack_table(tab):
    e, d = tab.shape
    bits = lax.bitcast_convert_type(tab.astype(jnp.bfloat16), jnp.uint16).astype(jnp.uint32)
    w = (bits[:, : d // 2] << 16) | bits[:, d // 2:]
    return w.reshape(e * N_CHUNK, LANES)


def _gather_rows(idx_ref, tab_ref, tile_ref, t):
    for j in range(PEER_ROWS):
        r = pl.multiple_of(idx_ref[t, j], N_CHUNK)
        tile_ref[pl.ds(j, N_CHUNK, stride=TILE_STRIDE), :] = tab_ref[pl.ds(r, N_CHUNK), :]


def _token_pipeline(gather, gather_next, compute, tiles):
    steps = TOK_BLK // TOK_SET

    @pl.when(pl.program_id(0) == 0)
    def _():
        for k in range(TOK_SET):
            gather(tiles[0].at[k], k)

    def step(t, cur, nxt, fill):
        for k in range(TOK_SET):
            compute(cur.at[k], t + k)
            fill(nxt.at[k], k)

    def body(i, carry):
        t = TOK_SET * i
        ahead = lambda tile, k: gather(tile, t + TOK_SET + k)

        @pl.when(i % 2 == 0)
        def _():
            step(t, tiles[0], tiles[1], ahead)

        @pl.when(i % 2 == 1)
        def _():
            step(t, tiles[1], tiles[0], ahead)

        return carry

    lax.fori_loop(0, steps - 1, body, 0)
    step(TOK_BLK - TOK_SET, tiles[1], tiles[0], gather_next)


def _chunk(tile_ref, c):
    return pltpu.bitcast(tile_ref[pl.ds(c * TILE_STRIDE, PEER_ROWS), :], jnp.bfloat16)


def _u_kernel(idx_ref, nxt_ref, x_ref, g_ref, tab_ref, w_ref, *tiles):
    row = lax.broadcasted_iota(jnp.int32, (16, 2 * PEER_ROWS), 0) & 7

    def scores(tile_ref, t):
        x16 = x_ref[t]
        top = jnp.zeros((16, 2 * PEER_ROWS), jnp.float32)
        bot = jnp.zeros((16, 2 * PEER_ROWS), jnp.float32)
        for c in range(N_CHUNK):
            y = lax.dot_general(x16, _chunk(tile_ref, c), (((1,), (1,)), ((), ())),
                                preferred_element_type=jnp.float32)
            top = top + jnp.where(row == c, y, 0.0)
            bot = bot + jnp.where(row == c + N_CHUNK, y, 0.0)
        tot = top + pltpu.roll(bot, 1, axis=1)
        a = jnp.sum(tot, axis=0, keepdims=True)
        gelu = 0.5 * a * (1.0 + lax.erf(a * (2.0 ** -0.5)))
        w_ref[t] = g_ref[t] * gelu

    _token_pipeline(functools.partial(_gather_rows, idx_ref, tab_ref),
                    functools.partial(_gather_rows, nxt_ref, tab_ref), scores, tiles)


def _next_tokens_spec(n):
    per_blk = TOK_BLK // TOK_SET
    return pl.BlockSpec((TOK_SET, PEER_ROWS),
                        lambda i: (jnp.minimum((i + 1) * per_blk, n // TOK_SET - 1), 0),
                        memory_space=pltpu.SMEM)


def peer_scores(idx, x16, g, tab):
    n = idx.shape[0]
    tok = lambda *s: pl.BlockSpec((TOK_BLK,) + s, lambda i: (i,) + (0,) * len(s))
    return pl.pallas_call(
        _u_kernel,
        grid=(n // TOK_BLK,),
        in_specs=[
            pl.BlockSpec((TOK_BLK, PEER_ROWS), lambda i: (i, 0), memory_space=pltpu.SMEM),
            _next_tokens_spec(n),
            tok(16, LANES), tok(1, 2 * PEER_ROWS),
            pl.BlockSpec(tab.shape, lambda i: (0, 0), pipeline_mode=pl.Buffered(1)),
        ],
        out_specs=tok(1, 2 * PEER_ROWS),
        out_shape=jax.ShapeDtypeStruct((n, 1, 2 * PEER_ROWS), jnp.float32),
        scratch_shapes=[pltpu.VMEM((TOK_SET, N_CHUNK * TILE_STRIDE, LANES), jnp.uint32)] * 2,
        compiler_params=_params(("arbitrary",)),
        name="peer_scores",
    )(idx, idx, x16, g, tab)


def _v_kernel(idx_ref, nxt_ref, w_ref, h_ref, g2_ref, tab_ref, o_ref, *tiles):
    row = lax.broadcasted_iota(jnp.int32, (16, 2 * PEER_ROWS), 0)

    def combine(tile_ref, t):
        w = w_ref[t]
        w_hi = w.astype(jnp.bfloat16).astype(jnp.float32)
        w_lo = w - w_hi
        parts = (w_hi, pltpu.roll(w_hi, 2 * PEER_ROWS - 1, axis=1),
                 w_lo, pltpu.roll(w_lo, 2 * PEER_ROWS - 1, axis=1))
        acc = jnp.zeros((16, LANES), jnp.float32)
        for c in range(N_CHUNK):
            lhs = jnp.zeros((16, 2 * PEER_ROWS), jnp.float32)
            for k, part in enumerate(parts):
                lhs = jnp.where(row == c + N_CHUNK * k, part, lhs)
            acc = acc + jnp.dot(lhs.astype(jnp.bfloat16), _chunk(tile_ref, c),
                                preferred_element_type=jnp.float32)
        o_ref[t] = h_ref[t] + g2_ref[0, 0] * (acc[:8] + acc[8:])

    _token_pipeline(functools.partial(_gather_rows, idx_ref, tab_ref),
                    functools.partial(_gather_rows, nxt_ref, tab_ref), combine, tiles)


def peer_combine(idx, w, h8, g2, tab, blocks_per_sample, ctx_blocks):
    n = idx.shape[0]
    tok = lambda *s: pl.BlockSpec((TOK_BLK,) + s, lambda i: (i,) + (0,) * len(s))
    g2_map = lambda i: (i // blocks_per_sample,
                        jnp.where(i % blocks_per_sample < ctx_blocks, 0, 1), 0, 0)
    return pl.pallas_call(
        _v_kernel,
        grid=(n // TOK_BLK,),
        in_specs=[
            pl.BlockSpec((TOK_BLK, PEER_ROWS), lambda i: (i, 0), memory_space=pltpu.SMEM),
            _next_tokens_spec(n),
            tok(1, 2 * PEER_ROWS), tok(8, LANES),
            pl.BlockSpec((1, 1, 8, LANES), g2_map),
            pl.BlockSpec(tab.shape, lambda i: (0, 0), pipeline_mode=pl.Buffered(1)),
        ],
        out_specs=tok(8, LANES),
        out_shape=jax.ShapeDtypeStruct((n, 8, LANES), jnp.float32),
        scratch_shapes=[pltpu.VMEM((TOK_SET, N_CHUNK * TILE_STRIDE, LANES), jnp.uint32)] * 2,
        compiler_params=_params(("arbitrary",)),
        name="peer_combine",
    )(idx, idx, w, h8, g2, tab)


def _slots(w, width, offset=0):
    lead = w.shape[:-1]
    n = w.shape[-1] // width
    w = w.reshape(lead + (n, width))
    w = jnp.pad(w, [(0, 0)] * len(lead) + [(0, 0), (offset, LANES - width - offset)])
    return w.reshape(lead + (n * LANES,))


def _rope_tables(rot_dim, lane0, t, c):
    s = t - c
    q = rot_dim // 4
    pos = jnp.arange(s, dtype=jnp.float32)
    rows = jnp.floor(pos / GRID_W)
    cols = pos - rows * GRID_W
    inv = ROPE_BASE ** (-jnp.arange(q, dtype=jnp.float32) / q)
    ar = rows[:, None] * inv
    ac = cols[:, None] * inv
    zero = jnp.zeros_like(ar)
    cos = jnp.cos(jnp.concatenate([ar, ar, ac, ac], axis=-1))
    up = jnp.concatenate([-jnp.sin(ar), zero, -jnp.sin(ac), zero], axis=-1)
    dn = jnp.concatenate([zero, jnp.sin(ar), zero, jnp.sin(ac)], axis=-1)
    pad = lambda a, fill: jnp.pad(
        jnp.pad(a, ((0, 0), (lane0, LANES - lane0 - rot_dim)), constant_values=fill),
        ((c, 0), (0, 0)), constant_values=fill)
    cos = jnp.pad(jnp.pad(cos, ((0, 0), (lane0, LANES - lane0 - rot_dim)), constant_values=1.0),
                  ((c, 0), (0, 0)), constant_values=1.0)
    return jnp.stack([cos, pad(up, 0.0), pad(dn, 0.0)])


def kernel(x, c, ctx, c_ctx, ada_w, ada_b, norm1_g, norm2_g, w_in, mla_qa_g, mla_wuq, mla_kva_g, mla_wukv, mla_qn_g, mla_kn_g, swa_qn_g, swa_kn_g, swa_sink, w_out, peer_wq, peer_keys, peer_u, peer_v):
    b, s, d = x.shape
    nctx = ctx.shape[1]
    t = nctx + s
    depth = ada_w.shape[0]
    r = min(256, nctx)
    bf = jnp.bfloat16

    cond = jnp.zeros((16, d), jnp.float32).at[:b].set(c).at[b].set(c_ctx)
    mod_all = modulation(cond, ada_w, ada_b).reshape(depth, 16, N_MOD, d)
    rope_m = _rope_tables(MLA_ROPE, MLA_NOPE, t, nctx)
    rope_s = _rope_tables(SWA_DIM, 0, t, nctx)

    h = jnp.concatenate([ctx, x], axis=1)
    for l in range(depth):
        last = l == depth - 1
        with_ctx = not last
        mod = mod_all[l]
        wi = w_in[l]
        kv0 = Q_COLS + MLA_KV_RANK
        sk0 = kv0 + MLA_ROPE
        sv0 = sk0 + SWA_KV_HEADS * SWA_DIM
        sv = wi[:, sv0:].reshape(d, SWA_KV_HEADS, 1, SWA_DIM)
        win = jnp.concatenate([
            wi[:, :MLA_Q_RANK], wi[:, Q_COLS:kv0],
            _slots(wi[:, kv0:sk0], MLA_ROPE, MLA_NOPE),
            _slots(wi[:, MLA_Q_RANK:Q_COLS], SWA_DIM),
            _slots(wi[:, sk0:sv0], SWA_DIM),
            jnp.broadcast_to(sv, (d, SWA_KV_HEADS, 2, SWA_DIM)).reshape(d, SWA_KV_HEADS * LANES),
        ], axis=1).astype(bf)
        wuq = _slots(mla_wuq[l], MLA_QK).astype(bf)
        wukv = mla_wukv[l].reshape(MLA_KV_RANK, MLA_HEADS, MLA_NOPE + MLA_V)
        wuk = _slots(wukv[:, :, :MLA_NOPE].reshape(MLA_KV_RANK, -1), MLA_NOPE).astype(bf)
        wv = wukv[:, :, MLA_NOPE:].reshape(MLA_KV_RANK, MLA_HEADS // 2, 2, MLA_V)
        zero = jnp.zeros_like(wv[:, :, 0])
        wuv = jnp.stack([jnp.concatenate([wv[:, :, 0], zero], axis=-1),
                         jnp.concatenate([zero, wv[:, :, 1]], axis=-1)], axis=2)
        wuv = wuv.reshape(MLA_KV_RANK, MLA_HEADS * LANES).astype(bf)
        row = lambda g: g.reshape(1, -1)

        qm, km, vm, qs, ks, vlo, vhi = projections(
            h, mod, row(norm1_g[l]), win, row(mla_qa_g[l]), wuq, row(mla_kva_g[l]), wuk, wuv,
            row(_slots(mla_qn_g[l], MLA_QK)), row(_slots(mla_kn_g[l], MLA_QK)),
            row(_slots(swa_qn_g[l], SWA_DIM)), row(_slots(swa_kn_g[l], SWA_DIM)),
            rope_m, rope_s, r, nctx)
        om = mla_attention(qm, km, vm, r, nctx, with_ctx)
        osw = swa_attention(swa_sink[l], qs, ks, vlo, vhi, nctx, with_ctx)
        h1, x2, q16 = route(
            h, om, osw, w_out[l].astype(bf), mod, row(norm2_g[l]), peer_wq[l].astype(bf),
            r, nctx, with_ctx)
        idx, gate = peer_topk(q16, peer_keys[l].astype(bf))

        tl = h1.shape[1]
        n = b * tl
        x16 = x2.reshape(n, 16, LANES)
        gate = jnp.stack([jnp.zeros_like(gate), gate], axis=-1).reshape(n, 1, 2 * PEER_ROWS)
        w = peer_scores(idx, x16, gate, pack_table(peer_u[l]))
        g2 = jnp.stack([jnp.broadcast_to(mod[b, 5], (b, d)), mod[:b, 5]], axis=1)
        h = peer_combine(idx, w, h1.reshape(n, 8, LANES), g2.reshape(b, 2, 8, LANES),
                         pack_table(peer_v[l]), tl // TOK_BLK,
                         nctx // TOK_BLK if with_ctx else 0).reshape(b, tl, d)
    return h
```

```python
import functools
import jax
import jax.numpy as jnp
from jax import lax
from jax.experimental import pallas as pl
from jax.experimental.pallas import tpu as pltpu

LANES = 128
EPS = 1e-6
ROPE_BASE = 10000.0
GRID_W = 64
N_MOD = 6

MLA_HEADS = 8
MLA_NOPE = 64
MLA_ROPE = 32
MLA_QK = MLA_NOPE + MLA_ROPE
MLA_V = 64
MLA_Q_RANK = 384
MLA_KV_RANK = 256
SWA_HEADS = 8
SWA_KV_HEADS = 2
SWA_GROUP = SWA_HEADS // SWA_KV_HEADS
SWA_DIM = 64
WINDOW = 128
WIN_KEYS = 3 * WINDOW
Q_COLS = MLA_Q_RANK + SWA_HEADS * SWA_DIM

PEER_HEADS = 8
PEER_NKEYS = 128
PEER_DHALF = 128
PEER_TOPK = 16
PEER_ROWS = PEER_HEADS * PEER_TOPK
N_CHUNK = 4
TILE_STRIDE = PEER_ROWS + 1
TOPK_BLK = 1024
STAGE_STRIDE = PEER_NKEYS + 8
TOK_BLK = 128
TOK_SET = 8

OFF_QA = 0
OFF_KVA = OFF_QA + MLA_Q_RANK
OFF_KR = OFF_KVA + MLA_KV_RANK
OFF_SQ = OFF_KR + LANES
OFF_SK = OFF_SQ + SWA_HEADS * LANES
OFF_SV = OFF_SK + SWA_KV_HEADS * LANES
N_IN = OFF_SV + SWA_KV_HEADS * LANES

VMEM_LIMIT = 56 * 1024 * 1024
NEG_INF = float("-inf")
LOG2_E = 1.4426950408889634


def _params(sem, vmem=VMEM_LIMIT):
    return pltpu.CompilerParams(dimension_semantics=sem, vmem_limit_bytes=vmem)


def _rms(x, g, n):
    ms = jnp.sum(x * x, axis=-1, keepdims=True) * (1.0 / n)
    return x * lax.rsqrt(ms + EPS) * g


def _rope(x, cos, sin_up, sin_dn, shift):
    return (x * cos + pltpu.roll(x, LANES - shift, axis=1) * sin_up
            + pltpu.roll(x, shift, axis=1) * sin_dn)


def _mod_kernel(c_ref, w_ref, b_ref, o_ref):
    c = c_ref[...]
    s = c * (1.0 / (1.0 + jnp.exp(-c)))
    o_ref[0] = jnp.dot(s, w_ref[0], preferred_element_type=jnp.float32,
                       precision=lax.Precision.HIGHEST) + b_ref[0]


def modulation(cond, ada_w, ada_b):
    nl, d, n6 = ada_w.shape
    tn = 1536
    return pl.pallas_call(
        _mod_kernel,
        grid=(nl, n6 // tn),
        in_specs=[
            pl.BlockSpec(cond.shape, lambda l, j: (0, 0)),
            pl.BlockSpec((1, d, tn), lambda l, j: (l, 0, j)),
            pl.BlockSpec((1, 1, tn), lambda l, j: (l, 0, j)),
        ],
        out_specs=pl.BlockSpec((1, cond.shape[0], tn), lambda l, j: (l, 0, j)),
        out_shape=jax.ShapeDtypeStruct((nl, cond.shape[0], n6), jnp.float32),
        compiler_params=_params(("arbitrary", "arbitrary")),
        name="modulation",
    )(cond, ada_w, ada_b.reshape(nl, 1, n6))


def _proj_kernel(h_ref, mod_ref, n1_ref, win_ref, qag_ref, wuq_ref, kvag_ref, wuk_ref, wuv_ref,
                 qn_ref, kn_ref, sqn_ref, skn_ref, rm_ref, rs_ref,
                 qm_ref, km_ref, vm_ref, qs_ref, ks_ref, vlo_ref, vhi_ref):
    h = h_ref[0]
    d = h.shape[-1]
    mod = mod_ref[0]
    a = _rms(h, n1_ref[...], d) * (1.0 + mod[1:2]) + mod[0:1]
    p = jnp.dot(a.astype(jnp.bfloat16), win_ref[...], preferred_element_type=jnp.float32)

    cm, sm_up, sm_dn = rm_ref[0], rm_ref[1], rm_ref[2]
    cs, ss_up, ss_dn = rs_ref[0], rs_ref[1], rs_ref[2]

    qa = _rms(p[:, OFF_QA:OFF_QA + MLA_Q_RANK], qag_ref[...], MLA_Q_RANK)
    q = jnp.dot(qa.astype(jnp.bfloat16), wuq_ref[...], preferred_element_type=jnp.float32)
    kva = _rms(p[:, OFF_KVA:OFF_KVA + MLA_KV_RANK], kvag_ref[...], MLA_KV_RANK).astype(jnp.bfloat16)
    kn = jnp.dot(kva, wuk_ref[...], preferred_element_type=jnp.float32)
    vm = jnp.dot(kva, wuv_ref[...], preferred_element_type=jnp.float32)
    kr = p[:, OFF_KR:OFF_KR + LANES]
    for hd in range(MLA_HEADS):
        sl = slice(hd * LANES, (hd + 1) * LANES)
        qh = _rope(_rms(q[:, sl], qn_ref[...], MLA_QK), cm, sm_up, sm_dn, MLA_ROPE // 4)
        qm_ref[0, hd] = (qh * (MLA_QK ** -0.5 * LOG2_E)).astype(jnp.bfloat16)
        kh = _rope(_rms(kn[:, sl] + kr, kn_ref[...], MLA_QK), cm, sm_up, sm_dn, MLA_ROPE // 4)
        km_ref[0, hd] = kh.astype(jnp.bfloat16)
        vm_ref[0, hd] = vm[:, sl].astype(jnp.bfloat16)
    for hd in range(SWA_HEADS):
        x = p[:, OFF_SQ + hd * LANES:OFF_SQ + (hd + 1) * LANES]
        qh = _rope(_rms(x, sqn_ref[...], SWA_DIM), cs, ss_up, ss_dn, SWA_DIM // 4)
        qs_ref[0, hd] = (qh * (SWA_DIM ** -0.5)).astype(jnp.bfloat16)
    lane = lax.broadcasted_iota(jnp.int32, (h.shape[0], LANES), 1)
    for g in range(SWA_KV_HEADS):
        x = p[:, OFF_SK + g * LANES:OFF_SK + (g + 1) * LANES]
        kh = _rope(_rms(x, skn_ref[...], SWA_DIM), cs, ss_up, ss_dn, SWA_DIM // 4)
        ks_ref[0, g] = kh.astype(jnp.bfloat16)
        v = p[:, OFF_SV + g * LANES:OFF_SV + (g + 1) * LANES]
        vlo_ref[0, g] = jnp.where(lane < SWA_DIM, v, 0.0).astype(jnp.bfloat16)
        vhi_ref[0, g] = jnp.where(lane >= SWA_DIM, v, 0.0).astype(jnp.bfloat16)


def projections(h, mod, n1, win, qag, wuq, kvag, wuk, wuv, qn, kn, sqn, skn, rope_m, rope_s, r, c):
    b, t, d = h.shape
    nctx = c // r
    full = lambda a: pl.BlockSpec(a.shape, lambda i, j: (0,) * a.ndim)
    head_out = lambda nh: pl.BlockSpec((1, nh, r, LANES), lambda i, j: (i, 0, j, 0))
    head_shape = lambda nh: jax.ShapeDtypeStruct((b, nh, t, LANES), jnp.bfloat16)
    return pl.pallas_call(
        _proj_kernel,
        grid=(b, t // r),
        in_specs=[
            pl.BlockSpec((1, r, d), lambda i, j: (i, j, 0)),
            pl.BlockSpec((1, N_MOD, d), lambda i, j: (jnp.where(j < nctx, b, i), 0, 0)),
            full(n1), full(win), full(qag), full(wuq), full(kvag), full(wuk), full(wuv),
            full(qn), full(kn), full(sqn), full(skn),
            pl.BlockSpec((3, r, LANES), lambda i, j: (0, j, 0)),
            pl.BlockSpec((3, r, LANES), lambda i, j: (0, j, 0)),
        ],
        out_specs=[head_out(MLA_HEADS), head_out(MLA_HEADS), head_out(MLA_HEADS),
                   head_out(SWA_HEADS), head_out(SWA_KV_HEADS), head_out(SWA_KV_HEADS),
                   head_out(SWA_KV_HEADS)],
        out_shape=[head_shape(MLA_HEADS), head_shape(MLA_HEADS), head_shape(MLA_HEADS),
                   head_shape(SWA_HEADS), head_shape(SWA_KV_HEADS), head_shape(SWA_KV_HEADS),
                   head_shape(SWA_KV_HEADS)],
        compiler_params=_params(("parallel", "arbitrary")),
        name="projections",
    )(h, mod, n1, win, qag, wuq, kvag, wuk, wuv, qn, kn, sqn, skn, rope_m, rope_s)


def _mla_kernel(q_ref, k_ref, v_ref, o_ref, *, nctx_blocks, c):
    qi = pl.program_id(2)

    def attend(nk):
        acc = None
        for i in range(2):
            q = q_ref[0, i]
            s = lax.dot_general(q, k_ref[0, i, :nk], (((1,), (1,)), ((), ())),
                                preferred_element_type=jnp.float32)
            m = jnp.max(s, axis=-1, keepdims=True)
            p = jnp.exp2(s - m)
            l = jnp.sum(p, axis=-1, keepdims=True)
            o = jnp.dot(p.astype(jnp.bfloat16), v_ref[0, i, :nk], preferred_element_type=jnp.float32)
            o = o * (1.0 / l)
            acc = o if acc is None else acc + o
        o_ref[0] = acc.astype(o_ref.dtype)

    if nctx_blocks:
        @pl.when(qi < nctx_blocks)
        def _():
            attend(c)

        @pl.when(qi >= nctx_blocks)
        def _():
            attend(k_ref.shape[2])
    else:
        attend(k_ref.shape[2])


def mla_attention(qm, km, vm, tq, c, with_ctx):
    b, nh, t, _ = qm.shape
    off = 0 if with_ctx else c // tq
    nq = t // tq - off
    return pl.pallas_call(
        functools.partial(_mla_kernel, nctx_blocks=(c // tq if with_ctx else 0), c=c),
        grid=(b, nh // 2, nq),
        in_specs=[
            pl.BlockSpec((1, 2, tq, LANES), lambda i, hp, j: (i, hp, j + off, 0)),
            pl.BlockSpec((1, 2, t, LANES), lambda i, hp, j: (i, hp, 0, 0)),
            pl.BlockSpec((1, 2, t, LANES), lambda i, hp, j: (i, hp, 0, 0)),
        ],
        out_specs=pl.BlockSpec((1, tq, LANES), lambda i, hp, j: (i, j, hp)),
        out_shape=jax.ShapeDtypeStruct((b, nq * tq, nh // 2 * LANES), jnp.bfloat16),
        compiler_params=_params(("parallel", "arbitrary", "arbitrary")),
        name="mla_attention",
    )(qm, km, vm)


def _swa_kernel(sink_ref, q_ref, k_ref, vlo_ref, vhi_ref, o_ref, *, off, c):
    g = pl.program_id(1)
    qi = pl.program_id(2) + off
    t = k_ref.shape[2]
    rows = SWA_GROUP * WINDOW
    q = q_ref[0].reshape(rows, LANES)
    r_idx = lax.broadcasted_iota(jnp.int32, (rows, 1), 0)
    sink = jnp.zeros((rows, 1), jnp.float32)
    for i in range(SWA_GROUP):
        sink = jnp.where(r_idx // WINDOW == i, sink_ref[g * SWA_GROUP + i], sink)

    ws = pl.multiple_of(jnp.clip(qi * WINDOW - WINDOW, c, t - WIN_KEYS), WINDOW)
    nt = (((1,), (1,)), ((), ()))
    s_ctx = lax.dot_general(q, k_ref[0, 0, :c], nt, preferred_element_type=jnp.float32)
    s_loc = lax.dot_general(q, k_ref[0, 0, pl.ds(ws, WIN_KEYS)], nt, preferred_element_type=jnp.float32)
    qpos = qi * WINDOW + (r_idx % WINDOW)
    kpos = ws + lax.broadcasted_iota(jnp.int32, (1, WIN_KEYS), 1)
    reach = jnp.where(qi * WINDOW >= c, WINDOW, -1)
    s_loc = jnp.where(jnp.abs(qpos - kpos) <= reach, s_loc, NEG_INF)
    m = jnp.maximum(jnp.maximum(jnp.max(s_ctx, axis=-1, keepdims=True),
                                jnp.max(s_loc, axis=-1, keepdims=True)), sink)
    p_ctx = jnp.exp(s_ctx - m)
    p_loc = jnp.exp(s_loc - m)
    l = (jnp.sum(p_ctx, axis=-1, keepdims=True) + jnp.sum(p_loc, axis=-1, keepdims=True)
         + jnp.exp(sink - m))
    inv = 1.0 / l
    p_ctx = p_ctx.astype(jnp.bfloat16)
    p_loc = p_loc.astype(jnp.bfloat16)
    outs = []
    for pair in range(SWA_GROUP // 2):
        acc = None
        for i, v_ref in enumerate((vlo_ref, vhi_ref)):
            rs = slice((2 * pair + i) * WINDOW, (2 * pair + i + 1) * WINDOW)
            o = (jnp.dot(p_ctx[rs], v_ref[0, 0, :c], preferred_element_type=jnp.float32)
                 + jnp.dot(p_loc[rs], v_ref[0, 0, pl.ds(ws, WIN_KEYS)], preferred_element_type=jnp.float32))
            o = o * inv[rs]
            acc = o if acc is None else acc + o
        outs.append(acc)
    o_ref[0] = jnp.concatenate(outs, axis=-1).astype(o_ref.dtype)


def swa_attention(sink, qs, ks, vlo, vhi, c, with_ctx):
    b, nh, t, _ = qs.shape
    off = 0 if with_ctx else c // WINDOW
    nq = t // WINDOW - off
    kv_spec = pl.BlockSpec((1, 1, t, LANES), lambda i, g, j: (i, g, 0, 0))
    return pl.pallas_call(
        functools.partial(_swa_kernel, off=off, c=c),
        grid=(b, SWA_KV_HEADS, nq),
        in_specs=[
            pl.BlockSpec(memory_space=pltpu.SMEM),
            pl.BlockSpec((1, SWA_GROUP, WINDOW, LANES), lambda i, g, j: (i, g, j + off, 0)),
            kv_spec, kv_spec, kv_spec,
        ],
        out_specs=pl.BlockSpec((1, WINDOW, SWA_GROUP // 2 * LANES), lambda i, g, j: (i, j, g)),
        out_shape=jax.ShapeDtypeStruct((b, nq * WINDOW, SWA_HEADS // 2 * LANES), jnp.bfloat16),
        compiler_params=_params(("parallel", "arbitrary", "arbitrary")),
        name="swa_attention",
    )(sink, qs, ks, vlo, vhi)


def _tree(op, xs):
    xs = list(xs)
    while len(xs) > 1:
        xs = [op(xs[i], xs[i + 1]) for i in range(0, len(xs) - 1, 2)] + (xs[-1:] if len(xs) % 2 else [])
    return xs[0]


def _top16_sweeps(ids, *problems):
    n = len(ids)
    big = float(max(ids) + 1)
    order = sorted(range(n), key=lambda k: ids[k])
    runs = [order[i:i + PEER_TOPK] for i in range(0, n, PEER_TOPK)]

    def step(r, carry):
        for val_ref, out_v, out_i in problems:
            m = _tree(jnp.maximum, [val_ref[k] for k in range(n)])
            firsts = []
            for run in runs:
                am = jnp.full(m.shape, big, jnp.float32)
                for k in reversed(run):
                    am = jnp.where(val_ref[k] == m, float(ids[k]), am)
                firsts.append(am)
            am = _tree(jnp.minimum, firsts)
            for k in range(n):
                val_ref[k] = jnp.where(am == float(ids[k]), NEG_INF, val_ref[k])
            out_v[r] = m
            out_i[r] = am
        return carry

    lax.fori_loop(0, PEER_TOPK, step, 0)


PAIRS = [(a, b) for a in range(PEER_TOPK) for b in range(PEER_TOPK) if (a + 1) * (b + 1) <= PEER_TOPK]


def _route_kernel(h_ref, om_ref, os_ref, wo_ref, mod_ref, n2_ref, wq_ref,
                  h1_ref, x2_ref, q_ref):
    h = h_ref[0]
    d = h.shape[-1]
    mod = mod_ref[0]
    half = om_ref.shape[-1]
    mix = (jnp.dot(om_ref[0], wo_ref[:half], preferred_element_type=jnp.float32)
           + jnp.dot(os_ref[0], wo_ref[half:], preferred_element_type=jnp.float32))
    h1 = h + mod[2:3] * mix
    h1_ref[0] = h1
    x = _rms(h1, n2_ref[...], d) * (1.0 + mod[4:5]) + mod[3:4]
    xhi = x.astype(jnp.bfloat16)
    x2_ref[0, :, :d] = xhi
    x2_ref[0, :, d:] = (x - xhi.astype(jnp.float32)).astype(jnp.bfloat16)
    q = jnp.dot(xhi, wq_ref[...], preferred_element_type=jnp.float32)
    for k in range(2 * PEER_HEADS):
        q_ref[k] = q[:, k * PEER_DHALF:(k + 1) * PEER_DHALF].astype(jnp.bfloat16)


def _topk_kernel(q_ref, keys_ref, idx_ref, g_ref, stage, vals, sv, si, cand, cv, ci, out_i, out_g):
    groups = vals.shape[2]

    def head(hh, carry):
        for part in range(2):
            st = lax.dot_general(keys_ref[hh, part], q_ref[2 * hh + part], (((1,), (1,)), ((), ())),
                                 preferred_element_type=jnp.float32)
            for g in range(groups):
                stage[pl.ds(g * STAGE_STRIDE, PEER_NKEYS), :] = st[:, g * LANES:(g + 1) * LANES]
            for k in range(PEER_NKEYS):
                vals[part, k] = stage[pl.ds(k, groups, stride=STAGE_STRIDE), :]
        _top16_sweeps(list(range(PEER_NKEYS)), *[(vals.at[p], sv.at[p], si.at[p]) for p in range(2)])
        for n, (a, b) in enumerate(PAIRS):
            cand[n] = sv[0, a] + sv[1, b]
        _top16_sweeps([a * PEER_TOPK + b for a, b in PAIRS], (cand, cv, ci))
        top = cv[0]
        e = [jnp.exp(cv[r] - top) for r in range(PEER_TOPK)]
        inv = 1.0 / _tree(jnp.add, e)
        for r in range(PEER_TOPK):
            ia = jnp.floor(ci[r] * (1.0 / PEER_TOPK))
            ib = ci[r] - ia * PEER_TOPK
            i1 = jnp.zeros_like(ia)
            i2 = jnp.zeros_like(ia)
            for a in range(PEER_TOPK):
                i1 = jnp.where(ia == a, si[0, a], i1)
                i2 = jnp.where(ib == a, si[1, a], i2)
            out_i[hh * PEER_TOPK + r] = (i1 * PEER_NKEYS + i2) * N_CHUNK
            out_g[hh * PEER_TOPK + r] = e[r] * inv
        return carry

    lax.fori_loop(0, PEER_HEADS, head, 0)
    for g in range(groups):
        rows = slice(g * LANES, (g + 1) * LANES)
        idx_ref[rows, :] = out_i[:, g, :].T.astype(jnp.int32)
        g_ref[rows, :] = out_g[:, g, :].T


def peer_topk(q16, keys):
    n = q16.shape[1]
    tb = next(t for t in (TOPK_BLK, TOPK_BLK // 2, TOPK_BLK // 4, LANES) if n % t == 0)
    groups = tb // LANES
    blk = lambda dt: pltpu.VMEM((PEER_TOPK, groups, LANES), dt)
    return pl.pallas_call(
        _topk_kernel,
        grid=(n // tb,),
        in_specs=[
            pl.BlockSpec((2 * PEER_HEADS, tb, PEER_DHALF), lambda i: (0, i, 0)),
            pl.BlockSpec(keys.shape, lambda i: (0, 0, 0, 0)),
        ],
        out_specs=[pl.BlockSpec((tb, PEER_ROWS), lambda i: (i, 0))] * 2,
        out_shape=[jax.ShapeDtypeStruct((n, PEER_ROWS), jnp.int32),
                   jax.ShapeDtypeStruct((n, PEER_ROWS), jnp.float32)],
        scratch_shapes=[
            pltpu.VMEM((groups * STAGE_STRIDE, LANES), jnp.float32),
            pltpu.VMEM((2, PEER_NKEYS, groups, LANES), jnp.float32),
            pltpu.VMEM((2, PEER_TOPK, groups, LANES), jnp.float32),
            pltpu.VMEM((2, PEER_TOPK, groups, LANES), jnp.float32),
            pltpu.VMEM((len(PAIRS), groups, LANES), jnp.float32),
            blk(jnp.float32), blk(jnp.float32),
            pltpu.VMEM((PEER_ROWS, groups, LANES), jnp.float32),
            pltpu.VMEM((PEER_ROWS, groups, LANES), jnp.float32),
        ],
        compiler_params=_params(("arbitrary",)),
        name="peer_topk",
    )(q16, keys)


def route(h, om, osw, wo, mod, n2, wq, r, c, with_ctx):
    b, t, d = h.shape
    off = 0 if with_ctx else c // r
    nblk = t // r - off
    nctx = c // r
    tq = nblk * r
    full = lambda a: pl.BlockSpec(a.shape, lambda i, j: (0,) * a.ndim)
    row_spec = lambda w: pl.BlockSpec((1, r, w), lambda i, j: (i, j, 0))
    return pl.pallas_call(
        _route_kernel,
        grid=(b, nblk),
        in_specs=[
            pl.BlockSpec((1, r, d), lambda i, j: (i, j + off, 0)),
            row_spec(om.shape[-1]), row_spec(osw.shape[-1]),
            full(wo),
            pl.BlockSpec((1, N_MOD, d), lambda i, j: (jnp.where(j + off < nctx, b, i), 0, 0)),
            full(n2), full(wq),
        ],
        out_specs=[row_spec(d), row_spec(2 * d),
                   pl.BlockSpec((2 * PEER_HEADS, r, PEER_DHALF), lambda i, j: (0, i * nblk + j, 0))],
        out_shape=[
            jax.ShapeDtypeStruct((b, tq, d), jnp.float32),
            jax.ShapeDtypeStruct((b, tq, 2 * d), jnp.bfloat16),
            jax.ShapeDtypeStruct((2 * PEER_HEADS, b * tq, PEER_DHALF), jnp.bfloat16),
        ],
        compiler_params=_params(("parallel", "arbitrary")),
        name="peer_route",
    )(h, om, osw, wo, mod, n2, wq)


def _pack_kernel(x_ref, o_ref):
    x = x_ref[...]
    half = x.shape[1] // 2
    bits = lambda v: pltpu.bitcast(v.astype(jnp.bfloat16).astype(jnp.float32), jnp.uint32)
    w = (bits(x[:, :half]) & jnp.uint32(0xFFFF0000)) | (bits(x[:, half:]) >> 16)
    for c in range(N_CHUNK):
        o_ref[pl.ds(c, x.shape[0], stride=N_CHUNK), :] = w[:, c * LANES:(c + 1) * LANES]


def pack_table(tab):
    e, d = tab.shape
    be = 512
    return pl.pallas_call(
        _pack_kernel,
        grid=(e // be,),
        in_specs=[pl.BlockSpec((be, d), lambda i: (i, 0))],
        out_specs=pl.BlockSpec((be * N_CHUNK, LANES), lambda i: (i, 0)),
        out_shape=jax.ShapeDtypeStruct((e * N_CHUNK, LANES), jnp.uint32),
        compiler_params=_params(("parallel",)),
        name="pack_table",
    )(tab)


def _gather_rows(idx_ref, tab_ref, tile_ref, t):
    for j in range(PEER_ROWS):
        r = pl.multiple_of(idx_ref[t, j], N_CHUNK)
        tile_ref[pl.ds(j, N_CHUNK, stride=TILE_STRIDE), :] = tab_ref[pl.ds(r, N_CHUNK), :]


def _token_pipeline(gather, gather_next, compute, tiles):
    steps = TOK_BLK // TOK_SET

    @pl.when(pl.program_id(0) == 0)
    def _():
        for k in range(TOK_SET):
            gather(tiles[0].at[k], k)

    def step(t, cur, nxt, fill):
        for k in range(TOK_SET):
            compute(cur.at[k], t + k)
            fill(nxt.at[k], k)

    def body(i, carry):
        t = TOK_SET * i
        ahead = lambda tile, k: gather(tile, t + TOK_SET + k)

        @pl.when(i % 2 == 0)
        def _():
            step(t, tiles[0], tiles[1], ahead)

        @pl.when(i % 2 == 1)
        def _():
            step(t, tiles[1], tiles[0], ahead)

        return carry

    lax.fori_loop(0, steps - 1, body, 0)
    step(TOK_BLK - TOK_SET, tiles[1], tiles[0], gather_next)


def _chunk(tile_ref, c):
    return pltpu.bitcast(tile_ref[pl.ds(c * TILE_STRIDE, PEER_ROWS), :], jnp.bfloat16)


def _u_kernel(idx_ref, nxt_ref, x_ref, g_ref, tab_ref, w_ref, *tiles):
    row = lax.broadcasted_iota(jnp.int32, (16, 2 * PEER_ROWS), 0) & 7

    def scores(tile_ref, t):
        x16 = x_ref[t]
        top = jnp.zeros((16, 2 * PEER_ROWS), jnp.float32)
        bot = jnp.zeros((16, 2 * PEER_ROWS), jnp.float32)
        for c in range(N_CHUNK):
            y = lax.dot_general(x16, _chunk(tile_ref, c), (((1,), (1,)), ((), ())),
                                preferred_element_type=jnp.float32)
            top = top + jnp.where(row == c, y, 0.0)
            bot = bot + jnp.where(row == c + N_CHUNK, y, 0.0)
        tot = top + pltpu.roll(bot, 1, axis=1)
        a = jnp.sum(tot, axis=0, keepdims=True)
        gelu = 0.5 * a * (1.0 + lax.erf(a * (2.0 ** -0.5)))
        w_ref[t] = g_ref[t] * gelu

    _token_pipeline(functools.partial(_gather_rows, idx_ref, tab_ref),
                    functools.partial(_gather_rows, nxt_ref, tab_ref), scores, tiles)


def _next_tokens_spec(n):
    per_blk = TOK_BLK // TOK_SET
    return pl.BlockSpec((TOK_SET, PEER_ROWS),
                        lambda i: (jnp.minimum((i + 1) * per_blk, n // TOK_SET - 1), 0),
                        memory_space=pltpu.SMEM)


def peer_scores(idx, x16, g, tab):
    n = idx.shape[0]
    tok = lambda *s: pl.BlockSpec((TOK_BLK,) + s, lambda i: (i,) + (0,) * len(s))
    return pl.pallas_call(
        _u_kernel,
        grid=(n // TOK_BLK,),
        in_specs=[
            pl.BlockSpec((TOK_BLK, PEER_ROWS), lambda i: (i, 0), memory_space=pltpu.SMEM),
            _next_tokens_spec(n),
            tok(16, LANES), tok(1, 2 * PEER_ROWS),
            pl.BlockSpec(tab.shape, lambda i: (0, 0), pipeline_mode=pl.Buffered(1)),
        ],
        out_specs=tok(1, 2 * PEER_ROWS),
        out_shape=jax.ShapeDtypeStruct((n, 1, 2 * PEER_ROWS), jnp.float32),
        scratch_shapes=[pltpu.VMEM((TOK_SET, N_CHUNK * TILE_STRIDE, LANES), jnp.uint32)] * 2,
        compiler_params=_params(("arbitrary",)),
        name="peer_scores",
    )(idx, idx, x16, g, tab)


def _v_kernel(idx_ref, nxt_ref, w_ref, h_ref, g2_ref, tab_ref, o_ref, *tiles):
    row = lax.broadcasted_iota(jnp.int32, (16, 2 * PEER_ROWS), 0)

    def combine(tile_ref, t):
        w = w_ref[t]
        w_hi = w.astype(jnp.bfloat16).astype(jnp.float32)
        w_lo = w - w_hi
        parts = (w_hi, pltpu.roll(w_hi, 2 * PEER_ROWS - 1, axis=1),
                 w_lo, pltpu.roll(w_lo, 2 * PEER_ROWS - 1, axis=1))
        acc = jnp.zeros((16, LANES), jnp.float32)
        for c in range(N_CHUNK):
            lhs = jnp.zeros((16, 2 * PEER_ROWS), jnp.float32)
            for k, part in enumerate(parts):
                lhs = jnp.where(row == c + N_CHUNK * k, part, lhs)
            acc = acc + jnp.dot(lhs.astype(jnp.bfloat16), _chunk(tile_ref, c),
                                preferred_element_type=jnp.float32)
        o_ref[t] = h_ref[t] + g2_ref[0, 0] * (acc[:8] + acc[8:])

    _token_pipeline(functools.partial(_gather_rows, idx_ref, tab_ref),
                    functools.partial(_gather_rows, nxt_ref, tab_ref), combine, tiles)


def peer_combine(idx, w, h8, g2, tab, blocks_per_sample, ctx_blocks):
    n = idx.shape[0]
    tok = lambda *s: pl.BlockSpec((TOK_BLK,) + s, lambda i: (i,) + (0,) * len(s))
    g2_map = lambda i: (i // blocks_per_sample,
                        jnp.where(i % blocks_per_sample < ctx_blocks, 0, 1), 0, 0)
    return pl.pallas_call(
        _v_kernel,
        grid=(n // TOK_BLK,),
        in_specs=[
            pl.BlockSpec((TOK_BLK, PEER_ROWS), lambda i: (i, 0), memory_space=pltpu.SMEM),
            _next_tokens_spec(n),
            tok(1, 2 * PEER_ROWS), tok(8, LANES),
            pl.BlockSpec((1, 1, 8, LANES), g2_map),
            pl.BlockSpec(tab.shape, lambda i: (0, 0), pipeline_mode=pl.Buffered(1)),
        ],
        out_specs=tok(8, LANES),
        out_shape=jax.ShapeDtypeStruct((n, 8, LANES), jnp.float32),
        scratch_shapes=[pltpu.VMEM((TOK_SET, N_CHUNK * TILE_STRIDE, LANES), jnp.uint32)] * 2,
        compiler_params=_params(("arbitrary",)),
        name="peer_combine",
    )(idx, idx, w, h8, g2, tab)


def _slots(w, width, offset=0):
    lead = w.shape[:-1]
    n = w.shape[-1] // width
    w = w.reshape(lead + (n, width))
    w = jnp.pad(w, [(0, 0)] * len(lead) + [(0, 0), (offset, LANES - width - offset)])
    return w.reshape(lead + (n * LANES,))


def _rope_tables(rot_dim, lane0, t, c):
    s = t - c
    q = rot_dim // 4
    pos = jnp.arange(s, dtype=jnp.float32)
    rows = jnp.floor(pos / GRID_W)
    cols = pos - rows * GRID_W
    inv = ROPE_BASE ** (-jnp.arange(q, dtype=jnp.float32) / q)
    ar = rows[:, None] * inv
    ac = cols[:, None] * inv
    zero = jnp.zeros_like(ar)
    cos = jnp.cos(jnp.concatenate([ar, ar, ac, ac], axis=-1))
    up = jnp.concatenate([-jnp.sin(ar), zero, -jnp.sin(ac), zero], axis=-1)
    dn = jnp.concatenate([zero, jnp.sin(ar), zero, jnp.sin(ac)], axis=-1)
    pad = lambda a, fill: jnp.pad(
        jnp.pad(a, ((0, 0), (lane0, LANES - lane0 - rot_dim)), constant_values=fill),
        ((c, 0), (0, 0)), constant_values=fill)
    cos = jnp.pad(jnp.pad(cos, ((0, 0), (lane0, LANES - lane0 - rot_dim)), constant_values=1.0),
                  ((c, 0), (0, 0)), constant_values=1.0)
    return jnp.stack([cos, pad(up, 0.0), pad(dn, 0.0)])


def kernel(x, c, ctx, c_ctx, ada_w, ada_b, norm1_g, norm2_g, w_in, mla_qa_g, mla_wuq, mla_kva_g, mla_wukv, mla_qn_g, mla_kn_g, swa_qn_g, swa_kn_g, swa_sink, w_out, peer_wq, peer_keys, peer_u, peer_v):
    b, s, d = x.shape
    nctx = ctx.shape[1]
    t = nctx + s
    depth = ada_w.shape[0]
    r = min(256, nctx)
    bf = jnp.bfloat16

    cond = jnp.zeros((16, d), jnp.float32).at[:b].set(c).at[b].set(c_ctx)
    mod_all = modulation(cond, ada_w, ada_b).reshape(depth, 16, N_MOD, d)
    rope_m = _rope_tables(MLA_ROPE, MLA_NOPE, t, nctx)
    rope_s = _rope_tables(SWA_DIM, 0, t, nctx)

    h = jnp.concatenate([ctx, x], axis=1)
    for l in range(depth):
        last = l == depth - 1
        with_ctx = not last
        mod = mod_all[l]
        wi = w_in[l]
        kv0 = Q_COLS + MLA_KV_RANK
        sk0 = kv0 + MLA_ROPE
        sv0 = sk0 + SWA_KV_HEADS * SWA_DIM
        sv = wi[:, sv0:].reshape(d, SWA_KV_HEADS, 1, SWA_DIM)
        win = jnp.concatenate([
            wi[:, :MLA_Q_RANK], wi[:, Q_COLS:kv0],
            _slots(wi[:, kv0:sk0], MLA_ROPE, MLA_NOPE),
            _slots(wi[:, MLA_Q_RANK:Q_COLS], SWA_DIM),
            _slots(wi[:, sk0:sv0], SWA_DIM),
            jnp.broadcast_to(sv, (d, SWA_KV_HEADS, 2, SWA_DIM)).reshape(d, SWA_KV_HEADS * LANES),
        ], axis=1).astype(bf)
        wuq = _slots(mla_wuq[l], MLA_QK).astype(bf)
        wukv = mla_wukv[l].reshape(MLA_KV_RANK, MLA_HEADS, MLA_NOPE + MLA_V)
        wuk = _slots(wukv[:, :, :MLA_NOPE].reshape(MLA_KV_RANK, -1), MLA_NOPE).astype(bf)
        wv = wukv[:, :, MLA_NOPE:].reshape(MLA_KV_RANK, MLA_HEADS // 2, 2, MLA_V)
        zero = jnp.zeros_like(wv[:, :, 0])
        wuv = jnp.stack([jnp.concatenate([wv[:, :, 0], zero], axis=-1),
                         jnp.concatenate([zero, wv[:, :, 1]], axis=-1)], axis=2)
        wuv = wuv.reshape(MLA_KV_RANK, MLA_HEADS * LANES).astype(bf)
        row = lambda g: g.reshape(1, -1)

        qm, km, vm, qs, ks, vlo, vhi = projections(
            h, mod, row(norm1_g[l]), win, row(mla_qa_g[l]), wuq, row(mla_kva_g[l]), wuk, wuv,
            row(_slots(mla_qn_g[l], MLA_QK)), row(_slots(mla_kn_g[l], MLA_QK)),
            row(_slots(swa_qn_g[l], SWA_DIM)), row(_slots(swa_kn_g[l], SWA_DIM)),
            rope_m, rope_s, r, nctx)
        om = mla_attention(qm, km, vm, r, nctx, with_ctx)
        osw = swa_attention(swa_sink[l], qs, ks, vlo, vhi, nctx, with_ctx)
        h1, x2, q16 = route(
            h, om, osw, w_out[l].astype(bf), mod, row(norm2_g[l]), peer_wq[l].astype(bf),
            r, nctx, with_ctx)
        idx, gate = peer_topk(q16, peer_keys[l].astype(bf))

        tl = h1.shape[1]
        n = b * tl
        x16 = x2.reshape(n, 16, LANES)
        gate = jnp.stack([jnp.zeros_like(gate), gate], axis=-1).reshape(n, 1, 2 * PEER_ROWS)
        w = peer_scores(idx, x16, gate, pack_table(peer_u[l]))
        g2 = jnp.stack([jnp.broadcast_to(mod[b, 5], (b, d)), mod[:b, 5]], axis=1)
        h = peer_combine(idx, w, h1.reshape(n, 8, LANES), g2.reshape(b, 2, 8, LANES),
                         pack_table(peer_v[l]), tl // TOK_BLK,
                         nctx // TOK_BLK if with_ctx else 0).reshape(b, tl, d)
    return h
```

```python
import functools
import jax
import jax.numpy as jnp
from jax import lax
from jax.experimental import pallas as pl
from jax.experimental.pallas import tpu as pltpu
from jax.experimental.pallas import tpu_sc as plsc

LANES = 128
EPS = 1e-6
ROPE_BASE = 10000.0
GRID_W = 64
N_MOD = 6

MLA_HEADS = 8
MLA_NOPE = 64
MLA_ROPE = 32
MLA_QK = MLA_NOPE + MLA_ROPE
MLA_V = 64
MLA_Q_RANK = 384
MLA_KV_RANK = 256
SWA_HEADS = 8
SWA_KV_HEADS = 2
SWA_GROUP = SWA_HEADS // SWA_KV_HEADS
SWA_DIM = 64
WINDOW = 128
WIN_KEYS = 3 * WINDOW
Q_COLS = MLA_Q_RANK + SWA_HEADS * SWA_DIM

PEER_HEADS = 8
PEER_NKEYS = 128
PEER_DHALF = 128
PEER_TOPK = 16
PEER_ROWS = PEER_HEADS * PEER_TOPK
N_CHUNK = 4
TILE_STRIDE = PEER_ROWS + 1
TOPK_BLK = 1024
STAGE_STRIDE = PEER_NKEYS + 8
TOK_BLK = 128
TOK_SET = 8

OFF_QA = 0
OFF_KVA = OFF_QA + MLA_Q_RANK
OFF_KR = OFF_KVA + MLA_KV_RANK
OFF_SQ = OFF_KR + LANES
OFF_SK = OFF_SQ + SWA_HEADS * LANES
OFF_SV = OFF_SK + SWA_KV_HEADS * LANES
N_IN = OFF_SV + SWA_KV_HEADS * LANES

VMEM_LIMIT = 56 * 1024 * 1024
NEG_INF = float("-inf")
LOG2_E = 1.4426950408889634


def _params(sem, vmem=VMEM_LIMIT):
    return pltpu.CompilerParams(dimension_semantics=sem, vmem_limit_bytes=vmem)


def _rms(x, g, n):
    ms = jnp.sum(x * x, axis=-1, keepdims=True) * (1.0 / n)
    return x * lax.rsqrt(ms + EPS) * g


def _rope(x, cos, sin_up, sin_dn, shift):
    return (x * cos + pltpu.roll(x, LANES - shift, axis=1) * sin_up
            + pltpu.roll(x, shift, axis=1) * sin_dn)


def _mod_kernel(c_ref, w_ref, b_ref, o_ref):
    c = c_ref[...]
    s = c * (1.0 / (1.0 + jnp.exp(-c)))
    o_ref[0] = jnp.dot(s, w_ref[0], preferred_element_type=jnp.float32,
                       precision=lax.Precision.HIGHEST) + b_ref[0]


def modulation(cond, ada_w, ada_b):
    nl, d, n6 = ada_w.shape
    tn = 1536
    return pl.pallas_call(
        _mod_kernel,
        grid=(nl, n6 // tn),
        in_specs=[
            pl.BlockSpec(cond.shape, lambda l, j: (0, 0)),
            pl.BlockSpec((1, d, tn), lambda l, j: (l, 0, j)),
            pl.BlockSpec((1, 1, tn), lambda l, j: (l, 0, j)),
        ],
        out_specs=pl.BlockSpec((1, cond.shape[0], tn), lambda l, j: (l, 0, j)),
        out_shape=jax.ShapeDtypeStruct((nl, cond.shape[0], n6), jnp.float32),
        compiler_params=_params(("arbitrary", "arbitrary")),
        name="modulation",
    )(cond, ada_w, ada_b.reshape(nl, 1, n6))


def _proj_kernel(h_ref, mod_ref, n1_ref, win_ref, qag_ref, wuq_ref, kvag_ref, wuk_ref, wuv_ref,
                 qn_ref, kn_ref, sqn_ref, skn_ref, rm_ref, rs_ref,
                 qm_ref, km_ref, vm_ref, qs_ref, ks_ref, vlo_ref, vhi_ref):
    h = h_ref[0]
    d = h.shape[-1]
    mod = mod_ref[0]
    a = _rms(h, n1_ref[...], d) * (1.0 + mod[1:2]) + mod[0:1]
    p = jnp.dot(a.astype(jnp.bfloat16), win_ref[...], preferred_element_type=jnp.float32)

    cm, sm_up, sm_dn = rm_ref[0], rm_ref[1], rm_ref[2]
    cs, ss_up, ss_dn = rs_ref[0], rs_ref[1], rs_ref[2]

    qa = _rms(p[:, OFF_QA:OFF_QA + MLA_Q_RANK], qag_ref[...], MLA_Q_RANK)
    q = jnp.dot(qa.astype(jnp.bfloat16), wuq_ref[...], preferred_element_type=jnp.float32)
    kva = _rms(p[:, OFF_KVA:OFF_KVA + MLA_KV_RANK], kvag_ref[...], MLA_KV_RANK).astype(jnp.bfloat16)
    kn = jnp.dot(kva, wuk_ref[...], preferred_element_type=jnp.float32)
    vm = jnp.dot(kva, wuv_ref[...], preferred_element_type=jnp.float32)
    kr = p[:, OFF_KR:OFF_KR + LANES]
    for hd in range(MLA_HEADS):
        sl = slice(hd * LANES, (hd + 1) * LANES)
        qh = _rope(_rms(q[:, sl], qn_ref[...], MLA_QK), cm, sm_up, sm_dn, MLA_ROPE // 4)
        qm_ref[0, hd] = (qh * (MLA_QK ** -0.5 * LOG2_E)).astype(jnp.bfloat16)
        kh = _rope(_rms(kn[:, sl] + kr, kn_ref[...], MLA_QK), cm, sm_up, sm_dn, MLA_ROPE // 4)
        km_ref[0, hd] = kh.astype(jnp.bfloat16)
        vm_ref[0, hd] = vm[:, sl].astype(jnp.bfloat16)
    for hd in range(SWA_HEADS):
        x = p[:, OFF_SQ + hd * LANES:OFF_SQ + (hd + 1) * LANES]
        qh = _rope(_rms(x, sqn_ref[...], SWA_DIM), cs, ss_up, ss_dn, SWA_DIM // 4)
        qs_ref[0, hd] = (qh * (SWA_DIM ** -0.5)).astype(jnp.bfloat16)
    lane = lax.broadcasted_iota(jnp.int32, (h.shape[0], LANES), 1)
    for g in range(SWA_KV_HEADS):
        x = p[:, OFF_SK + g * LANES:OFF_SK + (g + 1) * LANES]
        kh = _rope(_rms(x, skn_ref[...], SWA_DIM), cs, ss_up, ss_dn, SWA_DIM // 4)
        ks_ref[0, g] = kh.astype(jnp.bfloat16)
        v = p[:, OFF_SV + g * LANES:OFF_SV + (g + 1) * LANES]
        vlo_ref[0, g] = jnp.where(lane < SWA_DIM, v, 0.0).astype(jnp.bfloat16)
        vhi_ref[0, g] = jnp.where(lane >= SWA_DIM, v, 0.0).astype(jnp.bfloat16)


def projections(h, mod, n1, win, qag, wuq, kvag, wuk, wuv, qn, kn, sqn, skn, rope_m, rope_s, r, c):
    b, t, d = h.shape
    nctx = c // r
    full = lambda a: pl.BlockSpec(a.shape, lambda i, j: (0,) * a.ndim)
    head_out = lambda nh: pl.BlockSpec((1, nh, r, LANES), lambda i, j: (i, 0, j, 0))
    head_shape = lambda nh: jax.ShapeDtypeStruct((b, nh, t, LANES), jnp.bfloat16)
    return pl.pallas_call(
        _proj_kernel,
        grid=(b, t // r),
        in_specs=[
            pl.BlockSpec((1, r, d), lambda i, j: (i, j, 0)),
            pl.BlockSpec((1, N_MOD, d), lambda i, j: (jnp.where(j < nctx, b, i), 0, 0)),
            full(n1), full(win), full(qag), full(wuq), full(kvag), full(wuk), full(wuv),
            full(qn), full(kn), full(sqn), full(skn),
            pl.BlockSpec((3, r, LANES), lambda i, j: (0, j, 0)),
            pl.BlockSpec((3, r, LANES), lambda i, j: (0, j, 0)),
        ],
        out_specs=[head_out(MLA_HEADS), head_out(MLA_HEADS), head_out(MLA_HEADS),
                   head_out(SWA_HEADS), head_out(SWA_KV_HEADS), head_out(SWA_KV_HEADS),
                   head_out(SWA_KV_HEADS)],
        out_shape=[head_shape(MLA_HEADS), head_shape(MLA_HEADS), head_shape(MLA_HEADS),
                   head_shape(SWA_HEADS), head_shape(SWA_KV_HEADS), head_shape(SWA_KV_HEADS),
                   head_shape(SWA_KV_HEADS)],
        compiler_params=_params(("parallel", "arbitrary")),
        name="projections",
    )(h, mod, n1, win, qag, wuq, kvag, wuk, wuv, qn, kn, sqn, skn, rope_m, rope_s)


def _mla_kernel(q_ref, k_ref, v_ref, o_ref, *, nctx_blocks, c):
    qi = pl.program_id(2)

    def attend(nk):
        acc = None
        for i in range(2):
            q = q_ref[0, i]
            s = lax.dot_general(q, k_ref[0, i, :nk], (((1,), (1,)), ((), ())),
                                preferred_element_type=jnp.float32)
            m = jnp.max(s, axis=-1, keepdims=True)
            p = jnp.exp2(s - m)
            l = jnp.sum(p, axis=-1, keepdims=True)
            o = jnp.dot(p.astype(jnp.bfloat16), v_ref[0, i, :nk], preferred_element_type=jnp.float32)
            o = o * (1.0 / l)
            acc = o if acc is None else acc + o
        o_ref[0] = acc.astype(o_ref.dtype)

    if nctx_blocks:
        @pl.when(qi < nctx_blocks)
        def _():
            attend(c)

        @pl.when(qi >= nctx_blocks)
        def _():
            attend(k_ref.shape[2])
    else:
        attend(k_ref.shape[2])


def mla_attention(qm, km, vm, tq, c, with_ctx):
    b, nh, t, _ = qm.shape
    off = 0 if with_ctx else c // tq
    nq = t // tq - off
    return pl.pallas_call(
        functools.partial(_mla_kernel, nctx_blocks=(c // tq if with_ctx else 0), c=c),
        grid=(b, nh // 2, nq),
        in_specs=[
            pl.BlockSpec((1, 2, tq, LANES), lambda i, hp, j: (i, hp, j + off, 0)),
            pl.BlockSpec((1, 2, t, LANES), lambda i, hp, j: (i, hp, 0, 0)),
            pl.BlockSpec((1, 2, t, LANES), lambda i, hp, j: (i, hp, 0, 0)),
        ],
        out_specs=pl.BlockSpec((1, tq, LANES), lambda i, hp, j: (i, j, hp)),
        out_shape=jax.ShapeDtypeStruct((b, nq * tq, nh // 2 * LANES), jnp.bfloat16),
        compiler_params=_params(("parallel", "arbitrary", "arbitrary")),
        name="mla_attention",
    )(qm, km, vm)


def _swa_kernel(sink_ref, q_ref, k_ref, vlo_ref, vhi_ref, o_ref, *, off, c):
    g = pl.program_id(1)
    qi = pl.program_id(2) + off
    t = k_ref.shape[2]
    rows = SWA_GROUP * WINDOW
    q = q_ref[0].reshape(rows, LANES)
    r_idx = lax.broadcasted_iota(jnp.int32, (rows, 1), 0)
    sink = jnp.zeros((rows, 1), jnp.float32)
    for i in range(SWA_GROUP):
        sink = jnp.where(r_idx // WINDOW == i, sink_ref[g * SWA_GROUP + i], sink)

    ws = pl.multiple_of(jnp.clip(qi * WINDOW - WINDOW, c, t - WIN_KEYS), WINDOW)
    nt = (((1,), (1,)), ((), ()))
    s_ctx = lax.dot_general(q, k_ref[0, 0, :c], nt, preferred_element_type=jnp.float32)
    s_loc = lax.dot_general(q, k_ref[0, 0, pl.ds(ws, WIN_KEYS)], nt, preferred_element_type=jnp.float32)
    qpos = qi * WINDOW + (r_idx % WINDOW)
    kpos = ws + lax.broadcasted_iota(jnp.int32, (1, WIN_KEYS), 1)
    reach = jnp.where(qi * WINDOW >= c, WINDOW, -1)
    s_loc = jnp.where(jnp.abs(qpos - kpos) <= reach, s_loc, NEG_INF)
    m = jnp.maximum(jnp.maximum(jnp.max(s_ctx, axis=-1, keepdims=True),
                                jnp.max(s_loc, axis=-1, keepdims=True)), sink)
    p_ctx = jnp.exp(s_ctx - m)
    p_loc = jnp.exp(s_loc - m)
    l = (jnp.sum(p_ctx, axis=-1, keepdims=True) + jnp.sum(p_loc, axis=-1, keepdims=True)
         + jnp.exp(sink - m))
    inv = 1.0 / l
    p_ctx = p_ctx.astype(jnp.bfloat16)
    p_loc = p_loc.astype(jnp.bfloat16)
    outs = []
    for pair in range(SWA_GROUP // 2):
        acc = None
        for i, v_ref in enumerate((vlo_ref, vhi_ref)):
            rs = slice((2 * pair + i) * WINDOW, (2 * pair + i + 1) * WINDOW)
            o = (jnp.dot(p_ctx[rs], v_ref[0, 0, :c], preferred_element_type=jnp.float32)
                 + jnp.dot(p_loc[rs], v_ref[0, 0, pl.ds(ws, WIN_KEYS)], preferred_element_type=jnp.float32))
            o = o * inv[rs]
            acc = o if acc is None else acc + o
        outs.append(acc)
    o_ref[0] = jnp.concatenate(outs, axis=-1).astype(o_ref.dtype)


def swa_attention(sink, qs, ks, vlo, vhi, c, with_ctx):
    b, nh, t, _ = qs.shape
    off = 0 if with_ctx else c // WINDOW
    nq = t // WINDOW - off
    kv_spec = pl.BlockSpec((1, 1, t, LANES), lambda i, g, j: (i, g, 0, 0))
    return pl.pallas_call(
        functools.partial(_swa_kernel, off=off, c=c),
        grid=(b, SWA_KV_HEADS, nq),
        in_specs=[
            pl.BlockSpec(memory_space=pltpu.SMEM),
            pl.BlockSpec((1, SWA_GROUP, WINDOW, LANES), lambda i, g, j: (i, g, j + off, 0)),
            kv_spec, kv_spec, kv_spec,
        ],
        out_specs=pl.BlockSpec((1, WINDOW, SWA_GROUP // 2 * LANES), lambda i, g, j: (i, j, g)),
        out_shape=jax.ShapeDtypeStruct((b, nq * WINDOW, SWA_HEADS // 2 * LANES), jnp.bfloat16),
        compiler_params=_params(("parallel", "arbitrary", "arbitrary")),
        name="swa_attention",
    )(sink, qs, ks, vlo, vhi)


def _tree(op, xs):
    xs = list(xs)
    while len(xs) > 1:
        xs = [op(xs[i], xs[i + 1]) for i in range(0, len(xs) - 1, 2)] + (xs[-1:] if len(xs) % 2 else [])
    return xs[0]


def _top16_sweeps(ids, *problems):
    n = len(ids)
    big = float(max(ids) + 1)
    order = sorted(range(n), key=lambda k: ids[k])
    runs = [order[i:i + PEER_TOPK] for i in range(0, n, PEER_TOPK)]

    def step(r, carry):
        for val_ref, out_v, out_i in problems:
            m = _tree(jnp.maximum, [val_ref[k] for k in range(n)])
            firsts = []
            for run in runs:
                am = jnp.full(m.shape, big, jnp.float32)
                for k in reversed(run):
                    am = jnp.where(val_ref[k] == m, float(ids[k]), am)
                firsts.append(am)
            am = _tree(jnp.minimum, firsts)
            for k in range(n):
                val_ref[k] = jnp.where(am == float(ids[k]), NEG_INF, val_ref[k])
            out_v[r] = m
            out_i[r] = am
        return carry

    lax.fori_loop(0, PEER_TOPK, step, 0)


PAIRS = [(a, b) for a in range(PEER_TOPK) for b in range(PEER_TOPK) if (a + 1) * (b + 1) <= PEER_TOPK]


def _route_kernel(h_ref, om_ref, os_ref, wo_ref, mod_ref, n2_ref, wq_ref,
                  h1_ref, x2_ref, q_ref):
    h = h_ref[0]
    d = h.shape[-1]
    mod = mod_ref[0]
    half = om_ref.shape[-1]
    mix = (jnp.dot(om_ref[0], wo_ref[:half], preferred_element_type=jnp.float32)
           + jnp.dot(os_ref[0], wo_ref[half:], preferred_element_type=jnp.float32))
    h1 = h + mod[2:3] * mix
    h1_ref[0] = h1
    x = _rms(h1, n2_ref[...], d) * (1.0 + mod[4:5]) + mod[3:4]
    xhi = x.astype(jnp.bfloat16)
    x2_ref[0, :, :d] = xhi
    x2_ref[0, :, d:] = (x - xhi.astype(jnp.float32)).astype(jnp.bfloat16)
    q = jnp.dot(xhi, wq_ref[...], preferred_element_type=jnp.float32)
    for k in range(2 * PEER_HEADS):
        q_ref[k] = q[:, k * PEER_DHALF:(k + 1) * PEER_DHALF].astype(jnp.bfloat16)


def _topk_kernel(q_ref, keys_ref, idx_ref, g_ref, stage, vals, sv, si, cand, cv, ci, out_i, out_g):
    groups = vals.shape[2]

    def head(hh, carry):
        for part in range(2):
            st = lax.dot_general(keys_ref[hh, part], q_ref[2 * hh + part], (((1,), (1,)), ((), ())),
                                 preferred_element_type=jnp.float32)
            for g in range(groups):
                stage[pl.ds(g * STAGE_STRIDE, PEER_NKEYS), :] = st[:, g * LANES:(g + 1) * LANES]
            for k in range(PEER_NKEYS):
                vals[part, k] = stage[pl.ds(k, groups, stride=STAGE_STRIDE), :]
        _top16_sweeps(list(range(PEER_NKEYS)), *[(vals.at[p], sv.at[p], si.at[p]) for p in range(2)])
        for n, (a, b) in enumerate(PAIRS):
            cand[n] = sv[0, a] + sv[1, b]
        _top16_sweeps([a * PEER_TOPK + b for a, b in PAIRS], (cand, cv, ci))
        top = cv[0]
        e = [jnp.exp(cv[r] - top) for r in range(PEER_TOPK)]
        inv = 1.0 / _tree(jnp.add, e)
        for r in range(PEER_TOPK):
            ia = jnp.floor(ci[r] * (1.0 / PEER_TOPK))
            ib = ci[r] - ia * PEER_TOPK
            i1 = jnp.zeros_like(ia)
            i2 = jnp.zeros_like(ia)
            for a in range(PEER_TOPK):
                i1 = jnp.where(ia == a, si[0, a], i1)
                i2 = jnp.where(ib == a, si[1, a], i2)
            out_i[hh * PEER_TOPK + r] = (i1 * PEER_NKEYS + i2) * N_CHUNK
            out_g[hh * PEER_TOPK + r] = e[r] * inv
        return carry

    lax.fori_loop(0, PEER_HEADS, head, 0)
    for g in range(groups):
        rows = slice(g * LANES, (g + 1) * LANES)
        idx_ref[rows, :] = out_i[:, g, :].T.astype(jnp.int32)
        g_ref[rows, :] = out_g[:, g, :].T


def peer_topk(q16, keys):
    n = q16.shape[1]
    tb = next(t for t in (TOPK_BLK, TOPK_BLK // 2, TOPK_BLK // 4, LANES) if n % t == 0)
    groups = tb // LANES
    blk = lambda dt: pltpu.VMEM((PEER_TOPK, groups, LANES), dt)
    return pl.pallas_call(
        _topk_kernel,
        grid=(n // tb,),
        in_specs=[
            pl.BlockSpec((2 * PEER_HEADS, tb, PEER_DHALF), lambda i: (0, i, 0)),
            pl.BlockSpec(keys.shape, lambda i: (0, 0, 0, 0)),
        ],
        out_specs=[pl.BlockSpec((tb, PEER_ROWS), lambda i: (i, 0))] * 2,
        out_shape=[jax.ShapeDtypeStruct((n, PEER_ROWS), jnp.int32),
                   jax.ShapeDtypeStruct((n, PEER_ROWS), jnp.float32)],
        scratch_shapes=[
            pltpu.VMEM((groups * STAGE_STRIDE, LANES), jnp.float32),
            pltpu.VMEM((2, PEER_NKEYS, groups, LANES), jnp.float32),
            pltpu.VMEM((2, PEER_TOPK, groups, LANES), jnp.float32),
            pltpu.VMEM((2, PEER_TOPK, groups, LANES), jnp.float32),
            pltpu.VMEM((len(PAIRS), groups, LANES), jnp.float32),
            blk(jnp.float32), blk(jnp.float32),
            pltpu.VMEM((PEER_ROWS, groups, LANES), jnp.float32),
            pltpu.VMEM((PEER_ROWS, groups, LANES), jnp.float32),
        ],
        compiler_params=_params(("arbitrary",)),
        name="peer_topk",
    )(q16, keys)


def route(h, om, osw, wo, mod, n2, wq, r, c, with_ctx):
    b, t, d = h.shape
    off = 0 if with_ctx else c // r
    nblk = t // r - off
    nctx = c // r
    tq = nblk * r
    full = lambda a: pl.BlockSpec(a.shape, lambda i, j: (0,) * a.ndim)
    row_spec = lambda w: pl.BlockSpec((1, r, w), lambda i, j: (i, j, 0))
    return pl.pallas_call(
        _route_kernel,
        grid=(b, nblk),
        in_specs=[
            pl.BlockSpec((1, r, d), lambda i, j: (i, j + off, 0)),
            row_spec(om.shape[-1]), row_spec(osw.shape[-1]),
            full(wo),
            pl.BlockSpec((1, N_MOD, d), lambda i, j: (jnp.where(j + off < nctx, b, i), 0, 0)),
            full(n2), full(wq),
        ],
        out_specs=[row_spec(d), row_spec(2 * d),
                   pl.BlockSpec((2 * PEER_HEADS, r, PEER_DHALF), lambda i, j: (0, i * nblk + j, 0))],
        out_shape=[
            jax.ShapeDtypeStruct((b, tq, d), jnp.float32),
            jax.ShapeDtypeStruct((b, tq, 2 * d), jnp.bfloat16),
            jax.ShapeDtypeStruct((2 * PEER_HEADS, b * tq, PEER_DHALF), jnp.bfloat16),
        ],
        compiler_params=_params(("parallel", "arbitrary")),
        name="peer_route",
    )(h, om, osw, wo, mod, n2, wq)


def _pack_kernel(x_ref, o_ref):
    x = x_ref[0]
    half = x.shape[1] // 2
    bits = lambda v: pltpu.bitcast(v.astype(jnp.bfloat16).astype(jnp.float32), jnp.uint32)
    w = (bits(x[:, :half]) & jnp.uint32(0xFFFF0000)) | (bits(x[:, half:]) >> 16)
    for c in range(N_CHUNK):
        o_ref[pl.ds(c, x.shape[0], stride=N_CHUNK), :] = w[:, c * LANES:(c + 1) * LANES]


def pack_table(tabs, l):
    _, e, d = tabs.shape
    be = 512
    return pl.pallas_call(
        _pack_kernel,
        grid=(e // be,),
        in_specs=[pl.BlockSpec((1, be, d), lambda i: (l, i, 0))],
        out_specs=pl.BlockSpec((be * N_CHUNK, LANES), lambda i: (i, 0)),
        out_shape=jax.ShapeDtypeStruct((e * N_CHUNK, LANES), jnp.uint32),
        compiler_params=_params(("parallel",)),
        name="pack_table",
    )(tabs)


def _gather_rows(idx_ref, tab_ref, tile_ref, t):
    for j in range(PEER_ROWS):
        r = pl.multiple_of(idx_ref[t, j], N_CHUNK)
        tile_ref[pl.ds(j, N_CHUNK, stride=TILE_STRIDE), :] = tab_ref[pl.ds(r, N_CHUNK), :]


def _token_pipeline(gather, gather_next, compute, tiles):
    steps = TOK_BLK // TOK_SET

    @pl.when(pl.program_id(0) == 0)
    def _():
        for k in range(TOK_SET):
            gather(tiles[0].at[k], k)

    def step(t, cur, nxt, fill):
        for k in range(TOK_SET):
            compute(cur.at[k], t + k)
            fill(nxt.at[k], k)

    def body(i, carry):
        t = TOK_SET * i
        ahead = lambda tile, k: gather(tile, t + TOK_SET + k)

        @pl.when(i % 2 == 0)
        def _():
            step(t, tiles[0], tiles[1], ahead)

        @pl.when(i % 2 == 1)
        def _():
            step(t, tiles[1], tiles[0], ahead)

        return carry

    lax.fori_loop(0, steps - 1, body, 0)
    step(TOK_BLK - TOK_SET, tiles[1], tiles[0], gather_next)


def _chunk(tile_ref, c):
    return pltpu.bitcast(tile_ref[pl.ds(c * TILE_STRIDE, PEER_ROWS), :], jnp.bfloat16)


def _u_kernel(idx_ref, nxt_ref, x_ref, g_ref, tab_ref, w_ref, *tiles):
    row = lax.broadcasted_iota(jnp.int32, (16, 2 * PEER_ROWS), 0) & 7

    def scores(tile_ref, t):
        x16 = x_ref[t]
        top = jnp.zeros((16, 2 * PEER_ROWS), jnp.float32)
        bot = jnp.zeros((16, 2 * PEER_ROWS), jnp.float32)
        for c in range(N_CHUNK):
            y = lax.dot_general(x16, _chunk(tile_ref, c), (((1,), (1,)), ((), ())),
                                preferred_element_type=jnp.float32)
            top = top + jnp.where(row == c, y, 0.0)
            bot = bot + jnp.where(row == c + N_CHUNK, y, 0.0)
        tot = top + pltpu.roll(bot, 1, axis=1)
        a = jnp.sum(tot, axis=0, keepdims=True)
        gelu = 0.5 * a * (1.0 + lax.erf(a * (2.0 ** -0.5)))
        w_ref[t] = g_ref[t] * gelu

    _token_pipeline(functools.partial(_gather_rows, idx_ref, tab_ref),
                    functools.partial(_gather_rows, nxt_ref, tab_ref), scores, tiles)


def _next_tokens_spec(n):
    per_blk = TOK_BLK // TOK_SET
    return pl.BlockSpec((TOK_SET, PEER_ROWS),
                        lambda i: (jnp.minimum((i + 1) * per_blk, n // TOK_SET - 1), 0),
                        memory_space=pltpu.SMEM)


def peer_scores(idx, x16, g, tab):
    n = idx.shape[0]
    tok = lambda *s: pl.BlockSpec((TOK_BLK,) + s, lambda i: (i,) + (0,) * len(s))
    return pl.pallas_call(
        _u_kernel,
        grid=(n // TOK_BLK,),
        in_specs=[
            pl.BlockSpec((TOK_BLK, PEER_ROWS), lambda i: (i, 0), memory_space=pltpu.SMEM),
            _next_tokens_spec(n),
            tok(16, LANES), tok(1, 2 * PEER_ROWS),
            pl.BlockSpec(tab.shape, lambda i: (0, 0), pipeline_mode=pl.Buffered(1)),
        ],
        out_specs=tok(1, 2 * PEER_ROWS),
        out_shape=jax.ShapeDtypeStruct((n, 1, 2 * PEER_ROWS), jnp.float32),
        scratch_shapes=[pltpu.VMEM((TOK_SET, N_CHUNK * TILE_STRIDE, LANES), jnp.uint32)] * 2,
        compiler_params=_params(("arbitrary",)),
        name="peer_scores",
    )(idx, idx, x16, g, tab)


def _v_kernel(idx_ref, nxt_ref, w_ref, h_ref, g2_ref, tab_ref, o_ref, *tiles):
    row = lax.broadcasted_iota(jnp.int32, (16, 2 * PEER_ROWS), 0)

    def combine(tile_ref, t):
        w = w_ref[t]
        w_hi = w.astype(jnp.bfloat16).astype(jnp.float32)
        w_lo = w - w_hi
        parts = (w_hi, pltpu.roll(w_hi, 2 * PEER_ROWS - 1, axis=1),
                 w_lo, pltpu.roll(w_lo, 2 * PEER_ROWS - 1, axis=1))
        acc = jnp.zeros((16, LANES), jnp.float32)
        for c in range(N_CHUNK):
            lhs = jnp.zeros((16, 2 * PEER_ROWS), jnp.float32)
            for k, part in enumerate(parts):
                lhs = jnp.where(row == c + N_CHUNK * k, part, lhs)
            acc = acc + jnp.dot(lhs.astype(jnp.bfloat16), _chunk(tile_ref, c),
                                preferred_element_type=jnp.float32)
        o_ref[t] = h_ref[t] + g2_ref[0, 0] * (acc[:8] + acc[8:])

    _token_pipeline(functools.partial(_gather_rows, idx_ref, tab_ref),
                    functools.partial(_gather_rows, nxt_ref, tab_ref), combine, tiles)


def peer_combine(idx, w, h8, g2, tab, blocks_per_sample, ctx_blocks):
    n = idx.shape[0]
    tok = lambda *s: pl.BlockSpec((TOK_BLK,) + s, lambda i: (i,) + (0,) * len(s))
    g2_map = lambda i: (i // blocks_per_sample,
                        jnp.where(i % blocks_per_sample < ctx_blocks, 0, 1), 0, 0)
    return pl.pallas_call(
        _v_kernel,
        grid=(n // TOK_BLK,),
        in_specs=[
            pl.BlockSpec((TOK_BLK, PEER_ROWS), lambda i: (i, 0), memory_space=pltpu.SMEM),
            _next_tokens_spec(n),
            tok(1, 2 * PEER_ROWS), tok(8, LANES),
            pl.BlockSpec((1, 1, 8, LANES), g2_map),
            pl.BlockSpec(tab.shape, lambda i: (0, 0), pipeline_mode=pl.Buffered(1)),
        ],
        out_specs=tok(8, LANES),
        out_shape=jax.ShapeDtypeStruct((n, 8, LANES), jnp.float32),
        scratch_shapes=[pltpu.VMEM((TOK_SET, N_CHUNK * TILE_STRIDE, LANES), jnp.uint32)] * 2,
        compiler_params=_params(("arbitrary",)),
        name="peer_combine",
    )(idx, idx, w, h8, g2, tab)


SC_LANES = 16
SC_WORKERS = 32
SC_ROWS = 32
SC_TOKENS = 2048


def peer_combine_sc(idx, w, tab):
    m = idx.shape[0]
    d = tab.shape[1]
    per = m // SC_WORKERS
    mesh = plsc.VectorSubcoreMesh(core_axis_name="c", subcore_axis_name="s")

    @functools.partial(
        pl.kernel, mesh=mesh,
        out_type=jax.ShapeDtypeStruct((m, d), jnp.float32),
        scratch_types=[
            pltpu.VMEM((PEER_ROWS,), jnp.int32),
            pltpu.VMEM((PEER_ROWS,), jnp.float32),
            pltpu.VMEM((SC_ROWS, d), jnp.float32),
            pltpu.VMEM((d,), jnp.float32),
        ],
        compiler_params=pltpu.CompilerParams(needs_layout_passes=False),
        name="peer_combine_sc",
    )
    def body(idx_hbm, w_hbm, tab_hbm, out_hbm, idx_v, w_v, rows_v, acc_v):
        wid = lax.axis_index("s") * 2 + lax.axis_index("c")

        @pl.loop(0, per)
        def _(i):
            t = wid * per + i
            pltpu.sync_copy(idx_hbm.at[t], idx_v)
            pltpu.sync_copy(w_hbm.at[t], w_v)
            for c in range(d // SC_LANES):
                acc_v[pl.ds(c * SC_LANES, SC_LANES)] = jnp.zeros((SC_LANES,), jnp.float32)
            for q in range(PEER_ROWS // SC_ROWS):
                pltpu.sync_copy(tab_hbm.at[idx_v.at[pl.ds(q * SC_ROWS, SC_ROWS)]], rows_v)

                @pl.loop(0, SC_ROWS)
                def _(j):
                    wj = plsc.load_gather(w_v, [jnp.full((SC_LANES,), q * SC_ROWS, jnp.int32) + j])
                    for c in range(d // SC_LANES):
                        sl = pl.ds(c * SC_LANES, SC_LANES)
                        acc_v[sl] = acc_v[sl] + wj * rows_v[j, sl]

            pltpu.sync_copy(acc_v, out_hbm.at[t])

    return body(idx, w, tab)


def _residual_kernel(h_ref, g_ref, a_ref, o_ref):
    o_ref[...] = h_ref[...] + g_ref[...] * a_ref[...]


def gated_residual(h, g, a):
    m, d = h.shape
    blk = pl.BlockSpec((256, d), lambda i: (i, 0))
    return pl.pallas_call(
        _residual_kernel, grid=(m // 256,),
        in_specs=[blk, pl.BlockSpec((1, d), lambda i: (0, 0)), blk],
        out_specs=blk, out_shape=jax.ShapeDtypeStruct((m, d), jnp.float32),
        compiler_params=_params(("parallel",)), name="gated_residual",
    )(h, g, a)


def _slots(w, width, offset=0):
    lead = w.shape[:-1]
    n = w.shape[-1] // width
    w = w.reshape(lead + (n, width))
    w = jnp.pad(w, [(0, 0)] * len(lead) + [(0, 0), (offset, LANES - width - offset)])
    return w.reshape(lead + (n * LANES,))


def _rope_tables(rot_dim, lane0, t, c):
    s = t - c
    q = rot_dim // 4
    pos = jnp.arange(s, dtype=jnp.float32)
    rows = jnp.floor(pos / GRID_W)
    cols = pos - rows * GRID_W
    inv = ROPE_BASE ** (-jnp.arange(q, dtype=jnp.float32) / q)
    ar = rows[:, None] * inv
    ac = cols[:, None] * inv
    zero = jnp.zeros_like(ar)
    cos = jnp.cos(jnp.concatenate([ar, ar, ac, ac], axis=-1))
    up = jnp.concatenate([-jnp.sin(ar), zero, -jnp.sin(ac), zero], axis=-1)
    dn = jnp.concatenate([zero, jnp.sin(ar), zero, jnp.sin(ac)], axis=-1)
    pad = lambda a, fill: jnp.pad(
        jnp.pad(a, ((0, 0), (lane0, LANES - lane0 - rot_dim)), constant_values=fill),
        ((c, 0), (0, 0)), constant_values=fill)
    cos = jnp.pad(jnp.pad(cos, ((0, 0), (lane0, LANES - lane0 - rot_dim)), constant_values=1.0),
                  ((c, 0), (0, 0)), constant_values=1.0)
    return jnp.stack([cos, pad(up, 0.0), pad(dn, 0.0)])


def kernel(x, c, ctx, c_ctx, ada_w, ada_b, norm1_g, norm2_g, w_in, mla_qa_g, mla_wuq, mla_kva_g, mla_wukv, mla_qn_g, mla_kn_g, swa_qn_g, swa_kn_g, swa_sink, w_out, peer_wq, peer_keys, peer_u, peer_v):
    b, s, d = x.shape
    nctx = ctx.shape[1]
    t = nctx + s
    depth = ada_w.shape[0]
    r = min(256, nctx)
    bf = jnp.bfloat16

    cond = jnp.zeros((16, d), jnp.float32).at[:b].set(c).at[b].set(c_ctx)
    mod_all = modulation(cond, ada_w, ada_b).reshape(depth, 16, N_MOD, d)
    rope_m = _rope_tables(MLA_ROPE, MLA_NOPE, t, nctx)
    rope_s = _rope_tables(SWA_DIM, 0, t, nctx)

    h = jnp.concatenate([ctx, x], axis=1)
    for l in range(depth):
        last = l == depth - 1
        with_ctx = not last
        mod = mod_all[l]
        wi = w_in[l]
        kv0 = Q_COLS + MLA_KV_RANK
        sk0 = kv0 + MLA_ROPE
        sv0 = sk0 + SWA_KV_HEADS * SWA_DIM
        sv = wi[:, sv0:].reshape(d, SWA_KV_HEADS, 1, SWA_DIM)
        win = jnp.concatenate([
            wi[:, :MLA_Q_RANK], wi[:, Q_COLS:kv0],
            _slots(wi[:, kv0:sk0], MLA_ROPE, MLA_NOPE),
            _slots(wi[:, MLA_Q_RANK:Q_COLS], SWA_DIM),
            _slots(wi[:, sk0:sv0], SWA_DIM),
            jnp.broadcast_to(sv, (d, SWA_KV_HEADS, 2, SWA_DIM)).reshape(d, SWA_KV_HEADS * LANES),
        ], axis=1).astype(bf)
        wuq = _slots(mla_wuq[l], MLA_QK).astype(bf)
        wukv = mla_wukv[l].reshape(MLA_KV_RANK, MLA_HEADS, MLA_NOPE + MLA_V)
        wuk = _slots(wukv[:, :, :MLA_NOPE].reshape(MLA_KV_RANK, -1), MLA_NOPE).astype(bf)
        wv = wukv[:, :, MLA_NOPE:].reshape(MLA_KV_RANK, MLA_HEADS // 2, 2, MLA_V)
        zero = jnp.zeros_like(wv[:, :, 0])
        wuv = jnp.stack([jnp.concatenate([wv[:, :, 0], zero], axis=-1),
                         jnp.concatenate([zero, wv[:, :, 1]], axis=-1)], axis=2)
        wuv = wuv.reshape(MLA_KV_RANK, MLA_HEADS * LANES).astype(bf)
        row = lambda g: g.reshape(1, -1)

        qm, km, vm, qs, ks, vlo, vhi = projections(
            h, mod, row(norm1_g[l]), win, row(mla_qa_g[l]), wuq, row(mla_kva_g[l]), wuk, wuv,
            row(_slots(mla_qn_g[l], MLA_QK)), row(_slots(mla_kn_g[l], MLA_QK)),
            row(_slots(swa_qn_g[l], SWA_DIM)), row(_slots(swa_kn_g[l], SWA_DIM)),
            rope_m, rope_s, r, nctx)
        om = mla_attention(qm, km, vm, r, nctx, with_ctx)
        osw = swa_attention(swa_sink[l], qs, ks, vlo, vhi, nctx, with_ctx)
        h1, x2, q16 = route(
            h, om, osw, w_out[l].astype(bf), mod, row(norm2_g[l]), peer_wq[l].astype(bf),
            r, nctx, with_ctx)
        idx, gate = peer_topk(q16, peer_keys[l].astype(bf))

        tl = h1.shape[1]
        n = b * tl
        x16 = x2.reshape(n, 16, LANES)
        gate = jnp.stack([jnp.zeros_like(gate), gate], axis=-1).reshape(n, 1, 2 * PEER_ROWS)
        w = peer_scores(idx, x16, gate, pack_table(peer_u, l))
        g2 = jnp.stack([jnp.broadcast_to(mod[b, 5], (b, d)), mod[:b, 5]], axis=1)
        n_tc = n - SC_TOKENS
        h1f = h1.reshape(n, d)
        experts = (idx[n_tc:] >> 2) + l * peer_v.shape[1]
        acc_sc = peer_combine_sc(experts, w[n_tc:, 0, 1::2], peer_v.reshape(-1, d))
        h_sc = gated_residual(h1f[n_tc:], mod[b - 1, 5].reshape(1, d), acc_sc)
        h_tc = peer_combine(idx[:n_tc], w[:n_tc], h1f[:n_tc].reshape(n_tc, 8, LANES),
                            g2.reshape(b, 2, 8, LANES), pack_table(peer_v, l), tl // TOK_BLK,
                            nctx // TOK_BLK if with_ctx else 0).reshape(n_tc, d)
        h = jnp.concatenate([h_tc, h_sc], axis=0).reshape(b, tl, d)
    return h
```

```python
import functools
import jax
import jax.numpy as jnp
from jax import lax
from jax.experimental import pallas as pl
from jax.experimental.pallas import tpu as pltpu
from jax.experimental.pallas import tpu_sc as plsc

LANES = 128
EPS = 1e-6
ROPE_BASE = 10000.0
GRID_W = 64
N_MOD = 6

MLA_HEADS = 8
MLA_NOPE = 64
MLA_ROPE = 32
MLA_QK = MLA_NOPE + MLA_ROPE
MLA_V = 64
MLA_Q_RANK = 384
MLA_KV_RANK = 256
SWA_HEADS = 8
SWA_KV_HEADS = 2
SWA_GROUP = SWA_HEADS // SWA_KV_HEADS
SWA_DIM = 64
WINDOW = 128
WIN_KEYS = 3 * WINDOW
Q_COLS = MLA_Q_RANK + SWA_HEADS * SWA_DIM

PEER_HEADS = 8
PEER_NKEYS = 128
PEER_DHALF = 128
PEER_TOPK = 16
PEER_ROWS = PEER_HEADS * PEER_TOPK
N_CHUNK = 4
TILE_STRIDE = PEER_ROWS + 1
TOPK_BLK = 1024
STAGE_STRIDE = PEER_NKEYS + 8
TOK_BLK = 128
TOK_SET = 8

OFF_QA = 0
OFF_KVA = OFF_QA + MLA_Q_RANK
OFF_KR = OFF_KVA + MLA_KV_RANK
OFF_SQ = OFF_KR + LANES
OFF_SK = OFF_SQ + SWA_HEADS * LANES
OFF_SV = OFF_SK + SWA_KV_HEADS * LANES
N_IN = OFF_SV + SWA_KV_HEADS * LANES

VMEM_LIMIT = 56 * 1024 * 1024
NEG_INF = float("-inf")
LOG2_E = 1.4426950408889634


def _params(sem, vmem=VMEM_LIMIT):
    return pltpu.CompilerParams(dimension_semantics=sem, vmem_limit_bytes=vmem)


def _rms(x, g, n):
    ms = jnp.sum(x * x, axis=-1, keepdims=True) * (1.0 / n)
    return x * lax.rsqrt(ms + EPS) * g


def _rope(x, cos, sin_up, sin_dn, shift):
    return (x * cos + pltpu.roll(x, LANES - shift, axis=1) * sin_up
            + pltpu.roll(x, shift, axis=1) * sin_dn)


def _mod_kernel(c_ref, w_ref, b_ref, o_ref):
    c = c_ref[...]
    s = c * (1.0 / (1.0 + jnp.exp(-c)))
    o_ref[0] = jnp.dot(s, w_ref[0], preferred_element_type=jnp.float32,
                       precision=lax.Precision.HIGHEST) + b_ref[0]


def modulation(cond, ada_w, ada_b):
    nl, d, n6 = ada_w.shape
    tn = 1536
    return pl.pallas_call(
        _mod_kernel,
        grid=(nl, n6 // tn),
        in_specs=[
            pl.BlockSpec(cond.shape, lambda l, j: (0, 0)),
            pl.BlockSpec((1, d, tn), lambda l, j: (l, 0, j)),
            pl.BlockSpec((1, 1, tn), lambda l, j: (l, 0, j)),
        ],
        out_specs=pl.BlockSpec((1, cond.shape[0], tn), lambda l, j: (l, 0, j)),
        out_shape=jax.ShapeDtypeStruct((nl, cond.shape[0], n6), jnp.float32),
        compiler_params=_params(("arbitrary", "arbitrary")),
        name="modulation",
    )(cond, ada_w, ada_b.reshape(nl, 1, n6))


def _proj_kernel(h_ref, mod_ref, n1_ref, win_ref, qag_ref, wuq_ref, kvag_ref, wuk_ref, wuv_ref,
                 qn_ref, kn_ref, sqn_ref, skn_ref, rm_ref, rs_ref,
                 qm_ref, km_ref, vm_ref, qs_ref, ks_ref, vlo_ref, vhi_ref):
    h = h_ref[0]
    d = h.shape[-1]
    mod = mod_ref[0]
    a = _rms(h, n1_ref[...], d) * (1.0 + mod[1:2]) + mod[0:1]
    p = jnp.dot(a.astype(jnp.bfloat16), win_ref[...], preferred_element_type=jnp.float32)

    cm, sm_up, sm_dn = rm_ref[0], rm_ref[1], rm_ref[2]
    cs, ss_up, ss_dn = rs_ref[0], rs_ref[1], rs_ref[2]

    qa = _rms(p[:, OFF_QA:OFF_QA + MLA_Q_RANK], qag_ref[...], MLA_Q_RANK)
    q = jnp.dot(qa.astype(jnp.bfloat16), wuq_ref[...], preferred_element_type=jnp.float32)
    kva = _rms(p[:, OFF_KVA:OFF_KVA + MLA_KV_RANK], kvag_ref[...], MLA_KV_RANK).astype(jnp.bfloat16)
    kn = jnp.dot(kva, wuk_ref[...], preferred_element_type=jnp.float32)
    vm = jnp.dot(kva, wuv_ref[...], preferred_element_type=jnp.float32)
    kr = p[:, OFF_KR:OFF_KR + LANES]
    for hd in range(MLA_HEADS):
        sl = slice(hd * LANES, (hd + 1) * LANES)
        qh = _rope(_rms(q[:, sl], qn_ref[...], MLA_QK), cm, sm_up, sm_dn, MLA_ROPE // 4)
        qm_ref[0, hd] = (qh * (MLA_QK ** -0.5 * LOG2_E)).astype(jnp.bfloat16)
        kh = _rope(_rms(kn[:, sl] + kr, kn_ref[...], MLA_QK), cm, sm_up, sm_dn, MLA_ROPE // 4)
        km_ref[0, hd] = kh.astype(jnp.bfloat16)
        vm_ref[0, hd] = vm[:, sl].astype(jnp.bfloat16)
    for hd in range(SWA_HEADS):
        x = p[:, OFF_SQ + hd * LANES:OFF_SQ + (hd + 1) * LANES]
        qh = _rope(_rms(x, sqn_ref[...], SWA_DIM), cs, ss_up, ss_dn, SWA_DIM // 4)
        qs_ref[0, hd] = (qh * (SWA_DIM ** -0.5)).astype(jnp.bfloat16)
    lane = lax.broadcasted_iota(jnp.int32, (h.shape[0], LANES), 1)
    for g in range(SWA_KV_HEADS):
        x = p[:, OFF_SK + g * LANES:OFF_SK + (g + 1) * LANES]
        kh = _rope(_rms(x, skn_ref[...], SWA_DIM), cs, ss_up, ss_dn, SWA_DIM // 4)
        ks_ref[0, g] = kh.astype(jnp.bfloat16)
        v = p[:, OFF_SV + g * LANES:OFF_SV + (g + 1) * LANES]
        vlo_ref[0, g] = jnp.where(lane < SWA_DIM, v, 0.0).astype(jnp.bfloat16)
        vhi_ref[0, g] = jnp.where(lane >= SWA_DIM, v, 0.0).astype(jnp.bfloat16)


def projections(h, mod, n1, win, qag, wuq, kvag, wuk, wuv, qn, kn, sqn, skn, rope_m, rope_s, r, c):
    b, t, d = h.shape
    nctx = c // r
    full = lambda a: pl.BlockSpec(a.shape, lambda i, j: (0,) * a.ndim)
    head_out = lambda nh: pl.BlockSpec((1, nh, r, LANES), lambda i, j: (i, 0, j, 0))
    head_shape = lambda nh: jax.ShapeDtypeStruct((b, nh, t, LANES), jnp.bfloat16)
    return pl.pallas_call(
        _proj_kernel,
        grid=(b, t // r),
        in_specs=[
            pl.BlockSpec((1, r, d), lambda i, j: (i, j, 0)),
            pl.BlockSpec((1, N_MOD, d), lambda i, j: (jnp.where(j < nctx, b, i), 0, 0)),
            full(n1), full(win), full(qag), full(wuq), full(kvag), full(wuk), full(wuv),
            full(qn), full(kn), full(sqn), full(skn),
            pl.BlockSpec((3, r, LANES), lambda i, j: (0, j, 0)),
            pl.BlockSpec((3, r, LANES), lambda i, j: (0, j, 0)),
        ],
        out_specs=[head_out(MLA_HEADS), head_out(MLA_HEADS), head_out(MLA_HEADS),
                   head_out(SWA_HEADS), head_out(SWA_KV_HEADS), head_out(SWA_KV_HEADS),
                   head_out(SWA_KV_HEADS)],
        out_shape=[head_shape(MLA_HEADS), head_shape(MLA_HEADS), head_shape(MLA_HEADS),
                   head_shape(SWA_HEADS), head_shape(SWA_KV_HEADS), head_shape(SWA_KV_HEADS),
                   head_shape(SWA_KV_HEADS)],
        compiler_params=_params(("parallel", "arbitrary")),
        name="projections",
    )(h, mod, n1, win, qag, wuq, kvag, wuk, wuv, qn, kn, sqn, skn, rope_m, rope_s)


def _mla_kernel(q_ref, k_ref, v_ref, o_ref, *, nctx_blocks, c):
    qi = pl.program_id(2)

    def attend(nk):
        acc = None
        for i in range(2):
            q = q_ref[0, i]
            s = lax.dot_general(q, k_ref[0, i, :nk], (((1,), (1,)), ((), ())),
                                preferred_element_type=jnp.float32)
            m = jnp.max(s, axis=-1, keepdims=True)
            p = jnp.exp2(s - m)
            l = jnp.sum(p, axis=-1, keepdims=True)
            o = jnp.dot(p.astype(jnp.bfloat16), v_ref[0, i, :nk], preferred_element_type=jnp.float32)
            o = o * (1.0 / l)
            acc = o if acc is None else acc + o
        o_ref[0] = acc.astype(o_ref.dtype)

    if nctx_blocks:
        @pl.when(qi < nctx_blocks)
        def _():
            attend(c)

        @pl.when(qi >= nctx_blocks)
        def _():
            attend(k_ref.shape[2])
    else:
        attend(k_ref.shape[2])


def mla_attention(qm, km, vm, tq, c, with_ctx):
    b, nh, t, _ = qm.shape
    off = 0 if with_ctx else c // tq
    nq = t // tq - off
    return pl.pallas_call(
        functools.partial(_mla_kernel, nctx_blocks=(c // tq if with_ctx else 0), c=c),
        grid=(b, nh // 2, nq),
        in_specs=[
            pl.BlockSpec((1, 2, tq, LANES), lambda i, hp, j: (i, hp, j + off, 0)),
            pl.BlockSpec((1, 2, t, LANES), lambda i, hp, j: (i, hp, 0, 0)),
            pl.BlockSpec((1, 2, t, LANES), lambda i, hp, j: (i, hp, 0, 0)),
        ],
        out_specs=pl.BlockSpec((1, tq, LANES), lambda i, hp, j: (i, j, hp)),
        out_shape=jax.ShapeDtypeStruct((b, nq * tq, nh // 2 * LANES), jnp.bfloat16),
        compiler_params=_params(("parallel", "arbitrary", "arbitrary")),
        name="mla_attention",
    )(qm, km, vm)


def _swa_kernel(sink_ref, q_ref, k_ref, vlo_ref, vhi_ref, o_ref, *, off, c):
    g = pl.program_id(1)
    qi = pl.program_id(2) + off
    t = k_ref.shape[2]
    rows = SWA_GROUP * WINDOW
    q = q_ref[0].reshape(rows, LANES)
    r_idx = lax.broadcasted_iota(jnp.int32, (rows, 1), 0)
    sink = jnp.zeros((rows, 1), jnp.float32)
    for i in range(SWA_GROUP):
        sink = jnp.where(r_idx // WINDOW == i, sink_ref[g * SWA_GROUP + i], sink)

    ws = pl.multiple_of(jnp.clip(qi * WINDOW - WINDOW, c, t - WIN_KEYS), WINDOW)
    nt = (((1,), (1,)), ((), ()))
    s_ctx = lax.dot_general(q, k_ref[0, 0, :c], nt, preferred_element_type=jnp.float32)
    s_loc = lax.dot_general(q, k_ref[0, 0, pl.ds(ws, WIN_KEYS)], nt, preferred_element_type=jnp.float32)
    qpos = qi * WINDOW + (r_idx % WINDOW)
    kpos = ws + lax.broadcasted_iota(jnp.int32, (1, WIN_KEYS), 1)
    reach = jnp.where(qi * WINDOW >= c, WINDOW, -1)
    s_loc = jnp.where(jnp.abs(qpos - kpos) <= reach, s_loc, NEG_INF)
    m = jnp.maximum(jnp.maximum(jnp.max(s_ctx, axis=-1, keepdims=True),
                                jnp.max(s_loc, axis=-1, keepdims=True)), sink)
    p_ctx = jnp.exp(s_ctx - m)
    p_loc = jnp.exp(s_loc - m)
    l = (jnp.sum(p_ctx, axis=-1, keepdims=True) + jnp.sum(p_loc, axis=-1, keepdims=True)
         + jnp.exp(sink - m))
    inv = 1.0 / l
    p_ctx = p_ctx.astype(jnp.bfloat16)
    p_loc = p_loc.astype(jnp.bfloat16)
    outs = []
    for pair in range(SWA_GROUP // 2):
        acc = None
        for i, v_ref in enumerate((vlo_ref, vhi_ref)):
            rs = slice((2 * pair + i) * WINDOW, (2 * pair + i + 1) * WINDOW)
            o = (jnp.dot(p_ctx[rs], v_ref[0, 0, :c], preferred_element_type=jnp.float32)
                 + jnp.dot(p_loc[rs], v_ref[0, 0, pl.ds(ws, WIN_KEYS)], preferred_element_type=jnp.float32))
            o = o * inv[rs]
            acc = o if acc is None else acc + o
        outs.append(acc)
    o_ref[0] = jnp.concatenate(outs, axis=-1).astype(o_ref.dtype)


def swa_attention(sink, qs, ks, vlo, vhi, c, with_ctx):
    b, nh, t, _ = qs.shape
    off = 0 if with_ctx else c // WINDOW
    nq = t // WINDOW - off
    kv_spec = pl.BlockSpec((1, 1, t, LANES), lambda i, g, j: (i, g, 0, 0))
    return pl.pallas_call(
        functools.partial(_swa_kernel, off=off, c=c),
        grid=(b, SWA_KV_HEADS, nq),
        in_specs=[
            pl.BlockSpec(memory_space=pltpu.SMEM),
            pl.BlockSpec((1, SWA_GROUP, WINDOW, LANES), lambda i, g, j: (i, g, j + off, 0)),
            kv_spec, kv_spec, kv_spec,
        ],
        out_specs=pl.BlockSpec((1, WINDOW, SWA_GROUP // 2 * LANES), lambda i, g, j: (i, j, g)),
        out_shape=jax.ShapeDtypeStruct((b, nq * WINDOW, SWA_HEADS // 2 * LANES), jnp.bfloat16),
        compiler_params=_params(("parallel", "arbitrary", "arbitrary")),
        name="swa_attention",
    )(sink, qs, ks, vlo, vhi)


def _tree(op, xs):
    xs = list(xs)
    while len(xs) > 1:
        xs = [op(xs[i], xs[i + 1]) for i in range(0, len(xs) - 1, 2)] + (xs[-1:] if len(xs) % 2 else [])
    return xs[0]


def _top16_sweeps(ids, *problems):
    n = len(ids)
    big = float(max(ids) + 1)
    order = sorted(range(n), key=lambda k: ids[k])
    runs = [order[i:i + PEER_TOPK] for i in range(0, n, PEER_TOPK)]

    def step(r, carry):
        for val_ref, out_v, out_i in problems:
            m = _tree(jnp.maximum, [val_ref[k] for k in range(n)])
            firsts = []
            for run in runs:
                am = jnp.full(m.shape, big, jnp.float32)
                for k in reversed(run):
                    am = jnp.where(val_ref[k] == m, float(ids[k]), am)
                firsts.append(am)
            am = _tree(jnp.minimum, firsts)
            for k in range(n):
                val_ref[k] = jnp.where(am == float(ids[k]), NEG_INF, val_ref[k])
            out_v[r] = m
            out_i[r] = am
        return carry

    lax.fori_loop(0, PEER_TOPK, step, 0)


PAIRS = [(a, b) for a in range(PEER_TOPK) for b in range(PEER_TOPK) if (a + 1) * (b + 1) <= PEER_TOPK]


def _route_kernel(h_ref, om_ref, os_ref, wo_ref, mod_ref, n2_ref, wq_ref,
                  h1_ref, x2_ref, q_ref):
    h = h_ref[0]
    d = h.shape[-1]
    mod = mod_ref[0]
    half = om_ref.shape[-1]
    mix = (jnp.dot(om_ref[0], wo_ref[:half], preferred_element_type=jnp.float32)
           + jnp.dot(os_ref[0], wo_ref[half:], preferred_element_type=jnp.float32))
    h1 = h + mod[2:3] * mix
    h1_ref[0] = h1
    x = _rms(h1, n2_ref[...], d) * (1.0 + mod[4:5]) + mod[3:4]
    xhi = x.astype(jnp.bfloat16)
    x2_ref[0, :, :d] = xhi
    x2_ref[0, :, d:] = (x - xhi.astype(jnp.float32)).astype(jnp.bfloat16)
    q = jnp.dot(xhi, wq_ref[...], preferred_element_type=jnp.float32)
    for k in range(2 * PEER_HEADS):
        q_ref[k] = q[:, k * PEER_DHALF:(k + 1) * PEER_DHALF].astype(jnp.bfloat16)


def _topk_kernel(q_ref, keys_ref, idx_ref, g_ref, stage, vals, sv, si, cand, cv, ci, out_i, out_g):
    groups = vals.shape[2]

    def head(hh, carry):
        for part in range(2):
            st = lax.dot_general(keys_ref[hh, part], q_ref[2 * hh + part], (((1,), (1,)), ((), ())),
                                 preferred_element_type=jnp.float32)
            for g in range(groups):
                stage[pl.ds(g * STAGE_STRIDE, PEER_NKEYS), :] = st[:, g * LANES:(g + 1) * LANES]
            for k in range(PEER_NKEYS):
                vals[part, k] = stage[pl.ds(k, groups, stride=STAGE_STRIDE), :]
        _top16_sweeps(list(range(PEER_NKEYS)), *[(vals.at[p], sv.at[p], si.at[p]) for p in range(2)])
        for n, (a, b) in enumerate(PAIRS):
            cand[n] = sv[0, a] + sv[1, b]
        _top16_sweeps([a * PEER_TOPK + b for a, b in PAIRS], (cand, cv, ci))
        top = cv[0]
        e = [jnp.exp(cv[r] - top) for r in range(PEER_TOPK)]
        inv = 1.0 / _tree(jnp.add, e)
        for r in range(PEER_TOPK):
            ia = jnp.floor(ci[r] * (1.0 / PEER_TOPK))
            ib = ci[r] - ia * PEER_TOPK
            i1 = jnp.zeros_like(ia)
            i2 = jnp.zeros_like(ia)
            for a in range(PEER_TOPK):
                i1 = jnp.where(ia == a, si[0, a], i1)
                i2 = jnp.where(ib == a, si[1, a], i2)
            out_i[hh * PEER_TOPK + r] = (i1 * PEER_NKEYS + i2) * N_CHUNK
            out_g[hh * PEER_TOPK + r] = e[r] * inv
        return carry

    lax.fori_loop(0, PEER_HEADS, head, 0)
    for g in range(groups):
        rows = slice(g * LANES, (g + 1) * LANES)
        idx_ref[rows, :] = out_i[:, g, :].T.astype(jnp.int32)
        g_ref[rows, :] = out_g[:, g, :].T


def peer_topk(q16, keys):
    n = q16.shape[1]
    tb = next(t for t in (TOPK_BLK, TOPK_BLK // 2, TOPK_BLK // 4, LANES) if n % t == 0)
    groups = tb // LANES
    blk = lambda dt: pltpu.VMEM((PEER_TOPK, groups, LANES), dt)
    return pl.pallas_call(
        _topk_kernel,
        grid=(n // tb,),
        in_specs=[
            pl.BlockSpec((2 * PEER_HEADS, tb, PEER_DHALF), lambda i: (0, i, 0)),
            pl.BlockSpec(keys.shape, lambda i: (0, 0, 0, 0)),
        ],
        out_specs=[pl.BlockSpec((tb, PEER_ROWS), lambda i: (i, 0))] * 2,
        out_shape=[jax.ShapeDtypeStruct((n, PEER_ROWS), jnp.int32),
                   jax.ShapeDtypeStruct((n, PEER_ROWS), jnp.float32)],
        scratch_shapes=[
            pltpu.VMEM((groups * STAGE_STRIDE, LANES), jnp.float32),
            pltpu.VMEM((2, PEER_NKEYS, groups, LANES), jnp.float32),
            pltpu.VMEM((2, PEER_TOPK, groups, LANES), jnp.float32),
            pltpu.VMEM((2, PEER_TOPK, groups, LANES), jnp.float32),
            pltpu.VMEM((len(PAIRS), groups, LANES), jnp.float32),
            blk(jnp.float32), blk(jnp.float32),
            pltpu.VMEM((PEER_ROWS, groups, LANES), jnp.float32),
            pltpu.VMEM((PEER_ROWS, groups, LANES), jnp.float32),
        ],
        compiler_params=_params(("arbitrary",)),
        name="peer_topk",
    )(q16, keys)


def route(h, om, osw, wo, mod, n2, wq, r, c, with_ctx):
    b, t, d = h.shape
    off = 0 if with_ctx else c // r
    nblk = t // r - off
    nctx = c // r
    tq = nblk * r
    full = lambda a: pl.BlockSpec(a.shape, lambda i, j: (0,) * a.ndim)
    row_spec = lambda w: pl.BlockSpec((1, r, w), lambda i, j: (i, j, 0))
    return pl.pallas_call(
        _route_kernel,
        grid=(b, nblk),
        in_specs=[
            pl.BlockSpec((1, r, d), lambda i, j: (i, j + off, 0)),
            row_spec(om.shape[-1]), row_spec(osw.shape[-1]),
            full(wo),
            pl.BlockSpec((1, N_MOD, d), lambda i, j: (jnp.where(j + off < nctx, b, i), 0, 0)),
            full(n2), full(wq),
        ],
        out_specs=[row_spec(d), row_spec(2 * d),
                   pl.BlockSpec((2 * PEER_HEADS, r, PEER_DHALF), lambda i, j: (0, i * nblk + j, 0))],
        out_shape=[
            jax.ShapeDtypeStruct((b, tq, d), jnp.float32),
            jax.ShapeDtypeStruct((b, tq, 2 * d), jnp.bfloat16),
            jax.ShapeDtypeStruct((2 * PEER_HEADS, b * tq, PEER_DHALF), jnp.bfloat16),
        ],
        compiler_params=_params(("parallel", "arbitrary")),
        name="peer_route",
    )(h, om, osw, wo, mod, n2, wq)


def _pack_kernel(x_ref, o_ref):
    x = x_ref[0]
    half = x.shape[1] // 2
    bits = lambda v: pltpu.bitcast(v.astype(jnp.bfloat16).astype(jnp.float32), jnp.uint32)
    w = (bits(x[:, :half]) & jnp.uint32(0xFFFF0000)) | (bits(x[:, half:]) >> 16)
    for c in range(N_CHUNK):
        o_ref[pl.ds(c, x.shape[0], stride=N_CHUNK), :] = w[:, c * LANES:(c + 1) * LANES]


def pack_table(tabs, l):
    _, e, d = tabs.shape
    be = 512
    return pl.pallas_call(
        _pack_kernel,
        grid=(e // be,),
        in_specs=[pl.BlockSpec((1, be, d), lambda i: (l, i, 0))],
        out_specs=pl.BlockSpec((be * N_CHUNK, LANES), lambda i: (i, 0)),
        out_shape=jax.ShapeDtypeStruct((e * N_CHUNK, LANES), jnp.uint32),
        compiler_params=_params(("parallel",)),
        name="pack_table",
    )(tabs)


def _gather_rows(idx_ref, tab_ref, tile_ref, t):
    for j in range(PEER_ROWS):
        r = pl.multiple_of(idx_ref[t, j], N_CHUNK)
        tile_ref[pl.ds(j, N_CHUNK, stride=TILE_STRIDE), :] = tab_ref[pl.ds(r, N_CHUNK), :]


def _token_pipeline(gather, gather_next, compute, tiles):
    steps = TOK_BLK // TOK_SET

    @pl.when(pl.program_id(0) == 0)
    def _():
        for k in range(TOK_SET):
            gather(tiles[0].at[k], k)

    def step(t, cur, nxt, fill):
        for k in range(TOK_SET):
            compute(cur.at[k], t + k)
            fill(nxt.at[k], k)

    def body(i, carry):
        t = TOK_SET * i
        ahead = lambda tile, k: gather(tile, t + TOK_SET + k)

        @pl.when(i % 2 == 0)
        def _():
            step(t, tiles[0], tiles[1], ahead)

        @pl.when(i % 2 == 1)
        def _():
            step(t, tiles[1], tiles[0], ahead)

        return carry

    lax.fori_loop(0, steps - 1, body, 0)
    step(TOK_BLK - TOK_SET, tiles[1], tiles[0], gather_next)


def _chunk(tile_ref, c):
    return pltpu.bitcast(tile_ref[pl.ds(c * TILE_STRIDE, PEER_ROWS), :], jnp.bfloat16)


def _u_kernel(idx_ref, nxt_ref, x_ref, g_ref, tab_ref, w_ref, *tiles):
    row = lax.broadcasted_iota(jnp.int32, (16, 2 * PEER_ROWS), 0) & 7

    def scores(tile_ref, t):
        x16 = x_ref[t]
        top = jnp.zeros((16, 2 * PEER_ROWS), jnp.float32)
        bot = jnp.zeros((16, 2 * PEER_ROWS), jnp.float32)
        for c in range(N_CHUNK):
            y = lax.dot_general(x16, _chunk(tile_ref, c), (((1,), (1,)), ((), ())),
                                preferred_element_type=jnp.float32)
            top = top + jnp.where(row == c, y, 0.0)
            bot = bot + jnp.where(row == c + N_CHUNK, y, 0.0)
        tot = top + pltpu.roll(bot, 1, axis=1)
        a = jnp.sum(tot, axis=0, keepdims=True)
        gelu = 0.5 * a * (1.0 + lax.erf(a * (2.0 ** -0.5)))
        w_ref[t] = g_ref[t] * gelu

    _token_pipeline(functools.partial(_gather_rows, idx_ref, tab_ref),
                    functools.partial(_gather_rows, nxt_ref, tab_ref), scores, tiles)


def _next_tokens_spec(n):
    per_blk = TOK_BLK // TOK_SET
    return pl.BlockSpec((TOK_SET, PEER_ROWS),
                        lambda i: (jnp.minimum((i + 1) * per_blk, n // TOK_SET - 1), 0),
                        memory_space=pltpu.SMEM)


def peer_scores(idx, x16, g, tab):
    n = idx.shape[0]
    tok = lambda *s: pl.BlockSpec((TOK_BLK,) + s, lambda i: (i,) + (0,) * len(s))
    return pl.pallas_call(
        _u_kernel,
        grid=(n // TOK_BLK,),
        in_specs=[
            pl.BlockSpec((TOK_BLK, PEER_ROWS), lambda i: (i, 0), memory_space=pltpu.SMEM),
            _next_tokens_spec(n),
            tok(16, LANES), tok(1, 2 * PEER_ROWS),
            pl.BlockSpec(tab.shape, lambda i: (0, 0), pipeline_mode=pl.Buffered(1)),
        ],
        out_specs=tok(1, 2 * PEER_ROWS),
        out_shape=jax.ShapeDtypeStruct((n, 1, 2 * PEER_ROWS), jnp.float32),
        scratch_shapes=[pltpu.VMEM((TOK_SET, N_CHUNK * TILE_STRIDE, LANES), jnp.uint32)] * 2,
        compiler_params=_params(("arbitrary",)),
        name="peer_scores",
    )(idx, idx, x16, g, tab)


def _v_kernel(idx_ref, nxt_ref, w_ref, h_ref, g2_ref, tab_ref, o_ref, *tiles):
    row = lax.broadcasted_iota(jnp.int32, (16, 2 * PEER_ROWS), 0)

    def combine(tile_ref, t):
        w = w_ref[t]
        w_hi = w.astype(jnp.bfloat16).astype(jnp.float32)
        w_lo = w - w_hi
        parts = (w_hi, pltpu.roll(w_hi, 2 * PEER_ROWS - 1, axis=1),
                 w_lo, pltpu.roll(w_lo, 2 * PEER_ROWS - 1, axis=1))
        acc = jnp.zeros((16, LANES), jnp.float32)
        for c in range(N_CHUNK):
            lhs = jnp.zeros((16, 2 * PEER_ROWS), jnp.float32)
            for k, part in enumerate(parts):
                lhs = jnp.where(row == c + N_CHUNK * k, part, lhs)
            acc = acc + jnp.dot(lhs.astype(jnp.bfloat16), _chunk(tile_ref, c),
                                preferred_element_type=jnp.float32)
        o_ref[t] = h_ref[t] + g2_ref[0, 0] * (acc[:8] + acc[8:])

    _token_pipeline(functools.partial(_gather_rows, idx_ref, tab_ref),
                    functools.partial(_gather_rows, nxt_ref, tab_ref), combine, tiles)


def peer_combine(idx, w, h8, g2, tab, blocks_per_sample, ctx_blocks):
    n = idx.shape[0]
    tok = lambda *s: pl.BlockSpec((TOK_BLK,) + s, lambda i: (i,) + (0,) * len(s))
    g2_map = lambda i: (i // blocks_per_sample,
                        jnp.where(i % blocks_per_sample < ctx_blocks, 0, 1), 0, 0)
    return pl.pallas_call(
        _v_kernel,
        grid=(n // TOK_BLK,),
        in_specs=[
            pl.BlockSpec((TOK_BLK, PEER_ROWS), lambda i: (i, 0), memory_space=pltpu.SMEM),
            _next_tokens_spec(n),
            tok(1, 2 * PEER_ROWS), tok(8, LANES),
            pl.BlockSpec((1, 1, 8, LANES), g2_map),
            pl.BlockSpec(tab.shape, lambda i: (0, 0), pipeline_mode=pl.Buffered(1)),
        ],
        out_specs=tok(8, LANES),
        out_shape=jax.ShapeDtypeStruct((n, 8, LANES), jnp.float32),
        scratch_shapes=[pltpu.VMEM((TOK_SET, N_CHUNK * TILE_STRIDE, LANES), jnp.uint32)] * 2,
        compiler_params=_params(("arbitrary",)),
        name="peer_combine",
    )(idx, idx, w, h8, g2, tab)


SC_LANES = 16
SC_WORKERS = 32
SC_ROWS = 32
SC_TOKENS = 4096


def peer_combine_sc(idx, w, tab):
    m = idx.shape[0]
    d = tab.shape[1]
    per = m // SC_WORKERS
    mesh = plsc.VectorSubcoreMesh(core_axis_name="c", subcore_axis_name="s")

    quarters = PEER_ROWS // SC_ROWS
    group = 16

    @functools.partial(
        pl.kernel, mesh=mesh,
        out_type=jax.ShapeDtypeStruct((m, d), jnp.float32),
        scratch_types=[
            pltpu.VMEM((2, PEER_ROWS), jnp.int32),
            pltpu.VMEM((PEER_ROWS,), jnp.float32),
            pltpu.VMEM((2, SC_ROWS, d), jnp.float32),
            pltpu.VMEM((d,), jnp.float32),
            pltpu.SemaphoreType.DMA((2,)),
        ],
        compiler_params=pltpu.CompilerParams(needs_layout_passes=False),
        name="peer_combine_sc",
    )
    def body(idx_hbm, w_hbm, tab_hbm, out_hbm, idx_v, w_v, rows_v, acc_v, sems):
        base = (lax.axis_index("s") * 2 + lax.axis_index("c")) * per

        def gather(slot, q, buf):
            rows = idx_v.at[slot, pl.ds(q * SC_ROWS, SC_ROWS)]
            return pltpu.make_async_copy(tab_hbm.at[rows], rows_v.at[buf], sems.at[buf])

        def accumulate(buf, q):
            for g0 in range(0, d // SC_LANES, group):
                lanes = [pl.ds((g0 + c) * SC_LANES, SC_LANES) for c in range(group)]

                def row(j, accs):
                    wj = plsc.load_gather(w_v, [jnp.full((SC_LANES,), q * SC_ROWS, jnp.int32) + j])
                    return tuple(a + wj * rows_v[buf, j, sl] for a, sl in zip(accs, lanes))

                accs = lax.fori_loop(0, SC_ROWS, row, tuple(acc_v[sl] for sl in lanes))
                for sl, a in zip(lanes, accs):
                    acc_v[sl] = a

        pltpu.sync_copy(idx_hbm.at[base], idx_v.at[0])
        gather(0, 0, 0).start()

        @pl.loop(0, per, step=2)
        def _(i):
            for p in range(2):
                t = base + i + p
                pltpu.sync_copy(w_hbm.at[t], w_v)
                for c in range(d // SC_LANES):
                    acc_v[pl.ds(c * SC_LANES, SC_LANES)] = jnp.zeros((SC_LANES,), jnp.float32)
                for q in range(quarters):
                    buf = q % 2
                    if q + 1 < quarters:
                        gather(p, q + 1, 1 - buf).start()
                    else:
                        nxt = jnp.minimum(t + 1, base + per - 1)
                        pltpu.sync_copy(idx_hbm.at[nxt], idx_v.at[1 - p])
                        gather(1 - p, 0, 1 - buf).start()
                    gather(p, q, buf).wait()
                    accumulate(buf, q)
                pltpu.sync_copy(acc_v, out_hbm.at[t])

        gather(0, 0, 0).wait()

    return body(idx, w, tab)


def _residual_kernel(h_ref, g_ref, a_ref, o_ref):
    o_ref[...] = h_ref[...] + g_ref[...] * a_ref[...]


def gated_residual(h, g, a):
    m, d = h.shape
    blk = pl.BlockSpec((256, d), lambda i: (i, 0))
    return pl.pallas_call(
        _residual_kernel, grid=(m // 256,),
        in_specs=[blk, pl.BlockSpec((1, d), lambda i: (0, 0)), blk],
        out_specs=blk, out_shape=jax.ShapeDtypeStruct((m, d), jnp.float32),
        compiler_params=_params(("parallel",)), name="gated_residual",
    )(h, g, a)


def _slots(w, width, offset=0):
    lead = w.shape[:-1]
    n = w.shape[-1] // width
    w = w.reshape(lead + (n, width))
    w = jnp.pad(w, [(0, 0)] * len(lead) + [(0, 0), (offset, LANES - width - offset)])
    return w.reshape(lead + (n * LANES,))


def _rope_tables(rot_dim, lane0, t, c):
    s = t - c
    q = rot_dim // 4
    pos = jnp.arange(s, dtype=jnp.float32)
    rows = jnp.floor(pos / GRID_W)
    cols = pos - rows * GRID_W
    inv = ROPE_BASE ** (-jnp.arange(q, dtype=jnp.float32) / q)
    ar = rows[:, None] * inv
    ac = cols[:, None] * inv
    zero = jnp.zeros_like(ar)
    cos = jnp.cos(jnp.concatenate([ar, ar, ac, ac], axis=-1))
    up = jnp.concatenate([-jnp.sin(ar), zero, -jnp.sin(ac), zero], axis=-1)
    dn = jnp.concatenate([zero, jnp.sin(ar), zero, jnp.sin(ac)], axis=-1)
    pad = lambda a, fill: jnp.pad(
        jnp.pad(a, ((0, 0), (lane0, LANES - lane0 - rot_dim)), constant_values=fill),
        ((c, 0), (0, 0)), constant_values=fill)
    cos = jnp.pad(jnp.pad(cos, ((0, 0), (lane0, LANES - lane0 - rot_dim)), constant_values=1.0),
                  ((c, 0), (0, 0)), constant_values=1.0)
    return jnp.stack([cos, pad(up, 0.0), pad(dn, 0.0)])


def kernel(x, c, ctx, c_ctx, ada_w, ada_b, norm1_g, norm2_g, w_in, mla_qa_g, mla_wuq, mla_kva_g, mla_wukv, mla_qn_g, mla_kn_g, swa_qn_g, swa_kn_g, swa_sink, w_out, peer_wq, peer_keys, peer_u, peer_v):
    b, s, d = x.shape
    nctx = ctx.shape[1]
    t = nctx + s
    depth = ada_w.shape[0]
    r = min(256, nctx)
    bf = jnp.bfloat16

    cond = jnp.zeros((16, d), jnp.float32).at[:b].set(c).at[b].set(c_ctx)
    mod_all = modulation(cond, ada_w, ada_b).reshape(depth, 16, N_MOD, d)
    rope_m = _rope_tables(MLA_ROPE, MLA_NOPE, t, nctx)
    rope_s = _rope_tables(SWA_DIM, 0, t, nctx)

    h = jnp.concatenate([ctx, x], axis=1)
    for l in range(depth):
        last = l == depth - 1
        with_ctx = not last
        mod = mod_all[l]
        wi = w_in[l]
        kv0 = Q_COLS + MLA_KV_RANK
        sk0 = kv0 + MLA_ROPE
        sv0 = sk0 + SWA_KV_HEADS * SWA_DIM
        sv = wi[:, sv0:].reshape(d, SWA_KV_HEADS, 1, SWA_DIM)
        win = jnp.concatenate([
            wi[:, :MLA_Q_RANK], wi[:, Q_COLS:kv0],
            _slots(wi[:, kv0:sk0], MLA_ROPE, MLA_NOPE),
            _slots(wi[:, MLA_Q_RANK:Q_COLS], SWA_DIM),
            _slots(wi[:, sk0:sv0], SWA_DIM),
            jnp.broadcast_to(sv, (d, SWA_KV_HEADS, 2, SWA_DIM)).reshape(d, SWA_KV_HEADS * LANES),
        ], axis=1).astype(bf)
        wuq = _slots(mla_wuq[l], MLA_QK).astype(bf)
        wukv = mla_wukv[l].reshape(MLA_KV_RANK, MLA_HEADS, MLA_NOPE + MLA_V)
        wuk = _slots(wukv[:, :, :MLA_NOPE].reshape(MLA_KV_RANK, -1), MLA_NOPE).astype(bf)
        wv = wukv[:, :, MLA_NOPE:].reshape(MLA_KV_RANK, MLA_HEADS // 2, 2, MLA_V)
        zero = jnp.zeros_like(wv[:, :, 0])
        wuv = jnp.stack([jnp.concatenate([wv[:, :, 0], zero], axis=-1),
                         jnp.concatenate([zero, wv[:, :, 1]], axis=-1)], axis=2)
        wuv = wuv.reshape(MLA_KV_RANK, MLA_HEADS * LANES).astype(bf)
        row = lambda g: g.reshape(1, -1)

        qm, km, vm, qs, ks, vlo, vhi = projections(
            h, mod, row(norm1_g[l]), win, row(mla_qa_g[l]), wuq, row(mla_kva_g[l]), wuk, wuv,
            row(_slots(mla_qn_g[l], MLA_QK)), row(_slots(mla_kn_g[l], MLA_QK)),
            row(_slots(swa_qn_g[l], SWA_DIM)), row(_slots(swa_kn_g[l], SWA_DIM)),
            rope_m, rope_s, r, nctx)
        om = mla_attention(qm, km, vm, r, nctx, with_ctx)
        osw = swa_attention(swa_sink[l], qs, ks, vlo, vhi, nctx, with_ctx)
        h1, x2, q16 = route(
            h, om, osw, w_out[l].astype(bf), mod, row(norm2_g[l]), peer_wq[l].astype(bf),
            r, nctx, with_ctx)
        idx, gate = peer_topk(q16, peer_keys[l].astype(bf))

        tl = h1.shape[1]
        n = b * tl
        x16 = x2.reshape(n, 16, LANES)
        gate = jnp.stack([jnp.zeros_like(gate), gate], axis=-1).reshape(n, 1, 2 * PEER_ROWS)
        w = peer_scores(idx, x16, gate, pack_table(peer_u, l))
        g2 = jnp.stack([jnp.broadcast_to(mod[b, 5], (b, d)), mod[:b, 5]], axis=1)
        n_tc = n - SC_TOKENS
        h1f = h1.reshape(n, d)
        experts = (idx[n_tc:] >> 2) + l * peer_v.shape[1]
        acc_sc = peer_combine_sc(experts, w[n_tc:, 0, 1::2], peer_v.reshape(-1, d))
        h_sc = gated_residual(h1f[n_tc:], mod[b - 1, 5].reshape(1, d), acc_sc)
        h_tc = peer_combine(idx[:n_tc], w[:n_tc], h1f[:n_tc].reshape(n_tc, 8, LANES),
                            g2.reshape(b, 2, 8, LANES), pack_table(peer_v, l), tl // TOK_BLK,
                            nctx // TOK_BLK if with_ctx else 0).reshape(n_tc, d)
        h = jnp.concatenate([h_tc, h_sc], axis=0).reshape(b, tl, d)
    return h
```

```python
import functools
import jax
import jax.numpy as jnp
from jax import lax
from jax.experimental import pallas as pl
from jax.experimental.pallas import tpu as pltpu
from jax.experimental.pallas import tpu_sc as plsc

LANES = 128
EPS = 1e-6
ROPE_BASE = 10000.0
GRID_W = 64
N_MOD = 6

MLA_HEADS = 8
MLA_NOPE = 64
MLA_ROPE = 32
MLA_QK = MLA_NOPE + MLA_ROPE
MLA_V = 64
MLA_Q_RANK = 384
MLA_KV_RANK = 256
SWA_HEADS = 8
SWA_KV_HEADS = 2
SWA_GROUP = SWA_HEADS // SWA_KV_HEADS
SWA_DIM = 64
WINDOW = 128
WIN_KEYS = 3 * WINDOW
Q_COLS = MLA_Q_RANK + SWA_HEADS * SWA_DIM

PEER_HEADS = 8
PEER_NKEYS = 128
PEER_DHALF = 128
PEER_TOPK = 16
PEER_ROWS = PEER_HEADS * PEER_TOPK
N_CHUNK = 4
TILE_STRIDE = PEER_ROWS + 1
TOPK_BLK = 1024
STAGE_STRIDE = PEER_NKEYS + 8
TOK_BLK = 128
TOK_SET = 8

OFF_QA = 0
OFF_KVA = OFF_QA + MLA_Q_RANK
OFF_KR = OFF_KVA + MLA_KV_RANK
OFF_SQ = OFF_KR + LANES
OFF_SK = OFF_SQ + SWA_HEADS * LANES
OFF_SV = OFF_SK + SWA_KV_HEADS * LANES
N_IN = OFF_SV + SWA_KV_HEADS * LANES

VMEM_LIMIT = 56 * 1024 * 1024
NEG_INF = float("-inf")
LOG2_E = 1.4426950408889634


def _params(sem, vmem=VMEM_LIMIT):
    return pltpu.CompilerParams(dimension_semantics=sem, vmem_limit_bytes=vmem)


def _rms(x, g, n):
    ms = jnp.sum(x * x, axis=-1, keepdims=True) * (1.0 / n)
    return x * lax.rsqrt(ms + EPS) * g


def _rope(x, cos, sin_up, sin_dn, shift):
    return (x * cos + pltpu.roll(x, LANES - shift, axis=1) * sin_up
            + pltpu.roll(x, shift, axis=1) * sin_dn)


def _mod_kernel(c_ref, w_ref, b_ref, o_ref):
    c = c_ref[...]
    s = c * (1.0 / (1.0 + jnp.exp(-c)))
    o_ref[0] = jnp.dot(s, w_ref[0], preferred_element_type=jnp.float32,
                       precision=lax.Precision.HIGHEST) + b_ref[0]


def modulation(cond, ada_w, ada_b):
    nl, d, n6 = ada_w.shape
    tn = 1536
    return pl.pallas_call(
        _mod_kernel,
        grid=(nl, n6 // tn),
        in_specs=[
            pl.BlockSpec(cond.shape, lambda l, j: (0, 0)),
            pl.BlockSpec((1, d, tn), lambda l, j: (l, 0, j)),
            pl.BlockSpec((1, 1, tn), lambda l, j: (l, 0, j)),
        ],
        out_specs=pl.BlockSpec((1, cond.shape[0], tn), lambda l, j: (l, 0, j)),
        out_shape=jax.ShapeDtypeStruct((nl, cond.shape[0], n6), jnp.float32),
        compiler_params=_params(("arbitrary", "arbitrary")),
        name="modulation",
    )(cond, ada_w, ada_b.reshape(nl, 1, n6))


def _proj_kernel(h_ref, mod_ref, n1_ref, win_ref, qag_ref, wuq_ref, kvag_ref, wuk_ref, wuv_ref,
                 qn_ref, kn_ref, sqn_ref, skn_ref, rm_ref, rs_ref,
                 qm_ref, km_ref, vm_ref, qs_ref, ks_ref, vlo_ref, vhi_ref):
    h = h_ref[0]
    d = h.shape[-1]
    mod = mod_ref[0]
    a = _rms(h, n1_ref[...], d) * (1.0 + mod[1:2]) + mod[0:1]
    p = jnp.dot(a.astype(jnp.bfloat16), win_ref[...], preferred_element_type=jnp.float32)

    cm, sm_up, sm_dn = rm_ref[0], rm_ref[1], rm_ref[2]
    cs, ss_up, ss_dn = rs_ref[0], rs_ref[1], rs_ref[2]

    qa = _rms(p[:, OFF_QA:OFF_QA + MLA_Q_RANK], qag_ref[...], MLA_Q_RANK)
    q = jnp.dot(qa.astype(jnp.bfloat16), wuq_ref[...], preferred_element_type=jnp.float32)
    kva = _rms(p[:, OFF_KVA:OFF_KVA + MLA_KV_RANK], kvag_ref[...], MLA_KV_RANK).astype(jnp.bfloat16)
    kn = jnp.dot(kva, wuk_ref[...], preferred_element_type=jnp.float32)
    vm = jnp.dot(kva, wuv_ref[...], preferred_element_type=jnp.float32)
    kr = p[:, OFF_KR:OFF_KR + LANES]
    for hd in range(MLA_HEADS):
        sl = slice(hd * LANES, (hd + 1) * LANES)
        qh = _rope(_rms(q[:, sl], qn_ref[...], MLA_QK), cm, sm_up, sm_dn, MLA_ROPE // 4)
        qm_ref[0, hd] = (qh * (MLA_QK ** -0.5 * LOG2_E)).astype(jnp.bfloat16)
        kh = _rope(_rms(kn[:, sl] + kr, kn_ref[...], MLA_QK), cm, sm_up, sm_dn, MLA_ROPE // 4)
        km_ref[0, hd] = kh.astype(jnp.bfloat16)
        vm_ref[0, hd] = vm[:, sl].astype(jnp.bfloat16)
    for hd in range(SWA_HEADS):
        x = p[:, OFF_SQ + hd * LANES:OFF_SQ + (hd + 1) * LANES]
        qh = _rope(_rms(x, sqn_ref[...], SWA_DIM), cs, ss_up, ss_dn, SWA_DIM // 4)
        qs_ref[0, hd] = (qh * (SWA_DIM ** -0.5)).astype(jnp.bfloat16)
    lane = lax.broadcasted_iota(jnp.int32, (h.shape[0], LANES), 1)
    for g in range(SWA_KV_HEADS):
        x = p[:, OFF_SK + g * LANES:OFF_SK + (g + 1) * LANES]
        kh = _rope(_rms(x, skn_ref[...], SWA_DIM), cs, ss_up, ss_dn, SWA_DIM // 4)
        ks_ref[0, g] = kh.astype(jnp.bfloat16)
        v = p[:, OFF_SV + g * LANES:OFF_SV + (g + 1) * LANES]
        vlo_ref[0, g] = jnp.where(lane < SWA_DIM, v, 0.0).astype(jnp.bfloat16)
        vhi_ref[0, g] = jnp.where(lane >= SWA_DIM, v, 0.0).astype(jnp.bfloat16)


def projections(h, mod, n1, win, qag, wuq, kvag, wuk, wuv, qn, kn, sqn, skn, rope_m, rope_s, r, c):
    b, t, d = h.shape
    nctx = c // r
    full = lambda a: pl.BlockSpec(a.shape, lambda i, j: (0,) * a.ndim)
    head_out = lambda nh: pl.BlockSpec((1, nh, r, LANES), lambda i, j: (i, 0, j, 0))
    head_shape = lambda nh: jax.ShapeDtypeStruct((b, nh, t, LANES), jnp.bfloat16)
    return pl.pallas_call(
        _proj_kernel,
        grid=(b, t // r),
        in_specs=[
            pl.BlockSpec((1, r, d), lambda i, j: (i, j, 0)),
            pl.BlockSpec((1, N_MOD, d), lambda i, j: (jnp.where(j < nctx, b, i), 0, 0)),
            full(n1), full(win), full(qag), full(wuq), full(kvag), full(wuk), full(wuv),
            full(qn), full(kn), full(sqn), full(skn),
            pl.BlockSpec((3, r, LANES), lambda i, j: (0, j, 0)),
            pl.BlockSpec((3, r, LANES), lambda i, j: (0, j, 0)),
        ],
        out_specs=[head_out(MLA_HEADS), head_out(MLA_HEADS), head_out(MLA_HEADS),
                   head_out(SWA_HEADS), head_out(SWA_KV_HEADS), head_out(SWA_KV_HEADS),
                   head_out(SWA_KV_HEADS)],
        out_shape=[head_shape(MLA_HEADS), head_shape(MLA_HEADS), head_shape(MLA_HEADS),
                   head_shape(SWA_HEADS), head_shape(SWA_KV_HEADS), head_shape(SWA_KV_HEADS),
                   head_shape(SWA_KV_HEADS)],
        compiler_params=_params(("parallel", "arbitrary")),
        name="projections",
    )(h, mod, n1, win, qag, wuq, kvag, wuk, wuv, qn, kn, sqn, skn, rope_m, rope_s)


def _mla_kernel(q_ref, k_ref, v_ref, o_ref, *, nctx_blocks, c):
    qi = pl.program_id(2)

    def attend(nk):
        acc = None
        for i in range(2):
            q = q_ref[0, i]
            s = lax.dot_general(q, k_ref[0, i, :nk], (((1,), (1,)), ((), ())),
                                preferred_element_type=jnp.float32)
            m = jnp.max(s, axis=-1, keepdims=True)
            p = jnp.exp2(s - m)
            l = jnp.sum(p, axis=-1, keepdims=True)
            o = jnp.dot(p.astype(jnp.bfloat16), v_ref[0, i, :nk], preferred_element_type=jnp.float32)
            o = o * (1.0 / l)
            acc = o if acc is None else acc + o
        o_ref[0] = acc.astype(o_ref.dtype)

    if nctx_blocks:
        @pl.when(qi < nctx_blocks)
        def _():
            attend(c)

        @pl.when(qi >= nctx_blocks)
        def _():
            attend(k_ref.shape[2])
    else:
        attend(k_ref.shape[2])


def mla_attention(qm, km, vm, tq, c, with_ctx):
    b, nh, t, _ = qm.shape
    off = 0 if with_ctx else c // tq
    nq = t // tq - off
    return pl.pallas_call(
        functools.partial(_mla_kernel, nctx_blocks=(c // tq if with_ctx else 0), c=c),
        grid=(b, nh // 2, nq),
        in_specs=[
            pl.BlockSpec((1, 2, tq, LANES), lambda i, hp, j: (i, hp, j + off, 0)),
            pl.BlockSpec((1, 2, t, LANES), lambda i, hp, j: (i, hp, 0, 0)),
            pl.BlockSpec((1, 2, t, LANES), lambda i, hp, j: (i, hp, 0, 0)),
        ],
        out_specs=pl.BlockSpec((1, tq, LANES), lambda i, hp, j: (i, j, hp)),
        out_shape=jax.ShapeDtypeStruct((b, nq * tq, nh // 2 * LANES), jnp.bfloat16),
        compiler_params=_params(("parallel", "arbitrary", "arbitrary")),
        name="mla_attention",
    )(qm, km, vm)


def _swa_kernel(sink_ref, q_ref, k_ref, vlo_ref, vhi_ref, o_ref, *, off, c):
    g = pl.program_id(1)
    qi = pl.program_id(2) + off
    t = k_ref.shape[2]
    rows = SWA_GROUP * WINDOW
    q = q_ref[0].reshape(rows, LANES)
    r_idx = lax.broadcasted_iota(jnp.int32, (rows, 1), 0)
    sink = jnp.zeros((rows, 1), jnp.float32)
    for i in range(SWA_GROUP):
        sink = jnp.where(r_idx // WINDOW == i, sink_ref[g * SWA_GROUP + i], sink)

    ws = pl.multiple_of(jnp.clip(qi * WINDOW - WINDOW, c, t - WIN_KEYS), WINDOW)
    nt = (((1,), (1,)), ((), ()))
    s_ctx = lax.dot_general(q, k_ref[0, 0, :c], nt, preferred_element_type=jnp.float32)
    s_loc = lax.dot_general(q, k_ref[0, 0, pl.ds(ws, WIN_KEYS)], nt, preferred_element_type=jnp.float32)
    qpos = qi * WINDOW + (r_idx % WINDOW)
    kpos = ws + lax.broadcasted_iota(jnp.int32, (1, WIN_KEYS), 1)
    reach = jnp.where(qi * WINDOW >= c, WINDOW, -1)
    s_loc = jnp.where(jnp.abs(qpos - kpos) <= reach, s_loc, NEG_INF)
    m = jnp.maximum(jnp.maximum(jnp.max(s_ctx, axis=-1, keepdims=True),
                                jnp.max(s_loc, axis=-1, keepdims=True)), sink)
    p_ctx = jnp.exp(s_ctx - m)
    p_loc = jnp.exp(s_loc - m)
    l = (jnp.sum(p_ctx, axis=-1, keepdims=True) + jnp.sum(p_loc, axis=-1, keepdims=True)
         + jnp.exp(sink - m))
    inv = 1.0 / l
    p_ctx = p_ctx.astype(jnp.bfloat16)
    p_loc = p_loc.astype(jnp.bfloat16)
    outs = []
    for pair in range(SWA_GROUP // 2):
        acc = None
        for i, v_ref in enumerate((vlo_ref, vhi_ref)):
            rs = slice((2 * pair + i) * WINDOW, (2 * pair + i + 1) * WINDOW)
            o = (jnp.dot(p_ctx[rs], v_ref[0, 0, :c], preferred_element_type=jnp.float32)
                 + jnp.dot(p_loc[rs], v_ref[0, 0, pl.ds(ws, WIN_KEYS)], preferred_element_type=jnp.float32))
            o = o * inv[rs]
            acc = o if acc is None else acc + o
        outs.append(acc)
    o_ref[0] = jnp.concatenate(outs, axis=-1).astype(o_ref.dtype)


def swa_attention(sink, qs, ks, vlo, vhi, c, with_ctx):
    b, nh, t, _ = qs.shape
    off = 0 if with_ctx else c // WINDOW
    nq = t // WINDOW - off
    kv_spec = pl.BlockSpec((1, 1, t, LANES), lambda i, g, j: (i, g, 0, 0))
    return pl.pallas_call(
        functools.partial(_swa_kernel, off=off, c=c),
        grid=(b, SWA_KV_HEADS, nq),
        in_specs=[
            pl.BlockSpec(memory_space=pltpu.SMEM),
            pl.BlockSpec((1, SWA_GROUP, WINDOW, LANES), lambda i, g, j: (i, g, j + off, 0)),
            kv_spec, kv_spec, kv_spec,
        ],
        out_specs=pl.BlockSpec((1, WINDOW, SWA_GROUP // 2 * LANES), lambda i, g, j: (i, j, g)),
        out_shape=jax.ShapeDtypeStruct((b, nq * WINDOW, SWA_HEADS // 2 * LANES), jnp.bfloat16),
        compiler_params=_params(("parallel", "arbitrary", "arbitrary")),
        name="swa_attention",
    )(sink, qs, ks, vlo, vhi)


def _tree(op, xs):
    xs = list(xs)
    while len(xs) > 1:
        xs = [op(xs[i], xs[i + 1]) for i in range(0, len(xs) - 1, 2)] + (xs[-1:] if len(xs) % 2 else [])
    return xs[0]


def _top16_sweeps(ids, *problems):
    n = len(ids)
    big = float(max(ids) + 1)
    order = sorted(range(n), key=lambda k: ids[k])
    runs = [order[i:i + PEER_TOPK] for i in range(0, n, PEER_TOPK)]

    def step(r, carry):
        for val_ref, out_v, out_i in problems:
            m = _tree(jnp.maximum, [val_ref[k] for k in range(n)])
            firsts = []
            for run in runs:
                am = jnp.full(m.shape, big, jnp.float32)
                for k in reversed(run):
                    am = jnp.where(val_ref[k] == m, float(ids[k]), am)
                firsts.append(am)
            am = _tree(jnp.minimum, firsts)
            for k in range(n):
                val_ref[k] = jnp.where(am == float(ids[k]), NEG_INF, val_ref[k])
            out_v[r] = m
            out_i[r] = am
        return carry

    lax.fori_loop(0, PEER_TOPK, step, 0)


PAIRS = [(a, b) for a in range(PEER_TOPK) for b in range(PEER_TOPK) if (a + 1) * (b + 1) <= PEER_TOPK]


def _route_kernel(h_ref, om_ref, os_ref, wo_ref, mod_ref, n2_ref, wq_ref,
                  h1_ref, x2_ref, q_ref):
    h = h_ref[0]
    d = h.shape[-1]
    mod = mod_ref[0]
    half = om_ref.shape[-1]
    mix = (jnp.dot(om_ref[0], wo_ref[:half], preferred_element_type=jnp.float32)
           + jnp.dot(os_ref[0], wo_ref[half:], preferred_element_type=jnp.float32))
    h1 = h + mod[2:3] * mix
    h1_ref[0] = h1
    x = _rms(h1, n2_ref[...], d) * (1.0 + mod[4:5]) + mod[3:4]
    xhi = x.astype(jnp.bfloat16)
    x2_ref[0, :, :d] = xhi
    x2_ref[0, :, d:] = (x - xhi.astype(jnp.float32)).astype(jnp.bfloat16)
    q = jnp.dot(xhi, wq_ref[...], preferred_element_type=jnp.float32)
    for k in range(2 * PEER_HEADS):
        q_ref[k] = q[:, k * PEER_DHALF:(k + 1) * PEER_DHALF].astype(jnp.bfloat16)


def _topk_kernel(q_ref, keys_ref, idx_ref, g_ref, stage, vals, sv, si, cand, cv, ci, out_i, out_g):
    groups = vals.shape[2]

    def head(hh, carry):
        for part in range(2):
            st = lax.dot_general(keys_ref[hh, part], q_ref[2 * hh + part], (((1,), (1,)), ((), ())),
                                 preferred_element_type=jnp.float32)
            for g in range(groups):
                stage[pl.ds(g * STAGE_STRIDE, PEER_NKEYS), :] = st[:, g * LANES:(g + 1) * LANES]
            for k in range(PEER_NKEYS):
                vals[part, k] = stage[pl.ds(k, groups, stride=STAGE_STRIDE), :]
        _top16_sweeps(list(range(PEER_NKEYS)), *[(vals.at[p], sv.at[p], si.at[p]) for p in range(2)])
        for n, (a, b) in enumerate(PAIRS):
            cand[n] = sv[0, a] + sv[1, b]
        _top16_sweeps([a * PEER_TOPK + b for a, b in PAIRS], (cand, cv, ci))
        top = cv[0]
        e = [jnp.exp(cv[r] - top) for r in range(PEER_TOPK)]
        inv = 1.0 / _tree(jnp.add, e)
        for r in range(PEER_TOPK):
            ia = jnp.floor(ci[r] * (1.0 / PEER_TOPK))
            ib = ci[r] - ia * PEER_TOPK
            i1 = jnp.zeros_like(ia)
            i2 = jnp.zeros_like(ia)
            for a in range(PEER_TOPK):
                i1 = jnp.where(ia == a, si[0, a], i1)
                i2 = jnp.where(ib == a, si[1, a], i2)
            out_i[hh * PEER_TOPK + r] = (i1 * PEER_NKEYS + i2) * N_CHUNK
            out_g[hh * PEER_TOPK + r] = e[r] * inv
        return carry

    lax.fori_loop(0, PEER_HEADS, head, 0)
    for g in range(groups):
        rows = slice(g * LANES, (g + 1) * LANES)
        idx_ref[rows, :] = out_i[:, g, :].T.astype(jnp.int32)
        g_ref[rows, :] = out_g[:, g, :].T


def peer_topk(q16, keys):
    n = q16.shape[1]
    tb = next(t for t in (TOPK_BLK, TOPK_BLK // 2, TOPK_BLK // 4, LANES) if n % t == 0)
    groups = tb // LANES
    blk = lambda dt: pltpu.VMEM((PEER_TOPK, groups, LANES), dt)
    return pl.pallas_call(
        _topk_kernel,
        grid=(n // tb,),
        in_specs=[
            pl.BlockSpec((2 * PEER_HEADS, tb, PEER_DHALF), lambda i: (0, i, 0)),
            pl.BlockSpec(keys.shape, lambda i: (0, 0, 0, 0)),
        ],
        out_specs=[pl.BlockSpec((tb, PEER_ROWS), lambda i: (i, 0))] * 2,
        out_shape=[jax.ShapeDtypeStruct((n, PEER_ROWS), jnp.int32),
                   jax.ShapeDtypeStruct((n, PEER_ROWS), jnp.float32)],
        scratch_shapes=[
            pltpu.VMEM((groups * STAGE_STRIDE, LANES), jnp.float32),
            pltpu.VMEM((2, PEER_NKEYS, groups, LANES), jnp.float32),
            pltpu.VMEM((2, PEER_TOPK, groups, LANES), jnp.float32),
            pltpu.VMEM((2, PEER_TOPK, groups, LANES), jnp.float32),
            pltpu.VMEM((len(PAIRS), groups, LANES), jnp.float32),
            blk(jnp.float32), blk(jnp.float32),
            pltpu.VMEM((PEER_ROWS, groups, LANES), jnp.float32),
            pltpu.VMEM((PEER_ROWS, groups, LANES), jnp.float32),
        ],
        compiler_params=_params(("arbitrary",)),
        name="peer_topk",
    )(q16, keys)


def route(h, om, osw, wo, mod, n2, wq, r, c, with_ctx):
    b, t, d = h.shape
    off = 0 if with_ctx else c // r
    nblk = t // r - off
    nctx = c // r
    tq = nblk * r
    full = lambda a: pl.BlockSpec(a.shape, lambda i, j: (0,) * a.ndim)
    row_spec = lambda w: pl.BlockSpec((1, r, w), lambda i, j: (i, j, 0))
    return pl.pallas_call(
        _route_kernel,
        grid=(b, nblk),
        in_specs=[
            pl.BlockSpec((1, r, d), lambda i, j: (i, j + off, 0)),
            row_spec(om.shape[-1]), row_spec(osw.shape[-1]),
            full(wo),
            pl.BlockSpec((1, N_MOD, d), lambda i, j: (jnp.where(j + off < nctx, b, i), 0, 0)),
            full(n2), full(wq),
        ],
        out_specs=[row_spec(d), row_spec(2 * d),
                   pl.BlockSpec((2 * PEER_HEADS, r, PEER_DHALF), lambda i, j: (0, i * nblk + j, 0))],
        out_shape=[
            jax.ShapeDtypeStruct((b, tq, d), jnp.float32),
            jax.ShapeDtypeStruct((b, tq, 2 * d), jnp.bfloat16),
            jax.ShapeDtypeStruct((2 * PEER_HEADS, b * tq, PEER_DHALF), jnp.bfloat16),
        ],
        compiler_params=_params(("parallel", "arbitrary")),
        name="peer_route",
    )(h, om, osw, wo, mod, n2, wq)


def _pack_kernel(x_ref, o_ref):
    x = x_ref[0]
    half = x.shape[1] // 2
    bits = lambda v: pltpu.bitcast(v.astype(jnp.bfloat16).astype(jnp.float32), jnp.uint32)
    w = (bits(x[:, :half]) & jnp.uint32(0xFFFF0000)) | (bits(x[:, half:]) >> 16)
    for c in range(N_CHUNK):
        o_ref[pl.ds(c, x.shape[0], stride=N_CHUNK), :] = w[:, c * LANES:(c + 1) * LANES]


def pack_table(tabs, l):
    _, e, d = tabs.shape
    be = 512
    return pl.pallas_call(
        _pack_kernel,
        grid=(e // be,),
        in_specs=[pl.BlockSpec((1, be, d), lambda i: (l, i, 0))],
        out_specs=pl.BlockSpec((be * N_CHUNK, LANES), lambda i: (i, 0)),
        out_shape=jax.ShapeDtypeStruct((e * N_CHUNK, LANES), jnp.uint32),
        compiler_params=_params(("parallel",)),
        name="pack_table",
    )(tabs)


def _gather_rows(idx_ref, tab_ref, tile_ref, t):
    for j in range(PEER_ROWS):
        r = pl.multiple_of(idx_ref[t, j], N_CHUNK)
        tile_ref[pl.ds(j, N_CHUNK, stride=TILE_STRIDE), :] = tab_ref[pl.ds(r, N_CHUNK), :]


def _token_pipeline(gather, gather_next, compute, tiles):
    steps = TOK_BLK // TOK_SET

    @pl.when(pl.program_id(0) == 0)
    def _():
        for k in range(TOK_SET):
            gather(tiles[0].at[k], k)

    def step(t, cur, nxt, fill):
        for k in range(TOK_SET):
            compute(cur.at[k], t + k)
            fill(nxt.at[k], k)

    def body(i, carry):
        t = TOK_SET * i
        ahead = lambda tile, k: gather(tile, t + TOK_SET + k)

        @pl.when(i % 2 == 0)
        def _():
            step(t, tiles[0], tiles[1], ahead)

        @pl.when(i % 2 == 1)
        def _():
            step(t, tiles[1], tiles[0], ahead)

        return carry

    lax.fori_loop(0, steps - 1, body, 0)
    step(TOK_BLK - TOK_SET, tiles[1], tiles[0], gather_next)


def _chunk(tile_ref, c):
    return pltpu.bitcast(tile_ref[pl.ds(c * TILE_STRIDE, PEER_ROWS), :], jnp.bfloat16)


def _u_kernel(idx_ref, nxt_ref, x_ref, g_ref, tab_ref, w_ref, *tiles):
    row = lax.broadcasted_iota(jnp.int32, (16, 2 * PEER_ROWS), 0) & 7

    def scores(tile_ref, t):
        x16 = x_ref[t]
        top = jnp.zeros((16, 2 * PEER_ROWS), jnp.float32)
        bot = jnp.zeros((16, 2 * PEER_ROWS), jnp.float32)
        for c in range(N_CHUNK):
            y = lax.dot_general(x16, _chunk(tile_ref, c), (((1,), (1,)), ((), ())),
                                preferred_element_type=jnp.float32)
            top = top + jnp.where(row == c, y, 0.0)
            bot = bot + jnp.where(row == c + N_CHUNK, y, 0.0)
        tot = top + pltpu.roll(bot, 1, axis=1)
        a = jnp.sum(tot, axis=0, keepdims=True)
        gelu = 0.5 * a * (1.0 + lax.erf(a * (2.0 ** -0.5)))
        w_ref[t] = g_ref[t] * gelu

    _token_pipeline(functools.partial(_gather_rows, idx_ref, tab_ref),
                    functools.partial(_gather_rows, nxt_ref, tab_ref), scores, tiles)


def _next_tokens_spec(n):
    per_blk = TOK_BLK // TOK_SET
    return pl.BlockSpec((TOK_SET, PEER_ROWS),
                        lambda i: (jnp.minimum((i + 1) * per_blk, n // TOK_SET - 1), 0),
                        memory_space=pltpu.SMEM)


def peer_scores(idx, x16, g, tab):
    n = idx.shape[0]
    tok = lambda *s: pl.BlockSpec((TOK_BLK,) + s, lambda i: (i,) + (0,) * len(s))
    return pl.pallas_call(
        _u_kernel,
        grid=(n // TOK_BLK,),
        in_specs=[
            pl.BlockSpec((TOK_BLK, PEER_ROWS), lambda i: (i, 0), memory_space=pltpu.SMEM),
            _next_tokens_spec(n),
            tok(16, LANES), tok(1, 2 * PEER_ROWS),
            pl.BlockSpec(tab.shape, lambda i: (0, 0), pipeline_mode=pl.Buffered(1)),
        ],
        out_specs=tok(1, 2 * PEER_ROWS),
        out_shape=jax.ShapeDtypeStruct((n, 1, 2 * PEER_ROWS), jnp.float32),
        scratch_shapes=[pltpu.VMEM((TOK_SET, N_CHUNK * TILE_STRIDE, LANES), jnp.uint32)] * 2,
        compiler_params=_params(("arbitrary",)),
        name="peer_scores",
    )(idx, idx, x16, g, tab)


def _v_kernel(idx_ref, nxt_ref, w_ref, h_ref, g2_ref, tab_ref, o_ref, *tiles):
    row = lax.broadcasted_iota(jnp.int32, (16, 2 * PEER_ROWS), 0)

    def combine(tile_ref, t):
        w = w_ref[t]
        w_hi = w.astype(jnp.bfloat16).astype(jnp.float32)
        w_lo = w - w_hi
        parts = (w_hi, pltpu.roll(w_hi, 2 * PEER_ROWS - 1, axis=1),
                 w_lo, pltpu.roll(w_lo, 2 * PEER_ROWS - 1, axis=1))
        acc = jnp.zeros((16, LANES), jnp.float32)
        for c in range(N_CHUNK):
            lhs = jnp.zeros((16, 2 * PEER_ROWS), jnp.float32)
            for k, part in enumerate(parts):
                lhs = jnp.where(row == c + N_CHUNK * k, part, lhs)
            acc = acc + jnp.dot(lhs.astype(jnp.bfloat16), _chunk(tile_ref, c),
                                preferred_element_type=jnp.float32)
        o_ref[t] = h_ref[t] + g2_ref[0, 0] * (acc[:8] + acc[8:])

    _token_pipeline(functools.partial(_gather_rows, idx_ref, tab_ref),
                    functools.partial(_gather_rows, nxt_ref, tab_ref), combine, tiles)


def peer_combine(idx, w, h8, g2, tab, blocks_per_sample, ctx_blocks):
    n = idx.shape[0]
    tok = lambda *s: pl.BlockSpec((TOK_BLK,) + s, lambda i: (i,) + (0,) * len(s))
    g2_map = lambda i: (i // blocks_per_sample,
                        jnp.where(i % blocks_per_sample < ctx_blocks, 0, 1), 0, 0)
    return pl.pallas_call(
        _v_kernel,
        grid=(n // TOK_BLK,),
        in_specs=[
            pl.BlockSpec((TOK_BLK, PEER_ROWS), lambda i: (i, 0), memory_space=pltpu.SMEM),
            _next_tokens_spec(n),
            tok(1, 2 * PEER_ROWS), tok(8, LANES),
            pl.BlockSpec((1, 1, 8, LANES), g2_map),
            pl.BlockSpec(tab.shape, lambda i: (0, 0), pipeline_mode=pl.Buffered(1)),
        ],
        out_specs=tok(8, LANES),
        out_shape=jax.ShapeDtypeStruct((n, 8, LANES), jnp.float32),
        scratch_shapes=[pltpu.VMEM((TOK_SET, N_CHUNK * TILE_STRIDE, LANES), jnp.uint32)] * 2,
        compiler_params=_params(("arbitrary",)),
        name="peer_combine",
    )(idx, idx, w, h8, g2, tab)


SC_LANES = 16
SC_WORKERS = 32
SC_ROWS = 32
SC_SHARE = 0.47


def peer_combine_sc(idx, w, tab):
    m = idx.shape[0]
    d = tab.shape[1]
    per = m // SC_WORKERS
    mesh = plsc.VectorSubcoreMesh(core_axis_name="c", subcore_axis_name="s")

    quarters = PEER_ROWS // SC_ROWS
    group = 16

    @functools.partial(
        pl.kernel, mesh=mesh,
        out_type=jax.ShapeDtypeStruct((m, d), jnp.float32),
        scratch_types=[
            pltpu.VMEM((2, PEER_ROWS), jnp.int32),
            pltpu.VMEM((PEER_ROWS,), jnp.float32),
            pltpu.VMEM((2, SC_ROWS, d), jnp.float32),
            pltpu.VMEM((d,), jnp.float32),
            pltpu.SemaphoreType.DMA((2,)),
        ],
        compiler_params=pltpu.CompilerParams(needs_layout_passes=False),
        name="peer_combine_sc",
    )
    def body(idx_hbm, w_hbm, tab_hbm, out_hbm, idx_v, w_v, rows_v, acc_v, sems):
        base = (lax.axis_index("s") * 2 + lax.axis_index("c")) * per

        def gather(slot, q, buf):
            rows = idx_v.at[slot, pl.ds(q * SC_ROWS, SC_ROWS)]
            return pltpu.make_async_copy(tab_hbm.at[rows], rows_v.at[buf], sems.at[buf])

        def accumulate(buf, q):
            for g0 in range(0, d // SC_LANES, group):
                lanes = [pl.ds((g0 + c) * SC_LANES, SC_LANES) for c in range(group)]

                def row(j, accs):
                    wj = plsc.load_gather(w_v, [jnp.full((SC_LANES,), q * SC_ROWS, jnp.int32) + j])
                    return tuple(a + wj * rows_v[buf, j, sl] for a, sl in zip(accs, lanes))

                accs = lax.fori_loop(0, SC_ROWS, row, tuple(acc_v[sl] for sl in lanes))
                for sl, a in zip(lanes, accs):
                    acc_v[sl] = a

        pltpu.sync_copy(idx_hbm.at[base], idx_v.at[0])
        gather(0, 0, 0).start()

        @pl.loop(0, per, step=2)
        def _(i):
            for p in range(2):
                t = base + i + p
                pltpu.sync_copy(w_hbm.at[t], w_v)
                for c in range(d // SC_LANES):
                    acc_v[pl.ds(c * SC_LANES, SC_LANES)] = jnp.zeros((SC_LANES,), jnp.float32)
                for q in range(quarters):
                    buf = q % 2
                    if q + 1 < quarters:
                        gather(p, q + 1, 1 - buf).start()
                    else:
                        nxt = jnp.minimum(t + 1, base + per - 1)
                        pltpu.sync_copy(idx_hbm.at[nxt], idx_v.at[1 - p])
                        gather(1 - p, 0, 1 - buf).start()
                    gather(p, q, buf).wait()
                    accumulate(buf, q)
                pltpu.sync_copy(acc_v, out_hbm.at[t])

        gather(0, 0, 0).wait()

    return body(idx, w, tab)


def _residual_kernel(h_ref, g_ref, a_ref, o_ref):
    o_ref[...] = h_ref[...] + g_ref[0, 0] * a_ref[...]


def gated_residual(h, g2, a, first_row, rows_per_sample, ctx_rows):
    m, d = h.shape
    rb = 256
    blk = pl.BlockSpec((rb, d), lambda i: (i, 0))
    per, ctx, off = rows_per_sample // rb, ctx_rows // rb, first_row // rb
    g_map = lambda i: ((i + off) // per, jnp.where((i + off) % per < ctx, 0, 1), 0, 0)
    return pl.pallas_call(
        _residual_kernel, grid=(m // rb,),
        in_specs=[blk, pl.BlockSpec((1, 1, 1, d), g_map), blk],
        out_specs=blk, out_shape=jax.ShapeDtypeStruct((m, d), jnp.float32),
        compiler_params=_params(("parallel",)), name="gated_residual",
    )(h, g2, a)


def _slots(w, width, offset=0):
    lead = w.shape[:-1]
    n = w.shape[-1] // width
    w = w.reshape(lead + (n, width))
    w = jnp.pad(w, [(0, 0)] * len(lead) + [(0, 0), (offset, LANES - width - offset)])
    return w.reshape(lead + (n * LANES,))


def _rope_tables(rot_dim, lane0, t, c):
    s = t - c
    q = rot_dim // 4
    pos = jnp.arange(s, dtype=jnp.float32)
    rows = jnp.floor(pos / GRID_W)
    cols = pos - rows * GRID_W
    inv = ROPE_BASE ** (-jnp.arange(q, dtype=jnp.float32) / q)
    ar = rows[:, None] * inv
    ac = cols[:, None] * inv
    zero = jnp.zeros_like(ar)
    cos = jnp.cos(jnp.concatenate([ar, ar, ac, ac], axis=-1))
    up = jnp.concatenate([-jnp.sin(ar), zero, -jnp.sin(ac), zero], axis=-1)
    dn = jnp.concatenate([zero, jnp.sin(ar), zero, jnp.sin(ac)], axis=-1)
    pad = lambda a, fill: jnp.pad(
        jnp.pad(a, ((0, 0), (lane0, LANES - lane0 - rot_dim)), constant_values=fill),
        ((c, 0), (0, 0)), constant_values=fill)
    cos = jnp.pad(jnp.pad(cos, ((0, 0), (lane0, LANES - lane0 - rot_dim)), constant_values=1.0),
                  ((c, 0), (0, 0)), constant_values=1.0)
    return jnp.stack([cos, pad(up, 0.0), pad(dn, 0.0)])


def kernel(x, c, ctx, c_ctx, ada_w, ada_b, norm1_g, norm2_g, w_in, mla_qa_g, mla_wuq, mla_kva_g, mla_wukv, mla_qn_g, mla_kn_g, swa_qn_g, swa_kn_g, swa_sink, w_out, peer_wq, peer_keys, peer_u, peer_v):
    b, s, d = x.shape
    nctx = ctx.shape[1]
    t = nctx + s
    depth = ada_w.shape[0]
    r = min(256, nctx)
    bf = jnp.bfloat16

    cond = jnp.zeros((16, d), jnp.float32).at[:b].set(c).at[b].set(c_ctx)
    mod_all = modulation(cond, ada_w, ada_b).reshape(depth, 16, N_MOD, d)
    rope_m = _rope_tables(MLA_ROPE, MLA_NOPE, t, nctx)
    rope_s = _rope_tables(SWA_DIM, 0, t, nctx)

    h = jnp.concatenate([ctx, x], axis=1)
    for l in range(depth):
        last = l == depth - 1
        with_ctx = not last
        mod = mod_all[l]
        wi = w_in[l]
        kv0 = Q_COLS + MLA_KV_RANK
        sk0 = kv0 + MLA_ROPE
        sv0 = sk0 + SWA_KV_HEADS * SWA_DIM
        sv = wi[:, sv0:].reshape(d, SWA_KV_HEADS, 1, SWA_DIM)
        win = jnp.concatenate([
            wi[:, :MLA_Q_RANK], wi[:, Q_COLS:kv0],
            _slots(wi[:, kv0:sk0], MLA_ROPE, MLA_NOPE),
            _slots(wi[:, MLA_Q_RANK:Q_COLS], SWA_DIM),
            _slots(wi[:, sk0:sv0], SWA_DIM),
            jnp.broadcast_to(sv, (d, SWA_KV_HEADS, 2, SWA_DIM)).reshape(d, SWA_KV_HEADS * LANES),
        ], axis=1).astype(bf)
        wuq = _slots(mla_wuq[l], MLA_QK).astype(bf)
        wukv = mla_wukv[l].reshape(MLA_KV_RANK, MLA_HEADS, MLA_NOPE + MLA_V)
        wuk = _slots(wukv[:, :, :MLA_NOPE].reshape(MLA_KV_RANK, -1), MLA_NOPE).astype(bf)
        wv = wukv[:, :, MLA_NOPE:].reshape(MLA_KV_RANK, MLA_HEADS // 2, 2, MLA_V)
        zero = jnp.zeros_like(wv[:, :, 0])
        wuv = jnp.stack([jnp.concatenate([wv[:, :, 0], zero], axis=-1),
                         jnp.concatenate([zero, wv[:, :, 1]], axis=-1)], axis=2)
        wuv = wuv.reshape(MLA_KV_RANK, MLA_HEADS * LANES).astype(bf)
        row = lambda g: g.reshape(1, -1)

        qm, km, vm, qs, ks, vlo, vhi = projections(
            h, mod, row(norm1_g[l]), win, row(mla_qa_g[l]), wuq, row(mla_kva_g[l]), wuk, wuv,
            row(_slots(mla_qn_g[l], MLA_QK)), row(_slots(mla_kn_g[l], MLA_QK)),
            row(_slots(swa_qn_g[l], SWA_DIM)), row(_slots(swa_kn_g[l], SWA_DIM)),
            rope_m, rope_s, r, nctx)
        om = mla_attention(qm, km, vm, r, nctx, with_ctx)
        osw = swa_attention(swa_sink[l], qs, ks, vlo, vhi, nctx, with_ctx)
        h1, x2, q16 = route(
            h, om, osw, w_out[l].astype(bf), mod, row(norm2_g[l]), peer_wq[l].astype(bf),
            r, nctx, with_ctx)
        idx, gate = peer_topk(q16, peer_keys[l].astype(bf))

        tl = h1.shape[1]
        n = b * tl
        x16 = x2.reshape(n, 16, LANES)
        gate = jnp.stack([jnp.zeros_like(gate), gate], axis=-1).reshape(n, 1, 2 * PEER_ROWS)
        g2 = jnp.stack([jnp.broadcast_to(mod[b, 5], (b, d)), mod[:b, 5]], axis=1)
        n_sc = int(n * SC_SHARE) // 256 * 256
        n_tc = n - n_sc
        ctx_rows = nctx if with_ctx else 0
        h1f = h1.reshape(n, d)
        utab = pack_table(peer_u, l)
        w_sc = peer_scores(idx[n_tc:], x16[n_tc:], gate[n_tc:], utab)
        experts = (idx[n_tc:] >> 2) + l * peer_v.shape[1]
        acc_sc = peer_combine_sc(experts, w_sc[:, 0, 1::2], peer_v.reshape(-1, d))
        w = peer_scores(idx[:n_tc], x16[:n_tc], gate[:n_tc], utab)
        h_tc = peer_combine(idx[:n_tc], w, h1f[:n_tc].reshape(n_tc, 8, LANES),
                            g2.reshape(b, 2, 8, LANES), pack_table(peer_v, l), tl // TOK_BLK,
                            ctx_rows // TOK_BLK).reshape(n_tc, d)
        h_sc = gated_residual(h1f[n_tc:], g2.reshape(b, 2, 1, d), acc_sc, n_tc, tl, ctx_rows)
        h = jnp.concatenate([h_tc, h_sc], axis=0).reshape(b, tl, d)
    return h
```

```python
import functools
import jax
import jax.numpy as jnp
from jax import lax
from jax.experimental import pallas as pl
from jax.experimental.pallas import tpu as pltpu
from jax.experimental.pallas import tpu_sc as plsc

LANES = 128
EPS = 1e-6
ROPE_BASE = 10000.0
GRID_W = 64
N_MOD = 6

MLA_HEADS = 8
MLA_NOPE = 64
MLA_ROPE = 32
MLA_QK = MLA_NOPE + MLA_ROPE
MLA_V = 64
MLA_Q_RANK = 384
MLA_KV_RANK = 256
SWA_HEADS = 8
SWA_KV_HEADS = 2
SWA_GROUP = SWA_HEADS // SWA_KV_HEADS
SWA_DIM = 64
WINDOW = 128
WIN_KEYS = 3 * WINDOW
Q_COLS = MLA_Q_RANK + SWA_HEADS * SWA_DIM

PEER_HEADS = 8
PEER_NKEYS = 128
PEER_DHALF = 128
PEER_TOPK = 16
PEER_ROWS = PEER_HEADS * PEER_TOPK
N_CHUNK = 4
TILE_STRIDE = PEER_ROWS + 1
TOPK_BLK = 1024
STAGE_STRIDE = PEER_NKEYS + 8
TOK_BLK = 128
TOK_SET = 8

OFF_QA = 0
OFF_KVA = OFF_QA + MLA_Q_RANK
OFF_KR = OFF_KVA + MLA_KV_RANK
OFF_SQ = OFF_KR + LANES
OFF_SK = OFF_SQ + SWA_HEADS * LANES
OFF_SV = OFF_SK + SWA_KV_HEADS * LANES
N_IN = OFF_SV + SWA_KV_HEADS * LANES

VMEM_LIMIT = 56 * 1024 * 1024
NEG_INF = float("-inf")
LOG2_E = 1.4426950408889634


def _params(sem, vmem=VMEM_LIMIT):
    return pltpu.CompilerParams(dimension_semantics=sem, vmem_limit_bytes=vmem)


def _rms(x, g, n):
    ms = jnp.sum(x * x, axis=-1, keepdims=True) * (1.0 / n)
    return x * lax.rsqrt(ms + EPS) * g


def _rope(x, cos, sin_up, sin_dn, shift):
    return (x * cos + pltpu.roll(x, LANES - shift, axis=1) * sin_up
            + pltpu.roll(x, shift, axis=1) * sin_dn)


def _mod_kernel(c_ref, w_ref, b_ref, o_ref):
    c = c_ref[...]
    s = c * (1.0 / (1.0 + jnp.exp(-c)))
    o_ref[0] = jnp.dot(s, w_ref[0], preferred_element_type=jnp.float32,
                       precision=lax.Precision.HIGHEST) + b_ref[0]


def modulation(cond, ada_w, ada_b):
    nl, d, n6 = ada_w.shape
    tn = 1536
    return pl.pallas_call(
        _mod_kernel,
        grid=(nl, n6 // tn),
        in_specs=[
            pl.BlockSpec(cond.shape, lambda l, j: (0, 0)),
            pl.BlockSpec((1, d, tn), lambda l, j: (l, 0, j)),
            pl.BlockSpec((1, 1, tn), lambda l, j: (l, 0, j)),
        ],
        out_specs=pl.BlockSpec((1, cond.shape[0], tn), lambda l, j: (l, 0, j)),
        out_shape=jax.ShapeDtypeStruct((nl, cond.shape[0], n6), jnp.float32),
        compiler_params=_params(("arbitrary", "arbitrary")),
        name="modulation",
    )(cond, ada_w, ada_b.reshape(nl, 1, n6))


def _proj_kernel(h_ref, mod_ref, n1_ref, win_ref, qag_ref, wuq_ref, kvag_ref, wuk_ref, wuv_ref,
                 qn_ref, kn_ref, sqn_ref, skn_ref, rm_ref, rs_ref,
                 qm_ref, km_ref, vm_ref, qs_ref, ks_ref, vlo_ref, vhi_ref):
    h = h_ref[0]
    d = h.shape[-1]
    mod = mod_ref[0]
    a = _rms(h, n1_ref[...], d) * (1.0 + mod[1:2]) + mod[0:1]
    p = jnp.dot(a.astype(jnp.bfloat16), win_ref[...], preferred_element_type=jnp.float32)

    cm, sm_up, sm_dn = rm_ref[0], rm_ref[1], rm_ref[2]
    cs, ss_up, ss_dn = rs_ref[0], rs_ref[1], rs_ref[2]

    qa = _rms(p[:, OFF_QA:OFF_QA + MLA_Q_RANK], qag_ref[...], MLA_Q_RANK)
    q = jnp.dot(qa.astype(jnp.bfloat16), wuq_ref[...], preferred_element_type=jnp.float32)
    kva = _rms(p[:, OFF_KVA:OFF_KVA + MLA_KV_RANK], kvag_ref[...], MLA_KV_RANK).astype(jnp.bfloat16)
    kn = jnp.dot(kva, wuk_ref[...], preferred_element_type=jnp.float32)
    vm = jnp.dot(kva, wuv_ref[...], preferred_element_type=jnp.float32)
    kr = p[:, OFF_KR:OFF_KR + LANES]
    for hd in range(MLA_HEADS):
        sl = slice(hd * LANES, (hd + 1) * LANES)
        qh = _rope(_rms(q[:, sl], qn_ref[...], MLA_QK), cm, sm_up, sm_dn, MLA_ROPE // 4)
        qm_ref[0, hd] = (qh * (MLA_QK ** -0.5 * LOG2_E)).astype(jnp.bfloat16)
        kh = _rope(_rms(kn[:, sl] + kr, kn_ref[...], MLA_QK), cm, sm_up, sm_dn, MLA_ROPE // 4)
        km_ref[0, hd] = kh.astype(jnp.bfloat16)
        vm_ref[0, hd] = vm[:, sl].astype(jnp.bfloat16)
    for hd in range(SWA_HEADS):
        x = p[:, OFF_SQ + hd * LANES:OFF_SQ + (hd + 1) * LANES]
        qh = _rope(_rms(x, sqn_ref[...], SWA_DIM), cs, ss_up, ss_dn, SWA_DIM // 4)
        qs_ref[0, hd] = (qh * (SWA_DIM ** -0.5)).astype(jnp.bfloat16)
    lane = lax.broadcasted_iota(jnp.int32, (h.shape[0], LANES), 1)
    for g in range(SWA_KV_HEADS):
        x = p[:, OFF_SK + g * LANES:OFF_SK + (g + 1) * LANES]
        kh = _rope(_rms(x, skn_ref[...], SWA_DIM), cs, ss_up, ss_dn, SWA_DIM // 4)
        ks_ref[0, g] = kh.astype(jnp.bfloat16)
        v = p[:, OFF_SV + g * LANES:OFF_SV + (g + 1) * LANES]
        vlo_ref[0, g] = jnp.where(lane < SWA_DIM, v, 0.0).astype(jnp.bfloat16)
        vhi_ref[0, g] = jnp.where(lane >= SWA_DIM, v, 0.0).astype(jnp.bfloat16)


def projections(h, mod, n1, win, qag, wuq, kvag, wuk, wuv, qn, kn, sqn, skn, rope_m, rope_s, r, c):
    b, t, d = h.shape
    nctx = c // r
    full = lambda a: pl.BlockSpec(a.shape, lambda i, j: (0,) * a.ndim)
    head_out = lambda nh: pl.BlockSpec((1, nh, r, LANES), lambda i, j: (i, 0, j, 0))
    head_shape = lambda nh: jax.ShapeDtypeStruct((b, nh, t, LANES), jnp.bfloat16)
    return pl.pallas_call(
        _proj_kernel,
        grid=(b, t // r),
        in_specs=[
            pl.BlockSpec((1, r, d), lambda i, j: (i, j, 0)),
            pl.BlockSpec((1, N_MOD, d), lambda i, j: (jnp.where(j < nctx, b, i), 0, 0)),
            full(n1), full(win), full(qag), full(wuq), full(kvag), full(wuk), full(wuv),
            full(qn), full(kn), full(sqn), full(skn),
            pl.BlockSpec((3, r, LANES), lambda i, j: (0, j, 0)),
            pl.BlockSpec((3, r, LANES), lambda i, j: (0, j, 0)),
        ],
        out_specs=[head_out(MLA_HEADS), head_out(MLA_HEADS), head_out(MLA_HEADS),
                   head_out(SWA_HEADS), head_out(SWA_KV_HEADS), head_out(SWA_KV_HEADS),
                   head_out(SWA_KV_HEADS)],
        out_shape=[head_shape(MLA_HEADS), head_shape(MLA_HEADS), head_shape(MLA_HEADS),
                   head_shape(SWA_HEADS), head_shape(SWA_KV_HEADS), head_shape(SWA_KV_HEADS),
                   head_shape(SWA_KV_HEADS)],
        compiler_params=_params(("parallel", "arbitrary")),
        name="projections",
    )(h, mod, n1, win, qag, wuq, kvag, wuk, wuv, qn, kn, sqn, skn, rope_m, rope_s)


def _mla_kernel(q_ref, k_ref, v_ref, o_ref, *, nctx_blocks, c):
    qi = pl.program_id(2)

    def attend(nk):
        acc = None
        for i in range(2):
            q = q_ref[0, i]
            s = lax.dot_general(q, k_ref[0, i, :nk], (((1,), (1,)), ((), ())),
                                preferred_element_type=jnp.float32)
            m = jnp.max(s, axis=-1, keepdims=True)
            p = jnp.exp2(s - m)
            l = jnp.sum(p, axis=-1, keepdims=True)
            o = jnp.dot(p.astype(jnp.bfloat16), v_ref[0, i, :nk], preferred_element_type=jnp.float32)
            o = o * (1.0 / l)
            acc = o if acc is None else acc + o
        o_ref[0] = acc.astype(o_ref.dtype)

    if nctx_blocks:
        @pl.when(qi < nctx_blocks)
        def _():
            attend(c)

        @pl.when(qi >= nctx_blocks)
        def _():
            attend(k_ref.shape[2])
    else:
        attend(k_ref.shape[2])


def mla_attention(qm, km, vm, tq, c, with_ctx):
    b, nh, t, _ = qm.shape
    off = 0 if with_ctx else c // tq
    nq = t // tq - off
    return pl.pallas_call(
        functools.partial(_mla_kernel, nctx_blocks=(c // tq if with_ctx else 0), c=c),
        grid=(b, nh // 2, nq),
        in_specs=[
            pl.BlockSpec((1, 2, tq, LANES), lambda i, hp, j: (i, hp, j + off, 0)),
            pl.BlockSpec((1, 2, t, LANES), lambda i, hp, j: (i, hp, 0, 0)),
            pl.BlockSpec((1, 2, t, LANES), lambda i, hp, j: (i, hp, 0, 0)),
        ],
        out_specs=pl.BlockSpec((1, tq, LANES), lambda i, hp, j: (i, j, hp)),
        out_shape=jax.ShapeDtypeStruct((b, nq * tq, nh // 2 * LANES), jnp.bfloat16),
        compiler_params=_params(("parallel", "arbitrary", "arbitrary")),
        name="mla_attention",
    )(qm, km, vm)


def _swa_kernel(sink_ref, q_ref, k_ref, vlo_ref, vhi_ref, o_ref, *, off, c):
    g = pl.program_id(1)
    qi = pl.program_id(2) + off
    t = k_ref.shape[2]
    rows = SWA_GROUP * WINDOW
    q = q_ref[0].reshape(rows, LANES)
    r_idx = lax.broadcasted_iota(jnp.int32, (rows, 1), 0)
    sink = jnp.zeros((rows, 1), jnp.float32)
    for i in range(SWA_GROUP):
        sink = jnp.where(r_idx // WINDOW == i, sink_ref[g * SWA_GROUP + i], sink)

    ws = pl.multiple_of(jnp.clip(qi * WINDOW - WINDOW, c, t - WIN_KEYS), WINDOW)
    nt = (((1,), (1,)), ((), ()))
    s_ctx = lax.dot_general(q, k_ref[0, 0, :c], nt, preferred_element_type=jnp.float32)
    s_loc = lax.dot_general(q, k_ref[0, 0, pl.ds(ws, WIN_KEYS)], nt, preferred_element_type=jnp.float32)
    qpos = qi * WINDOW + (r_idx % WINDOW)
    kpos = ws + lax.broadcasted_iota(jnp.int32, (1, WIN_KEYS), 1)
    reach = jnp.where(qi * WINDOW >= c, WINDOW, -1)
    s_loc = jnp.where(jnp.abs(qpos - kpos) <= reach, s_loc, NEG_INF)
    m = jnp.maximum(jnp.maximum(jnp.max(s_ctx, axis=-1, keepdims=True),
                                jnp.max(s_loc, axis=-1, keepdims=True)), sink)
    p_ctx = jnp.exp(s_ctx - m)
    p_loc = jnp.exp(s_loc - m)
    l = (jnp.sum(p_ctx, axis=-1, keepdims=True) + jnp.sum(p_loc, axis=-1, keepdims=True)
         + jnp.exp(sink - m))
    inv = 1.0 / l
    p_ctx = p_ctx.astype(jnp.bfloat16)
    p_loc = p_loc.astype(jnp.bfloat16)
    outs = []
    for pair in range(SWA_GROUP // 2):
        acc = None
        for i, v_ref in enumerate((vlo_ref, vhi_ref)):
            rs = slice((2 * pair + i) * WINDOW, (2 * pair + i + 1) * WINDOW)
            o = (jnp.dot(p_ctx[rs], v_ref[0, 0, :c], preferred_element_type=jnp.float32)
                 + jnp.dot(p_loc[rs], v_ref[0, 0, pl.ds(ws, WIN_KEYS)], preferred_element_type=jnp.float32))
            o = o * inv[rs]
            acc = o if acc is None else acc + o
        outs.append(acc)
    o_ref[0] = jnp.concatenate(outs, axis=-1).astype(o_ref.dtype)


def swa_attention(sink, qs, ks, vlo, vhi, c, with_ctx):
    b, nh, t, _ = qs.shape
    off = 0 if with_ctx else c // WINDOW
    nq = t // WINDOW - off
    kv_spec = pl.BlockSpec((1, 1, t, LANES), lambda i, g, j: (i, g, 0, 0))
    return pl.pallas_call(
        functools.partial(_swa_kernel, off=off, c=c),
        grid=(b, SWA_KV_HEADS, nq),
        in_specs=[
            pl.BlockSpec(memory_space=pltpu.SMEM),
            pl.BlockSpec((1, SWA_GROUP, WINDOW, LANES), lambda i, g, j: (i, g, j + off, 0)),
            kv_spec, kv_spec, kv_spec,
        ],
        out_specs=pl.BlockSpec((1, WINDOW, SWA_GROUP // 2 * LANES), lambda i, g, j: (i, j, g)),
        out_shape=jax.ShapeDtypeStruct((b, nq * WINDOW, SWA_HEADS // 2 * LANES), jnp.bfloat16),
        compiler_params=_params(("parallel", "arbitrary", "arbitrary")),
        name="swa_attention",
    )(sink, qs, ks, vlo, vhi)


def _tree(op, xs):
    xs = list(xs)
    while len(xs) > 1:
        xs = [op(xs[i], xs[i + 1]) for i in range(0, len(xs) - 1, 2)] + (xs[-1:] if len(xs) % 2 else [])
    return xs[0]


def _top16_sweeps(ids, *problems):
    n = len(ids)
    big = float(max(ids) + 1)
    order = sorted(range(n), key=lambda k: ids[k])
    runs = [order[i:i + PEER_TOPK] for i in range(0, n, PEER_TOPK)]

    def step(r, carry):
        for val_ref, out_v, out_i in problems:
            m = _tree(jnp.maximum, [val_ref[k] for k in range(n)])
            firsts = []
            for run in runs:
                am = jnp.full(m.shape, big, jnp.float32)
                for k in reversed(run):
                    am = jnp.where(val_ref[k] == m, float(ids[k]), am)
                firsts.append(am)
            am = _tree(jnp.minimum, firsts)
            for k in range(n):
                val_ref[k] = jnp.where(am == float(ids[k]), NEG_INF, val_ref[k])
            out_v[r] = m
            out_i[r] = am
        return carry

    lax.fori_loop(0, PEER_TOPK, step, 0)


PAIRS = [(a, b) for a in range(PEER_TOPK) for b in range(PEER_TOPK) if (a + 1) * (b + 1) <= PEER_TOPK]


def _route_kernel(h_ref, om_ref, os_ref, wo_ref, mod_ref, n2_ref, wq_ref,
                  h1_ref, x2_ref, q_ref):
    h = h_ref[0]
    d = h.shape[-1]
    mod = mod_ref[0]
    half = om_ref.shape[-1]
    mix = (jnp.dot(om_ref[0], wo_ref[:half], preferred_element_type=jnp.float32)
           + jnp.dot(os_ref[0], wo_ref[half:], preferred_element_type=jnp.float32))
    h1 = h + mod[2:3] * mix
    h1_ref[0] = h1
    x = _rms(h1, n2_ref[...], d) * (1.0 + mod[4:5]) + mod[3:4]
    xhi = x.astype(jnp.bfloat16)
    x2_ref[0, :, :d] = xhi
    x2_ref[0, :, d:] = (x - xhi.astype(jnp.float32)).astype(jnp.bfloat16)
    q = jnp.dot(xhi, wq_ref[...], preferred_element_type=jnp.float32)
    for k in range(2 * PEER_HEADS):
        q_ref[k] = q[:, k * PEER_DHALF:(k + 1) * PEER_DHALF].astype(jnp.bfloat16)


def _topk_kernel(q_ref, keys_ref, idx_ref, g_ref, stage, vals, sv, si, cand, cv, ci, out_i, out_g):
    groups = vals.shape[2]

    def head(hh, carry):
        for part in range(2):
            st = lax.dot_general(keys_ref[hh, part], q_ref[2 * hh + part], (((1,), (1,)), ((), ())),
                                 preferred_element_type=jnp.float32)
            for g in range(groups):
                stage[pl.ds(g * STAGE_STRIDE, PEER_NKEYS), :] = st[:, g * LANES:(g + 1) * LANES]
            for k in range(PEER_NKEYS):
                vals[part, k] = stage[pl.ds(k, groups, stride=STAGE_STRIDE), :]
        _top16_sweeps(list(range(PEER_NKEYS)), *[(vals.at[p], sv.at[p], si.at[p]) for p in range(2)])
        for n, (a, b) in enumerate(PAIRS):
            cand[n] = sv[0, a] + sv[1, b]
        _top16_sweeps([a * PEER_TOPK + b for a, b in PAIRS], (cand, cv, ci))
        top = cv[0]
        e = [jnp.exp(cv[r] - top) for r in range(PEER_TOPK)]
        inv = 1.0 / _tree(jnp.add, e)
        for r in range(PEER_TOPK):
            ia = jnp.floor(ci[r] * (1.0 / PEER_TOPK))
            ib = ci[r] - ia * PEER_TOPK
            i1 = jnp.zeros_like(ia)
            i2 = jnp.zeros_like(ia)
            for a in range(PEER_TOPK):
                i1 = jnp.where(ia == a, si[0, a], i1)
                i2 = jnp.where(ib == a, si[1, a], i2)
            out_i[hh * PEER_TOPK + r] = (i1 * PEER_NKEYS + i2) * N_CHUNK
            out_g[hh * PEER_TOPK + r] = e[r] * inv
        return carry

    lax.fori_loop(0, PEER_HEADS, head, 0)
    for g in range(groups):
        rows = slice(g * LANES, (g + 1) * LANES)
        idx_ref[rows, :] = out_i[:, g, :].T.astype(jnp.int32)
        g_ref[rows, :] = out_g[:, g, :].T


def peer_topk(q16, keys):
    n = q16.shape[1]
    tb = next(t for t in (TOPK_BLK, TOPK_BLK // 2, TOPK_BLK // 4, LANES) if n % t == 0)
    groups = tb // LANES
    blk = lambda dt: pltpu.VMEM((PEER_TOPK, groups, LANES), dt)
    return pl.pallas_call(
        _topk_kernel,
        grid=(n // tb,),
        in_specs=[
            pl.BlockSpec((2 * PEER_HEADS, tb, PEER_DHALF), lambda i: (0, i, 0)),
            pl.BlockSpec(keys.shape, lambda i: (0, 0, 0, 0)),
        ],
        out_specs=[pl.BlockSpec((tb, PEER_ROWS), lambda i: (i, 0))] * 2,
        out_shape=[jax.ShapeDtypeStruct((n, PEER_ROWS), jnp.int32),
                   jax.ShapeDtypeStruct((n, PEER_ROWS), jnp.float32)],
        scratch_shapes=[
            pltpu.VMEM((groups * STAGE_STRIDE, LANES), jnp.float32),
            pltpu.VMEM((2, PEER_NKEYS, groups, LANES), jnp.float32),
            pltpu.VMEM((2, PEER_TOPK, groups, LANES), jnp.float32),
            pltpu.VMEM((2, PEER_TOPK, groups, LANES), jnp.float32),
            pltpu.VMEM((len(PAIRS), groups, LANES), jnp.float32),
            blk(jnp.float32), blk(jnp.float32),
            pltpu.VMEM((PEER_ROWS, groups, LANES), jnp.float32),
            pltpu.VMEM((PEER_ROWS, groups, LANES), jnp.float32),
        ],
        compiler_params=_params(("arbitrary",)),
        name="peer_topk",
    )(q16, keys)


def route(h, om, osw, wo, mod, n2, wq, r, c, with_ctx):
    b, t, d = h.shape
    off = 0 if with_ctx else c // r
    nblk = t // r - off
    nctx = c // r
    tq = nblk * r
    full = lambda a: pl.BlockSpec(a.shape, lambda i, j: (0,) * a.ndim)
    row_spec = lambda w: pl.BlockSpec((1, r, w), lambda i, j: (i, j, 0))
    return pl.pallas_call(
        _route_kernel,
        grid=(b, nblk),
        in_specs=[
            pl.BlockSpec((1, r, d), lambda i, j: (i, j + off, 0)),
            row_spec(om.shape[-1]), row_spec(osw.shape[-1]),
            full(wo),
            pl.BlockSpec((1, N_MOD, d), lambda i, j: (jnp.where(j + off < nctx, b, i), 0, 0)),
            full(n2), full(wq),
        ],
        out_specs=[row_spec(d), row_spec(2 * d),
                   pl.BlockSpec((2 * PEER_HEADS, r, PEER_DHALF), lambda i, j: (0, i * nblk + j, 0))],
        out_shape=[
            jax.ShapeDtypeStruct((b, tq, d), jnp.float32),
            jax.ShapeDtypeStruct((b, tq, 2 * d), jnp.bfloat16),
            jax.ShapeDtypeStruct((2 * PEER_HEADS, b * tq, PEER_DHALF), jnp.bfloat16),
        ],
        compiler_params=_params(("parallel", "arbitrary")),
        name="peer_route",
    )(h, om, osw, wo, mod, n2, wq)


def _pack_kernel(x_ref, o_ref):
    x = x_ref[0]
    half = x.shape[1] // 2
    bits = lambda v: pltpu.bitcast(v.astype(jnp.bfloat16).astype(jnp.float32), jnp.uint32)
    w = (bits(x[:, :half]) & jnp.uint32(0xFFFF0000)) | (bits(x[:, half:]) >> 16)
    for c in range(N_CHUNK):
        o_ref[pl.ds(c, x.shape[0], stride=N_CHUNK), :] = w[:, c * LANES:(c + 1) * LANES]


def pack_table(tabs, l):
    _, e, d = tabs.shape
    be = 512
    return pl.pallas_call(
        _pack_kernel,
        grid=(e // be,),
        in_specs=[pl.BlockSpec((1, be, d), lambda i: (l, i, 0))],
        out_specs=pl.BlockSpec((be * N_CHUNK, LANES), lambda i: (i, 0)),
        out_shape=jax.ShapeDtypeStruct((e * N_CHUNK, LANES), jnp.uint32),
        compiler_params=_params(("parallel",)),
        name="pack_table",
    )(tabs)


def _gather_rows(idx_ref, tab_ref, tile_ref, t):
    for j in range(PEER_ROWS):
        r = pl.multiple_of(idx_ref[t, j], N_CHUNK)
        tile_ref[pl.ds(j, N_CHUNK, stride=TILE_STRIDE), :] = tab_ref[pl.ds(r, N_CHUNK), :]


def _token_pipeline(gather, gather_next, compute, tiles):
    steps = TOK_BLK // TOK_SET

    @pl.when(pl.program_id(0) == 0)
    def _():
        for k in range(TOK_SET):
            gather(tiles[0].at[k], k)

    def step(t, cur, nxt, fill):
        for k in range(TOK_SET):
            compute(cur.at[k], t + k)
            fill(nxt.at[k], k)

    def body(i, carry):
        t = TOK_SET * i
        ahead = lambda tile, k: gather(tile, t + TOK_SET + k)

        @pl.when(i % 2 == 0)
        def _():
            step(t, tiles[0], tiles[1], ahead)

        @pl.when(i % 2 == 1)
        def _():
            step(t, tiles[1], tiles[0], ahead)

        return carry

    lax.fori_loop(0, steps - 1, body, 0)
    step(TOK_BLK - TOK_SET, tiles[1], tiles[0], gather_next)


def _chunk(tile_ref, c):
    return pltpu.bitcast(tile_ref[pl.ds(c * TILE_STRIDE, PEER_ROWS), :], jnp.bfloat16)


def _u_kernel(idx_ref, nxt_ref, x_ref, g_ref, tab_ref, w_ref, *tiles):
    row = lax.broadcasted_iota(jnp.int32, (16, 2 * PEER_ROWS), 0) & 7

    def scores(tile_ref, t):
        x16 = x_ref[t]
        top = jnp.zeros((16, 2 * PEER_ROWS), jnp.float32)
        bot = jnp.zeros((16, 2 * PEER_ROWS), jnp.float32)
        for c in range(N_CHUNK):
            y = lax.dot_general(x16, _chunk(tile_ref, c), (((1,), (1,)), ((), ())),
                                preferred_element_type=jnp.float32)
            top = top + jnp.where(row == c, y, 0.0)
            bot = bot + jnp.where(row == c + N_CHUNK, y, 0.0)
        tot = top + pltpu.roll(bot, 1, axis=1)
        a = jnp.sum(tot, axis=0, keepdims=True)
        gelu = 0.5 * a * (1.0 + lax.erf(a * (2.0 ** -0.5)))
        w_ref[t] = g_ref[t] * gelu

    _token_pipeline(functools.partial(_gather_rows, idx_ref, tab_ref),
                    functools.partial(_gather_rows, nxt_ref, tab_ref), scores, tiles)


def _next_tokens_spec(n):
    per_blk = TOK_BLK // TOK_SET
    return pl.BlockSpec((TOK_SET, PEER_ROWS),
                        lambda i: (jnp.minimum((i + 1) * per_blk, n // TOK_SET - 1), 0),
                        memory_space=pltpu.SMEM)


def peer_scores(idx, x16, g, tab):
    n = idx.shape[0]
    tok = lambda *s: pl.BlockSpec((TOK_BLK,) + s, lambda i: (i,) + (0,) * len(s))
    return pl.pallas_call(
        _u_kernel,
        grid=(n // TOK_BLK,),
        in_specs=[
            pl.BlockSpec((TOK_BLK, PEER_ROWS), lambda i: (i, 0), memory_space=pltpu.SMEM),
            _next_tokens_spec(n),
            tok(16, LANES), tok(1, 2 * PEER_ROWS),
            pl.BlockSpec(tab.shape, lambda i: (0, 0), pipeline_mode=pl.Buffered(1)),
        ],
        out_specs=tok(1, 2 * PEER_ROWS),
        out_shape=jax.ShapeDtypeStruct((n, 1, 2 * PEER_ROWS), jnp.float32),
        scratch_shapes=[pltpu.VMEM((TOK_SET, N_CHUNK * TILE_STRIDE, LANES), jnp.uint32)] * 2,
        compiler_params=_params(("arbitrary",)),
        name="peer_scores",
    )(idx, idx, x16, g, tab)


def _v_kernel(idx_ref, nxt_ref, w_ref, h_ref, g2_ref, tab_ref, o_ref, *tiles):
    row = lax.broadcasted_iota(jnp.int32, (16, 2 * PEER_ROWS), 0)

    def combine(tile_ref, t):
        w = w_ref[t]
        w_hi = w.astype(jnp.bfloat16).astype(jnp.float32)
        w_lo = w - w_hi
        parts = (w_hi, pltpu.roll(w_hi, 2 * PEER_ROWS - 1, axis=1),
                 w_lo, pltpu.roll(w_lo, 2 * PEER_ROWS - 1, axis=1))
        acc = jnp.zeros((16, LANES), jnp.float32)
        for c in range(N_CHUNK):
            lhs = jnp.zeros((16, 2 * PEER_ROWS), jnp.float32)
            for k, part in enumerate(parts):
                lhs = jnp.where(row == c + N_CHUNK * k, part, lhs)
            acc = acc + jnp.dot(lhs.astype(jnp.bfloat16), _chunk(tile_ref, c),
                                preferred_element_type=jnp.float32)
        o_ref[t] = h_ref[t] + g2_ref[0, 0] * (acc[:8] + acc[8:])

    _token_pipeline(functools.partial(_gather_rows, idx_ref, tab_ref),
                    functools.partial(_gather_rows, nxt_ref, tab_ref), combine, tiles)


def peer_combine(idx, w, h8, g2, tab, blocks_per_sample, ctx_blocks):
    n = idx.shape[0]
    tok = lambda *s: pl.BlockSpec((TOK_BLK,) + s, lambda i: (i,) + (0,) * len(s))
    g2_map = lambda i: (i // blocks_per_sample,
                        jnp.where(i % blocks_per_sample < ctx_blocks, 0, 1), 0, 0)
    return pl.pallas_call(
        _v_kernel,
        grid=(n // TOK_BLK,),
        in_specs=[
            pl.BlockSpec((TOK_BLK, PEER_ROWS), lambda i: (i, 0), memory_space=pltpu.SMEM),
            _next_tokens_spec(n),
            tok(1, 2 * PEER_ROWS), tok(8, LANES),
            pl.BlockSpec((1, 1, 8, LANES), g2_map),
            pl.BlockSpec(tab.shape, lambda i: (0, 0), pipeline_mode=pl.Buffered(1)),
        ],
        out_specs=tok(8, LANES),
        out_shape=jax.ShapeDtypeStruct((n, 8, LANES), jnp.float32),
        scratch_shapes=[pltpu.VMEM((TOK_SET, N_CHUNK * TILE_STRIDE, LANES), jnp.uint32)] * 2,
        compiler_params=_params(("arbitrary",)),
        name="peer_combine",
    )(idx, idx, w, h8, g2, tab)


SC_LANES = 16
SC_WORKERS = 32
SC_ROWS = 32
SC_SHARE = 0.48


def peer_combine_sc(idx, w, tab):
    m = idx.shape[0]
    d = tab.shape[1]
    per = m // SC_WORKERS
    mesh = plsc.VectorSubcoreMesh(core_axis_name="c", subcore_axis_name="s")

    quarters = PEER_ROWS // SC_ROWS
    group = 16

    @functools.partial(
        pl.kernel, mesh=mesh,
        out_type=jax.ShapeDtypeStruct((m, d), jnp.float32),
        scratch_types=[
            pltpu.VMEM((2, PEER_ROWS), jnp.int32),
            pltpu.VMEM((2 * PEER_ROWS,), jnp.float32),
            pltpu.VMEM((2, SC_ROWS, d), jnp.float32),
            pltpu.VMEM((d,), jnp.float32),
            pltpu.SemaphoreType.DMA((2,)),
        ],
        compiler_params=pltpu.CompilerParams(needs_layout_passes=False),
        name="peer_combine_sc",
    )
    def body(idx_hbm, w_hbm, tab_hbm, out_hbm, idx_v, w_v, rows_v, acc_v, sems):
        base = (lax.axis_index("s") * 2 + lax.axis_index("c")) * per

        def gather(slot, q, buf):
            rows = idx_v.at[slot, pl.ds(q * SC_ROWS, SC_ROWS)]
            return pltpu.make_async_copy(tab_hbm.at[rows], rows_v.at[buf], sems.at[buf])

        def accumulate(buf, q):
            for g0 in range(0, d // SC_LANES, group):
                lanes = [pl.ds((g0 + c) * SC_LANES, SC_LANES) for c in range(group)]

                def row(j, accs):
                    wj = plsc.load_gather(w_v, [jnp.full((SC_LANES,), 2 * q * SC_ROWS + 1, jnp.int32) + 2 * j])
                    return tuple(a + wj * rows_v[buf, j, sl] for a, sl in zip(accs, lanes))

                accs = lax.fori_loop(0, SC_ROWS, row, tuple(acc_v[sl] for sl in lanes))
                for sl, a in zip(lanes, accs):
                    acc_v[sl] = a

        pltpu.sync_copy(idx_hbm.at[base], idx_v.at[0])
        gather(0, 0, 0).start()

        @pl.loop(0, per, step=2)
        def _(i):
            for p in range(2):
                t = base + i + p
                pltpu.sync_copy(w_hbm.at[t], w_v)
                for c in range(d // SC_LANES):
                    acc_v[pl.ds(c * SC_LANES, SC_LANES)] = jnp.zeros((SC_LANES,), jnp.float32)
                for q in range(quarters):
                    buf = q % 2
                    if q + 1 < quarters:
                        gather(p, q + 1, 1 - buf).start()
                    else:
                        nxt = jnp.minimum(t + 1, base + per - 1)
                        pltpu.sync_copy(idx_hbm.at[nxt], idx_v.at[1 - p])
                        gather(1 - p, 0, 1 - buf).start()
                    gather(p, q, buf).wait()
                    accumulate(buf, q)
                pltpu.sync_copy(acc_v, out_hbm.at[t])

        gather(0, 0, 0).wait()

    return body(idx, w, tab)


def _residual_kernel(h_ref, g_ref, a_ref, o_ref):
    o_ref[...] = h_ref[...] + g_ref[0, 0] * a_ref[...]


def gated_residual(h, g2, a, first_row, rows_per_sample, ctx_rows):
    m, d = h.shape
    rb = 256
    blk = pl.BlockSpec((rb, d), lambda i: (i, 0))
    per, ctx, off = rows_per_sample // rb, ctx_rows // rb, first_row // rb
    g_map = lambda i: ((i + off) // per, jnp.where((i + off) % per < ctx, 0, 1), 0, 0)
    return pl.pallas_call(
        _residual_kernel, grid=(m // rb,),
        in_specs=[blk, pl.BlockSpec((1, 1, 1, d), g_map), blk],
        out_specs=blk, out_shape=jax.ShapeDtypeStruct((m, d), jnp.float32),
        compiler_params=_params(("parallel",)), name="gated_residual",
    )(h, g2, a)


def _slots(w, width, offset=0):
    lead = w.shape[:-1]
    n = w.shape[-1] // width
    w = w.reshape(lead + (n, width))
    w = jnp.pad(w, [(0, 0)] * len(lead) + [(0, 0), (offset, LANES - width - offset)])
    return w.reshape(lead + (n * LANES,))


def _rope_tables(rot_dim, lane0, t, c):
    s = t - c
    q = rot_dim // 4
    pos = jnp.arange(s, dtype=jnp.float32)
    rows = jnp.floor(pos / GRID_W)
    cols = pos - rows * GRID_W
    inv = ROPE_BASE ** (-jnp.arange(q, dtype=jnp.float32) / q)
    ar = rows[:, None] * inv
    ac = cols[:, None] * inv
    zero = jnp.zeros_like(ar)
    cos = jnp.cos(jnp.concatenate([ar, ar, ac, ac], axis=-1))
    up = jnp.concatenate([-jnp.sin(ar), zero, -jnp.sin(ac), zero], axis=-1)
    dn = jnp.concatenate([zero, jnp.sin(ar), zero, jnp.sin(ac)], axis=-1)
    pad = lambda a, fill: jnp.pad(
        jnp.pad(a, ((0, 0), (lane0, LANES - lane0 - rot_dim)), constant_values=fill),
        ((c, 0), (0, 0)), constant_values=fill)
    cos = jnp.pad(jnp.pad(cos, ((0, 0), (lane0, LANES - lane0 - rot_dim)), constant_values=1.0),
                  ((c, 0), (0, 0)), constant_values=1.0)
    return jnp.stack([cos, pad(up, 0.0), pad(dn, 0.0)])


def kernel(x, c, ctx, c_ctx, ada_w, ada_b, norm1_g, norm2_g, w_in, mla_qa_g, mla_wuq, mla_kva_g, mla_wukv, mla_qn_g, mla_kn_g, swa_qn_g, swa_kn_g, swa_sink, w_out, peer_wq, peer_keys, peer_u, peer_v):
    b, s, d = x.shape
    nctx = ctx.shape[1]
    t = nctx + s
    depth = ada_w.shape[0]
    r = min(256, nctx)
    bf = jnp.bfloat16

    cond = jnp.zeros((16, d), jnp.float32).at[:b].set(c).at[b].set(c_ctx)
    mod_all = modulation(cond, ada_w, ada_b).reshape(depth, 16, N_MOD, d)
    rope_m = _rope_tables(MLA_ROPE, MLA_NOPE, t, nctx)
    rope_s = _rope_tables(SWA_DIM, 0, t, nctx)

    h = jnp.concatenate([ctx, x], axis=1)
    for l in range(depth):
        last = l == depth - 1
        with_ctx = not last
        mod = mod_all[l]
        wi = w_in[l]
        kv0 = Q_COLS + MLA_KV_RANK
        sk0 = kv0 + MLA_ROPE
        sv0 = sk0 + SWA_KV_HEADS * SWA_DIM
        sv = wi[:, sv0:].reshape(d, SWA_KV_HEADS, 1, SWA_DIM)
        win = jnp.concatenate([
            wi[:, :MLA_Q_RANK], wi[:, Q_COLS:kv0],
            _slots(wi[:, kv0:sk0], MLA_ROPE, MLA_NOPE),
            _slots(wi[:, MLA_Q_RANK:Q_COLS], SWA_DIM),
            _slots(wi[:, sk0:sv0], SWA_DIM),
            jnp.broadcast_to(sv, (d, SWA_KV_HEADS, 2, SWA_DIM)).reshape(d, SWA_KV_HEADS * LANES),
        ], axis=1).astype(bf)
        wuq = _slots(mla_wuq[l], MLA_QK).astype(bf)
        wukv = mla_wukv[l].reshape(MLA_KV_RANK, MLA_HEADS, MLA_NOPE + MLA_V)
        wuk = _slots(wukv[:, :, :MLA_NOPE].reshape(MLA_KV_RANK, -1), MLA_NOPE).astype(bf)
        wv = wukv[:, :, MLA_NOPE:].reshape(MLA_KV_RANK, MLA_HEADS // 2, 2, MLA_V)
        zero = jnp.zeros_like(wv[:, :, 0])
        wuv = jnp.stack([jnp.concatenate([wv[:, :, 0], zero], axis=-1),
                         jnp.concatenate([zero, wv[:, :, 1]], axis=-1)], axis=2)
        wuv = wuv.reshape(MLA_KV_RANK, MLA_HEADS * LANES).astype(bf)
        row = lambda g: g.reshape(1, -1)

        qm, km, vm, qs, ks, vlo, vhi = projections(
            h, mod, row(norm1_g[l]), win, row(mla_qa_g[l]), wuq, row(mla_kva_g[l]), wuk, wuv,
            row(_slots(mla_qn_g[l], MLA_QK)), row(_slots(mla_kn_g[l], MLA_QK)),
            row(_slots(swa_qn_g[l], SWA_DIM)), row(_slots(swa_kn_g[l], SWA_DIM)),
            rope_m, rope_s, r, nctx)
        om = mla_attention(qm, km, vm, r, nctx, with_ctx)
        osw = swa_attention(swa_sink[l], qs, ks, vlo, vhi, nctx, with_ctx)
        h1, x2, q16 = route(
            h, om, osw, w_out[l].astype(bf), mod, row(norm2_g[l]), peer_wq[l].astype(bf),
            r, nctx, with_ctx)
        idx, gate = peer_topk(q16, peer_keys[l].astype(bf))

        tl = h1.shape[1]
        n = b * tl
        x16 = x2.reshape(n, 16, LANES)
        gate = jnp.stack([jnp.zeros_like(gate), gate], axis=-1).reshape(n, 1, 2 * PEER_ROWS)
        g2 = jnp.stack([jnp.broadcast_to(mod[b, 5], (b, d)), mod[:b, 5]], axis=1)
        n_sc = int(n * SC_SHARE) // 256 * 256
        n_tc = n - n_sc
        ctx_rows = nctx if with_ctx else 0
        h1f = h1.reshape(n, d)
        utab = pack_table(peer_u, l)
        w_sc = peer_scores(idx[n_tc:], x16[n_tc:], gate[n_tc:], utab)
        experts = (idx[n_tc:] >> 2) + l * peer_v.shape[1]
        acc_sc = peer_combine_sc(experts, w_sc.reshape(n_sc, 2 * PEER_ROWS), peer_v.reshape(-1, d))
        w = peer_scores(idx[:n_tc], x16[:n_tc], gate[:n_tc], utab)
        h_tc = peer_combine(idx[:n_tc], w, h1f[:n_tc].reshape(n_tc, 8, LANES),
                            g2.reshape(b, 2, 8, LANES), pack_table(peer_v, l), tl // TOK_BLK,
                            ctx_rows // TOK_BLK).reshape(n_tc, d)
        h_sc = gated_residual(h1f[n_tc:], g2.reshape(b, 2, 1, d), acc_sc, n_tc, tl, ctx_rows)
        h = jnp.concatenate([h_tc, h_sc], axis=0).reshape(b, tl, d)
    return h
```

```python
import functools
import jax
import jax.numpy as jnp
from jax import lax
from jax.experimental import pallas as pl
from jax.experimental.pallas import tpu as pltpu
from jax.experimental.pallas import tpu_sc as plsc

LANES = 128
EPS = 1e-6
ROPE_BASE = 10000.0
GRID_W = 64
N_MOD = 6

MLA_HEADS = 8
MLA_NOPE = 64
MLA_ROPE = 32
MLA_QK = MLA_NOPE + MLA_ROPE
MLA_V = 64
MLA_Q_RANK = 384
MLA_KV_RANK = 256
SWA_HEADS = 8
SWA_KV_HEADS = 2
SWA_GROUP = SWA_HEADS // SWA_KV_HEADS
SWA_DIM = 64
WINDOW = 128
WIN_KEYS = 3 * WINDOW
Q_COLS = MLA_Q_RANK + SWA_HEADS * SWA_DIM

PEER_HEADS = 8
PEER_NKEYS = 128
PEER_DHALF = 128
PEER_TOPK = 16
PEER_ROWS = PEER_HEADS * PEER_TOPK
N_CHUNK = 4
TILE_STRIDE = PEER_ROWS + 1
TOPK_BLK = 1024
STAGE_STRIDE = PEER_NKEYS + 8
TOK_BLK = 128
TOK_SET = 8

OFF_QA = 0
OFF_KVA = OFF_QA + MLA_Q_RANK
OFF_KR = OFF_KVA + MLA_KV_RANK
OFF_SQ = OFF_KR + LANES
OFF_SK = OFF_SQ + SWA_HEADS * LANES
OFF_SV = OFF_SK + SWA_KV_HEADS * LANES
N_IN = OFF_SV + SWA_KV_HEADS * LANES

VMEM_LIMIT = 56 * 1024 * 1024
NEG_INF = float("-inf")
LOG2_E = 1.4426950408889634


def _params(sem, vmem=VMEM_LIMIT):
    return pltpu.CompilerParams(dimension_semantics=sem, vmem_limit_bytes=vmem)


def _rms(x, g, n):
    ms = jnp.sum(x * x, axis=-1, keepdims=True) * (1.0 / n)
    return x * lax.rsqrt(ms + EPS) * g


def _rope(x, cos, sin_up, sin_dn, shift):
    return (x * cos + pltpu.roll(x, LANES - shift, axis=1) * sin_up
            + pltpu.roll(x, shift, axis=1) * sin_dn)


def _mod_kernel(c_ref, w_ref, b_ref, o_ref):
    c = c_ref[...]
    s = c * (1.0 / (1.0 + jnp.exp(-c)))
    o_ref[0] = jnp.dot(s, w_ref[0], preferred_element_type=jnp.float32,
                       precision=lax.Precision.HIGHEST) + b_ref[0]


def modulation(cond, ada_w, ada_b):
    nl, d, n6 = ada_w.shape
    tn = 1536
    return pl.pallas_call(
        _mod_kernel,
        grid=(nl, n6 // tn),
        in_specs=[
            pl.BlockSpec(cond.shape, lambda l, j: (0, 0)),
            pl.BlockSpec((1, d, tn), lambda l, j: (l, 0, j)),
            pl.BlockSpec((1, 1, tn), lambda l, j: (l, 0, j)),
        ],
        out_specs=pl.BlockSpec((1, cond.shape[0], tn), lambda l, j: (l, 0, j)),
        out_shape=jax.ShapeDtypeStruct((nl, cond.shape[0], n6), jnp.float32),
        compiler_params=_params(("arbitrary", "arbitrary")),
        name="modulation",
    )(cond, ada_w, ada_b.reshape(nl, 1, n6))


def _proj_kernel(h_ref, mod_ref, n1_ref, win_ref, qag_ref, wuq_ref, kvag_ref, wuk_ref, wuv_ref,
                 qn_ref, kn_ref, sqn_ref, skn_ref, rm_ref, rs_ref,
                 qm_ref, km_ref, vm_ref, qs_ref, ks_ref, vlo_ref, vhi_ref):
    h = h_ref[0]
    d = h.shape[-1]
    mod = mod_ref[0]
    a = _rms(h, n1_ref[...], d) * (1.0 + mod[1:2]) + mod[0:1]
    p = jnp.dot(a.astype(jnp.bfloat16), win_ref[...], preferred_element_type=jnp.float32)

    cm, sm_up, sm_dn = rm_ref[0], rm_ref[1], rm_ref[2]
    cs, ss_up, ss_dn = rs_ref[0], rs_ref[1], rs_ref[2]

    qa = _rms(p[:, OFF_QA:OFF_QA + MLA_Q_RANK], qag_ref[...], MLA_Q_RANK)
    q = jnp.dot(qa.astype(jnp.bfloat16), wuq_ref[...], preferred_element_type=jnp.float32)
    kva = _rms(p[:, OFF_KVA:OFF_KVA + MLA_KV_RANK], kvag_ref[...], MLA_KV_RANK).astype(jnp.bfloat16)
    kn = jnp.dot(kva, wuk_ref[...], preferred_element_type=jnp.float32)
    vm = jnp.dot(kva, wuv_ref[...], preferred_element_type=jnp.float32)
    kr = p[:, OFF_KR:OFF_KR + LANES]
    for hd in range(MLA_HEADS):
        sl = slice(hd * LANES, (hd + 1) * LANES)
        qh = _rope(_rms(q[:, sl], qn_ref[...], MLA_QK), cm, sm_up, sm_dn, MLA_ROPE // 4)
        qm_ref[0, hd] = (qh * (MLA_QK ** -0.5 * LOG2_E)).astype(jnp.bfloat16)
        kh = _rope(_rms(kn[:, sl] + kr, kn_ref[...], MLA_QK), cm, sm_up, sm_dn, MLA_ROPE // 4)
        km_ref[0, hd] = kh.astype(jnp.bfloat16)
        vm_ref[0, hd] = vm[:, sl].astype(jnp.bfloat16)
    for hd in range(SWA_HEADS):
        x = p[:, OFF_SQ + hd * LANES:OFF_SQ + (hd + 1) * LANES]
        qh = _rope(_rms(x, sqn_ref[...], SWA_DIM), cs, ss_up, ss_dn, SWA_DIM // 4)
        qs_ref[0, hd] = (qh * (SWA_DIM ** -0.5)).astype(jnp.bfloat16)
    lane = lax.broadcasted_iota(jnp.int32, (h.shape[0], LANES), 1)
    for g in range(SWA_KV_HEADS):
        x = p[:, OFF_SK + g * LANES:OFF_SK + (g + 1) * LANES]
        kh = _rope(_rms(x, skn_ref[...], SWA_DIM), cs, ss_up, ss_dn, SWA_DIM // 4)
        ks_ref[0, g] = kh.astype(jnp.bfloat16)
        v = p[:, OFF_SV + g * LANES:OFF_SV + (g + 1) * LANES]
        vlo_ref[0, g] = jnp.where(lane < SWA_DIM, v, 0.0).astype(jnp.bfloat16)
        vhi_ref[0, g] = jnp.where(lane >= SWA_DIM, v, 0.0).astype(jnp.bfloat16)


def projections(h, mod, n1, win, qag, wuq, kvag, wuk, wuv, qn, kn, sqn, skn, rope_m, rope_s, r, c):
    b, t, d = h.shape
    nctx = c // r
    full = lambda a: pl.BlockSpec(a.shape, lambda i, j: (0,) * a.ndim)
    head_out = lambda nh: pl.BlockSpec((1, nh, r, LANES), lambda i, j: (i, 0, j, 0))
    head_shape = lambda nh: jax.ShapeDtypeStruct((b, nh, t, LANES), jnp.bfloat16)
    return pl.pallas_call(
        _proj_kernel,
        grid=(b, t // r),
        in_specs=[
            pl.BlockSpec((1, r, d), lambda i, j: (i, j, 0)),
            pl.BlockSpec((1, N_MOD, d), lambda i, j: (jnp.where(j < nctx, b, i), 0, 0)),
            full(n1), full(win), full(qag), full(wuq), full(kvag), full(wuk), full(wuv),
            full(qn), full(kn), full(sqn), full(skn),
            pl.BlockSpec((3, r, LANES), lambda i, j: (0, j, 0)),
            pl.BlockSpec((3, r, LANES), lambda i, j: (0, j, 0)),
        ],
        out_specs=[head_out(MLA_HEADS), head_out(MLA_HEADS), head_out(MLA_HEADS),
                   head_out(SWA_HEADS), head_out(SWA_KV_HEADS), head_out(SWA_KV_HEADS),
                   head_out(SWA_KV_HEADS)],
        out_shape=[head_shape(MLA_HEADS), head_shape(MLA_HEADS), head_shape(MLA_HEADS),
                   head_shape(SWA_HEADS), head_shape(SWA_KV_HEADS), head_shape(SWA_KV_HEADS),
                   head_shape(SWA_KV_HEADS)],
        compiler_params=_params(("parallel", "arbitrary")),
        name="projections",
    )(h, mod, n1, win, qag, wuq, kvag, wuk, wuv, qn, kn, sqn, skn, rope_m, rope_s)


def _mla_kernel(q_ref, k_ref, v_ref, o_ref, *, nctx_blocks, c):
    qi = pl.program_id(2)

    def attend(nk):
        acc = None
        for i in range(2):
            q = q_ref[0, i]
            s = lax.dot_general(q, k_ref[0, i, :nk], (((1,), (1,)), ((), ())),
                                preferred_element_type=jnp.float32)
            m = jnp.max(s, axis=-1, keepdims=True)
            p = jnp.exp2(s - m)
            l = jnp.sum(p, axis=-1, keepdims=True)
            o = jnp.dot(p.astype(jnp.bfloat16), v_ref[0, i, :nk], preferred_element_type=jnp.float32)
            o = o * (1.0 / l)
            acc = o if acc is None else acc + o
        o_ref[0] = acc.astype(o_ref.dtype)

    if nctx_blocks:
        @pl.when(qi < nctx_blocks)
        def _():
            attend(c)

        @pl.when(qi >= nctx_blocks)
        def _():
            attend(k_ref.shape[2])
    else:
        attend(k_ref.shape[2])


def mla_attention(qm, km, vm, tq, c, with_ctx):
    b, nh, t, _ = qm.shape
    off = 0 if with_ctx else c // tq
    nq = t // tq - off
    return pl.pallas_call(
        functools.partial(_mla_kernel, nctx_blocks=(c // tq if with_ctx else 0), c=c),
        grid=(b, nh // 2, nq),
        in_specs=[
            pl.BlockSpec((1, 2, tq, LANES), lambda i, hp, j: (i, hp, j + off, 0)),
            pl.BlockSpec((1, 2, t, LANES), lambda i, hp, j: (i, hp, 0, 0)),
            pl.BlockSpec((1, 2, t, LANES), lambda i, hp, j: (i, hp, 0, 0)),
        ],
        out_specs=pl.BlockSpec((1, tq, LANES), lambda i, hp, j: (i, j, hp)),
        out_shape=jax.ShapeDtypeStruct((b, nq * tq, nh // 2 * LANES), jnp.bfloat16),
        compiler_params=_params(("parallel", "arbitrary", "arbitrary")),
        name="mla_attention",
    )(qm, km, vm)


def _swa_kernel(sink_ref, q_ref, k_ref, vlo_ref, vhi_ref, o_ref, *, off, c):
    g = pl.program_id(1)
    qi = pl.program_id(2) + off
    t = k_ref.shape[2]
    rows = SWA_GROUP * WINDOW
    q = q_ref[0].reshape(rows, LANES)
    r_idx = lax.broadcasted_iota(jnp.int32, (rows, 1), 0)
    sink = jnp.zeros((rows, 1), jnp.float32)
    for i in range(SWA_GROUP):
        sink = jnp.where(r_idx // WINDOW == i, sink_ref[g * SWA_GROUP + i], sink)

    ws = pl.multiple_of(jnp.clip(qi * WINDOW - WINDOW, c, t - WIN_KEYS), WINDOW)
    nt = (((1,), (1,)), ((), ()))
    s_ctx = lax.dot_general(q, k_ref[0, 0, :c], nt, preferred_element_type=jnp.float32)
    s_loc = lax.dot_general(q, k_ref[0, 0, pl.ds(ws, WIN_KEYS)], nt, preferred_element_type=jnp.float32)
    qpos = qi * WINDOW + (r_idx % WINDOW)
    kpos = ws + lax.broadcasted_iota(jnp.int32, (1, WIN_KEYS), 1)
    reach = jnp.where(qi * WINDOW >= c, WINDOW, -1)
    s_loc = jnp.where(jnp.abs(qpos - kpos) <= reach, s_loc, NEG_INF)
    m = jnp.maximum(jnp.maximum(jnp.max(s_ctx, axis=-1, keepdims=True),
                                jnp.max(s_loc, axis=-1, keepdims=True)), sink)
    p_ctx = jnp.exp(s_ctx - m)
    p_loc = jnp.exp(s_loc - m)
    l = (jnp.sum(p_ctx, axis=-1, keepdims=True) + jnp.sum(p_loc, axis=-1, keepdims=True)
         + jnp.exp(sink - m))
    inv = 1.0 / l
    p_ctx = p_ctx.astype(jnp.bfloat16)
    p_loc = p_loc.astype(jnp.bfloat16)
    outs = []
    for pair in range(SWA_GROUP // 2):
        acc = None
        for i, v_ref in enumerate((vlo_ref, vhi_ref)):
            rs = slice((2 * pair + i) * WINDOW, (2 * pair + i + 1) * WINDOW)
            o = (jnp.dot(p_ctx[rs], v_ref[0, 0, :c], preferred_element_type=jnp.float32)
                 + jnp.dot(p_loc[rs], v_ref[0, 0, pl.ds(ws, WIN_KEYS)], preferred_element_type=jnp.float32))
            o = o * inv[rs]
            acc = o if acc is None else acc + o
        outs.append(acc)
    o_ref[0] = jnp.concatenate(outs, axis=-1).astype(o_ref.dtype)


def swa_attention(sink, qs, ks, vlo, vhi, c, with_ctx):
    b, nh, t, _ = qs.shape
    off = 0 if with_ctx else c // WINDOW
    nq = t // WINDOW - off
    kv_spec = pl.BlockSpec((1, 1, t, LANES), lambda i, g, j: (i, g, 0, 0))
    return pl.pallas_call(
        functools.partial(_swa_kernel, off=off, c=c),
        grid=(b, SWA_KV_HEADS, nq),
        in_specs=[
            pl.BlockSpec(memory_space=pltpu.SMEM),
            pl.BlockSpec((1, SWA_GROUP, WINDOW, LANES), lambda i, g, j: (i, g, j + off, 0)),
            kv_spec, kv_spec, kv_spec,
        ],
        out_specs=pl.BlockSpec((1, WINDOW, SWA_GROUP // 2 * LANES), lambda i, g, j: (i, j, g)),
        out_shape=jax.ShapeDtypeStruct((b, nq * WINDOW, SWA_HEADS // 2 * LANES), jnp.bfloat16),
        compiler_params=_params(("parallel", "arbitrary", "arbitrary")),
        name="swa_attention",
    )(sink, qs, ks, vlo, vhi)


def _tree(op, xs):
    xs = list(xs)
    while len(xs) > 1:
        xs = [op(xs[i], xs[i + 1]) for i in range(0, len(xs) - 1, 2)] + (xs[-1:] if len(xs) % 2 else [])
    return xs[0]


def _top16_sweeps(ids, *problems):
    n = len(ids)
    big = float(max(ids) + 1)
    order = sorted(range(n), key=lambda k: ids[k])
    runs = [order[i:i + PEER_TOPK] for i in range(0, n, PEER_TOPK)]

    def step(r, carry):
        for val_ref, out_v, out_i in problems:
            m = _tree(jnp.maximum, [val_ref[k] for k in range(n)])
            firsts = []
            for run in runs:
                am = jnp.full(m.shape, big, jnp.float32)
                for k in reversed(run):
                    am = jnp.where(val_ref[k] == m, float(ids[k]), am)
                firsts.append(am)
            am = _tree(jnp.minimum, firsts)
            for k in range(n):
                val_ref[k] = jnp.where(am == float(ids[k]), NEG_INF, val_ref[k])
            out_v[r] = m
            out_i[r] = am
        return carry

    lax.fori_loop(0, PEER_TOPK, step, 0)


PAIRS = [(a, b) for a in range(PEER_TOPK) for b in range(PEER_TOPK) if (a + 1) * (b + 1) <= PEER_TOPK]


def _route_kernel(h_ref, om_ref, os_ref, wo_ref, mod_ref, n2_ref, wq_ref,
                  h1_ref, x2_ref, q_ref):
    h = h_ref[0]
    d = h.shape[-1]
    mod = mod_ref[0]
    half = om_ref.shape[-1]
    mix = (jnp.dot(om_ref[0], wo_ref[:half], preferred_element_type=jnp.float32)
           + jnp.dot(os_ref[0], wo_ref[half:], preferred_element_type=jnp.float32))
    h1 = h + mod[2:3] * mix
    h1_ref[0] = h1
    x = _rms(h1, n2_ref[...], d) * (1.0 + mod[4:5]) + mod[3:4]
    xhi = x.astype(jnp.bfloat16)
    x2_ref[0, :, :d] = xhi
    x2_ref[0, :, d:] = (x - xhi.astype(jnp.float32)).astype(jnp.bfloat16)
    q = jnp.dot(xhi, wq_ref[...], preferred_element_type=jnp.float32)
    for k in range(2 * PEER_HEADS):
        q_ref[k] = q[:, k * PEER_DHALF:(k + 1) * PEER_DHALF].astype(jnp.bfloat16)


def _topk_kernel(q_ref, keys_ref, idx_ref, g_ref, stage, vals, sv, si, cand, cv, ci, out_i, out_g):
    groups = vals.shape[2]

    def head(hh, carry):
        for part in range(2):
            st = lax.dot_general(keys_ref[hh, part], q_ref[2 * hh + part], (((1,), (1,)), ((), ())),
                                 preferred_element_type=jnp.float32)
            for g in range(groups):
                stage[pl.ds(g * STAGE_STRIDE, PEER_NKEYS), :] = st[:, g * LANES:(g + 1) * LANES]
            for k in range(PEER_NKEYS):
                vals[part, k] = stage[pl.ds(k, groups, stride=STAGE_STRIDE), :]
        _top16_sweeps(list(range(PEER_NKEYS)), *[(vals.at[p], sv.at[p], si.at[p]) for p in range(2)])
        for n, (a, b) in enumerate(PAIRS):
            cand[n] = sv[0, a] + sv[1, b]
        _top16_sweeps([a * PEER_TOPK + b for a, b in PAIRS], (cand, cv, ci))
        top = cv[0]
        e = [jnp.exp(cv[r] - top) for r in range(PEER_TOPK)]
        inv = 1.0 / _tree(jnp.add, e)
        for r in range(PEER_TOPK):
            ia = jnp.floor(ci[r] * (1.0 / PEER_TOPK))
            ib = ci[r] - ia * PEER_TOPK
            i1 = jnp.zeros_like(ia)
            i2 = jnp.zeros_like(ia)
            for a in range(PEER_TOPK):
                i1 = jnp.where(ia == a, si[0, a], i1)
                i2 = jnp.where(ib == a, si[1, a], i2)
            out_i[hh * PEER_TOPK + r] = (i1 * PEER_NKEYS + i2) * N_CHUNK
            out_g[hh * PEER_TOPK + r] = e[r] * inv
        return carry

    lax.fori_loop(0, PEER_HEADS, head, 0)
    for g in range(groups):
        rows = slice(g * LANES, (g + 1) * LANES)
        idx_ref[rows, :] = out_i[:, g, :].T.astype(jnp.int32)
        g_ref[rows, :] = out_g[:, g, :].T


def peer_topk(q16, keys):
    n = q16.shape[1]
    tb = next(t for t in (TOPK_BLK, TOPK_BLK // 2, TOPK_BLK // 4, LANES) if n % t == 0)
    groups = tb // LANES
    blk = lambda dt: pltpu.VMEM((PEER_TOPK, groups, LANES), dt)
    return pl.pallas_call(
        _topk_kernel,
        grid=(n // tb,),
        in_specs=[
            pl.BlockSpec((2 * PEER_HEADS, tb, PEER_DHALF), lambda i: (0, i, 0)),
            pl.BlockSpec(keys.shape, lambda i: (0, 0, 0, 0)),
        ],
        out_specs=[pl.BlockSpec((tb, PEER_ROWS), lambda i: (i, 0))] * 2,
        out_shape=[jax.ShapeDtypeStruct((n, PEER_ROWS), jnp.int32),
                   jax.ShapeDtypeStruct((n, PEER_ROWS), jnp.float32)],
        scratch_shapes=[
            pltpu.VMEM((groups * STAGE_STRIDE, LANES), jnp.float32),
            pltpu.VMEM((2, PEER_NKEYS, groups, LANES), jnp.float32),
            pltpu.VMEM((2, PEER_TOPK, groups, LANES), jnp.float32),
            pltpu.VMEM((2, PEER_TOPK, groups, LANES), jnp.float32),
            pltpu.VMEM((len(PAIRS), groups, LANES), jnp.float32),
            blk(jnp.float32), blk(jnp.float32),
            pltpu.VMEM((PEER_ROWS, groups, LANES), jnp.float32),
            pltpu.VMEM((PEER_ROWS, groups, LANES), jnp.float32),
        ],
        compiler_params=_params(("arbitrary",)),
        name="peer_topk",
    )(q16, keys)


def route(h, om, osw, wo, mod, n2, wq, r, c, with_ctx):
    b, t, d = h.shape
    off = 0 if with_ctx else c // r
    nblk = t // r - off
    nctx = c // r
    tq = nblk * r
    full = lambda a: pl.BlockSpec(a.shape, lambda i, j: (0,) * a.ndim)
    row_spec = lambda w: pl.BlockSpec((1, r, w), lambda i, j: (i, j, 0))
    return pl.pallas_call(
        _route_kernel,
        grid=(b, nblk),
        in_specs=[
            pl.BlockSpec((1, r, d), lambda i, j: (i, j + off, 0)),
            row_spec(om.shape[-1]), row_spec(osw.shape[-1]),
            full(wo),
            pl.BlockSpec((1, N_MOD, d), lambda i, j: (jnp.where(j + off < nctx, b, i), 0, 0)),
            full(n2), full(wq),
        ],
        out_specs=[row_spec(d), row_spec(2 * d),
                   pl.BlockSpec((2 * PEER_HEADS, r, PEER_DHALF), lambda i, j: (0, i * nblk + j, 0))],
        out_shape=[
            jax.ShapeDtypeStruct((b, tq, d), jnp.float32),
            jax.ShapeDtypeStruct((b, tq, 2 * d), jnp.bfloat16),
            jax.ShapeDtypeStruct((2 * PEER_HEADS, b * tq, PEER_DHALF), jnp.bfloat16),
        ],
        compiler_params=_params(("parallel", "arbitrary")),
        name="peer_route",
    )(h, om, osw, wo, mod, n2, wq)


def _pack_kernel(x_ref, o_ref):
    x = x_ref[0]
    half = x.shape[1] // 2
    bits = lambda v: pltpu.bitcast(v.astype(jnp.bfloat16).astype(jnp.float32), jnp.uint32)
    w = (bits(x[:, :half]) & jnp.uint32(0xFFFF0000)) | (bits(x[:, half:]) >> 16)
    for c in range(N_CHUNK):
        o_ref[pl.ds(c, x.shape[0], stride=N_CHUNK), :] = w[:, c * LANES:(c + 1) * LANES]


def pack_table(tabs, l):
    _, e, d = tabs.shape
    be = 512
    return pl.pallas_call(
        _pack_kernel,
        grid=(e // be,),
        in_specs=[pl.BlockSpec((1, be, d), lambda i: (l, i, 0))],
        out_specs=pl.BlockSpec((be * N_CHUNK, LANES), lambda i: (i, 0)),
        out_shape=jax.ShapeDtypeStruct((e * N_CHUNK, LANES), jnp.uint32),
        compiler_params=_params(("parallel",)),
        name="pack_table",
    )(tabs)


def _gather_rows(idx_ref, tab_ref, tile_ref, t):
    for j in range(PEER_ROWS):
        r = pl.multiple_of(idx_ref[t, j], N_CHUNK)
        tile_ref[pl.ds(j, N_CHUNK, stride=TILE_STRIDE), :] = tab_ref[pl.ds(r, N_CHUNK), :]


def _token_pipeline(gather, gather_next, compute, tiles):
    steps = TOK_BLK // TOK_SET

    @pl.when(pl.program_id(0) == 0)
    def _():
        for k in range(TOK_SET):
            gather(tiles[0].at[k], k)

    def step(t, cur, nxt, fill):
        for k in range(TOK_SET):
            compute(cur.at[k], t + k)
            fill(nxt.at[k], k)

    def body(i, carry):
        t = TOK_SET * i
        ahead = lambda tile, k: gather(tile, t + TOK_SET + k)

        @pl.when(i % 2 == 0)
        def _():
            step(t, tiles[0], tiles[1], ahead)

        @pl.when(i % 2 == 1)
        def _():
            step(t, tiles[1], tiles[0], ahead)

        return carry

    lax.fori_loop(0, steps - 1, body, 0)
    step(TOK_BLK - TOK_SET, tiles[1], tiles[0], gather_next)


def _chunk(tile_ref, c):
    return pltpu.bitcast(tile_ref[pl.ds(c * TILE_STRIDE, PEER_ROWS), :], jnp.bfloat16)


def _u_kernel(idx_ref, nxt_ref, x_ref, g_ref, tab_ref, w_ref, *tiles):
    row = lax.broadcasted_iota(jnp.int32, (16, 2 * PEER_ROWS), 0) & 7

    def scores(tile_ref, t):
        x16 = x_ref[t]
        top = jnp.zeros((16, 2 * PEER_ROWS), jnp.float32)
        bot = jnp.zeros((16, 2 * PEER_ROWS), jnp.float32)
        for c in range(N_CHUNK):
            y = lax.dot_general(x16, _chunk(tile_ref, c), (((1,), (1,)), ((), ())),
                                preferred_element_type=jnp.float32)
            top = top + jnp.where(row == c, y, 0.0)
            bot = bot + jnp.where(row == c + N_CHUNK, y, 0.0)
        tot = top + pltpu.roll(bot, 1, axis=1)
        a = jnp.sum(tot, axis=0, keepdims=True)
        gelu = 0.5 * a * (1.0 + lax.erf(a * (2.0 ** -0.5)))
        w_ref[t] = g_ref[t] * gelu

    _token_pipeline(functools.partial(_gather_rows, idx_ref, tab_ref),
                    functools.partial(_gather_rows, nxt_ref, tab_ref), scores, tiles)


def _next_tokens_spec(n):
    per_blk = TOK_BLK // TOK_SET
    return pl.BlockSpec((TOK_SET, PEER_ROWS),
                        lambda i: (jnp.minimum((i + 1) * per_blk, n // TOK_SET - 1), 0),
                        memory_space=pltpu.SMEM)


def peer_scores(idx, x16, g, tab):
    n = idx.shape[0]
    tok = lambda *s: pl.BlockSpec((TOK_BLK,) + s, lambda i: (i,) + (0,) * len(s))
    return pl.pallas_call(
        _u_kernel,
        grid=(n // TOK_BLK,),
        in_specs=[
            pl.BlockSpec((TOK_BLK, PEER_ROWS), lambda i: (i, 0), memory_space=pltpu.SMEM),
            _next_tokens_spec(n),
            tok(16, LANES), tok(1, 2 * PEER_ROWS),
            pl.BlockSpec(tab.shape, lambda i: (0, 0), pipeline_mode=pl.Buffered(1)),
        ],
        out_specs=tok(1, 2 * PEER_ROWS),
        out_shape=jax.ShapeDtypeStruct((n, 1, 2 * PEER_ROWS), jnp.float32),
        scratch_shapes=[pltpu.VMEM((TOK_SET, N_CHUNK * TILE_STRIDE, LANES), jnp.uint32)] * 2,
        compiler_params=_params(("arbitrary",)),
        name="peer_scores",
    )(idx, idx, x16, g, tab)


def _v_kernel(idx_ref, nxt_ref, w_ref, h_ref, g2_ref, tab_ref, o_ref, *tiles):
    row = lax.broadcasted_iota(jnp.int32, (16, 2 * PEER_ROWS), 0)

    def combine(tile_ref, t):
        w = w_ref[t]
        w_hi = w.astype(jnp.bfloat16).astype(jnp.float32)
        w_lo = w - w_hi
        parts = (w_hi, pltpu.roll(w_hi, 2 * PEER_ROWS - 1, axis=1),
                 w_lo, pltpu.roll(w_lo, 2 * PEER_ROWS - 1, axis=1))
        acc = jnp.zeros((16, LANES), jnp.float32)
        for c in range(N_CHUNK):
            lhs = jnp.zeros((16, 2 * PEER_ROWS), jnp.float32)
            for k, part in enumerate(parts):
                lhs = jnp.where(row == c + N_CHUNK * k, part, lhs)
            acc = acc + jnp.dot(lhs.astype(jnp.bfloat16), _chunk(tile_ref, c),
                                preferred_element_type=jnp.float32)
        o_ref[t] = h_ref[t] + g2_ref[0, 0] * (acc[:8] + acc[8:])

    _token_pipeline(functools.partial(_gather_rows, idx_ref, tab_ref),
                    functools.partial(_gather_rows, nxt_ref, tab_ref), combine, tiles)


def peer_combine(idx, w, h8, g2, tab, blocks_per_sample, ctx_blocks):
    n = idx.shape[0]
    tok = lambda *s: pl.BlockSpec((TOK_BLK,) + s, lambda i: (i,) + (0,) * len(s))
    g2_map = lambda i: (i // blocks_per_sample,
                        jnp.where(i % blocks_per_sample < ctx_blocks, 0, 1), 0, 0)
    return pl.pallas_call(
        _v_kernel,
        grid=(n // TOK_BLK,),
        in_specs=[
            pl.BlockSpec((TOK_BLK, PEER_ROWS), lambda i: (i, 0), memory_space=pltpu.SMEM),
            _next_tokens_spec(n),
            tok(1, 2 * PEER_ROWS), tok(8, LANES),
            pl.BlockSpec((1, 1, 8, LANES), g2_map),
            pl.BlockSpec(tab.shape, lambda i: (0, 0), pipeline_mode=pl.Buffered(1)),
        ],
        out_specs=tok(8, LANES),
        out_shape=jax.ShapeDtypeStruct((n, 8, LANES), jnp.float32),
        scratch_shapes=[pltpu.VMEM((TOK_SET, N_CHUNK * TILE_STRIDE, LANES), jnp.uint32)] * 2,
        compiler_params=_params(("arbitrary",)),
        name="peer_combine",
    )(idx, idx, w, h8, g2, tab)


SC_LANES = 16
SC_WORKERS = 32
SC_ROWS = 32
SC_SHARE_LAST = 0.48
GROUPS = 2


def peer_combine_sc(idx, w, tab):
    m = idx.shape[0]
    d = tab.shape[1]
    per = m // SC_WORKERS
    mesh = plsc.VectorSubcoreMesh(core_axis_name="c", subcore_axis_name="s")

    quarters = PEER_ROWS // SC_ROWS
    group = 16

    @functools.partial(
        pl.kernel, mesh=mesh,
        out_type=jax.ShapeDtypeStruct((m, d), jnp.float32),
        scratch_types=[
            pltpu.VMEM((2, PEER_ROWS), jnp.int32),
            pltpu.VMEM((2 * PEER_ROWS,), jnp.float32),
            pltpu.VMEM((2, SC_ROWS, d), jnp.float32),
            pltpu.VMEM((d,), jnp.float32),
            pltpu.SemaphoreType.DMA((2,)),
        ],
        compiler_params=pltpu.CompilerParams(needs_layout_passes=False),
        name="peer_combine_sc",
    )
    def body(idx_hbm, w_hbm, tab_hbm, out_hbm, idx_v, w_v, rows_v, acc_v, sems):
        base = (lax.axis_index("s") * 2 + lax.axis_index("c")) * per

        def gather(slot, q, buf):
            rows = idx_v.at[slot, pl.ds(q * SC_ROWS, SC_ROWS)]
            return pltpu.make_async_copy(tab_hbm.at[rows], rows_v.at[buf], sems.at[buf])

        def accumulate(buf, q):
            for g0 in range(0, d // SC_LANES, group):
                lanes = [pl.ds((g0 + c) * SC_LANES, SC_LANES) for c in range(group)]

                def row(j, accs):
                    wj = plsc.load_gather(w_v, [jnp.full((SC_LANES,), 2 * q * SC_ROWS + 1, jnp.int32) + 2 * j])
                    return tuple(a + wj * rows_v[buf, j, sl] for a, sl in zip(accs, lanes))

                accs = lax.fori_loop(0, SC_ROWS, row, tuple(acc_v[sl] for sl in lanes))
                for sl, a in zip(lanes, accs):
                    acc_v[sl] = a

        pltpu.sync_copy(idx_hbm.at[base], idx_v.at[0])
        gather(0, 0, 0).start()

        @pl.loop(0, per, step=2)
        def _(i):
            for p in range(2):
                t = base + i + p
                pltpu.sync_copy(w_hbm.at[t], w_v)
                for c in range(d // SC_LANES):
                    acc_v[pl.ds(c * SC_LANES, SC_LANES)] = jnp.zeros((SC_LANES,), jnp.float32)
                for q in range(quarters):
                    buf = q % 2
                    if q + 1 < quarters:
                        gather(p, q + 1, 1 - buf).start()
                    else:
                        nxt = jnp.minimum(t + 1, base + per - 1)
                        pltpu.sync_copy(idx_hbm.at[nxt], idx_v.at[1 - p])
                        gather(1 - p, 0, 1 - buf).start()
                    gather(p, q, buf).wait()
                    accumulate(buf, q)
                pltpu.sync_copy(acc_v, out_hbm.at[t])

        gather(0, 0, 0).wait()

    return body(idx, w, tab)


def _residual_kernel(h_ref, g_ref, a_ref, o_ref):
    o_ref[...] = h_ref[...] + g_ref[0, 0] * a_ref[...]


def gated_residual(h, g2, a, first_row, rows_per_sample, ctx_rows):
    m, d = h.shape
    rb = 256
    blk = pl.BlockSpec((rb, d), lambda i: (i, 0))
    per, ctx, off = rows_per_sample // rb, ctx_rows // rb, first_row // rb
    g_map = lambda i: ((i + off) // per, jnp.where((i + off) % per < ctx, 0, 1), 0, 0)
    return pl.pallas_call(
        _residual_kernel, grid=(m // rb,),
        in_specs=[blk, pl.BlockSpec((1, 1, 1, d), g_map), blk],
        out_specs=blk, out_shape=jax.ShapeDtypeStruct((m, d), jnp.float32),
        compiler_params=_params(("parallel",)), name="gated_residual",
    )(h, g2, a)


def _slots(w, width, offset=0):
    lead = w.shape[:-1]
    n = w.shape[-1] // width
    w = w.reshape(lead + (n, width))
    w = jnp.pad(w, [(0, 0)] * len(lead) + [(0, 0), (offset, LANES - width - offset)])
    return w.reshape(lead + (n * LANES,))


def _rope_tables(rot_dim, lane0, t, c):
    s = t - c
    q = rot_dim // 4
    pos = jnp.arange(s, dtype=jnp.float32)
    rows = jnp.floor(pos / GRID_W)
    cols = pos - rows * GRID_W
    inv = ROPE_BASE ** (-jnp.arange(q, dtype=jnp.float32) / q)
    ar = rows[:, None] * inv
    ac = cols[:, None] * inv
    zero = jnp.zeros_like(ar)
    cos = jnp.cos(jnp.concatenate([ar, ar, ac, ac], axis=-1))
    up = jnp.concatenate([-jnp.sin(ar), zero, -jnp.sin(ac), zero], axis=-1)
    dn = jnp.concatenate([zero, jnp.sin(ar), zero, jnp.sin(ac)], axis=-1)
    pad = lambda a, fill: jnp.pad(
        jnp.pad(a, ((0, 0), (lane0, LANES - lane0 - rot_dim)), constant_values=fill),
        ((c, 0), (0, 0)), constant_values=fill)
    cos = jnp.pad(jnp.pad(cos, ((0, 0), (lane0, LANES - lane0 - rot_dim)), constant_values=1.0),
                  ((c, 0), (0, 0)), constant_values=1.0)
    return jnp.stack([cos, pad(up, 0.0), pad(dn, 0.0)])


def kernel(x, c, ctx, c_ctx, ada_w, ada_b, norm1_g, norm2_g, w_in, mla_qa_g, mla_wuq, mla_kva_g, mla_wukv, mla_qn_g, mla_kn_g, swa_qn_g, swa_kn_g, swa_sink, w_out, peer_wq, peer_keys, peer_u, peer_v):
    b, s, d = x.shape
    nctx = ctx.shape[1]
    t = nctx + s
    depth = ada_w.shape[0]
    r = min(256, nctx)
    bf = jnp.bfloat16

    cond = jnp.zeros((16, d), jnp.float32).at[:b].set(c).at[b].set(c_ctx)
    mod_all = modulation(cond, ada_w, ada_b).reshape(depth, 16, N_MOD, d)
    rope_m = _rope_tables(MLA_ROPE, MLA_NOPE, t, nctx)
    rope_s = _rope_tables(SWA_DIM, 0, t, nctx)

    bg = b // GROUPS
    hs = [jnp.concatenate([ctx[g * bg:(g + 1) * bg], x[g * bg:(g + 1) * bg]], axis=1) for g in range(GROUPS)]
    for l in range(depth):
        last = l == depth - 1
        with_ctx = not last
        wi = w_in[l]
        kv0 = Q_COLS + MLA_KV_RANK
        sk0 = kv0 + MLA_ROPE
        sv0 = sk0 + SWA_KV_HEADS * SWA_DIM
        sv = wi[:, sv0:].reshape(d, SWA_KV_HEADS, 1, SWA_DIM)
        win = jnp.concatenate([
            wi[:, :MLA_Q_RANK], wi[:, Q_COLS:kv0],
            _slots(wi[:, kv0:sk0], MLA_ROPE, MLA_NOPE),
            _slots(wi[:, MLA_Q_RANK:Q_COLS], SWA_DIM),
            _slots(wi[:, sk0:sv0], SWA_DIM),
            jnp.broadcast_to(sv, (d, SWA_KV_HEADS, 2, SWA_DIM)).reshape(d, SWA_KV_HEADS * LANES),
        ], axis=1).astype(bf)
        wuq = _slots(mla_wuq[l], MLA_QK).astype(bf)
        wukv = mla_wukv[l].reshape(MLA_KV_RANK, MLA_HEADS, MLA_NOPE + MLA_V)
        wuk = _slots(wukv[:, :, :MLA_NOPE].reshape(MLA_KV_RANK, -1), MLA_NOPE).astype(bf)
        wv = wukv[:, :, MLA_NOPE:].reshape(MLA_KV_RANK, MLA_HEADS // 2, 2, MLA_V)
        zero = jnp.zeros_like(wv[:, :, 0])
        wuv = jnp.stack([jnp.concatenate([wv[:, :, 0], zero], axis=-1),
                         jnp.concatenate([zero, wv[:, :, 1]], axis=-1)], axis=2)
        wuv = wuv.reshape(MLA_KV_RANK, MLA_HEADS * LANES).astype(bf)
        row = lambda g: g.reshape(1, -1)
        wo, wq, keys = w_out[l].astype(bf), peer_wq[l].astype(bf), peer_keys[l].astype(bf)
        utab, vtab = pack_table(peer_u, l), pack_table(peer_v, l)
        vrows = peer_v.reshape(-1, d)

        for g in range(GROUPS):
            h = hs[g]
            mod = jnp.concatenate([mod_all[l, g * bg:(g + 1) * bg], mod_all[l, b:b + 1]], axis=0)
            qm, km, vm, qs, ks, vlo, vhi = projections(
                h, mod, row(norm1_g[l]), win, row(mla_qa_g[l]), wuq, row(mla_kva_g[l]), wuk, wuv,
                row(_slots(mla_qn_g[l], MLA_QK)), row(_slots(mla_kn_g[l], MLA_QK)),
                row(_slots(swa_qn_g[l], SWA_DIM)), row(_slots(swa_kn_g[l], SWA_DIM)),
                rope_m, rope_s, r, nctx)
            om = mla_attention(qm, km, vm, r, nctx, with_ctx)
            osw = swa_attention(swa_sink[l], qs, ks, vlo, vhi, nctx, with_ctx)
            h1, x2, q16 = route(h, om, osw, wo, mod, row(norm2_g[l]), wq, r, nctx, with_ctx)
            idx, gate = peer_topk(q16, keys)

            tl = h1.shape[1]
            n = bg * tl
            x16 = x2.reshape(n, 16, LANES)
            gate = jnp.stack([jnp.zeros_like(gate), gate], axis=-1).reshape(n, 1, 2 * PEER_ROWS)
            g2 = jnp.stack([jnp.broadcast_to(mod[bg, 5], (bg, d)), mod[:bg, 5]], axis=1)
            ctx_rows = nctx if with_ctx else 0
            h1f = h1.reshape(n, d)
            share = SC_SHARE_LAST if (last and g == GROUPS - 1) else 1.0
            n_sc = int(n * share) // 256 * 256
            n_tc = n - n_sc
            w_sc = peer_scores(idx[n_tc:], x16[n_tc:], gate[n_tc:], utab)
            experts = (idx[n_tc:] >> 2) + l * peer_v.shape[1]
            acc_sc = peer_combine_sc(experts, w_sc.reshape(n_sc, 2 * PEER_ROWS), vrows)
            h_sc = gated_residual(h1f[n_tc:], g2.reshape(bg, 2, 1, d), acc_sc, n_tc, tl, ctx_rows)
            if n_tc:
                w = peer_scores(idx[:n_tc], x16[:n_tc], gate[:n_tc], utab)
                h_tc = peer_combine(idx[:n_tc], w, h1f[:n_tc].reshape(n_tc, 8, LANES),
                                    g2.reshape(bg, 2, 8, LANES), vtab, tl // TOK_BLK,
                                    ctx_rows // TOK_BLK).reshape(n_tc, d)
                h_sc = jnp.concatenate([h_tc, h_sc], axis=0)
            hs[g] = h_sc.reshape(bg, tl, d)
    return jnp.concatenate(hs, axis=0)
```

```python
import functools
import jax
import jax.numpy as jnp
from jax import lax
from jax.experimental import pallas as pl
from jax.experimental.pallas import tpu as pltpu
from jax.experimental.pallas import tpu_sc as plsc

LANES = 128
EPS = 1e-6
ROPE_BASE = 10000.0
GRID_W = 64
N_MOD = 6

MLA_HEADS = 8
MLA_NOPE = 64
MLA_ROPE = 32
MLA_QK = MLA_NOPE + MLA_ROPE
MLA_V = 64
MLA_Q_RANK = 384
MLA_KV_RANK = 256
SWA_HEADS = 8
SWA_KV_HEADS = 2
SWA_GROUP = SWA_HEADS // SWA_KV_HEADS
SWA_DIM = 64
WINDOW = 128
WIN_KEYS = 3 * WINDOW
Q_COLS = MLA_Q_RANK + SWA_HEADS * SWA_DIM

PEER_HEADS = 8
PEER_NKEYS = 128
PEER_DHALF = 128
PEER_TOPK = 16
PEER_ROWS = PEER_HEADS * PEER_TOPK
N_CHUNK = 4
TILE_STRIDE = PEER_ROWS + 1
TOPK_BLK = 1024
STAGE_STRIDE = PEER_NKEYS + 8
TOK_BLK = 128
TOK_SET = 8

OFF_QA = 0
OFF_KVA = OFF_QA + MLA_Q_RANK
OFF_KR = OFF_KVA + MLA_KV_RANK
OFF_SQ = OFF_KR + LANES
OFF_SK = OFF_SQ + SWA_HEADS * LANES
OFF_SV = OFF_SK + SWA_KV_HEADS * LANES
N_IN = OFF_SV + SWA_KV_HEADS * LANES

VMEM_LIMIT = 56 * 1024 * 1024
NEG_INF = float("-inf")
LOG2_E = 1.4426950408889634


def _params(sem, vmem=VMEM_LIMIT):
    return pltpu.CompilerParams(dimension_semantics=sem, vmem_limit_bytes=vmem)


def _rms(x, g, n):
    ms = jnp.sum(x * x, axis=-1, keepdims=True) * (1.0 / n)
    return x * lax.rsqrt(ms + EPS) * g


def _rope(x, cos, sin_up, sin_dn, shift):
    return (x * cos + pltpu.roll(x, LANES - shift, axis=1) * sin_up
            + pltpu.roll(x, shift, axis=1) * sin_dn)


def _mod_kernel(c_ref, w_ref, b_ref, o_ref):
    c = c_ref[...]
    s = c * (1.0 / (1.0 + jnp.exp(-c)))
    o_ref[0] = jnp.dot(s, w_ref[0], preferred_element_type=jnp.float32,
                       precision=lax.Precision.HIGHEST) + b_ref[0]


def modulation(cond, ada_w, ada_b):
    nl, d, n6 = ada_w.shape
    tn = 1536
    return pl.pallas_call(
        _mod_kernel,
        grid=(nl, n6 // tn),
        in_specs=[
            pl.BlockSpec(cond.shape, lambda l, j: (0, 0)),
            pl.BlockSpec((1, d, tn), lambda l, j: (l, 0, j)),
            pl.BlockSpec((1, 1, tn), lambda l, j: (l, 0, j)),
        ],
        out_specs=pl.BlockSpec((1, cond.shape[0], tn), lambda l, j: (l, 0, j)),
        out_shape=jax.ShapeDtypeStruct((nl, cond.shape[0], n6), jnp.float32),
        compiler_params=_params(("arbitrary", "arbitrary")),
        name="modulation",
    )(cond, ada_w, ada_b.reshape(nl, 1, n6))


def _proj_kernel(h_ref, mod_ref, n1_ref, win_ref, qag_ref, wuq_ref, kvag_ref, wuk_ref, wuv_ref,
                 qn_ref, kn_ref, sqn_ref, skn_ref, rm_ref, rs_ref,
                 qm_ref, km_ref, vm_ref, qs_ref, ks_ref, vlo_ref, vhi_ref):
    h = h_ref[0]
    d = h.shape[-1]
    mod = mod_ref[0]
    a = _rms(h, n1_ref[...], d) * (1.0 + mod[1:2]) + mod[0:1]
    p = jnp.dot(a.astype(jnp.bfloat16), win_ref[...], preferred_element_type=jnp.float32)

    cm, sm_up, sm_dn = rm_ref[0], rm_ref[1], rm_ref[2]
    cs, ss_up, ss_dn = rs_ref[0], rs_ref[1], rs_ref[2]

    qa = _rms(p[:, OFF_QA:OFF_QA + MLA_Q_RANK], qag_ref[...], MLA_Q_RANK)
    q = jnp.dot(qa.astype(jnp.bfloat16), wuq_ref[...], preferred_element_type=jnp.float32)
    kva = _rms(p[:, OFF_KVA:OFF_KVA + MLA_KV_RANK], kvag_ref[...], MLA_KV_RANK).astype(jnp.bfloat16)
    kn = jnp.dot(kva, wuk_ref[...], preferred_element_type=jnp.float32)
    vm = jnp.dot(kva, wuv_ref[...], preferred_element_type=jnp.float32)
    kr = p[:, OFF_KR:OFF_KR + LANES]
    for hd in range(MLA_HEADS):
        sl = slice(hd * LANES, (hd + 1) * LANES)
        qh = _rope(_rms(q[:, sl], qn_ref[...], MLA_QK), cm, sm_up, sm_dn, MLA_ROPE // 4)
        qm_ref[0, hd] = (qh * (MLA_QK ** -0.5 * LOG2_E)).astype(jnp.bfloat16)
        kh = _rope(_rms(kn[:, sl] + kr, kn_ref[...], MLA_QK), cm, sm_up, sm_dn, MLA_ROPE // 4)
        km_ref[0, hd] = kh.astype(jnp.bfloat16)
        vm_ref[0, hd] = vm[:, sl].astype(jnp.bfloat16)
    for hd in range(SWA_HEADS):
        x = p[:, OFF_SQ + hd * LANES:OFF_SQ + (hd + 1) * LANES]
        qh = _rope(_rms(x, sqn_ref[...], SWA_DIM), cs, ss_up, ss_dn, SWA_DIM // 4)
        qs_ref[0, hd] = (qh * (SWA_DIM ** -0.5)).astype(jnp.bfloat16)
    lane = lax.broadcasted_iota(jnp.int32, (h.shape[0], LANES), 1)
    for g in range(SWA_KV_HEADS):
        x = p[:, OFF_SK + g * LANES:OFF_SK + (g + 1) * LANES]
        kh = _rope(_rms(x, skn_ref[...], SWA_DIM), cs, ss_up, ss_dn, SWA_DIM // 4)
        ks_ref[0, g] = kh.astype(jnp.bfloat16)
        v = p[:, OFF_SV + g * LANES:OFF_SV + (g + 1) * LANES]
        vlo_ref[0, g] = jnp.where(lane < SWA_DIM, v, 0.0).astype(jnp.bfloat16)
        vhi_ref[0, g] = jnp.where(lane >= SWA_DIM, v, 0.0).astype(jnp.bfloat16)


def projections(h, mod, n1, win, qag, wuq, kvag, wuk, wuv, qn, kn, sqn, skn, rope_m, rope_s, r, c):
    b, t, d = h.shape
    nctx = c // r
    full = lambda a: pl.BlockSpec(a.shape, lambda i, j: (0,) * a.ndim)
    head_out = lambda nh: pl.BlockSpec((1, nh, r, LANES), lambda i, j: (i, 0, j, 0))
    head_shape = lambda nh: jax.ShapeDtypeStruct((b, nh, t, LANES), jnp.bfloat16)
    return pl.pallas_call(
        _proj_kernel,
        grid=(b, t // r),
        in_specs=[
            pl.BlockSpec((1, r, d), lambda i, j: (i, j, 0)),
            pl.BlockSpec((1, N_MOD, d), lambda i, j: (jnp.where(j < nctx, b, i), 0, 0)),
            full(n1), full(win), full(qag), full(wuq), full(kvag), full(wuk), full(wuv),
            full(qn), full(kn), full(sqn), full(skn),
            pl.BlockSpec((3, r, LANES), lambda i, j: (0, j, 0)),
            pl.BlockSpec((3, r, LANES), lambda i, j: (0, j, 0)),
        ],
        out_specs=[head_out(MLA_HEADS), head_out(MLA_HEADS), head_out(MLA_HEADS),
                   head_out(SWA_HEADS), head_out(SWA_KV_HEADS), head_out(SWA_KV_HEADS),
                   head_out(SWA_KV_HEADS)],
        out_shape=[head_shape(MLA_HEADS), head_shape(MLA_HEADS), head_shape(MLA_HEADS),
                   head_shape(SWA_HEADS), head_shape(SWA_KV_HEADS), head_shape(SWA_KV_HEADS),
                   head_shape(SWA_KV_HEADS)],
        compiler_params=_params(("parallel", "arbitrary")),
        name="projections",
    )(h, mod, n1, win, qag, wuq, kvag, wuk, wuv, qn, kn, sqn, skn, rope_m, rope_s)


def _mla_kernel(q_ref, k_ref, v_ref, o_ref, *, nctx_blocks, c):
    qi = pl.program_id(2)

    def attend(nk):
        acc = None
        for i in range(2):
            q = q_ref[0, i]
            s = lax.dot_general(q, k_ref[0, i, :nk], (((1,), (1,)), ((), ())),
                                preferred_element_type=jnp.float32)
            m = jnp.max(s, axis=-1, keepdims=True)
            p = jnp.exp2(s - m)
            l = jnp.sum(p, axis=-1, keepdims=True)
            o = jnp.dot(p.astype(jnp.bfloat16), v_ref[0, i, :nk], preferred_element_type=jnp.float32)
            o = o * (1.0 / l)
            acc = o if acc is None else acc + o
        o_ref[0] = acc.astype(o_ref.dtype)

    if nctx_blocks:
        @pl.when(qi < nctx_blocks)
        def _():
            attend(c)

        @pl.when(qi >= nctx_blocks)
        def _():
            attend(k_ref.shape[2])
    else:
        attend(k_ref.shape[2])


def mla_attention(qm, km, vm, tq, c, with_ctx):
    b, nh, t, _ = qm.shape
    off = 0 if with_ctx else c // tq
    nq = t // tq - off
    return pl.pallas_call(
        functools.partial(_mla_kernel, nctx_blocks=(c // tq if with_ctx else 0), c=c),
        grid=(b, nh // 2, nq),
        in_specs=[
            pl.BlockSpec((1, 2, tq, LANES), lambda i, hp, j: (i, hp, j + off, 0)),
            pl.BlockSpec((1, 2, t, LANES), lambda i, hp, j: (i, hp, 0, 0)),
            pl.BlockSpec((1, 2, t, LANES), lambda i, hp, j: (i, hp, 0, 0)),
        ],
        out_specs=pl.BlockSpec((1, tq, LANES), lambda i, hp, j: (i, j, hp)),
        out_shape=jax.ShapeDtypeStruct((b, nq * tq, nh // 2 * LANES), jnp.bfloat16),
        compiler_params=_params(("parallel", "arbitrary", "arbitrary")),
        name="mla_attention",
    )(qm, km, vm)


def _swa_kernel(sink_ref, q_ref, k_ref, vlo_ref, vhi_ref, o_ref, *, off, c):
    g = pl.program_id(1)
    qi = pl.program_id(2) + off
    t = k_ref.shape[2]
    rows = SWA_GROUP * WINDOW
    q = q_ref[0].reshape(rows, LANES)
    r_idx = lax.broadcasted_iota(jnp.int32, (rows, 1), 0)
    sink = jnp.zeros((rows, 1), jnp.float32)
    for i in range(SWA_GROUP):
        sink = jnp.where(r_idx // WINDOW == i, sink_ref[g * SWA_GROUP + i], sink)

    ws = pl.multiple_of(jnp.clip(qi * WINDOW - WINDOW, c, t - WIN_KEYS), WINDOW)
    nt = (((1,), (1,)), ((), ()))
    s_ctx = lax.dot_general(q, k_ref[0, 0, :c], nt, preferred_element_type=jnp.float32)
    s_loc = lax.dot_general(q, k_ref[0, 0, pl.ds(ws, WIN_KEYS)], nt, preferred_element_type=jnp.float32)
    qpos = qi * WINDOW + (r_idx % WINDOW)
    kpos = ws + lax.broadcasted_iota(jnp.int32, (1, WIN_KEYS), 1)
    reach = jnp.where(qi * WINDOW >= c, WINDOW, -1)
    s_loc = jnp.where(jnp.abs(qpos - kpos) <= reach, s_loc, NEG_INF)
    m = jnp.maximum(jnp.maximum(jnp.max(s_ctx, axis=-1, keepdims=True),
                                jnp.max(s_loc, axis=-1, keepdims=True)), sink)
    p_ctx = jnp.exp(s_ctx - m)
    p_loc = jnp.exp(s_loc - m)
    l = (jnp.sum(p_ctx, axis=-1, keepdims=True) + jnp.sum(p_loc, axis=-1, keepdims=True)
         + jnp.exp(sink - m))
    inv = 1.0 / l
    p_ctx = p_ctx.astype(jnp.bfloat16)
    p_loc = p_loc.astype(jnp.bfloat16)
    outs = []
    for pair in range(SWA_GROUP // 2):
        acc = None
        for i, v_ref in enumerate((vlo_ref, vhi_ref)):
            rs = slice((2 * pair + i) * WINDOW, (2 * pair + i + 1) * WINDOW)
            o = (jnp.dot(p_ctx[rs], v_ref[0, 0, :c], preferred_element_type=jnp.float32)
                 + jnp.dot(p_loc[rs], v_ref[0, 0, pl.ds(ws, WIN_KEYS)], preferred_element_type=jnp.float32))
            o = o * inv[rs]
            acc = o if acc is None else acc + o
        outs.append(acc)
    o_ref[0] = jnp.concatenate(outs, axis=-1).astype(o_ref.dtype)


def swa_attention(sink, qs, ks, vlo, vhi, c, with_ctx):
    b, nh, t, _ = qs.shape
    off = 0 if with_ctx else c // WINDOW
    nq = t // WINDOW - off
    kv_spec = pl.BlockSpec((1, 1, t, LANES), lambda i, g, j: (i, g, 0, 0))
    return pl.pallas_call(
        functools.partial(_swa_kernel, off=off, c=c),
        grid=(b, SWA_KV_HEADS, nq),
        in_specs=[
            pl.BlockSpec(memory_space=pltpu.SMEM),
            pl.BlockSpec((1, SWA_GROUP, WINDOW, LANES), lambda i, g, j: (i, g, j + off, 0)),
            kv_spec, kv_spec, kv_spec,
        ],
        out_specs=pl.BlockSpec((1, WINDOW, SWA_GROUP // 2 * LANES), lambda i, g, j: (i, j, g)),
        out_shape=jax.ShapeDtypeStruct((b, nq * WINDOW, SWA_HEADS // 2 * LANES), jnp.bfloat16),
        compiler_params=_params(("parallel", "arbitrary", "arbitrary")),
        name="swa_attention",
    )(sink, qs, ks, vlo, vhi)


def _tree(op, xs):
    xs = list(xs)
    while len(xs) > 1:
        xs = [op(xs[i], xs[i + 1]) for i in range(0, len(xs) - 1, 2)] + (xs[-1:] if len(xs) % 2 else [])
    return xs[0]


def _top16_sweeps(ids, *problems):
    n = len(ids)
    big = float(max(ids) + 1)
    order = sorted(range(n), key=lambda k: ids[k])
    runs = [order[i:i + PEER_TOPK] for i in range(0, n, PEER_TOPK)]

    def step(r, carry):
        for val_ref, out_v, out_i in problems:
            m = _tree(jnp.maximum, [val_ref[k] for k in range(n)])
            firsts = []
            for run in runs:
                am = jnp.full(m.shape, big, jnp.float32)
                for k in reversed(run):
                    am = jnp.where(val_ref[k] == m, float(ids[k]), am)
                firsts.append(am)
            am = _tree(jnp.minimum, firsts)
            for k in range(n):
                val_ref[k] = jnp.where(am == float(ids[k]), NEG_INF, val_ref[k])
            out_v[r] = m
            out_i[r] = am
        return carry

    lax.fori_loop(0, PEER_TOPK, step, 0)


PAIRS = [(a, b) for a in range(PEER_TOPK) for b in range(PEER_TOPK) if (a + 1) * (b + 1) <= PEER_TOPK]


def _route_kernel(h_ref, om_ref, os_ref, wo_ref, mod_ref, n2_ref, wq_ref,
                  h1_ref, x2_ref, q_ref):
    h = h_ref[0]
    d = h.shape[-1]
    mod = mod_ref[0]
    half = om_ref.shape[-1]
    mix = (jnp.dot(om_ref[0], wo_ref[:half], preferred_element_type=jnp.float32)
           + jnp.dot(os_ref[0], wo_ref[half:], preferred_element_type=jnp.float32))
    h1 = h + mod[2:3] * mix
    h1_ref[0] = h1
    x = _rms(h1, n2_ref[...], d) * (1.0 + mod[4:5]) + mod[3:4]
    xhi = x.astype(jnp.bfloat16)
    x2_ref[0, :, :d] = xhi
    x2_ref[0, :, d:] = (x - xhi.astype(jnp.float32)).astype(jnp.bfloat16)
    q = jnp.dot(xhi, wq_ref[...], preferred_element_type=jnp.float32)
    for k in range(2 * PEER_HEADS):
        q_ref[k] = q[:, k * PEER_DHALF:(k + 1) * PEER_DHALF].astype(jnp.bfloat16)


def _topk_kernel(q_ref, keys_ref, idx_ref, g_ref, stage, vals, sv, si, cand, cv, ci, out_i, out_g):
    groups = vals.shape[2]

    def head(hh, carry):
        for part in range(2):
            st = lax.dot_general(keys_ref[hh, part], q_ref[2 * hh + part], (((1,), (1,)), ((), ())),
                                 preferred_element_type=jnp.float32)
            for g in range(groups):
                stage[pl.ds(g * STAGE_STRIDE, PEER_NKEYS), :] = st[:, g * LANES:(g + 1) * LANES]
            for k in range(PEER_NKEYS):
                vals[part, k] = stage[pl.ds(k, groups, stride=STAGE_STRIDE), :]
        _top16_sweeps(list(range(PEER_NKEYS)), *[(vals.at[p], sv.at[p], si.at[p]) for p in range(2)])
        for n, (a, b) in enumerate(PAIRS):
            cand[n] = sv[0, a] + sv[1, b]
        _top16_sweeps([a * PEER_TOPK + b for a, b in PAIRS], (cand, cv, ci))
        top = cv[0]
        e = [jnp.exp(cv[r] - top) for r in range(PEER_TOPK)]
        inv = 1.0 / _tree(jnp.add, e)
        for r in range(PEER_TOPK):
            ia = jnp.floor(ci[r] * (1.0 / PEER_TOPK))
            ib = ci[r] - ia * PEER_TOPK
            i1 = jnp.zeros_like(ia)
            i2 = jnp.zeros_like(ia)
            for a in range(PEER_TOPK):
                i1 = jnp.where(ia == a, si[0, a], i1)
                i2 = jnp.where(ib == a, si[1, a], i2)
            out_i[hh * PEER_TOPK + r] = (i1 * PEER_NKEYS + i2) * N_CHUNK
            out_g[hh * PEER_TOPK + r] = e[r] * inv
        return carry

    lax.fori_loop(0, PEER_HEADS, head, 0)
    for g in range(groups):
        rows = slice(g * LANES, (g + 1) * LANES)
        idx_ref[rows, :] = out_i[:, g, :].T.astype(jnp.int32)
        g_ref[rows, :] = out_g[:, g, :].T


def peer_topk(q16, keys):
    n = q16.shape[1]
    tb = next(t for t in (TOPK_BLK, TOPK_BLK // 2, TOPK_BLK // 4, LANES) if n % t == 0)
    groups = tb // LANES
    blk = lambda dt: pltpu.VMEM((PEER_TOPK, groups, LANES), dt)
    return pl.pallas_call(
        _topk_kernel,
        grid=(n // tb,),
        in_specs=[
            pl.BlockSpec((2 * PEER_HEADS, tb, PEER_DHALF), lambda i: (0, i, 0)),
            pl.BlockSpec(keys.shape, lambda i: (0, 0, 0, 0)),
        ],
        out_specs=[pl.BlockSpec((tb, PEER_ROWS), lambda i: (i, 0))] * 2,
        out_shape=[jax.ShapeDtypeStruct((n, PEER_ROWS), jnp.int32),
                   jax.ShapeDtypeStruct((n, PEER_ROWS), jnp.float32)],
        scratch_shapes=[
            pltpu.VMEM((groups * STAGE_STRIDE, LANES), jnp.float32),
            pltpu.VMEM((2, PEER_NKEYS, groups, LANES), jnp.float32),
            pltpu.VMEM((2, PEER_TOPK, groups, LANES), jnp.float32),
            pltpu.VMEM((2, PEER_TOPK, groups, LANES), jnp.float32),
            pltpu.VMEM((len(PAIRS), groups, LANES), jnp.float32),
            blk(jnp.float32), blk(jnp.float32),
            pltpu.VMEM((PEER_ROWS, groups, LANES), jnp.float32),
            pltpu.VMEM((PEER_ROWS, groups, LANES), jnp.float32),
        ],
        compiler_params=_params(("arbitrary",)),
        name="peer_topk",
    )(q16, keys)


def route(h, om, osw, wo, mod, n2, wq, r, c, with_ctx):
    b, t, d = h.shape
    off = 0 if with_ctx else c // r
    nblk = t // r - off
    nctx = c // r
    tq = nblk * r
    full = lambda a: pl.BlockSpec(a.shape, lambda i, j: (0,) * a.ndim)
    row_spec = lambda w: pl.BlockSpec((1, r, w), lambda i, j: (i, j, 0))
    return pl.pallas_call(
        _route_kernel,
        grid=(b, nblk),
        in_specs=[
            pl.BlockSpec((1, r, d), lambda i, j: (i, j + off, 0)),
            row_spec(om.shape[-1]), row_spec(osw.shape[-1]),
            full(wo),
            pl.BlockSpec((1, N_MOD, d), lambda i, j: (jnp.where(j + off < nctx, b, i), 0, 0)),
            full(n2), full(wq),
        ],
        out_specs=[row_spec(d), row_spec(2 * d),
                   pl.BlockSpec((2 * PEER_HEADS, r, PEER_DHALF), lambda i, j: (0, i * nblk + j, 0))],
        out_shape=[
            jax.ShapeDtypeStruct((b, tq, d), jnp.float32),
            jax.ShapeDtypeStruct((b, tq, 2 * d), jnp.bfloat16),
            jax.ShapeDtypeStruct((2 * PEER_HEADS, b * tq, PEER_DHALF), jnp.bfloat16),
        ],
        compiler_params=_params(("parallel", "arbitrary")),
        name="peer_route",
    )(h, om, osw, wo, mod, n2, wq)


def _pack_kernel(x_ref, o_ref):
    x = x_ref[0]
    half = x.shape[1] // 2
    bits = lambda v: pltpu.bitcast(v.astype(jnp.bfloat16).astype(jnp.float32), jnp.uint32)
    w = (bits(x[:, :half]) & jnp.uint32(0xFFFF0000)) | (bits(x[:, half:]) >> 16)
    for c in range(N_CHUNK):
        o_ref[pl.ds(c, x.shape[0], stride=N_CHUNK), :] = w[:, c * LANES:(c + 1) * LANES]


def pack_table(tabs, l):
    _, e, d = tabs.shape
    be = 512
    return pl.pallas_call(
        _pack_kernel,
        grid=(e // be,),
        in_specs=[pl.BlockSpec((1, be, d), lambda i: (l, i, 0))],
        out_specs=pl.BlockSpec((be * N_CHUNK, LANES), lambda i: (i, 0)),
        out_shape=jax.ShapeDtypeStruct((e * N_CHUNK, LANES), jnp.uint32),
        compiler_params=_params(("parallel",)),
        name="pack_table",
    )(tabs)


def _gather_rows(idx_ref, tab_ref, tile_ref, t):
    for j in range(PEER_ROWS):
        r = pl.multiple_of(idx_ref[t, j], N_CHUNK)
        tile_ref[pl.ds(j, N_CHUNK, stride=TILE_STRIDE), :] = tab_ref[pl.ds(r, N_CHUNK), :]


def _token_pipeline(gather, gather_next, compute, tiles):
    steps = TOK_BLK // TOK_SET

    @pl.when(pl.program_id(0) == 0)
    def _():
        for k in range(TOK_SET):
            gather(tiles[0].at[k], k)

    def step(t, cur, nxt, fill):
        for k in range(TOK_SET):
            compute(cur.at[k], t + k)
            fill(nxt.at[k], k)

    def body(i, carry):
        t = TOK_SET * i
        ahead = lambda tile, k: gather(tile, t + TOK_SET + k)

        @pl.when(i % 2 == 0)
        def _():
            step(t, tiles[0], tiles[1], ahead)

        @pl.when(i % 2 == 1)
        def _():
            step(t, tiles[1], tiles[0], ahead)

        return carry

    lax.fori_loop(0, steps - 1, body, 0)
    step(TOK_BLK - TOK_SET, tiles[1], tiles[0], gather_next)


def _chunk(tile_ref, c):
    return pltpu.bitcast(tile_ref[pl.ds(c * TILE_STRIDE, PEER_ROWS), :], jnp.bfloat16)


def _u_kernel(idx_ref, nxt_ref, x_ref, g_ref, tab_ref, w_ref, *tiles):
    row = lax.broadcasted_iota(jnp.int32, (16, 2 * PEER_ROWS), 0) & 7

    def scores(tile_ref, t):
        x16 = x_ref[t]
        top = jnp.zeros((16, 2 * PEER_ROWS), jnp.float32)
        bot = jnp.zeros((16, 2 * PEER_ROWS), jnp.float32)
        for c in range(N_CHUNK):
            y = lax.dot_general(x16, _chunk(tile_ref, c), (((1,), (1,)), ((), ())),
                                preferred_element_type=jnp.float32)
            top = top + jnp.where(row == c, y, 0.0)
            bot = bot + jnp.where(row == c + N_CHUNK, y, 0.0)
        tot = top + pltpu.roll(bot, 1, axis=1)
        a = jnp.sum(tot, axis=0, keepdims=True)
        gelu = 0.5 * a * (1.0 + lax.erf(a * (2.0 ** -0.5)))
        w_ref[t] = g_ref[t] * gelu

    _token_pipeline(functools.partial(_gather_rows, idx_ref, tab_ref),
                    functools.partial(_gather_rows, nxt_ref, tab_ref), scores, tiles)


def _next_tokens_spec(n):
    per_blk = TOK_BLK // TOK_SET
    return pl.BlockSpec((TOK_SET, PEER_ROWS),
                        lambda i: (jnp.minimum((i + 1) * per_blk, n // TOK_SET - 1), 0),
                        memory_space=pltpu.SMEM)


def peer_scores(idx, x16, g, tab):
    n = idx.shape[0]
    tok = lambda *s: pl.BlockSpec((TOK_BLK,) + s, lambda i: (i,) + (0,) * len(s))
    return pl.pallas_call(
        _u_kernel,
        grid=(n // TOK_BLK,),
        in_specs=[
            pl.BlockSpec((TOK_BLK, PEER_ROWS), lambda i: (i, 0), memory_space=pltpu.SMEM),
            _next_tokens_spec(n),
            tok(16, LANES), tok(1, 2 * PEER_ROWS),
            pl.BlockSpec(tab.shape, lambda i: (0, 0), pipeline_mode=pl.Buffered(1)),
        ],
        out_specs=tok(1, 2 * PEER_ROWS),
        out_shape=jax.ShapeDtypeStruct((n, 1, 2 * PEER_ROWS), jnp.float32),
        scratch_shapes=[pltpu.VMEM((TOK_SET, N_CHUNK * TILE_STRIDE, LANES), jnp.uint32)] * 2,
        compiler_params=_params(("arbitrary",)),
        name="peer_scores",
    )(idx, idx, x16, g, tab)


def _v_kernel(idx_ref, nxt_ref, w_ref, h_ref, g2_ref, tab_ref, o_ref, *tiles):
    row = lax.broadcasted_iota(jnp.int32, (16, 2 * PEER_ROWS), 0)

    def combine(tile_ref, t):
        w = w_ref[t]
        w_hi = w.astype(jnp.bfloat16).astype(jnp.float32)
        w_lo = w - w_hi
        parts = (w_hi, pltpu.roll(w_hi, 2 * PEER_ROWS - 1, axis=1),
                 w_lo, pltpu.roll(w_lo, 2 * PEER_ROWS - 1, axis=1))
        acc = jnp.zeros((16, LANES), jnp.float32)
        for c in range(N_CHUNK):
            lhs = jnp.zeros((16, 2 * PEER_ROWS), jnp.float32)
            for k, part in enumerate(parts):
                lhs = jnp.where(row == c + N_CHUNK * k, part, lhs)
            acc = acc + jnp.dot(lhs.astype(jnp.bfloat16), _chunk(tile_ref, c),
                                preferred_element_type=jnp.float32)
        o_ref[t] = h_ref[t] + g2_ref[0, 0] * (acc[:8] + acc[8:])

    _token_pipeline(functools.partial(_gather_rows, idx_ref, tab_ref),
                    functools.partial(_gather_rows, nxt_ref, tab_ref), combine, tiles)


def peer_combine(idx, w, h8, g2, tab, blocks_per_sample, ctx_blocks):
    n = idx.shape[0]
    tok = lambda *s: pl.BlockSpec((TOK_BLK,) + s, lambda i: (i,) + (0,) * len(s))
    g2_map = lambda i: (i // blocks_per_sample,
                        jnp.where(i % blocks_per_sample < ctx_blocks, 0, 1), 0, 0)
    return pl.pallas_call(
        _v_kernel,
        grid=(n // TOK_BLK,),
        in_specs=[
            pl.BlockSpec((TOK_BLK, PEER_ROWS), lambda i: (i, 0), memory_space=pltpu.SMEM),
            _next_tokens_spec(n),
            tok(1, 2 * PEER_ROWS), tok(8, LANES),
            pl.BlockSpec((1, 1, 8, LANES), g2_map),
            pl.BlockSpec(tab.shape, lambda i: (0, 0), pipeline_mode=pl.Buffered(1)),
        ],
        out_specs=tok(8, LANES),
        out_shape=jax.ShapeDtypeStruct((n, 8, LANES), jnp.float32),
        scratch_shapes=[pltpu.VMEM((TOK_SET, N_CHUNK * TILE_STRIDE, LANES), jnp.uint32)] * 2,
        compiler_params=_params(("arbitrary",)),
        name="peer_combine",
    )(idx, idx, w, h8, g2, tab)


SC_LANES = 16
SC_WORKERS = 32
SC_ROWS = 32
SC_SHARE_LAST = 0.48
GROUPS = 2


def peer_combine_sc(idx, w, tab):
    m = idx.shape[0]
    d = tab.shape[1]
    per = m // SC_WORKERS
    mesh = plsc.VectorSubcoreMesh(core_axis_name="c", subcore_axis_name="s")

    quarters = PEER_ROWS // SC_ROWS
    group = 16

    @functools.partial(
        pl.kernel, mesh=mesh,
        out_type=jax.ShapeDtypeStruct((m, d), jnp.float32),
        scratch_types=[
            pltpu.VMEM((2, PEER_ROWS), jnp.int32),
            pltpu.VMEM((2 * PEER_ROWS,), jnp.float32),
            pltpu.VMEM((2, SC_ROWS, d), jnp.float32),
            pltpu.VMEM((d,), jnp.float32),
            pltpu.SemaphoreType.DMA((2,)),
        ],
        compiler_params=pltpu.CompilerParams(needs_layout_passes=False),
        name="peer_combine_sc",
    )
    def body(idx_hbm, w_hbm, tab_hbm, out_hbm, idx_v, w_v, rows_v, acc_v, sems):
        base = (lax.axis_index("s") * 2 + lax.axis_index("c")) * per

        def gather(slot, q, buf):
            rows = idx_v.at[slot, pl.ds(q * SC_ROWS, SC_ROWS)]
            return pltpu.make_async_copy(tab_hbm.at[rows], rows_v.at[buf], sems.at[buf])

        def accumulate(buf, q):
            for g0 in range(0, d // SC_LANES, group):
                lanes = [pl.ds((g0 + c) * SC_LANES, SC_LANES) for c in range(group)]

                def row(j, accs):
                    wj = plsc.load_gather(w_v, [jnp.full((SC_LANES,), 2 * q * SC_ROWS + 1, jnp.int32) + 2 * j])
                    return tuple(a + wj * rows_v[buf, j, sl] for a, sl in zip(accs, lanes))

                accs = lax.fori_loop(0, SC_ROWS, row, tuple(acc_v[sl] for sl in lanes))
                for sl, a in zip(lanes, accs):
                    acc_v[sl] = a

        pltpu.sync_copy(idx_hbm.at[base], idx_v.at[0])
        gather(0, 0, 0).start()

        @pl.loop(0, per, step=2)
        def _(i):
            for p in range(2):
                t = base + i + p
                pltpu.sync_copy(w_hbm.at[t], w_v)
                for c in range(d // SC_LANES):
                    acc_v[pl.ds(c * SC_LANES, SC_LANES)] = jnp.zeros((SC_LANES,), jnp.float32)
                for q in range(quarters):
                    buf = q % 2
                    if q + 1 < quarters:
                        gather(p, q + 1, 1 - buf).start()
                    else:
                        nxt = jnp.minimum(t + 1, base + per - 1)
                        pltpu.sync_copy(idx_hbm.at[nxt], idx_v.at[1 - p])
                        gather(1 - p, 0, 1 - buf).start()
                    gather(p, q, buf).wait()
                    accumulate(buf, q)
                pltpu.sync_copy(acc_v, out_hbm.at[t])

        gather(0, 0, 0).wait()

    return body(idx, w, tab)


def _residual_kernel(h_ref, g_ref, a_ref, o_ref):
    o_ref[...] = h_ref[...] + g_ref[0, 0] * a_ref[...]


def gated_residual(h, g2, a, first_row, rows_per_sample, ctx_rows):
    m, d = h.shape
    rb = 256
    blk = pl.BlockSpec((rb, d), lambda i: (i, 0))
    per, ctx, off = rows_per_sample // rb, ctx_rows // rb, first_row // rb
    g_map = lambda i: ((i + off) // per, jnp.where((i + off) % per < ctx, 0, 1), 0, 0)
    return pl.pallas_call(
        _residual_kernel, grid=(m // rb,),
        in_specs=[blk, pl.BlockSpec((1, 1, 1, d), g_map), blk],
        out_specs=blk, out_shape=jax.ShapeDtypeStruct((m, d), jnp.float32),
        compiler_params=_params(("parallel",)), name="gated_residual",
    )(h, g2, a)


def _slots(w, width, offset=0):
    lead = w.shape[:-1]
    n = w.shape[-1] // width
    w = w.reshape(lead + (n, width))
    w = jnp.pad(w, [(0, 0)] * len(lead) + [(0, 0), (offset, LANES - width - offset)])
    return w.reshape(lead + (n * LANES,))


def _rope_tables(rot_dim, lane0, t, c):
    s = t - c
    q = rot_dim // 4
    pos = jnp.arange(s, dtype=jnp.float32)
    rows = jnp.floor(pos / GRID_W)
    cols = pos - rows * GRID_W
    inv = ROPE_BASE ** (-jnp.arange(q, dtype=jnp.float32) / q)
    ar = rows[:, None] * inv
    ac = cols[:, None] * inv
    zero = jnp.zeros_like(ar)
    cos = jnp.cos(jnp.concatenate([ar, ar, ac, ac], axis=-1))
    up = jnp.concatenate([-jnp.sin(ar), zero, -jnp.sin(ac), zero], axis=-1)
    dn = jnp.concatenate([zero, jnp.sin(ar), zero, jnp.sin(ac)], axis=-1)
    pad = lambda a, fill: jnp.pad(
        jnp.pad(a, ((0, 0), (lane0, LANES - lane0 - rot_dim)), constant_values=fill),
        ((c, 0), (0, 0)), constant_values=fill)
    cos = jnp.pad(jnp.pad(cos, ((0, 0), (lane0, LANES - lane0 - rot_dim)), constant_values=1.0),
                  ((c, 0), (0, 0)), constant_values=1.0)
    return jnp.stack([cos, pad(up, 0.0), pad(dn, 0.0)])


def kernel(x, c, ctx, c_ctx, ada_w, ada_b, norm1_g, norm2_g, w_in, mla_qa_g, mla_wuq, mla_kva_g, mla_wukv, mla_qn_g, mla_kn_g, swa_qn_g, swa_kn_g, swa_sink, w_out, peer_wq, peer_keys, peer_u, peer_v):
    b, s, d = x.shape
    nctx = ctx.shape[1]
    t = nctx + s
    depth = ada_w.shape[0]
    r = min(256, nctx)
    bf = jnp.bfloat16

    cond = jnp.zeros((16, d), jnp.float32).at[:b].set(c).at[b].set(c_ctx)
    mod_all = modulation(cond, ada_w, ada_b).reshape(depth, 16, N_MOD, d)
    rope_m = _rope_tables(MLA_ROPE, MLA_NOPE, t, nctx)
    rope_s = _rope_tables(SWA_DIM, 0, t, nctx)

    bg = b // GROUPS
    pending = None
    hs = [jnp.concatenate([ctx[g * bg:(g + 1) * bg], x[g * bg:(g + 1) * bg]], axis=1) for g in range(GROUPS)]
    for l in range(depth):
        last = l == depth - 1
        with_ctx = not last
        wi = w_in[l]
        kv0 = Q_COLS + MLA_KV_RANK
        sk0 = kv0 + MLA_ROPE
        sv0 = sk0 + SWA_KV_HEADS * SWA_DIM
        sv = wi[:, sv0:].reshape(d, SWA_KV_HEADS, 1, SWA_DIM)
        win = jnp.concatenate([
            wi[:, :MLA_Q_RANK], wi[:, Q_COLS:kv0],
            _slots(wi[:, kv0:sk0], MLA_ROPE, MLA_NOPE),
            _slots(wi[:, MLA_Q_RANK:Q_COLS], SWA_DIM),
            _slots(wi[:, sk0:sv0], SWA_DIM),
            jnp.broadcast_to(sv, (d, SWA_KV_HEADS, 2, SWA_DIM)).reshape(d, SWA_KV_HEADS * LANES),
        ], axis=1).astype(bf)
        wuq = _slots(mla_wuq[l], MLA_QK).astype(bf)
        wukv = mla_wukv[l].reshape(MLA_KV_RANK, MLA_HEADS, MLA_NOPE + MLA_V)
        wuk = _slots(wukv[:, :, :MLA_NOPE].reshape(MLA_KV_RANK, -1), MLA_NOPE).astype(bf)
        wv = wukv[:, :, MLA_NOPE:].reshape(MLA_KV_RANK, MLA_HEADS // 2, 2, MLA_V)
        zero = jnp.zeros_like(wv[:, :, 0])
        wuv = jnp.stack([jnp.concatenate([wv[:, :, 0], zero], axis=-1),
                         jnp.concatenate([zero, wv[:, :, 1]], axis=-1)], axis=2)
        wuv = wuv.reshape(MLA_KV_RANK, MLA_HEADS * LANES).astype(bf)
        row = lambda g: g.reshape(1, -1)
        wo, wq, keys = w_out[l].astype(bf), peer_wq[l].astype(bf), peer_keys[l].astype(bf)
        utab, vtab = pack_table(peer_u, l), pack_table(peer_v, l)
        vrows = peer_v.reshape(-1, d)

        for g in range(GROUPS):
            h = hs[g]
            if pending is not None:
                h, pending = lax.optimization_barrier((h, pending))
            mod = jnp.concatenate([mod_all[l, g * bg:(g + 1) * bg], mod_all[l, b:b + 1]], axis=0)
            qm, km, vm, qs, ks, vlo, vhi = projections(
                h, mod, row(norm1_g[l]), win, row(mla_qa_g[l]), wuq, row(mla_kva_g[l]), wuk, wuv,
                row(_slots(mla_qn_g[l], MLA_QK)), row(_slots(mla_kn_g[l], MLA_QK)),
                row(_slots(swa_qn_g[l], SWA_DIM)), row(_slots(swa_kn_g[l], SWA_DIM)),
                rope_m, rope_s, r, nctx)
            om = mla_attention(qm, km, vm, r, nctx, with_ctx)
            osw = swa_attention(swa_sink[l], qs, ks, vlo, vhi, nctx, with_ctx)
            h1, x2, q16 = route(h, om, osw, wo, mod, row(norm2_g[l]), wq, r, nctx, with_ctx)
            idx, gate = peer_topk(q16, keys)

            tl = h1.shape[1]
            n = bg * tl
            x16 = x2.reshape(n, 16, LANES)
            gate = jnp.stack([jnp.zeros_like(gate), gate], axis=-1).reshape(n, 1, 2 * PEER_ROWS)
            g2 = jnp.stack([jnp.broadcast_to(mod[bg, 5], (bg, d)), mod[:bg, 5]], axis=1)
            ctx_rows = nctx if with_ctx else 0
            h1f = h1.reshape(n, d)
            share = SC_SHARE_LAST if (last and g == GROUPS - 1) else 1.0
            n_sc = int(n * share) // 256 * 256
            n_tc = n - n_sc
            w_sc = pending = peer_scores(idx[n_tc:], x16[n_tc:], gate[n_tc:], utab)
            experts = (idx[n_tc:] >> 2) + l * peer_v.shape[1]
            acc_sc = peer_combine_sc(experts, w_sc.reshape(n_sc, 2 * PEER_ROWS), vrows)
            h_sc = gated_residual(h1f[n_tc:], g2.reshape(bg, 2, 1, d), acc_sc, n_tc, tl, ctx_rows)
            if n_tc:
                w = peer_scores(idx[:n_tc], x16[:n_tc], gate[:n_tc], utab)
                h_tc = peer_combine(idx[:n_tc], w, h1f[:n_tc].reshape(n_tc, 8, LANES),
                                    g2.reshape(bg, 2, 8, LANES), vtab, tl // TOK_BLK,
                                    ctx_rows // TOK_BLK).reshape(n_tc, d)
                h_sc = jnp.concatenate([h_tc, h_sc], axis=0)
            hs[g] = h_sc.reshape(bg, tl, d)
    return jnp.concatenate(hs, axis=0)
```

```python
import functools
import jax
import jax.numpy as jnp
from jax import lax
from jax.experimental import pallas as pl
from jax.experimental.pallas import tpu as pltpu
from jax.experimental.pallas import tpu_sc as plsc

LANES = 128
EPS = 1e-6
ROPE_BASE = 10000.0
GRID_W = 64
N_MOD = 6

MLA_HEADS = 8
MLA_NOPE = 64
MLA_ROPE = 32
MLA_QK = MLA_NOPE + MLA_ROPE
MLA_V = 64
MLA_Q_RANK = 384
MLA_KV_RANK = 256
SWA_HEADS = 8
SWA_KV_HEADS = 2
SWA_GROUP = SWA_HEADS // SWA_KV_HEADS
SWA_DIM = 64
WINDOW = 128
WIN_KEYS = 3 * WINDOW
Q_COLS = MLA_Q_RANK + SWA_HEADS * SWA_DIM

PEER_HEADS = 8
PEER_NKEYS = 128
PEER_DHALF = 128
PEER_TOPK = 16
PEER_ROWS = PEER_HEADS * PEER_TOPK
N_CHUNK = 4
TILE_STRIDE = PEER_ROWS + 1
TOPK_BLK = 1024
STAGE_STRIDE = PEER_NKEYS + 8
TOK_BLK = 128
TOK_SET = 8

OFF_QA = 0
OFF_KVA = OFF_QA + MLA_Q_RANK
OFF_KR = OFF_KVA + MLA_KV_RANK
OFF_SQ = OFF_KR + LANES
OFF_SK = OFF_SQ + SWA_HEADS * LANES
OFF_SV = OFF_SK + SWA_KV_HEADS * LANES
N_IN = OFF_SV + SWA_KV_HEADS * LANES

VMEM_LIMIT = 56 * 1024 * 1024
NEG_INF = float("-inf")
LOG2_E = 1.4426950408889634


def _params(sem, vmem=VMEM_LIMIT):
    return pltpu.CompilerParams(dimension_semantics=sem, vmem_limit_bytes=vmem)


def _rms(x, g, n):
    ms = jnp.sum(x * x, axis=-1, keepdims=True) * (1.0 / n)
    return x * lax.rsqrt(ms + EPS) * g


def _rope(x, cos, sin_up, sin_dn, shift):
    return (x * cos + pltpu.roll(x, LANES - shift, axis=1) * sin_up
            + pltpu.roll(x, shift, axis=1) * sin_dn)


def _mod_kernel(c_ref, w_ref, b_ref, o_ref):
    c = c_ref[...]
    s = c * (1.0 / (1.0 + jnp.exp(-c)))
    o_ref[0] = jnp.dot(s, w_ref[0], preferred_element_type=jnp.float32,
                       precision=lax.Precision.HIGHEST) + b_ref[0]


def modulation(cond, ada_w, ada_b):
    nl, d, n6 = ada_w.shape
    tn = 1536
    return pl.pallas_call(
        _mod_kernel,
        grid=(nl, n6 // tn),
        in_specs=[
            pl.BlockSpec(cond.shape, lambda l, j: (0, 0)),
            pl.BlockSpec((1, d, tn), lambda l, j: (l, 0, j)),
            pl.BlockSpec((1, 1, tn), lambda l, j: (l, 0, j)),
        ],
        out_specs=pl.BlockSpec((1, cond.shape[0], tn), lambda l, j: (l, 0, j)),
        out_shape=jax.ShapeDtypeStruct((nl, cond.shape[0], n6), jnp.float32),
        compiler_params=_params(("arbitrary", "arbitrary")),
        name="modulation",
    )(cond, ada_w, ada_b.reshape(nl, 1, n6))


def _proj_kernel(h_ref, mod_ref, n1_ref, win_ref, qag_ref, wuq_ref, kvag_ref, wuk_ref, wuv_ref,
                 qn_ref, kn_ref, sqn_ref, skn_ref, rm_ref, rs_ref,
                 qm_ref, km_ref, vm_ref, qs_ref, ks_ref, vlo_ref, vhi_ref):
    h = h_ref[0]
    d = h.shape[-1]
    mod = mod_ref[0]
    a = _rms(h, n1_ref[...], d) * (1.0 + mod[1:2]) + mod[0:1]
    p = jnp.dot(a.astype(jnp.bfloat16), win_ref[...], preferred_element_type=jnp.float32)

    cm, sm_up, sm_dn = rm_ref[0], rm_ref[1], rm_ref[2]
    cs, ss_up, ss_dn = rs_ref[0], rs_ref[1], rs_ref[2]

    qa = _rms(p[:, OFF_QA:OFF_QA + MLA_Q_RANK], qag_ref[...], MLA_Q_RANK)
    q = jnp.dot(qa.astype(jnp.bfloat16), wuq_ref[...], preferred_element_type=jnp.float32)
    kva = _rms(p[:, OFF_KVA:OFF_KVA + MLA_KV_RANK], kvag_ref[...], MLA_KV_RANK).astype(jnp.bfloat16)
    kn = jnp.dot(kva, wuk_ref[...], preferred_element_type=jnp.float32)
    vm = jnp.dot(kva, wuv_ref[...], preferred_element_type=jnp.float32)
    kr = p[:, OFF_KR:OFF_KR + LANES]
    for hd in range(MLA_HEADS):
        sl = slice(hd * LANES, (hd + 1) * LANES)
        qh = _rope(_rms(q[:, sl], qn_ref[...], MLA_QK), cm, sm_up, sm_dn, MLA_ROPE // 4)
        qm_ref[0, hd] = (qh * (MLA_QK ** -0.5 * LOG2_E)).astype(jnp.bfloat16)
        kh = _rope(_rms(kn[:, sl] + kr, kn_ref[...], MLA_QK), cm, sm_up, sm_dn, MLA_ROPE // 4)
        km_ref[0, hd] = kh.astype(jnp.bfloat16)
        vm_ref[0, hd] = vm[:, sl].astype(jnp.bfloat16)
    for hd in range(SWA_HEADS):
        x = p[:, OFF_SQ + hd * LANES:OFF_SQ + (hd + 1) * LANES]
        qh = _rope(_rms(x, sqn_ref[...], SWA_DIM), cs, ss_up, ss_dn, SWA_DIM // 4)
        qs_ref[0, hd] = (qh * (SWA_DIM ** -0.5)).astype(jnp.bfloat16)
    lane = lax.broadcasted_iota(jnp.int32, (h.shape[0], LANES), 1)
    for g in range(SWA_KV_HEADS):
        x = p[:, OFF_SK + g * LANES:OFF_SK + (g + 1) * LANES]
        kh = _rope(_rms(x, skn_ref[...], SWA_DIM), cs, ss_up, ss_dn, SWA_DIM // 4)
        ks_ref[0, g] = kh.astype(jnp.bfloat16)
        v = p[:, OFF_SV + g * LANES:OFF_SV + (g + 1) * LANES]
        vlo_ref[0, g] = jnp.where(lane < SWA_DIM, v, 0.0).astype(jnp.bfloat16)
        vhi_ref[0, g] = jnp.where(lane >= SWA_DIM, v, 0.0).astype(jnp.bfloat16)


def projections(h, mod, n1, win, qag, wuq, kvag, wuk, wuv, qn, kn, sqn, skn, rope_m, rope_s, r, c):
    b, t, d = h.shape
    nctx = c // r
    full = lambda a: pl.BlockSpec(a.shape, lambda i, j: (0,) * a.ndim)
    head_out = lambda nh: pl.BlockSpec((1, nh, r, LANES), lambda i, j: (i, 0, j, 0))
    head_shape = lambda nh: jax.ShapeDtypeStruct((b, nh, t, LANES), jnp.bfloat16)
    return pl.pallas_call(
        _proj_kernel,
        grid=(b, t // r),
        in_specs=[
            pl.BlockSpec((1, r, d), lambda i, j: (i, j, 0)),
            pl.BlockSpec((1, N_MOD, d), lambda i, j: (jnp.where(j < nctx, b, i), 0, 0)),
            full(n1), full(win), full(qag), full(wuq), full(kvag), full(wuk), full(wuv),
            full(qn), full(kn), full(sqn), full(skn),
            pl.BlockSpec((3, r, LANES), lambda i, j: (0, j, 0)),
            pl.BlockSpec((3, r, LANES), lambda i, j: (0, j, 0)),
        ],
        out_specs=[head_out(MLA_HEADS), head_out(MLA_HEADS), head_out(MLA_HEADS),
                   head_out(SWA_HEADS), head_out(SWA_KV_HEADS), head_out(SWA_KV_HEADS),
                   head_out(SWA_KV_HEADS)],
        out_shape=[head_shape(MLA_HEADS), head_shape(MLA_HEADS), head_shape(MLA_HEADS),
                   head_shape(SWA_HEADS), head_shape(SWA_KV_HEADS), head_shape(SWA_KV_HEADS),
                   head_shape(SWA_KV_HEADS)],
        compiler_params=_params(("parallel", "arbitrary")),
        name="projections",
    )(h, mod, n1, win, qag, wuq, kvag, wuk, wuv, qn, kn, sqn, skn, rope_m, rope_s)


def _mla_kernel(q_ref, k_ref, v_ref, o_ref, *, nctx_blocks, c):
    qi = pl.program_id(2)

    def attend(nk):
        acc = None
        for i in range(2):
            q = q_ref[0, i]
            s = lax.dot_general(q, k_ref[0, i, :nk], (((1,), (1,)), ((), ())),
                                preferred_element_type=jnp.float32)
            m = jnp.max(s, axis=-1, keepdims=True)
            p = jnp.exp2(s - m)
            l = jnp.sum(p, axis=-1, keepdims=True)
            o = jnp.dot(p.astype(jnp.bfloat16), v_ref[0, i, :nk], preferred_element_type=jnp.float32)
            o = o * (1.0 / l)
            acc = o if acc is None else acc + o
        o_ref[0] = acc.astype(o_ref.dtype)

    if nctx_blocks:
        @pl.when(qi < nctx_blocks)
        def _():
            attend(c)

        @pl.when(qi >= nctx_blocks)
        def _():
            attend(k_ref.shape[2])
    else:
        attend(k_ref.shape[2])


def mla_attention(qm, km, vm, tq, c, with_ctx):
    b, nh, t, _ = qm.shape
    off = 0 if with_ctx else c // tq
    nq = t // tq - off
    return pl.pallas_call(
        functools.partial(_mla_kernel, nctx_blocks=(c // tq if with_ctx else 0), c=c),
        grid=(b, nh // 2, nq),
        in_specs=[
            pl.BlockSpec((1, 2, tq, LANES), lambda i, hp, j: (i, hp, j + off, 0)),
            pl.BlockSpec((1, 2, t, LANES), lambda i, hp, j: (i, hp, 0, 0)),
            pl.BlockSpec((1, 2, t, LANES), lambda i, hp, j: (i, hp, 0, 0)),
        ],
        out_specs=pl.BlockSpec((1, tq, LANES), lambda i, hp, j: (i, j, hp)),
        out_shape=jax.ShapeDtypeStruct((b, nq * tq, nh // 2 * LANES), jnp.bfloat16),
        compiler_params=_params(("parallel", "arbitrary", "arbitrary")),
        name="mla_attention",
    )(qm, km, vm)


def _swa_kernel(sink_ref, q_ref, k_ref, vlo_ref, vhi_ref, o_ref, *, off, c):
    g = pl.program_id(1)
    qi = pl.program_id(2) + off
    t = k_ref.shape[2]
    rows = SWA_GROUP * WINDOW
    q = q_ref[0].reshape(rows, LANES)
    r_idx = lax.broadcasted_iota(jnp.int32, (rows, 1), 0)
    sink = jnp.zeros((rows, 1), jnp.float32)
    for i in range(SWA_GROUP):
        sink = jnp.where(r_idx // WINDOW == i, sink_ref[g * SWA_GROUP + i], sink)

    ws = pl.multiple_of(jnp.clip(qi * WINDOW - WINDOW, c, t - WIN_KEYS), WINDOW)
    nt = (((1,), (1,)), ((), ()))
    s_ctx = lax.dot_general(q, k_ref[0, 0, :c], nt, preferred_element_type=jnp.float32)
    s_loc = lax.dot_general(q, k_ref[0, 0, pl.ds(ws, WIN_KEYS)], nt, preferred_element_type=jnp.float32)
    qpos = qi * WINDOW + (r_idx % WINDOW)
    kpos = ws + lax.broadcasted_iota(jnp.int32, (1, WIN_KEYS), 1)
    reach = jnp.where(qi * WINDOW >= c, WINDOW, -1)
    s_loc = jnp.where(jnp.abs(qpos - kpos) <= reach, s_loc, NEG_INF)
    m = jnp.maximum(jnp.maximum(jnp.max(s_ctx, axis=-1, keepdims=True),
                                jnp.max(s_loc, axis=-1, keepdims=True)), sink)
    p_ctx = jnp.exp(s_ctx - m)
    p_loc = jnp.exp(s_loc - m)
    l = (jnp.sum(p_ctx, axis=-1, keepdims=True) + jnp.sum(p_loc, axis=-1, keepdims=True)
         + jnp.exp(sink - m))
    inv = 1.0 / l
    p_ctx = p_ctx.astype(jnp.bfloat16)
    p_loc = p_loc.astype(jnp.bfloat16)
    outs = []
    for pair in range(SWA_GROUP // 2):
        acc = None
        for i, v_ref in enumerate((vlo_ref, vhi_ref)):
            rs = slice((2 * pair + i) * WINDOW, (2 * pair + i + 1) * WINDOW)
            o = (jnp.dot(p_ctx[rs], v_ref[0, 0, :c], preferred_element_type=jnp.float32)
                 + jnp.dot(p_loc[rs], v_ref[0, 0, pl.ds(ws, WIN_KEYS)], preferred_element_type=jnp.float32))
            o = o * inv[rs]
            acc = o if acc is None else acc + o
        outs.append(acc)
    o_ref[0] = jnp.concatenate(outs, axis=-1).astype(o_ref.dtype)


def swa_attention(sink, qs, ks, vlo, vhi, c, with_ctx):
    b, nh, t, _ = qs.shape
    off = 0 if with_ctx else c // WINDOW
    nq = t // WINDOW - off
    kv_spec = pl.BlockSpec((1, 1, t, LANES), lambda i, g, j: (i, g, 0, 0))
    return pl.pallas_call(
        functools.partial(_swa_kernel, off=off, c=c),
        grid=(b, SWA_KV_HEADS, nq),
        in_specs=[
            pl.BlockSpec(memory_space=pltpu.SMEM),
            pl.BlockSpec((1, SWA_GROUP, WINDOW, LANES), lambda i, g, j: (i, g, j + off, 0)),
            kv_spec, kv_spec, kv_spec,
        ],
        out_specs=pl.BlockSpec((1, WINDOW, SWA_GROUP // 2 * LANES), lambda i, g, j: (i, j, g)),
        out_shape=jax.ShapeDtypeStruct((b, nq * WINDOW, SWA_HEADS // 2 * LANES), jnp.bfloat16),
        compiler_params=_params(("parallel", "arbitrary", "arbitrary")),
        name="swa_attention",
    )(sink, qs, ks, vlo, vhi)


def _tree(op, xs):
    xs = list(xs)
    while len(xs) > 1:
        xs = [op(xs[i], xs[i + 1]) for i in range(0, len(xs) - 1, 2)] + (xs[-1:] if len(xs) % 2 else [])
    return xs[0]


def _top16_sweeps(ids, *problems):
    n = len(ids)
    big = float(max(ids) + 1)
    order = sorted(range(n), key=lambda k: ids[k])
    runs = [order[i:i + PEER_TOPK] for i in range(0, n, PEER_TOPK)]

    def step(r, carry):
        for val_ref, out_v, out_i in problems:
            m = _tree(jnp.maximum, [val_ref[k] for k in range(n)])
            firsts = []
            for run in runs:
                am = jnp.full(m.shape, big, jnp.float32)
                for k in reversed(run):
                    am = jnp.where(val_ref[k] == m, float(ids[k]), am)
                firsts.append(am)
            am = _tree(jnp.minimum, firsts)
            for k in range(n):
                val_ref[k] = jnp.where(am == float(ids[k]), NEG_INF, val_ref[k])
            out_v[r] = m
            out_i[r] = am
        return carry

    lax.fori_loop(0, PEER_TOPK, step, 0)


PAIRS = [(a, b) for a in range(PEER_TOPK) for b in range(PEER_TOPK) if (a + 1) * (b + 1) <= PEER_TOPK]


def _route_kernel(h_ref, om_ref, os_ref, wo_ref, mod_ref, n2_ref, wq_ref,
                  h1_ref, x2_ref, q_ref):
    h = h_ref[0]
    d = h.shape[-1]
    mod = mod_ref[0]
    half = om_ref.shape[-1]
    mix = (jnp.dot(om_ref[0], wo_ref[:half], preferred_element_type=jnp.float32)
           + jnp.dot(os_ref[0], wo_ref[half:], preferred_element_type=jnp.float32))
    h1 = h + mod[2:3] * mix
    h1_ref[0] = h1
    x = _rms(h1, n2_ref[...], d) * (1.0 + mod[4:5]) + mod[3:4]
    xhi = x.astype(jnp.bfloat16)
    x2_ref[0, :, :d] = xhi
    x2_ref[0, :, d:] = (x - xhi.astype(jnp.float32)).astype(jnp.bfloat16)
    q = jnp.dot(xhi, wq_ref[...], preferred_element_type=jnp.float32)
    for k in range(2 * PEER_HEADS):
        q_ref[k] = q[:, k * PEER_DHALF:(k + 1) * PEER_DHALF].astype(jnp.bfloat16)


def _topk_kernel(q_ref, keys_ref, idx_ref, g_ref, stage, vals, sv, si, cand, cv, ci, out_i, out_g):
    groups = vals.shape[2]

    def head(hh, carry):
        for part in range(2):
            st = lax.dot_general(keys_ref[hh, part], q_ref[2 * hh + part], (((1,), (1,)), ((), ())),
                                 preferred_element_type=jnp.float32)
            for g in range(groups):
                stage[pl.ds(g * STAGE_STRIDE, PEER_NKEYS), :] = st[:, g * LANES:(g + 1) * LANES]
            for k in range(PEER_NKEYS):
                vals[part, k] = stage[pl.ds(k, groups, stride=STAGE_STRIDE), :]
        _top16_sweeps(list(range(PEER_NKEYS)), *[(vals.at[p], sv.at[p], si.at[p]) for p in range(2)])
        for n, (a, b) in enumerate(PAIRS):
            cand[n] = sv[0, a] + sv[1, b]
        _top16_sweeps([a * PEER_TOPK + b for a, b in PAIRS], (cand, cv, ci))
        top = cv[0]
        e = [jnp.exp(cv[r] - top) for r in range(PEER_TOPK)]
        inv = 1.0 / _tree(jnp.add, e)
        for r in range(PEER_TOPK):
            ia = jnp.floor(ci[r] * (1.0 / PEER_TOPK))
            ib = ci[r] - ia * PEER_TOPK
            i1 = jnp.zeros_like(ia)
            i2 = jnp.zeros_like(ia)
            for a in range(PEER_TOPK):
                i1 = jnp.where(ia == a, si[0, a], i1)
                i2 = jnp.where(ib == a, si[1, a], i2)
            out_i[hh * PEER_TOPK + r] = (i1 * PEER_NKEYS + i2) * N_CHUNK
            out_g[hh * PEER_TOPK + r] = e[r] * inv
        return carry

    lax.fori_loop(0, PEER_HEADS, head, 0)
    for g in range(groups):
        rows = slice(g * LANES, (g + 1) * LANES)
        idx_ref[rows, :] = out_i[:, g, :].T.astype(jnp.int32)
        g_ref[rows, :] = out_g[:, g, :].T


def peer_topk(q16, keys):
    n = q16.shape[1]
    tb = next(t for t in (TOPK_BLK, TOPK_BLK // 2, TOPK_BLK // 4, LANES) if n % t == 0)
    groups = tb // LANES
    blk = lambda dt: pltpu.VMEM((PEER_TOPK, groups, LANES), dt)
    return pl.pallas_call(
        _topk_kernel,
        grid=(n // tb,),
        in_specs=[
            pl.BlockSpec((2 * PEER_HEADS, tb, PEER_DHALF), lambda i: (0, i, 0)),
            pl.BlockSpec(keys.shape, lambda i: (0, 0, 0, 0)),
        ],
        out_specs=[pl.BlockSpec((tb, PEER_ROWS), lambda i: (i, 0))] * 2,
        out_shape=[jax.ShapeDtypeStruct((n, PEER_ROWS), jnp.int32),
                   jax.ShapeDtypeStruct((n, PEER_ROWS), jnp.float32)],
        scratch_shapes=[
            pltpu.VMEM((groups * STAGE_STRIDE, LANES), jnp.float32),
            pltpu.VMEM((2, PEER_NKEYS, groups, LANES), jnp.float32),
            pltpu.VMEM((2, PEER_TOPK, groups, LANES), jnp.float32),
            pltpu.VMEM((2, PEER_TOPK, groups, LANES), jnp.float32),
            pltpu.VMEM((len(PAIRS), groups, LANES), jnp.float32),
            blk(jnp.float32), blk(jnp.float32),
            pltpu.VMEM((PEER_ROWS, groups, LANES), jnp.float32),
            pltpu.VMEM((PEER_ROWS, groups, LANES), jnp.float32),
        ],
        compiler_params=_params(("arbitrary",)),
        name="peer_topk",
    )(q16, keys)


def route(h, om, osw, wo, mod, n2, wq, r, c, with_ctx):
    b, t, d = h.shape
    off = 0 if with_ctx else c // r
    nblk = t // r - off
    nctx = c // r
    tq = nblk * r
    full = lambda a: pl.BlockSpec(a.shape, lambda i, j: (0,) * a.ndim)
    row_spec = lambda w: pl.BlockSpec((1, r, w), lambda i, j: (i, j, 0))
    return pl.pallas_call(
        _route_kernel,
        grid=(b, nblk),
        in_specs=[
            pl.BlockSpec((1, r, d), lambda i, j: (i, j + off, 0)),
            row_spec(om.shape[-1]), row_spec(osw.shape[-1]),
            full(wo),
            pl.BlockSpec((1, N_MOD, d), lambda i, j: (jnp.where(j + off < nctx, b, i), 0, 0)),
            full(n2), full(wq),
        ],
        out_specs=[row_spec(d), row_spec(2 * d),
                   pl.BlockSpec((2 * PEER_HEADS, r, PEER_DHALF), lambda i, j: (0, i * nblk + j, 0))],
        out_shape=[
            jax.ShapeDtypeStruct((b, tq, d), jnp.float32),
            jax.ShapeDtypeStruct((b, tq, 2 * d), jnp.bfloat16),
            jax.ShapeDtypeStruct((2 * PEER_HEADS, b * tq, PEER_DHALF), jnp.bfloat16),
        ],
        compiler_params=_params(("parallel", "arbitrary")),
        name="peer_route",
    )(h, om, osw, wo, mod, n2, wq)


def _pack_kernel(x_ref, o_ref):
    x = x_ref[0]
    half = x.shape[1] // 2
    bits = lambda v: pltpu.bitcast(v.astype(jnp.bfloat16).astype(jnp.float32), jnp.uint32)
    w = (bits(x[:, :half]) & jnp.uint32(0xFFFF0000)) | (bits(x[:, half:]) >> 16)
    for c in range(N_CHUNK):
        o_ref[pl.ds(c, x.shape[0], stride=N_CHUNK), :] = w[:, c * LANES:(c + 1) * LANES]


def pack_table(tabs, l):
    _, e, d = tabs.shape
    be = 512
    return pl.pallas_call(
        _pack_kernel,
        grid=(e // be,),
        in_specs=[pl.BlockSpec((1, be, d), lambda i: (l, i, 0))],
        out_specs=pl.BlockSpec((be * N_CHUNK, LANES), lambda i: (i, 0)),
        out_shape=jax.ShapeDtypeStruct((e * N_CHUNK, LANES), jnp.uint32),
        compiler_params=_params(("parallel",)),
        name="pack_table",
    )(tabs)


def _gather_rows(idx_ref, tab_ref, tile_ref, t):
    for j in range(PEER_ROWS):
        r = pl.multiple_of(idx_ref[t, j], N_CHUNK)
        tile_ref[pl.ds(j, N_CHUNK, stride=TILE_STRIDE), :] = tab_ref[pl.ds(r, N_CHUNK), :]


def _token_pipeline(gather, gather_next, compute, tiles):
    steps = TOK_BLK // TOK_SET

    @pl.when(pl.program_id(0) == 0)
    def _():
        for k in range(TOK_SET):
            gather(tiles[0].at[k], k)

    def step(t, cur, nxt, fill):
        for k in range(TOK_SET):
            compute(cur.at[k], t + k)
            fill(nxt.at[k], k)

    def body(i, carry):
        t = TOK_SET * i
        ahead = lambda tile, k: gather(tile, t + TOK_SET + k)

        @pl.when(i % 2 == 0)
        def _():
            step(t, tiles[0], tiles[1], ahead)

        @pl.when(i % 2 == 1)
        def _():
            step(t, tiles[1], tiles[0], ahead)

        return carry

    lax.fori_loop(0, steps - 1, body, 0)
    step(TOK_BLK - TOK_SET, tiles[1], tiles[0], gather_next)


def _chunk(tile_ref, c):
    return pltpu.bitcast(tile_ref[pl.ds(c * TILE_STRIDE, PEER_ROWS), :], jnp.bfloat16)


def _u_kernel(idx_ref, nxt_ref, x_ref, g_ref, tab_ref, w_ref, *tiles):
    row = lax.broadcasted_iota(jnp.int32, (16, 2 * PEER_ROWS), 0) & 7

    def scores(tile_ref, t):
        x16 = x_ref[t]
        top = jnp.zeros((16, 2 * PEER_ROWS), jnp.float32)
        bot = jnp.zeros((16, 2 * PEER_ROWS), jnp.float32)
        for c in range(N_CHUNK):
            y = lax.dot_general(x16, _chunk(tile_ref, c), (((1,), (1,)), ((), ())),
                                preferred_element_type=jnp.float32)
            top = top + jnp.where(row == c, y, 0.0)
            bot = bot + jnp.where(row == c + N_CHUNK, y, 0.0)
        tot = top + pltpu.roll(bot, 1, axis=1)
        a = jnp.sum(tot, axis=0, keepdims=True)
        gelu = 0.5 * a * (1.0 + lax.erf(a * (2.0 ** -0.5)))
        w_ref[t] = g_ref[t] * gelu

    _token_pipeline(functools.partial(_gather_rows, idx_ref, tab_ref),
                    functools.partial(_gather_rows, nxt_ref, tab_ref), scores, tiles)


def _next_tokens_spec(n):
    per_blk = TOK_BLK // TOK_SET
    return pl.BlockSpec((TOK_SET, PEER_ROWS),
                        lambda i: (jnp.minimum((i + 1) * per_blk, n // TOK_SET - 1), 0),
                        memory_space=pltpu.SMEM)


def peer_scores(idx, x16, g, tab):
    n = idx.shape[0]
    tok = lambda *s: pl.BlockSpec((TOK_BLK,) + s, lambda i: (i,) + (0,) * len(s))
    return pl.pallas_call(
        _u_kernel,
        grid=(n // TOK_BLK,),
        in_specs=[
            pl.BlockSpec((TOK_BLK, PEER_ROWS), lambda i: (i, 0), memory_space=pltpu.SMEM),
            _next_tokens_spec(n),
            tok(16, LANES), tok(1, 2 * PEER_ROWS),
            pl.BlockSpec(tab.shape, lambda i: (0, 0), pipeline_mode=pl.Buffered(1)),
        ],
        out_specs=tok(1, 2 * PEER_ROWS),
        out_shape=jax.ShapeDtypeStruct((n, 1, 2 * PEER_ROWS), jnp.float32),
        scratch_shapes=[pltpu.VMEM((TOK_SET, N_CHUNK * TILE_STRIDE, LANES), jnp.uint32)] * 2,
        compiler_params=_params(("arbitrary",)),
        name="peer_scores",
    )(idx, idx, x16, g, tab)


def _v_kernel(idx_ref, nxt_ref, w_ref, h_ref, g2_ref, tab_ref, o_ref, *tiles):
    row = lax.broadcasted_iota(jnp.int32, (16, 2 * PEER_ROWS), 0)

    def combine(tile_ref, t):
        w = w_ref[t]
        w_hi = w.astype(jnp.bfloat16).astype(jnp.float32)
        w_lo = w - w_hi
        parts = (w_hi, pltpu.roll(w_hi, 2 * PEER_ROWS - 1, axis=1),
                 w_lo, pltpu.roll(w_lo, 2 * PEER_ROWS - 1, axis=1))
        acc = jnp.zeros((16, LANES), jnp.float32)
        for c in range(N_CHUNK):
            lhs = jnp.zeros((16, 2 * PEER_ROWS), jnp.float32)
            for k, part in enumerate(parts):
                lhs = jnp.where(row == c + N_CHUNK * k, part, lhs)
            acc = acc + jnp.dot(lhs.astype(jnp.bfloat16), _chunk(tile_ref, c),
                                preferred_element_type=jnp.float32)
        o_ref[t] = h_ref[t] + g2_ref[0, 0] * (acc[:8] + acc[8:])

    _token_pipeline(functools.partial(_gather_rows, idx_ref, tab_ref),
                    functools.partial(_gather_rows, nxt_ref, tab_ref), combine, tiles)


def peer_combine(idx, w, h8, g2, tab, blocks_per_sample, ctx_blocks):
    n = idx.shape[0]
    tok = lambda *s: pl.BlockSpec((TOK_BLK,) + s, lambda i: (i,) + (0,) * len(s))
    g2_map = lambda i: (i // blocks_per_sample,
                        jnp.where(i % blocks_per_sample < ctx_blocks, 0, 1), 0, 0)
    return pl.pallas_call(
        _v_kernel,
        grid=(n // TOK_BLK,),
        in_specs=[
            pl.BlockSpec((TOK_BLK, PEER_ROWS), lambda i: (i, 0), memory_space=pltpu.SMEM),
            _next_tokens_spec(n),
            tok(1, 2 * PEER_ROWS), tok(8, LANES),
            pl.BlockSpec((1, 1, 8, LANES), g2_map),
            pl.BlockSpec(tab.shape, lambda i: (0, 0), pipeline_mode=pl.Buffered(1)),
        ],
        out_specs=tok(8, LANES),
        out_shape=jax.ShapeDtypeStruct((n, 8, LANES), jnp.float32),
        scratch_shapes=[pltpu.VMEM((TOK_SET, N_CHUNK * TILE_STRIDE, LANES), jnp.uint32)] * 2,
        compiler_params=_params(("arbitrary",)),
        name="peer_combine",
    )(idx, idx, w, h8, g2, tab)


SC_LANES = 16
SC_WORKERS = 32
SC_ROWS = 32
SC_SHARE_LAST = 0.48
GROUPS = 2


def peer_combine_sc(idx, w, tab):
    m = idx.shape[0]
    d = tab.shape[1]
    per = m // SC_WORKERS
    mesh = plsc.VectorSubcoreMesh(core_axis_name="c", subcore_axis_name="s")

    quarters = PEER_ROWS // SC_ROWS
    group = 16

    @functools.partial(
        pl.kernel, mesh=mesh,
        out_type=jax.ShapeDtypeStruct((m, d), jnp.float32),
        scratch_types=[
            pltpu.VMEM((2, PEER_ROWS), jnp.int32),
            pltpu.VMEM((2 * PEER_ROWS,), jnp.float32),
            pltpu.VMEM((2, SC_ROWS, d), jnp.float32),
            pltpu.VMEM((d,), jnp.float32),
            pltpu.SemaphoreType.DMA((2,)),
        ],
        compiler_params=pltpu.CompilerParams(needs_layout_passes=False),
        cost_estimate=pl.CostEstimate(flops=2 * m * PEER_ROWS * d, transcendentals=0,
                                      bytes_accessed=4 * m * (PEER_ROWS * d + d + 3 * PEER_ROWS)),
        name="peer_combine_sc",
    )
    def body(idx_hbm, w_hbm, tab_hbm, out_hbm, idx_v, w_v, rows_v, acc_v, sems):
        base = (lax.axis_index("s") * 2 + lax.axis_index("c")) * per

        def gather(slot, q, buf):
            rows = idx_v.at[slot, pl.ds(q * SC_ROWS, SC_ROWS)]
            return pltpu.make_async_copy(tab_hbm.at[rows], rows_v.at[buf], sems.at[buf])

        def accumulate(buf, q):
            for g0 in range(0, d // SC_LANES, group):
                lanes = [pl.ds((g0 + c) * SC_LANES, SC_LANES) for c in range(group)]

                def row(j, accs):
                    wj = plsc.load_gather(w_v, [jnp.full((SC_LANES,), 2 * q * SC_ROWS + 1, jnp.int32) + 2 * j])
                    return tuple(a + wj * rows_v[buf, j, sl] for a, sl in zip(accs, lanes))

                accs = lax.fori_loop(0, SC_ROWS, row, tuple(acc_v[sl] for sl in lanes))
                for sl, a in zip(lanes, accs):
                    acc_v[sl] = a

        pltpu.sync_copy(idx_hbm.at[base], idx_v.at[0])
        gather(0, 0, 0).start()

        @pl.loop(0, per, step=2)
        def _(i):
            for p in range(2):
                t = base + i + p
                pltpu.sync_copy(w_hbm.at[t], w_v)
                for c in range(d // SC_LANES):
                    acc_v[pl.ds(c * SC_LANES, SC_LANES)] = jnp.zeros((SC_LANES,), jnp.float32)
                for q in range(quarters):
                    buf = q % 2
                    if q + 1 < quarters:
                        gather(p, q + 1, 1 - buf).start()
                    else:
                        nxt = jnp.minimum(t + 1, base + per - 1)
                        pltpu.sync_copy(idx_hbm.at[nxt], idx_v.at[1 - p])
                        gather(1 - p, 0, 1 - buf).start()
                    gather(p, q, buf).wait()
                    accumulate(buf, q)
                pltpu.sync_copy(acc_v, out_hbm.at[t])

        gather(0, 0, 0).wait()

    return body(idx, w, tab)


def _residual_kernel(h_ref, g_ref, a_ref, o_ref):
    o_ref[...] = h_ref[...] + g_ref[0, 0] * a_ref[...]


def gated_residual(h, g2, a, first_row, rows_per_sample, ctx_rows):
    m, d = h.shape
    rb = 256
    blk = pl.BlockSpec((rb, d), lambda i: (i, 0))
    per, ctx, off = rows_per_sample // rb, ctx_rows // rb, first_row // rb
    g_map = lambda i: ((i + off) // per, jnp.where((i + off) % per < ctx, 0, 1), 0, 0)
    return pl.pallas_call(
        _residual_kernel, grid=(m // rb,),
        in_specs=[blk, pl.BlockSpec((1, 1, 1, d), g_map), blk],
        out_specs=blk, out_shape=jax.ShapeDtypeStruct((m, d), jnp.float32),
        compiler_params=_params(("parallel",)), name="gated_residual",
    )(h, g2, a)


def _slots(w, width, offset=0):
    lead = w.shape[:-1]
    n = w.shape[-1] // width
    w = w.reshape(lead + (n, width))
    w = jnp.pad(w, [(0, 0)] * len(lead) + [(0, 0), (offset, LANES - width - offset)])
    return w.reshape(lead + (n * LANES,))


def _rope_tables(rot_dim, lane0, t, c):
    s = t - c
    q = rot_dim // 4
    pos = jnp.arange(s, dtype=jnp.float32)
    rows = jnp.floor(pos / GRID_W)
    cols = pos - rows * GRID_W
    inv = ROPE_BASE ** (-jnp.arange(q, dtype=jnp.float32) / q)
    ar = rows[:, None] * inv
    ac = cols[:, None] * inv
    zero = jnp.zeros_like(ar)
    cos = jnp.cos(jnp.concatenate([ar, ar, ac, ac], axis=-1))
    up = jnp.concatenate([-jnp.sin(ar), zero, -jnp.sin(ac), zero], axis=-1)
    dn = jnp.concatenate([zero, jnp.sin(ar), zero, jnp.sin(ac)], axis=-1)
    pad = lambda a, fill: jnp.pad(
        jnp.pad(a, ((0, 0), (lane0, LANES - lane0 - rot_dim)), constant_values=fill),
        ((c, 0), (0, 0)), constant_values=fill)
    cos = jnp.pad(jnp.pad(cos, ((0, 0), (lane0, LANES - lane0 - rot_dim)), constant_values=1.0),
                  ((c, 0), (0, 0)), constant_values=1.0)
    return jnp.stack([cos, pad(up, 0.0), pad(dn, 0.0)])


def kernel(x, c, ctx, c_ctx, ada_w, ada_b, norm1_g, norm2_g, w_in, mla_qa_g, mla_wuq, mla_kva_g, mla_wukv, mla_qn_g, mla_kn_g, swa_qn_g, swa_kn_g, swa_sink, w_out, peer_wq, peer_keys, peer_u, peer_v):
    b, s, d = x.shape
    nctx = ctx.shape[1]
    t = nctx + s
    depth = ada_w.shape[0]
    r = min(256, nctx)
    bf = jnp.bfloat16

    cond = jnp.zeros((16, d), jnp.float32).at[:b].set(c).at[b].set(c_ctx)
    mod_all = modulation(cond, ada_w, ada_b).reshape(depth, 16, N_MOD, d)
    rope_m = _rope_tables(MLA_ROPE, MLA_NOPE, t, nctx)
    rope_s = _rope_tables(SWA_DIM, 0, t, nctx)

    bg = b // GROUPS
    pending = None
    hs = [jnp.concatenate([ctx[g * bg:(g + 1) * bg], x[g * bg:(g + 1) * bg]], axis=1) for g in range(GROUPS)]
    for l in range(depth):
        last = l == depth - 1
        with_ctx = not last
        wi = w_in[l]
        kv0 = Q_COLS + MLA_KV_RANK
        sk0 = kv0 + MLA_ROPE
        sv0 = sk0 + SWA_KV_HEADS * SWA_DIM
        sv = wi[:, sv0:].reshape(d, SWA_KV_HEADS, 1, SWA_DIM)
        win = jnp.concatenate([
            wi[:, :MLA_Q_RANK], wi[:, Q_COLS:kv0],
            _slots(wi[:, kv0:sk0], MLA_ROPE, MLA_NOPE),
            _slots(wi[:, MLA_Q_RANK:Q_COLS], SWA_DIM),
            _slots(wi[:, sk0:sv0], SWA_DIM),
            jnp.broadcast_to(sv, (d, SWA_KV_HEADS, 2, SWA_DIM)).reshape(d, SWA_KV_HEADS * LANES),
        ], axis=1).astype(bf)
        wuq = _slots(mla_wuq[l], MLA_QK).astype(bf)
        wukv = mla_wukv[l].reshape(MLA_KV_RANK, MLA_HEADS, MLA_NOPE + MLA_V)
        wuk = _slots(wukv[:, :, :MLA_NOPE].reshape(MLA_KV_RANK, -1), MLA_NOPE).astype(bf)
        wv = wukv[:, :, MLA_NOPE:].reshape(MLA_KV_RANK, MLA_HEADS // 2, 2, MLA_V)
        zero = jnp.zeros_like(wv[:, :, 0])
        wuv = jnp.stack([jnp.concatenate([wv[:, :, 0], zero], axis=-1),
                         jnp.concatenate([zero, wv[:, :, 1]], axis=-1)], axis=2)
        wuv = wuv.reshape(MLA_KV_RANK, MLA_HEADS * LANES).astype(bf)
        row = lambda g: g.reshape(1, -1)
        wo, wq, keys = w_out[l].astype(bf), peer_wq[l].astype(bf), peer_keys[l].astype(bf)
        utab, vtab = pack_table(peer_u, l), pack_table(peer_v, l)
        vrows = peer_v.reshape(-1, d)

        for g in range(GROUPS):
            h = hs[g]
            if pending is not None:
                h, pending = lax.optimization_barrier((h, pending))
            mod = jnp.concatenate([mod_all[l, g * bg:(g + 1) * bg], mod_all[l, b:b + 1]], axis=0)
            qm, km, vm, qs, ks, vlo, vhi = projections(
                h, mod, row(norm1_g[l]), win, row(mla_qa_g[l]), wuq, row(mla_kva_g[l]), wuk, wuv,
                row(_slots(mla_qn_g[l], MLA_QK)), row(_slots(mla_kn_g[l], MLA_QK)),
                row(_slots(swa_qn_g[l], SWA_DIM)), row(_slots(swa_kn_g[l], SWA_DIM)),
                rope_m, rope_s, r, nctx)
            om = mla_attention(qm, km, vm, r, nctx, with_ctx)
            osw = swa_attention(swa_sink[l], qs, ks, vlo, vhi, nctx, with_ctx)
            h1, x2, q16 = route(h, om, osw, wo, mod, row(norm2_g[l]), wq, r, nctx, with_ctx)
            idx, gate = peer_topk(q16, keys)

            tl = h1.shape[1]
            n = bg * tl
            x16 = x2.reshape(n, 16, LANES)
            gate = jnp.stack([jnp.zeros_like(gate), gate], axis=-1).reshape(n, 1, 2 * PEER_ROWS)
            g2 = jnp.stack([jnp.broadcast_to(mod[bg, 5], (bg, d)), mod[:bg, 5]], axis=1)
            ctx_rows = nctx if with_ctx else 0
            h1f = h1.reshape(n, d)
            share = SC_SHARE_LAST if (last and g == GROUPS - 1) else 1.0
            n_sc = int(n * share) // 256 * 256
            n_tc = n - n_sc
            w_sc = pending = peer_scores(idx[n_tc:], x16[n_tc:], gate[n_tc:], utab)
            experts = (idx[n_tc:] >> 2) + l * peer_v.shape[1]
            acc_sc = peer_combine_sc(experts, w_sc.reshape(n_sc, 2 * PEER_ROWS), vrows)
            h_sc = gated_residual(h1f[n_tc:], g2.reshape(bg, 2, 1, d), acc_sc, n_tc, tl, ctx_rows)
            if n_tc:
                w = peer_scores(idx[:n_tc], x16[:n_tc], gate[:n_tc], utab)
                h_tc = peer_combine(idx[:n_tc], w, h1f[:n_tc].reshape(n_tc, 8, LANES),
                                    g2.reshape(bg, 2, 8, LANES), vtab, tl // TOK_BLK,
                                    ctx_rows // TOK_BLK).reshape(n_tc, d)
                h_sc = jnp.concatenate([h_tc, h_sc], axis=0)
            hs[g] = h_sc.reshape(bg, tl, d)
    return jnp.concatenate(hs, axis=0)
```

```python
import functools
import jax
import jax.numpy as jnp
from jax import lax
from jax.experimental import pallas as pl
from jax.experimental.pallas import tpu as pltpu
from jax.experimental.pallas import tpu_sc as plsc

LANES = 128
EPS = 1e-6
ROPE_BASE = 10000.0
GRID_W = 64
N_MOD = 6

MLA_HEADS = 8
MLA_NOPE = 64
MLA_ROPE = 32
MLA_QK = MLA_NOPE + MLA_ROPE
MLA_V = 64
MLA_Q_RANK = 384
MLA_KV_RANK = 256
SWA_HEADS = 8
SWA_KV_HEADS = 2
SWA_GROUP = SWA_HEADS // SWA_KV_HEADS
SWA_DIM = 64
WINDOW = 128
WIN_KEYS = 3 * WINDOW
Q_COLS = MLA_Q_RANK + SWA_HEADS * SWA_DIM

PEER_HEADS = 8
PEER_NKEYS = 128
PEER_DHALF = 128
PEER_TOPK = 16
PEER_ROWS = PEER_HEADS * PEER_TOPK
N_CHUNK = 4
TILE_STRIDE = PEER_ROWS + 1
TOPK_BLK = 1024
STAGE_STRIDE = PEER_NKEYS + 8
TOK_BLK = 128
TOK_SET = 8

OFF_QA = 0
OFF_KVA = OFF_QA + MLA_Q_RANK
OFF_KR = OFF_KVA + MLA_KV_RANK
OFF_SQ = OFF_KR + LANES
OFF_SK = OFF_SQ + SWA_HEADS * LANES
OFF_SV = OFF_SK + SWA_KV_HEADS * LANES
N_IN = OFF_SV + SWA_KV_HEADS * LANES

VMEM_LIMIT = 56 * 1024 * 1024
NEG_INF = float("-inf")
LOG2_E = 1.4426950408889634


def _params(sem, vmem=VMEM_LIMIT):
    return pltpu.CompilerParams(dimension_semantics=sem, vmem_limit_bytes=vmem)


def _rms(x, g, n):
    ms = jnp.sum(x * x, axis=-1, keepdims=True) * (1.0 / n)
    return x * lax.rsqrt(ms + EPS) * g


def _rope(x, cos, sin_up, sin_dn, shift):
    return (x * cos + pltpu.roll(x, LANES - shift, axis=1) * sin_up
            + pltpu.roll(x, shift, axis=1) * sin_dn)


def _mod_kernel(c_ref, w_ref, b_ref, o_ref):
    c = c_ref[...]
    s = c * (1.0 / (1.0 + jnp.exp(-c)))
    o_ref[0] = jnp.dot(s, w_ref[0], preferred_element_type=jnp.float32,
                       precision=lax.Precision.HIGHEST) + b_ref[0]


def modulation(cond, ada_w, ada_b):
    nl, d, n6 = ada_w.shape
    tn = 1536
    return pl.pallas_call(
        _mod_kernel,
        grid=(nl, n6 // tn),
        in_specs=[
            pl.BlockSpec(cond.shape, lambda l, j: (0, 0)),
            pl.BlockSpec((1, d, tn), lambda l, j: (l, 0, j)),
            pl.BlockSpec((1, 1, tn), lambda l, j: (l, 0, j)),
        ],
        out_specs=pl.BlockSpec((1, cond.shape[0], tn), lambda l, j: (l, 0, j)),
        out_shape=jax.ShapeDtypeStruct((nl, cond.shape[0], n6), jnp.float32),
        compiler_params=_params(("arbitrary", "arbitrary")),
        name="modulation",
    )(cond, ada_w, ada_b.reshape(nl, 1, n6))


def _proj_kernel(h_ref, mod_ref, n1_ref, win_ref, qag_ref, wuq_ref, kvag_ref, wuk_ref, wuv_ref,
                 qn_ref, kn_ref, sqn_ref, skn_ref, rm_ref, rs_ref,
                 qm_ref, km_ref, vm_ref, qs_ref, ks_ref, vlo_ref, vhi_ref):
    h = h_ref[0]
    d = h.shape[-1]
    mod = mod_ref[0]
    a = _rms(h, n1_ref[...], d) * (1.0 + mod[1:2]) + mod[0:1]
    p = jnp.dot(a.astype(jnp.bfloat16), win_ref[...], preferred_element_type=jnp.float32)

    cm, sm_up, sm_dn = rm_ref[0], rm_ref[1], rm_ref[2]
    cs, ss_up, ss_dn = rs_ref[0], rs_ref[1], rs_ref[2]

    qa = _rms(p[:, OFF_QA:OFF_QA + MLA_Q_RANK], qag_ref[...], MLA_Q_RANK)
    q = jnp.dot(qa.astype(jnp.bfloat16), wuq_ref[...], preferred_element_type=jnp.float32)
    kva = _rms(p[:, OFF_KVA:OFF_KVA + MLA_KV_RANK], kvag_ref[...], MLA_KV_RANK).astype(jnp.bfloat16)
    kn = jnp.dot(kva, wuk_ref[...], preferred_element_type=jnp.float32)
    vm = jnp.dot(kva, wuv_ref[...], preferred_element_type=jnp.float32)
    kr = p[:, OFF_KR:OFF_KR + LANES]
    for hd in range(MLA_HEADS):
        sl = slice(hd * LANES, (hd + 1) * LANES)
        qh = _rope(_rms(q[:, sl], qn_ref[...], MLA_QK), cm, sm_up, sm_dn, MLA_ROPE // 4)
        qm_ref[0, hd] = (qh * (MLA_QK ** -0.5 * LOG2_E)).astype(jnp.bfloat16)
        kh = _rope(_rms(kn[:, sl] + kr, kn_ref[...], MLA_QK), cm, sm_up, sm_dn, MLA_ROPE // 4)
        km_ref[0, hd] = kh.astype(jnp.bfloat16)
        vm_ref[0, hd] = vm[:, sl].astype(jnp.bfloat16)
    for hd in range(SWA_HEADS):
        x = p[:, OFF_SQ + hd * LANES:OFF_SQ + (hd + 1) * LANES]
        qh = _rope(_rms(x, sqn_ref[...], SWA_DIM), cs, ss_up, ss_dn, SWA_DIM // 4)
        qs_ref[0, hd] = (qh * (SWA_DIM ** -0.5)).astype(jnp.bfloat16)
    lane = lax.broadcasted_iota(jnp.int32, (h.shape[0], LANES), 1)
    for g in range(SWA_KV_HEADS):
        x = p[:, OFF_SK + g * LANES:OFF_SK + (g + 1) * LANES]
        kh = _rope(_rms(x, skn_ref[...], SWA_DIM), cs, ss_up, ss_dn, SWA_DIM // 4)
        ks_ref[0, g] = kh.astype(jnp.bfloat16)
        v = p[:, OFF_SV + g * LANES:OFF_SV + (g + 1) * LANES]
        vlo_ref[0, g] = jnp.where(lane < SWA_DIM, v, 0.0).astype(jnp.bfloat16)
        vhi_ref[0, g] = jnp.where(lane >= SWA_DIM, v, 0.0).astype(jnp.bfloat16)


def projections(h, mod, n1, win, qag, wuq, kvag, wuk, wuv, qn, kn, sqn, skn, rope_m, rope_s, r, c):
    b, t, d = h.shape
    nctx = c // r
    full = lambda a: pl.BlockSpec(a.shape, lambda i, j: (0,) * a.ndim)
    head_out = lambda nh: pl.BlockSpec((1, nh, r, LANES), lambda i, j: (i, 0, j, 0))
    head_shape = lambda nh: jax.ShapeDtypeStruct((b, nh, t, LANES), jnp.bfloat16)
    return pl.pallas_call(
        _proj_kernel,
        grid=(b, t // r),
        in_specs=[
            pl.BlockSpec((1, r, d), lambda i, j: (i, j, 0)),
            pl.BlockSpec((1, N_MOD, d), lambda i, j: (jnp.where(j < nctx, b, i), 0, 0)),
            full(n1), full(win), full(qag), full(wuq), full(kvag), full(wuk), full(wuv),
            full(qn), full(kn), full(sqn), full(skn),
            pl.BlockSpec((3, r, LANES), lambda i, j: (0, j, 0)),
            pl.BlockSpec((3, r, LANES), lambda i, j: (0, j, 0)),
        ],
        out_specs=[head_out(MLA_HEADS), head_out(MLA_HEADS), head_out(MLA_HEADS),
                   head_out(SWA_HEADS), head_out(SWA_KV_HEADS), head_out(SWA_KV_HEADS),
                   head_out(SWA_KV_HEADS)],
        out_shape=[head_shape(MLA_HEADS), head_shape(MLA_HEADS), head_shape(MLA_HEADS),
                   head_shape(SWA_HEADS), head_shape(SWA_KV_HEADS), head_shape(SWA_KV_HEADS),
                   head_shape(SWA_KV_HEADS)],
        compiler_params=_params(("parallel", "arbitrary")),
        name="projections",
    )(h, mod, n1, win, qag, wuq, kvag, wuk, wuv, qn, kn, sqn, skn, rope_m, rope_s)


def _mla_kernel(q_ref, k_ref, v_ref, o_ref, *, nctx_blocks, c):
    qi = pl.program_id(2)

    def attend(nk):
        acc = None
        for i in range(2):
            q = q_ref[0, i]
            s = lax.dot_general(q, k_ref[0, i, :nk], (((1,), (1,)), ((), ())),
                                preferred_element_type=jnp.float32)
            m = jnp.max(s, axis=-1, keepdims=True)
            p = jnp.exp2(s - m)
            l = jnp.sum(p, axis=-1, keepdims=True)
            o = jnp.dot(p.astype(jnp.bfloat16), v_ref[0, i, :nk], preferred_element_type=jnp.float32)
            o = o * (1.0 / l)
            acc = o if acc is None else acc + o
        o_ref[0] = acc.astype(o_ref.dtype)

    if nctx_blocks:
        @pl.when(qi < nctx_blocks)
        def _():
            attend(c)

        @pl.when(qi >= nctx_blocks)
        def _():
            attend(k_ref.shape[2])
    else:
        attend(k_ref.shape[2])


def mla_attention(qm, km, vm, tq, c, with_ctx):
    b, nh, t, _ = qm.shape
    off = 0 if with_ctx else c // tq
    nq = t // tq - off
    return pl.pallas_call(
        functools.partial(_mla_kernel, nctx_blocks=(c // tq if with_ctx else 0), c=c),
        grid=(b, nh // 2, nq),
        in_specs=[
            pl.BlockSpec((1, 2, tq, LANES), lambda i, hp, j: (i, hp, j + off, 0)),
            pl.BlockSpec((1, 2, t, LANES), lambda i, hp, j: (i, hp, 0, 0)),
            pl.BlockSpec((1, 2, t, LANES), lambda i, hp, j: (i, hp, 0, 0)),
        ],
        out_specs=pl.BlockSpec((1, tq, LANES), lambda i, hp, j: (i, j, hp)),
        out_shape=jax.ShapeDtypeStruct((b, nq * tq, nh // 2 * LANES), jnp.bfloat16),
        compiler_params=_params(("parallel", "arbitrary", "arbitrary")),
        name="mla_attention",
    )(qm, km, vm)


def _swa_kernel(sink_ref, q_ref, k_ref, vlo_ref, vhi_ref, o_ref, *, off, c):
    g = pl.program_id(1)
    qi = pl.program_id(2) + off
    t = k_ref.shape[2]
    rows = SWA_GROUP * WINDOW
    q = q_ref[0].reshape(rows, LANES)
    r_idx = lax.broadcasted_iota(jnp.int32, (rows, 1), 0)
    sink = jnp.zeros((rows, 1), jnp.float32)
    for i in range(SWA_GROUP):
        sink = jnp.where(r_idx // WINDOW == i, sink_ref[g * SWA_GROUP + i], sink)

    ws = pl.multiple_of(jnp.clip(qi * WINDOW - WINDOW, c, t - WIN_KEYS), WINDOW)
    nt = (((1,), (1,)), ((), ()))
    s_ctx = lax.dot_general(q, k_ref[0, 0, :c], nt, preferred_element_type=jnp.float32)
    s_loc = lax.dot_general(q, k_ref[0, 0, pl.ds(ws, WIN_KEYS)], nt, preferred_element_type=jnp.float32)
    qpos = qi * WINDOW + (r_idx % WINDOW)
    kpos = ws + lax.broadcasted_iota(jnp.int32, (1, WIN_KEYS), 1)
    reach = jnp.where(qi * WINDOW >= c, WINDOW, -1)
    s_loc = jnp.where(jnp.abs(qpos - kpos) <= reach, s_loc, NEG_INF)
    m = jnp.maximum(jnp.maximum(jnp.max(s_ctx, axis=-1, keepdims=True),
                                jnp.max(s_loc, axis=-1, keepdims=True)), sink)
    p_ctx = jnp.exp(s_ctx - m)
    p_loc = jnp.exp(s_loc - m)
    l = (jnp.sum(p_ctx, axis=-1, keepdims=True) + jnp.sum(p_loc, axis=-1, keepdims=True)
         + jnp.exp(sink - m))
    inv = 1.0 / l
    p_ctx = p_ctx.astype(jnp.bfloat16)
    p_loc = p_loc.astype(jnp.bfloat16)
    outs = []
    for pair in range(SWA_GROUP // 2):
        acc = None
        for i, v_ref in enumerate((vlo_ref, vhi_ref)):
            rs = slice((2 * pair + i) * WINDOW, (2 * pair + i + 1) * WINDOW)
            o = (jnp.dot(p_ctx[rs], v_ref[0, 0, :c], preferred_element_type=jnp.float32)
                 + jnp.dot(p_loc[rs], v_ref[0, 0, pl.ds(ws, WIN_KEYS)], preferred_element_type=jnp.float32))
            o = o * inv[rs]
            acc = o if acc is None else acc + o
        outs.append(acc)
    o_ref[0] = jnp.concatenate(outs, axis=-1).astype(o_ref.dtype)


def swa_attention(sink, qs, ks, vlo, vhi, c, with_ctx):
    b, nh, t, _ = qs.shape
    off = 0 if with_ctx else c // WINDOW
    nq = t // WINDOW - off
    kv_spec = pl.BlockSpec((1, 1, t, LANES), lambda i, g, j: (i, g, 0, 0))
    return pl.pallas_call(
        functools.partial(_swa_kernel, off=off, c=c),
        grid=(b, SWA_KV_HEADS, nq),
        in_specs=[
            pl.BlockSpec(memory_space=pltpu.SMEM),
            pl.BlockSpec((1, SWA_GROUP, WINDOW, LANES), lambda i, g, j: (i, g, j + off, 0)),
            kv_spec, kv_spec, kv_spec,
        ],
        out_specs=pl.BlockSpec((1, WINDOW, SWA_GROUP // 2 * LANES), lambda i, g, j: (i, j, g)),
        out_shape=jax.ShapeDtypeStruct((b, nq * WINDOW, SWA_HEADS // 2 * LANES), jnp.bfloat16),
        compiler_params=_params(("parallel", "arbitrary", "arbitrary")),
        name="swa_attention",
    )(sink, qs, ks, vlo, vhi)


def _tree(op, xs):
    xs = list(xs)
    while len(xs) > 1:
        xs = [op(xs[i], xs[i + 1]) for i in range(0, len(xs) - 1, 2)] + (xs[-1:] if len(xs) % 2 else [])
    return xs[0]


def _top16_sweeps(ids, *problems):
    n = len(ids)
    big = float(max(ids) + 1)
    order = sorted(range(n), key=lambda k: ids[k])
    runs = [order[i:i + PEER_TOPK] for i in range(0, n, PEER_TOPK)]

    def step(r, carry):
        for val_ref, out_v, out_i in problems:
            m = _tree(jnp.maximum, [val_ref[k] for k in range(n)])
            firsts = []
            for run in runs:
                am = jnp.full(m.shape, big, jnp.float32)
                for k in reversed(run):
                    am = jnp.where(val_ref[k] == m, float(ids[k]), am)
                firsts.append(am)
            am = _tree(jnp.minimum, firsts)
            for k in range(n):
                val_ref[k] = jnp.where(am == float(ids[k]), NEG_INF, val_ref[k])
            out_v[r] = m
            out_i[r] = am
        return carry

    lax.fori_loop(0, PEER_TOPK, step, 0)


PAIRS = [(a, b) for a in range(PEER_TOPK) for b in range(PEER_TOPK) if (a + 1) * (b + 1) <= PEER_TOPK]


def _route_kernel(h_ref, om_ref, os_ref, wo_ref, mod_ref, n2_ref, wq_ref,
                  h1_ref, x2_ref, q_ref):
    h = h_ref[0]
    d = h.shape[-1]
    mod = mod_ref[0]
    half = om_ref.shape[-1]
    mix = (jnp.dot(om_ref[0], wo_ref[:half], preferred_element_type=jnp.float32)
           + jnp.dot(os_ref[0], wo_ref[half:], preferred_element_type=jnp.float32))
    h1 = h + mod[2:3] * mix
    h1_ref[0] = h1
    x = _rms(h1, n2_ref[...], d) * (1.0 + mod[4:5]) + mod[3:4]
    xhi = x.astype(jnp.bfloat16)
    x2_ref[0, :, :d] = xhi
    x2_ref[0, :, d:] = (x - xhi.astype(jnp.float32)).astype(jnp.bfloat16)
    q = jnp.dot(xhi, wq_ref[...], preferred_element_type=jnp.float32)
    for k in range(2 * PEER_HEADS):
        q_ref[k] = q[:, k * PEER_DHALF:(k + 1) * PEER_DHALF].astype(jnp.bfloat16)


def _topk_kernel(q_ref, keys_ref, idx_ref, g_ref, stage, vals, sv, si, cand, cv, ci, out_i, out_g):
    groups = vals.shape[2]

    def head(hh, carry):
        for part in range(2):
            st = lax.dot_general(keys_ref[hh, part], q_ref[2 * hh + part], (((1,), (1,)), ((), ())),
                                 preferred_element_type=jnp.float32)
            for g in range(groups):
                stage[pl.ds(g * STAGE_STRIDE, PEER_NKEYS), :] = st[:, g * LANES:(g + 1) * LANES]
            for k in range(PEER_NKEYS):
                vals[part, k] = stage[pl.ds(k, groups, stride=STAGE_STRIDE), :]
        _top16_sweeps(list(range(PEER_NKEYS)), *[(vals.at[p], sv.at[p], si.at[p]) for p in range(2)])
        for n, (a, b) in enumerate(PAIRS):
            cand[n] = sv[0, a] + sv[1, b]
        _top16_sweeps([a * PEER_TOPK + b for a, b in PAIRS], (cand, cv, ci))
        top = cv[0]
        e = [jnp.exp(cv[r] - top) for r in range(PEER_TOPK)]
        inv = 1.0 / _tree(jnp.add, e)
        for r in range(PEER_TOPK):
            ia = jnp.floor(ci[r] * (1.0 / PEER_TOPK))
            ib = ci[r] - ia * PEER_TOPK
            i1 = jnp.zeros_like(ia)
            i2 = jnp.zeros_like(ia)
            for a in range(PEER_TOPK):
                i1 = jnp.where(ia == a, si[0, a], i1)
                i2 = jnp.where(ib == a, si[1, a], i2)
            out_i[hh * PEER_TOPK + r] = (i1 * PEER_NKEYS + i2) * N_CHUNK
            out_g[hh * PEER_TOPK + r] = e[r] * inv
        return carry

    lax.fori_loop(0, PEER_HEADS, head, 0)
    for g in range(groups):
        rows = slice(g * LANES, (g + 1) * LANES)
        idx_ref[rows, :] = out_i[:, g, :].T.astype(jnp.int32)
        g_ref[rows, :] = out_g[:, g, :].T


def peer_topk(q16, keys):
    n = q16.shape[1]
    tb = next(t for t in (TOPK_BLK, TOPK_BLK // 2, TOPK_BLK // 4, LANES) if n % t == 0)
    groups = tb // LANES
    blk = lambda dt: pltpu.VMEM((PEER_TOPK, groups, LANES), dt)
    return pl.pallas_call(
        _topk_kernel,
        grid=(n // tb,),
        in_specs=[
            pl.BlockSpec((2 * PEER_HEADS, tb, PEER_DHALF), lambda i: (0, i, 0)),
            pl.BlockSpec(keys.shape, lambda i: (0, 0, 0, 0)),
        ],
        out_specs=[pl.BlockSpec((tb, PEER_ROWS), lambda i: (i, 0))] * 2,
        out_shape=[jax.ShapeDtypeStruct((n, PEER_ROWS), jnp.int32),
                   jax.ShapeDtypeStruct((n, PEER_ROWS), jnp.float32)],
        scratch_shapes=[
            pltpu.VMEM((groups * STAGE_STRIDE, LANES), jnp.float32),
            pltpu.VMEM((2, PEER_NKEYS, groups, LANES), jnp.float32),
            pltpu.VMEM((2, PEER_TOPK, groups, LANES), jnp.float32),
            pltpu.VMEM((2, PEER_TOPK, groups, LANES), jnp.float32),
            pltpu.VMEM((len(PAIRS), groups, LANES), jnp.float32),
            blk(jnp.float32), blk(jnp.float32),
            pltpu.VMEM((PEER_ROWS, groups, LANES), jnp.float32),
            pltpu.VMEM((PEER_ROWS, groups, LANES), jnp.float32),
        ],
        compiler_params=_params(("arbitrary",)),
        name="peer_topk",
    )(q16, keys)


def route(h, om, osw, wo, mod, n2, wq, r, c, with_ctx):
    b, t, d = h.shape
    off = 0 if with_ctx else c // r
    nblk = t // r - off
    nctx = c // r
    tq = nblk * r
    full = lambda a: pl.BlockSpec(a.shape, lambda i, j: (0,) * a.ndim)
    row_spec = lambda w: pl.BlockSpec((1, r, w), lambda i, j: (i, j, 0))
    return pl.pallas_call(
        _route_kernel,
        grid=(b, nblk),
        in_specs=[
            pl.BlockSpec((1, r, d), lambda i, j: (i, j + off, 0)),
            row_spec(om.shape[-1]), row_spec(osw.shape[-1]),
            full(wo),
            pl.BlockSpec((1, N_MOD, d), lambda i, j: (jnp.where(j + off < nctx, b, i), 0, 0)),
            full(n2), full(wq),
        ],
        out_specs=[row_spec(d), row_spec(2 * d),
                   pl.BlockSpec((2 * PEER_HEADS, r, PEER_DHALF), lambda i, j: (0, i * nblk + j, 0))],
        out_shape=[
            jax.ShapeDtypeStruct((b, tq, d), jnp.float32),
            jax.ShapeDtypeStruct((b, tq, 2 * d), jnp.bfloat16),
            jax.ShapeDtypeStruct((2 * PEER_HEADS, b * tq, PEER_DHALF), jnp.bfloat16),
        ],
        compiler_params=_params(("parallel", "arbitrary")),
        name="peer_route",
    )(h, om, osw, wo, mod, n2, wq)


def _pack_kernel(x_ref, o_ref):
    x = x_ref[0]
    half = x.shape[1] // 2
    bits = lambda v: pltpu.bitcast(v.astype(jnp.bfloat16).astype(jnp.float32), jnp.uint32)
    w = (bits(x[:, :half]) & jnp.uint32(0xFFFF0000)) | (bits(x[:, half:]) >> 16)
    for c in range(N_CHUNK):
        o_ref[pl.ds(c, x.shape[0], stride=N_CHUNK), :] = w[:, c * LANES:(c + 1) * LANES]


def pack_table(tabs, l):
    _, e, d = tabs.shape
    be = 512
    return pl.pallas_call(
        _pack_kernel,
        grid=(e // be,),
        in_specs=[pl.BlockSpec((1, be, d), lambda i: (l, i, 0))],
        out_specs=pl.BlockSpec((be * N_CHUNK, LANES), lambda i: (i, 0)),
        out_shape=jax.ShapeDtypeStruct((e * N_CHUNK, LANES), jnp.uint32),
        compiler_params=_params(("parallel",)),
        name="pack_table",
    )(tabs)


def _gather_rows(idx_ref, tab_ref, tile_ref, t):
    for j in range(PEER_ROWS):
        r = pl.multiple_of(idx_ref[t, j], N_CHUNK)
        tile_ref[pl.ds(j, N_CHUNK, stride=TILE_STRIDE), :] = tab_ref[pl.ds(r, N_CHUNK), :]


def _token_pipeline(gather, gather_next, compute, tiles):
    steps = TOK_BLK // TOK_SET

    @pl.when(pl.program_id(0) == 0)
    def _():
        for k in range(TOK_SET):
            gather(tiles[0].at[k], k)

    def step(t, cur, nxt, fill):
        for k in range(TOK_SET):
            compute(cur.at[k], t + k)
            fill(nxt.at[k], k)

    def body(i, carry):
        t = TOK_SET * i
        ahead = lambda tile, k: gather(tile, t + TOK_SET + k)

        @pl.when(i % 2 == 0)
        def _():
            step(t, tiles[0], tiles[1], ahead)

        @pl.when(i % 2 == 1)
        def _():
            step(t, tiles[1], tiles[0], ahead)

        return carry

    lax.fori_loop(0, steps - 1, body, 0)
    step(TOK_BLK - TOK_SET, tiles[1], tiles[0], gather_next)


def _chunk(tile_ref, c):
    return pltpu.bitcast(tile_ref[pl.ds(c * TILE_STRIDE, PEER_ROWS), :], jnp.bfloat16)


def _u_kernel(idx_ref, nxt_ref, x_ref, g_ref, tab_ref, w_ref, *tiles):
    row = lax.broadcasted_iota(jnp.int32, (16, 2 * PEER_ROWS), 0) & 7

    def scores(tile_ref, t):
        x16 = x_ref[t]
        top = jnp.zeros((16, 2 * PEER_ROWS), jnp.float32)
        bot = jnp.zeros((16, 2 * PEER_ROWS), jnp.float32)
        for c in range(N_CHUNK):
            y = lax.dot_general(x16, _chunk(tile_ref, c), (((1,), (1,)), ((), ())),
                                preferred_element_type=jnp.float32)
            top = top + jnp.where(row == c, y, 0.0)
            bot = bot + jnp.where(row == c + N_CHUNK, y, 0.0)
        tot = top + pltpu.roll(bot, 1, axis=1)
        a = jnp.sum(tot, axis=0, keepdims=True)
        gelu = 0.5 * a * (1.0 + lax.erf(a * (2.0 ** -0.5)))
        w_ref[t] = g_ref[t] * gelu

    _token_pipeline(functools.partial(_gather_rows, idx_ref, tab_ref),
                    functools.partial(_gather_rows, nxt_ref, tab_ref), scores, tiles)


def _next_tokens_spec(n):
    per_blk = TOK_BLK // TOK_SET
    return pl.BlockSpec((TOK_SET, PEER_ROWS),
                        lambda i: (jnp.minimum((i + 1) * per_blk, n // TOK_SET - 1), 0),
                        memory_space=pltpu.SMEM)


def peer_scores(idx, x16, g, tab):
    n = idx.shape[0]
    tok = lambda *s: pl.BlockSpec((TOK_BLK,) + s, lambda i: (i,) + (0,) * len(s))
    return pl.pallas_call(
        _u_kernel,
        grid=(n // TOK_BLK,),
        in_specs=[
            pl.BlockSpec((TOK_BLK, PEER_ROWS), lambda i: (i, 0), memory_space=pltpu.SMEM),
            _next_tokens_spec(n),
            tok(16, LANES), tok(1, 2 * PEER_ROWS),
            pl.BlockSpec(tab.shape, lambda i: (0, 0), pipeline_mode=pl.Buffered(1)),
        ],
        out_specs=tok(1, 2 * PEER_ROWS),
        out_shape=jax.ShapeDtypeStruct((n, 1, 2 * PEER_ROWS), jnp.float32),
        scratch_shapes=[pltpu.VMEM((TOK_SET, N_CHUNK * TILE_STRIDE, LANES), jnp.uint32)] * 2,
        compiler_params=_params(("arbitrary",)),
        name="peer_scores",
    )(idx, idx, x16, g, tab)


def _v_kernel(idx_ref, nxt_ref, w_ref, h_ref, g2_ref, tab_ref, o_ref, *tiles):
    row = lax.broadcasted_iota(jnp.int32, (16, 2 * PEER_ROWS), 0)

    def combine(tile_ref, t):
        w = w_ref[t]
        w_hi = w.astype(jnp.bfloat16).astype(jnp.float32)
        w_lo = w - w_hi
        parts = (w_hi, pltpu.roll(w_hi, 2 * PEER_ROWS - 1, axis=1),
                 w_lo, pltpu.roll(w_lo, 2 * PEER_ROWS - 1, axis=1))
        acc = jnp.zeros((16, LANES), jnp.float32)
        for c in range(N_CHUNK):
            lhs = jnp.zeros((16, 2 * PEER_ROWS), jnp.float32)
            for k, part in enumerate(parts):
                lhs = jnp.where(row == c + N_CHUNK * k, part, lhs)
            acc = acc + jnp.dot(lhs.astype(jnp.bfloat16), _chunk(tile_ref, c),
                                preferred_element_type=jnp.float32)
        o_ref[t] = h_ref[t] + g2_ref[0, 0] * (acc[:8] + acc[8:])

    _token_pipeline(functools.partial(_gather_rows, idx_ref, tab_ref),
                    functools.partial(_gather_rows, nxt_ref, tab_ref), combine, tiles)


def peer_combine(idx, w, h8, g2, tab, blocks_per_sample, ctx_blocks):
    n = idx.shape[0]
    tok = lambda *s: pl.BlockSpec((TOK_BLK,) + s, lambda i: (i,) + (0,) * len(s))
    g2_map = lambda i: (i // blocks_per_sample,
                        jnp.where(i % blocks_per_sample < ctx_blocks, 0, 1), 0, 0)
    return pl.pallas_call(
        _v_kernel,
        grid=(n // TOK_BLK,),
        in_specs=[
            pl.BlockSpec((TOK_BLK, PEER_ROWS), lambda i: (i, 0), memory_space=pltpu.SMEM),
            _next_tokens_spec(n),
            tok(1, 2 * PEER_ROWS), tok(8, LANES),
            pl.BlockSpec((1, 1, 8, LANES), g2_map),
            pl.BlockSpec(tab.shape, lambda i: (0, 0), pipeline_mode=pl.Buffered(1)),
        ],
        out_specs=tok(8, LANES),
        out_shape=jax.ShapeDtypeStruct((n, 8, LANES), jnp.float32),
        scratch_shapes=[pltpu.VMEM((TOK_SET, N_CHUNK * TILE_STRIDE, LANES), jnp.uint32)] * 2,
        compiler_params=_params(("arbitrary",)),
        name="peer_combine",
    )(idx, idx, w, h8, g2, tab)


SC_LANES = 16
SC_WORKERS = 32
SC_ROWS = 32
SC_SHARE_LAST = 0.48
GROUPS = 2


def peer_combine_sc(idx, w, tab):
    m = idx.shape[0]
    d = tab.shape[1]
    per = m // SC_WORKERS
    mesh = plsc.VectorSubcoreMesh(core_axis_name="c", subcore_axis_name="s")

    quarters = PEER_ROWS // SC_ROWS
    group = 16

    @functools.partial(
        pl.kernel, mesh=mesh,
        out_type=jax.ShapeDtypeStruct((m, d), jnp.float32),
        scratch_types=[
            pltpu.VMEM((2, PEER_ROWS), jnp.int32),
            pltpu.VMEM((2 * PEER_ROWS,), jnp.float32),
            pltpu.VMEM((2, SC_ROWS, d), jnp.float32),
            pltpu.VMEM((d,), jnp.float32),
            pltpu.SemaphoreType.DMA((2,)),
        ],
        compiler_params=pltpu.CompilerParams(needs_layout_passes=False),
        cost_estimate=pl.CostEstimate(flops=2 * m * PEER_ROWS * d, transcendentals=0,
                                      bytes_accessed=4 * m * (PEER_ROWS * d + d + 3 * PEER_ROWS)),
        name="peer_combine_sc",
    )
    def body(idx_hbm, w_hbm, tab_hbm, out_hbm, idx_v, w_v, rows_v, acc_v, sems):
        base = (lax.axis_index("s") * 2 + lax.axis_index("c")) * per

        def gather(slot, q, buf):
            rows = idx_v.at[slot, pl.ds(q * SC_ROWS, SC_ROWS)]
            return pltpu.make_async_copy(tab_hbm.at[rows], rows_v.at[buf], sems.at[buf])

        def accumulate(buf, q):
            for g0 in range(0, d // SC_LANES, group):
                lanes = [pl.ds((g0 + c) * SC_LANES, SC_LANES) for c in range(group)]

                def row(j, accs):
                    wj = plsc.load_gather(w_v, [jnp.full((SC_LANES,), 2 * q * SC_ROWS + 1, jnp.int32) + 2 * j])
                    return tuple(a + wj * rows_v[buf, j, sl] for a, sl in zip(accs, lanes))

                accs = lax.fori_loop(0, SC_ROWS, row, tuple(acc_v[sl] for sl in lanes))
                for sl, a in zip(lanes, accs):
                    acc_v[sl] = a

        pltpu.sync_copy(idx_hbm.at[base], idx_v.at[0])
        gather(0, 0, 0).start()

        @pl.loop(0, per, step=2)
        def _(i):
            for p in range(2):
                t = base + i + p
                pltpu.sync_copy(w_hbm.at[t], w_v)
                for c in range(d // SC_LANES):
                    acc_v[pl.ds(c * SC_LANES, SC_LANES)] = jnp.zeros((SC_LANES,), jnp.float32)
                for q in range(quarters):
                    buf = q % 2
                    if q + 1 < quarters:
                        gather(p, q + 1, 1 - buf).start()
                    else:
                        nxt = jnp.minimum(t + 1, base + per - 1)
                        pltpu.sync_copy(idx_hbm.at[nxt], idx_v.at[1 - p])
                        gather(1 - p, 0, 1 - buf).start()
                    gather(p, q, buf).wait()
                    accumulate(buf, q)
                pltpu.sync_copy(acc_v, out_hbm.at[t])

        gather(0, 0, 0).wait()

    return body(idx, w, tab)


def _residual_kernel(h_ref, g_ref, a_ref, o_ref):
    o_ref[...] = h_ref[...] + g_ref[0, 0] * a_ref[...]


def gated_residual(h, g2, a, first_row, rows_per_sample, ctx_rows):
    m, d = h.shape
    rb = 256
    blk = pl.BlockSpec((rb, d), lambda i: (i, 0))
    per, ctx, off = rows_per_sample // rb, ctx_rows // rb, first_row // rb
    g_map = lambda i: ((i + off) // per, jnp.where((i + off) % per < ctx, 0, 1), 0, 0)
    return pl.pallas_call(
        _residual_kernel, grid=(m // rb,),
        in_specs=[blk, pl.BlockSpec((1, 1, 1, d), g_map), blk],
        out_specs=blk, out_shape=jax.ShapeDtypeStruct((m, d), jnp.float32),
        compiler_params=_params(("parallel",)), name="gated_residual",
    )(h, g2, a)


def _slots(w, width, offset=0):
    lead = w.shape[:-1]
    n = w.shape[-1] // width
    w = w.reshape(lead + (n, width))
    w = jnp.pad(w, [(0, 0)] * len(lead) + [(0, 0), (offset, LANES - width - offset)])
    return w.reshape(lead + (n * LANES,))


def _rope_tables(rot_dim, lane0, t, c):
    s = t - c
    q = rot_dim // 4
    pos = jnp.arange(s, dtype=jnp.float32)
    rows = jnp.floor(pos / GRID_W)
    cols = pos - rows * GRID_W
    inv = ROPE_BASE ** (-jnp.arange(q, dtype=jnp.float32) / q)
    ar = rows[:, None] * inv
    ac = cols[:, None] * inv
    zero = jnp.zeros_like(ar)
    cos = jnp.cos(jnp.concatenate([ar, ar, ac, ac], axis=-1))
    up = jnp.concatenate([-jnp.sin(ar), zero, -jnp.sin(ac), zero], axis=-1)
    dn = jnp.concatenate([zero, jnp.sin(ar), zero, jnp.sin(ac)], axis=-1)
    pad = lambda a, fill: jnp.pad(
        jnp.pad(a, ((0, 0), (lane0, LANES - lane0 - rot_dim)), constant_values=fill),
        ((c, 0), (0, 0)), constant_values=fill)
    cos = jnp.pad(jnp.pad(cos, ((0, 0), (lane0, LANES - lane0 - rot_dim)), constant_values=1.0),
                  ((c, 0), (0, 0)), constant_values=1.0)
    return jnp.stack([cos, pad(up, 0.0), pad(dn, 0.0)])


def kernel(x, c, ctx, c_ctx, ada_w, ada_b, norm1_g, norm2_g, w_in, mla_qa_g, mla_wuq, mla_kva_g, mla_wukv, mla_qn_g, mla_kn_g, swa_qn_g, swa_kn_g, swa_sink, w_out, peer_wq, peer_keys, peer_u, peer_v):
    b, s, d = x.shape
    nctx = ctx.shape[1]
    t = nctx + s
    depth = ada_w.shape[0]
    r = min(256, nctx)
    bf = jnp.bfloat16

    cond = jnp.zeros((16, d), jnp.float32).at[:b].set(c).at[b].set(c_ctx)
    mod_all = modulation(cond, ada_w, ada_b).reshape(depth, 16, N_MOD, d)
    rope_m = _rope_tables(MLA_ROPE, MLA_NOPE, t, nctx)
    rope_s = _rope_tables(SWA_DIM, 0, t, nctx)

    bg = b // GROUPS
    pending = None
    hs = [jnp.concatenate([ctx[g * bg:(g + 1) * bg], x[g * bg:(g + 1) * bg]], axis=1) for g in range(GROUPS)]
    for l in range(depth):
        last = l == depth - 1
        with_ctx = not last
        wi = w_in[l]
        kv0 = Q_COLS + MLA_KV_RANK
        sk0 = kv0 + MLA_ROPE
        sv0 = sk0 + SWA_KV_HEADS * SWA_DIM
        sv = wi[:, sv0:].reshape(d, SWA_KV_HEADS, 1, SWA_DIM)
        win = jnp.concatenate([
            wi[:, :MLA_Q_RANK], wi[:, Q_COLS:kv0],
            _slots(wi[:, kv0:sk0], MLA_ROPE, MLA_NOPE),
            _slots(wi[:, MLA_Q_RANK:Q_COLS], SWA_DIM),
            _slots(wi[:, sk0:sv0], SWA_DIM),
            jnp.broadcast_to(sv, (d, SWA_KV_HEADS, 2, SWA_DIM)).reshape(d, SWA_KV_HEADS * LANES),
        ], axis=1).astype(bf)
        wuq = _slots(mla_wuq[l], MLA_QK).astype(bf)
        wukv = mla_wukv[l].reshape(MLA_KV_RANK, MLA_HEADS, MLA_NOPE + MLA_V)
        wuk = _slots(wukv[:, :, :MLA_NOPE].reshape(MLA_KV_RANK, -1), MLA_NOPE).astype(bf)
        wv = wukv[:, :, MLA_NOPE:].reshape(MLA_KV_RANK, MLA_HEADS // 2, 2, MLA_V)
        zero = jnp.zeros_like(wv[:, :, 0])
        wuv = jnp.stack([jnp.concatenate([wv[:, :, 0], zero], axis=-1),
                         jnp.concatenate([zero, wv[:, :, 1]], axis=-1)], axis=2)
        wuv = wuv.reshape(MLA_KV_RANK, MLA_HEADS * LANES).astype(bf)
        row = lambda g: g.reshape(1, -1)
        wo, wq, keys = w_out[l].astype(bf), peer_wq[l].astype(bf), peer_keys[l].astype(bf)
        utab, vtab = pack_table(peer_u, l), pack_table(peer_v, l)
        vrows = peer_v.reshape(-1, d)

        for g in range(GROUPS):
            h = hs[g]
            if pending is not None:
                h, pending = lax.optimization_barrier((h, pending))
            mod = jnp.concatenate([mod_all[l, g * bg:(g + 1) * bg], mod_all[l, b:b + 1]], axis=0)
            qm, km, vm, qs, ks, vlo, vhi = projections(
                h, mod, row(norm1_g[l]), win, row(mla_qa_g[l]), wuq, row(mla_kva_g[l]), wuk, wuv,
                row(_slots(mla_qn_g[l], MLA_QK)), row(_slots(mla_kn_g[l], MLA_QK)),
                row(_slots(swa_qn_g[l], SWA_DIM)), row(_slots(swa_kn_g[l], SWA_DIM)),
                rope_m, rope_s, r, nctx)
            om = mla_attention(qm, km, vm, r, nctx, with_ctx)
            osw = swa_attention(swa_sink[l], qs, ks, vlo, vhi, nctx, with_ctx)
            h1, x2, q16 = route(h, om, osw, wo, mod, row(norm2_g[l]), wq, r, nctx, with_ctx)
            idx, gate = peer_topk(q16, keys)

            tl = h1.shape[1]
            n = bg * tl
            x16 = x2.reshape(n, 16, LANES)
            gate = jnp.stack([jnp.zeros_like(gate), gate], axis=-1).reshape(n, 1, 2 * PEER_ROWS)
            g2 = jnp.stack([jnp.broadcast_to(mod[bg, 5], (bg, d)), mod[:bg, 5]], axis=1)
            ctx_rows = nctx if with_ctx else 0
            h1f = h1.reshape(n, d)
            share = SC_SHARE_LAST if (last and g == GROUPS - 1) else 1.0
            n_sc = int(n * share) // 256 * 256
            n_tc = n - n_sc
            w_sc = peer_scores(idx[n_tc:], x16[n_tc:], gate[n_tc:], utab).reshape(n_sc, 2 * PEER_ROWS)
            experts = (idx[n_tc:] >> 2) + l * peer_v.shape[1]
            pending = (experts, w_sc)
            acc_sc = peer_combine_sc(experts, w_sc, vrows)
            h_sc = gated_residual(h1f[n_tc:], g2.reshape(bg, 2, 1, d), acc_sc, n_tc, tl, ctx_rows)
            if n_tc:
                w = peer_scores(idx[:n_tc], x16[:n_tc], gate[:n_tc], utab)
                h_tc = peer_combine(idx[:n_tc], w, h1f[:n_tc].reshape(n_tc, 8, LANES),
                                    g2.reshape(bg, 2, 8, LANES), vtab, tl // TOK_BLK,
                                    ctx_rows // TOK_BLK).reshape(n_tc, d)
                h_sc = jnp.concatenate([h_tc, h_sc], axis=0)
            hs[g] = h_sc.reshape(bg, tl, d)
    return jnp.concatenate(hs, axis=0)
```

```python
import functools
import jax
import jax.numpy as jnp
from jax import lax
from jax.experimental import pallas as pl
from jax.experimental.pallas import tpu as pltpu
from jax.experimental.pallas import tpu_sc as plsc

LANES = 128
EPS = 1e-6
ROPE_BASE = 10000.0
GRID_W = 64
N_MOD = 6

MLA_HEADS = 8
MLA_NOPE = 64
MLA_ROPE = 32
MLA_QK = MLA_NOPE + MLA_ROPE
MLA_V = 64
MLA_Q_RANK = 384
MLA_KV_RANK = 256
SWA_HEADS = 8
SWA_KV_HEADS = 2
SWA_GROUP = SWA_HEADS // SWA_KV_HEADS
SWA_DIM = 64
WINDOW = 128
WIN_KEYS = 3 * WINDOW
Q_COLS = MLA_Q_RANK + SWA_HEADS * SWA_DIM

PEER_HEADS = 8
PEER_NKEYS = 128
PEER_DHALF = 128
PEER_TOPK = 16
PEER_ROWS = PEER_HEADS * PEER_TOPK
N_CHUNK = 4
TILE_STRIDE = PEER_ROWS + 1
TOPK_BLK = 1024
STAGE_STRIDE = PEER_NKEYS + 8
TOK_BLK = 128
TOK_SET = 8

OFF_QA = 0
OFF_KVA = OFF_QA + MLA_Q_RANK
OFF_KR = OFF_KVA + MLA_KV_RANK
OFF_SQ = OFF_KR + LANES
OFF_SK = OFF_SQ + SWA_HEADS * LANES
OFF_SV = OFF_SK + SWA_KV_HEADS * LANES
N_IN = OFF_SV + SWA_KV_HEADS * LANES

VMEM_LIMIT = 56 * 1024 * 1024
NEG_INF = float("-inf")
LOG2_E = 1.4426950408889634


def _params(sem, vmem=VMEM_LIMIT):
    return pltpu.CompilerParams(dimension_semantics=sem, vmem_limit_bytes=vmem)


def _rms(x, g, n):
    ms = jnp.sum(x * x, axis=-1, keepdims=True) * (1.0 / n)
    return x * lax.rsqrt(ms + EPS) * g


def _rope(x, cos, sin_up, sin_dn, shift):
    return (x * cos + pltpu.roll(x, LANES - shift, axis=1) * sin_up
            + pltpu.roll(x, shift, axis=1) * sin_dn)


def _mod_kernel(c_ref, w_ref, b_ref, o_ref):
    c = c_ref[...]
    s = c * (1.0 / (1.0 + jnp.exp(-c)))
    o_ref[0] = jnp.dot(s, w_ref[0], preferred_element_type=jnp.float32,
                       precision=lax.Precision.HIGHEST) + b_ref[0]


def modulation(cond, ada_w, ada_b):
    nl, d, n6 = ada_w.shape
    tn = 1536
    return pl.pallas_call(
        _mod_kernel,
        grid=(nl, n6 // tn),
        in_specs=[
            pl.BlockSpec(cond.shape, lambda l, j: (0, 0)),
            pl.BlockSpec((1, d, tn), lambda l, j: (l, 0, j)),
            pl.BlockSpec((1, 1, tn), lambda l, j: (l, 0, j)),
        ],
        out_specs=pl.BlockSpec((1, cond.shape[0], tn), lambda l, j: (l, 0, j)),
        out_shape=jax.ShapeDtypeStruct((nl, cond.shape[0], n6), jnp.float32),
        compiler_params=_params(("arbitrary", "arbitrary")),
        name="modulation",
    )(cond, ada_w, ada_b.reshape(nl, 1, n6))


def _proj_kernel(h_ref, mod_ref, n1_ref, win_ref, qag_ref, wuq_ref, kvag_ref, wuk_ref, wuv_ref,
                 qn_ref, kn_ref, sqn_ref, skn_ref, rm_ref, rs_ref,
                 qm_ref, km_ref, vm_ref, qs_ref, ks_ref, vlo_ref, vhi_ref):
    h = h_ref[0]
    d = h.shape[-1]
    mod = mod_ref[0]
    a = _rms(h, n1_ref[...], d) * (1.0 + mod[1:2]) + mod[0:1]
    p = jnp.dot(a.astype(jnp.bfloat16), win_ref[...], preferred_element_type=jnp.float32)

    cm, sm_up, sm_dn = rm_ref[0], rm_ref[1], rm_ref[2]
    cs, ss_up, ss_dn = rs_ref[0], rs_ref[1], rs_ref[2]

    qa = _rms(p[:, OFF_QA:OFF_QA + MLA_Q_RANK], qag_ref[...], MLA_Q_RANK)
    q = jnp.dot(qa.astype(jnp.bfloat16), wuq_ref[...], preferred_element_type=jnp.float32)
    kva = _rms(p[:, OFF_KVA:OFF_KVA + MLA_KV_RANK], kvag_ref[...], MLA_KV_RANK).astype(jnp.bfloat16)
    kn = jnp.dot(kva, wuk_ref[...], preferred_element_type=jnp.float32)
    vm = jnp.dot(kva, wuv_ref[...], preferred_element_type=jnp.float32)
    kr = p[:, OFF_KR:OFF_KR + LANES]
    for hd in range(MLA_HEADS):
        sl = slice(hd * LANES, (hd + 1) * LANES)
        qh = _rope(_rms(q[:, sl], qn_ref[...], MLA_QK), cm, sm_up, sm_dn, MLA_ROPE // 4)
        qm_ref[0, hd] = (qh * (MLA_QK ** -0.5 * LOG2_E)).astype(jnp.bfloat16)
        kh = _rope(_rms(kn[:, sl] + kr, kn_ref[...], MLA_QK), cm, sm_up, sm_dn, MLA_ROPE // 4)
        km_ref[0, hd] = kh.astype(jnp.bfloat16)
        vm_ref[0, hd] = vm[:, sl].astype(jnp.bfloat16)
    for hd in range(SWA_HEADS):
        x = p[:, OFF_SQ + hd * LANES:OFF_SQ + (hd + 1) * LANES]
        qh = _rope(_rms(x, sqn_ref[...], SWA_DIM), cs, ss_up, ss_dn, SWA_DIM // 4)
        qs_ref[0, hd] = (qh * (SWA_DIM ** -0.5)).astype(jnp.bfloat16)
    lane = lax.broadcasted_iota(jnp.int32, (h.shape[0], LANES), 1)
    for g in range(SWA_KV_HEADS):
        x = p[:, OFF_SK + g * LANES:OFF_SK + (g + 1) * LANES]
        kh = _rope(_rms(x, skn_ref[...], SWA_DIM), cs, ss_up, ss_dn, SWA_DIM // 4)
        ks_ref[0, g] = kh.astype(jnp.bfloat16)
        v = p[:, OFF_SV + g * LANES:OFF_SV + (g + 1) * LANES]
        vlo_ref[0, g] = jnp.where(lane < SWA_DIM, v, 0.0).astype(jnp.bfloat16)
        vhi_ref[0, g] = jnp.where(lane >= SWA_DIM, v, 0.0).astype(jnp.bfloat16)


def projections(h, mod, n1, win, qag, wuq, kvag, wuk, wuv, qn, kn, sqn, skn, rope_m, rope_s, r, c):
    b, t, d = h.shape
    nctx = c // r
    full = lambda a: pl.BlockSpec(a.shape, lambda i, j: (0,) * a.ndim)
    head_out = lambda nh: pl.BlockSpec((1, nh, r, LANES), lambda i, j: (i, 0, j, 0))
    head_shape = lambda nh: jax.ShapeDtypeStruct((b, nh, t, LANES), jnp.bfloat16)
    return pl.pallas_call(
        _proj_kernel,
        grid=(b, t // r),
        in_specs=[
            pl.BlockSpec((1, r, d), lambda i, j: (i, j, 0)),
            pl.BlockSpec((1, N_MOD, d), lambda i, j: (jnp.where(j < nctx, b, i), 0, 0)),
            full(n1), full(win), full(qag), full(wuq), full(kvag), full(wuk), full(wuv),
            full(qn), full(kn), full(sqn), full(skn),
            pl.BlockSpec((3, r, LANES), lambda i, j: (0, j, 0)),
            pl.BlockSpec((3, r, LANES), lambda i, j: (0, j, 0)),
        ],
        out_specs=[head_out(MLA_HEADS), head_out(MLA_HEADS), head_out(MLA_HEADS),
                   head_out(SWA_HEADS), head_out(SWA_KV_HEADS), head_out(SWA_KV_HEADS),
                   head_out(SWA_KV_HEADS)],
        out_shape=[head_shape(MLA_HEADS), head_shape(MLA_HEADS), head_shape(MLA_HEADS),
                   head_shape(SWA_HEADS), head_shape(SWA_KV_HEADS), head_shape(SWA_KV_HEADS),
                   head_shape(SWA_KV_HEADS)],
        compiler_params=_params(("parallel", "arbitrary")),
        name="projections",
    )(h, mod, n1, win, qag, wuq, kvag, wuk, wuv, qn, kn, sqn, skn, rope_m, rope_s)


def _mla_kernel(q_ref, k_ref, v_ref, o_ref, *, nctx_blocks, c):
    qi = pl.program_id(2)

    def attend(nk):
        acc = None
        for i in range(2):
            q = q_ref[0, i]
            s = lax.dot_general(q, k_ref[0, i, :nk], (((1,), (1,)), ((), ())),
                                preferred_element_type=jnp.float32)
            m = jnp.max(s, axis=-1, keepdims=True)
            p = jnp.exp2(s - m)
            l = jnp.sum(p, axis=-1, keepdims=True)
            o = jnp.dot(p.astype(jnp.bfloat16), v_ref[0, i, :nk], preferred_element_type=jnp.float32)
            o = o * (1.0 / l)
            acc = o if acc is None else acc + o
        o_ref[0] = acc.astype(o_ref.dtype)

    if nctx_blocks:
        @pl.when(qi < nctx_blocks)
        def _():
            attend(c)

        @pl.when(qi >= nctx_blocks)
        def _():
            attend(k_ref.shape[2])
    else:
        attend(k_ref.shape[2])


def mla_attention(qm, km, vm, tq, c, with_ctx):
    b, nh, t, _ = qm.shape
    off = 0 if with_ctx else c // tq
    nq = t // tq - off
    return pl.pallas_call(
        functools.partial(_mla_kernel, nctx_blocks=(c // tq if with_ctx else 0), c=c),
        grid=(b, nh // 2, nq),
        in_specs=[
            pl.BlockSpec((1, 2, tq, LANES), lambda i, hp, j: (i, hp, j + off, 0)),
            pl.BlockSpec((1, 2, t, LANES), lambda i, hp, j: (i, hp, 0, 0)),
            pl.BlockSpec((1, 2, t, LANES), lambda i, hp, j: (i, hp, 0, 0)),
        ],
        out_specs=pl.BlockSpec((1, tq, LANES), lambda i, hp, j: (i, j, hp)),
        out_shape=jax.ShapeDtypeStruct((b, nq * tq, nh // 2 * LANES), jnp.bfloat16),
        compiler_params=_params(("parallel", "arbitrary", "arbitrary")),
        name="mla_attention",
    )(qm, km, vm)


def _swa_kernel(sink_ref, q_ref, k_ref, vlo_ref, vhi_ref, o_ref, *, off, c):
    g = pl.program_id(1)
    qi = pl.program_id(2) + off
    t = k_ref.shape[2]
    rows = SWA_GROUP * WINDOW
    q = q_ref[0].reshape(rows, LANES)
    r_idx = lax.broadcasted_iota(jnp.int32, (rows, 1), 0)
    sink = jnp.zeros((rows, 1), jnp.float32)
    for i in range(SWA_GROUP):
        sink = jnp.where(r_idx // WINDOW == i, sink_ref[g * SWA_GROUP + i], sink)

    ws = pl.multiple_of(jnp.clip(qi * WINDOW - WINDOW, c, t - WIN_KEYS), WINDOW)
    nt = (((1,), (1,)), ((), ()))
    s_ctx = lax.dot_general(q, k_ref[0, 0, :c], nt, preferred_element_type=jnp.float32)
    s_loc = lax.dot_general(q, k_ref[0, 0, pl.ds(ws, WIN_KEYS)], nt, preferred_element_type=jnp.float32)
    qpos = qi * WINDOW + (r_idx % WINDOW)
    kpos = ws + lax.broadcasted_iota(jnp.int32, (1, WIN_KEYS), 1)
    reach = jnp.where(qi * WINDOW >= c, WINDOW, -1)
    s_loc = jnp.where(jnp.abs(qpos - kpos) <= reach, s_loc, NEG_INF)
    m = jnp.maximum(jnp.maximum(jnp.max(s_ctx, axis=-1, keepdims=True),
                                jnp.max(s_loc, axis=-1, keepdims=True)), sink)
    p_ctx = jnp.exp(s_ctx - m)
    p_loc = jnp.exp(s_loc - m)
    l = (jnp.sum(p_ctx, axis=-1, keepdims=True) + jnp.sum(p_loc, axis=-1, keepdims=True)
         + jnp.exp(sink - m))
    inv = 1.0 / l
    p_ctx = p_ctx.astype(jnp.bfloat16)
    p_loc = p_loc.astype(jnp.bfloat16)
    outs = []
    for pair in range(SWA_GROUP // 2):
        acc = None
        for i, v_ref in enumerate((vlo_ref, vhi_ref)):
            rs = slice((2 * pair + i) * WINDOW, (2 * pair + i + 1) * WINDOW)
            o = (jnp.dot(p_ctx[rs], v_ref[0, 0, :c], preferred_element_type=jnp.float32)
                 + jnp.dot(p_loc[rs], v_ref[0, 0, pl.ds(ws, WIN_KEYS)], preferred_element_type=jnp.float32))
            o = o * inv[rs]
            acc = o if acc is None else acc + o
        outs.append(acc)
    o_ref[0] = jnp.concatenate(outs, axis=-1).astype(o_ref.dtype)


def swa_attention(sink, qs, ks, vlo, vhi, c, with_ctx):
    b, nh, t, _ = qs.shape
    off = 0 if with_ctx else c // WINDOW
    nq = t // WINDOW - off
    kv_spec = pl.BlockSpec((1, 1, t, LANES), lambda i, g, j: (i, g, 0, 0))
    return pl.pallas_call(
        functools.partial(_swa_kernel, off=off, c=c),
        grid=(b, SWA_KV_HEADS, nq),
        in_specs=[
            pl.BlockSpec(memory_space=pltpu.SMEM),
            pl.BlockSpec((1, SWA_GROUP, WINDOW, LANES), lambda i, g, j: (i, g, j + off, 0)),
            kv_spec, kv_spec, kv_spec,
        ],
        out_specs=pl.BlockSpec((1, WINDOW, SWA_GROUP // 2 * LANES), lambda i, g, j: (i, j, g)),
        out_shape=jax.ShapeDtypeStruct((b, nq * WINDOW, SWA_HEADS // 2 * LANES), jnp.bfloat16),
        compiler_params=_params(("parallel", "arbitrary", "arbitrary")),
        name="swa_attention",
    )(sink, qs, ks, vlo, vhi)


def _tree(op, xs):
    xs = list(xs)
    while len(xs) > 1:
        xs = [op(xs[i], xs[i + 1]) for i in range(0, len(xs) - 1, 2)] + (xs[-1:] if len(xs) % 2 else [])
    return xs[0]


def _top16_sweeps(ids, *problems):
    n = len(ids)
    big = float(max(ids) + 1)
    order = sorted(range(n), key=lambda k: ids[k])
    runs = [order[i:i + PEER_TOPK] for i in range(0, n, PEER_TOPK)]

    def step(r, carry):
        for val_ref, out_v, out_i in problems:
            m = _tree(jnp.maximum, [val_ref[k] for k in range(n)])
            firsts = []
            for run in runs:
                am = jnp.full(m.shape, big, jnp.float32)
                for k in reversed(run):
                    am = jnp.where(val_ref[k] == m, float(ids[k]), am)
                firsts.append(am)
            am = _tree(jnp.minimum, firsts)
            for k in range(n):
                val_ref[k] = jnp.where(am == float(ids[k]), NEG_INF, val_ref[k])
            out_v[r] = m
            out_i[r] = am
        return carry

    lax.fori_loop(0, PEER_TOPK, step, 0)


PAIRS = [(a, b) for a in range(PEER_TOPK) for b in range(PEER_TOPK) if (a + 1) * (b + 1) <= PEER_TOPK]


def _route_kernel(h_ref, om_ref, os_ref, wo_ref, mod_ref, n2_ref, wq_ref,
                  h1_ref, x2_ref, q_ref):
    h = h_ref[0]
    d = h.shape[-1]
    mod = mod_ref[0]
    half = om_ref.shape[-1]
    mix = (jnp.dot(om_ref[0], wo_ref[:half], preferred_element_type=jnp.float32)
           + jnp.dot(os_ref[0], wo_ref[half:], preferred_element_type=jnp.float32))
    h1 = h + mod[2:3] * mix
    h1_ref[0] = h1
    x = _rms(h1, n2_ref[...], d) * (1.0 + mod[4:5]) + mod[3:4]
    xhi = x.astype(jnp.bfloat16)
    x2_ref[0, :, :d] = xhi
    x2_ref[0, :, d:] = (x - xhi.astype(jnp.float32)).astype(jnp.bfloat16)
    q = jnp.dot(xhi, wq_ref[...], preferred_element_type=jnp.float32)
    for k in range(2 * PEER_HEADS):
        q_ref[k] = q[:, k * PEER_DHALF:(k + 1) * PEER_DHALF].astype(jnp.bfloat16)


def _topk_kernel(q_ref, keys_ref, idx_ref, g_ref, stage, vals, sv, si, cand, cv, ci, out_i, out_g):
    groups = vals.shape[2]

    def head(hh, carry):
        for part in range(2):
            st = lax.dot_general(keys_ref[hh, part], q_ref[2 * hh + part], (((1,), (1,)), ((), ())),
                                 preferred_element_type=jnp.float32)
            for g in range(groups):
                stage[pl.ds(g * STAGE_STRIDE, PEER_NKEYS), :] = st[:, g * LANES:(g + 1) * LANES]
            for k in range(PEER_NKEYS):
                vals[part, k] = stage[pl.ds(k, groups, stride=STAGE_STRIDE), :]
        _top16_sweeps(list(range(PEER_NKEYS)), *[(vals.at[p], sv.at[p], si.at[p]) for p in range(2)])
        for n, (a, b) in enumerate(PAIRS):
            cand[n] = sv[0, a] + sv[1, b]
        _top16_sweeps([a * PEER_TOPK + b for a, b in PAIRS], (cand, cv, ci))
        top = cv[0]
        e = [jnp.exp(cv[r] - top) for r in range(PEER_TOPK)]
        inv = 1.0 / _tree(jnp.add, e)
        for r in range(PEER_TOPK):
            ia = jnp.floor(ci[r] * (1.0 / PEER_TOPK))
            ib = ci[r] - ia * PEER_TOPK
            i1 = jnp.zeros_like(ia)
            i2 = jnp.zeros_like(ia)
            for a in range(PEER_TOPK):
                i1 = jnp.where(ia == a, si[0, a], i1)
                i2 = jnp.where(ib == a, si[1, a], i2)
            out_i[hh * PEER_TOPK + r] = (i1 * PEER_NKEYS + i2) * N_CHUNK
            out_g[hh * PEER_TOPK + r] = e[r] * inv
        return carry

    lax.fori_loop(0, PEER_HEADS, head, 0)
    for g in range(groups):
        rows = slice(g * LANES, (g + 1) * LANES)
        idx_ref[rows, :] = out_i[:, g, :].T.astype(jnp.int32)
        g_ref[rows, :] = out_g[:, g, :].T


def peer_topk(q16, keys):
    n = q16.shape[1]
    tb = next(t for t in (TOPK_BLK, TOPK_BLK // 2, TOPK_BLK // 4, LANES) if n % t == 0)
    groups = tb // LANES
    blk = lambda dt: pltpu.VMEM((PEER_TOPK, groups, LANES), dt)
    return pl.pallas_call(
        _topk_kernel,
        grid=(n // tb,),
        in_specs=[
            pl.BlockSpec((2 * PEER_HEADS, tb, PEER_DHALF), lambda i: (0, i, 0)),
            pl.BlockSpec(keys.shape, lambda i: (0, 0, 0, 0)),
        ],
        out_specs=[pl.BlockSpec((tb, PEER_ROWS), lambda i: (i, 0))] * 2,
        out_shape=[jax.ShapeDtypeStruct((n, PEER_ROWS), jnp.int32),
                   jax.ShapeDtypeStruct((n, PEER_ROWS), jnp.float32)],
        scratch_shapes=[
            pltpu.VMEM((groups * STAGE_STRIDE, LANES), jnp.float32),
            pltpu.VMEM((2, PEER_NKEYS, groups, LANES), jnp.float32),
            pltpu.VMEM((2, PEER_TOPK, groups, LANES), jnp.float32),
            pltpu.VMEM((2, PEER_TOPK, groups, LANES), jnp.float32),
            pltpu.VMEM((len(PAIRS), groups, LANES), jnp.float32),
            blk(jnp.float32), blk(jnp.float32),
            pltpu.VMEM((PEER_ROWS, groups, LANES), jnp.float32),
            pltpu.VMEM((PEER_ROWS, groups, LANES), jnp.float32),
        ],
        compiler_params=_params(("arbitrary",)),
        name="peer_topk",
    )(q16, keys)


def route(h, om, osw, wo, mod, n2, wq, r, c, with_ctx):
    b, t, d = h.shape
    off = 0 if with_ctx else c // r
    nblk = t // r - off
    nctx = c // r
    tq = nblk * r
    full = lambda a: pl.BlockSpec(a.shape, lambda i, j: (0,) * a.ndim)
    row_spec = lambda w: pl.BlockSpec((1, r, w), lambda i, j: (i, j, 0))
    return pl.pallas_call(
        _route_kernel,
        grid=(b, nblk),
        in_specs=[
            pl.BlockSpec((1, r, d), lambda i, j: (i, j + off, 0)),
            row_spec(om.shape[-1]), row_spec(osw.shape[-1]),
            full(wo),
            pl.BlockSpec((1, N_MOD, d), lambda i, j: (jnp.where(j + off < nctx, b, i), 0, 0)),
            full(n2), full(wq),
        ],
        out_specs=[row_spec(d), row_spec(2 * d),
                   pl.BlockSpec((2 * PEER_HEADS, r, PEER_DHALF), lambda i, j: (0, i * nblk + j, 0))],
        out_shape=[
            jax.ShapeDtypeStruct((b, tq, d), jnp.float32),
            jax.ShapeDtypeStruct((b, tq, 2 * d), jnp.bfloat16),
            jax.ShapeDtypeStruct((2 * PEER_HEADS, b * tq, PEER_DHALF), jnp.bfloat16),
        ],
        compiler_params=_params(("parallel", "arbitrary")),
        name="peer_route",
    )(h, om, osw, wo, mod, n2, wq)


def _pack_kernel(x_ref, o_ref):
    x = x_ref[0]
    half = x.shape[1] // 2
    bits = lambda v: pltpu.bitcast(v.astype(jnp.bfloat16).astype(jnp.float32), jnp.uint32)
    w = (bits(x[:, :half]) & jnp.uint32(0xFFFF0000)) | (bits(x[:, half:]) >> 16)
    for c in range(N_CHUNK):
        o_ref[pl.ds(c, x.shape[0], stride=N_CHUNK), :] = w[:, c * LANES:(c + 1) * LANES]


def pack_table(tabs, l):
    _, e, d = tabs.shape
    be = 512
    return pl.pallas_call(
        _pack_kernel,
        grid=(e // be,),
        in_specs=[pl.BlockSpec((1, be, d), lambda i: (l, i, 0))],
        out_specs=pl.BlockSpec((be * N_CHUNK, LANES), lambda i: (i, 0)),
        out_shape=jax.ShapeDtypeStruct((e * N_CHUNK, LANES), jnp.uint32),
        compiler_params=_params(("parallel",)),
        name="pack_table",
    )(tabs)


def _gather_rows(idx_ref, tab_ref, tile_ref, t):
    for j in range(PEER_ROWS):
        r = pl.multiple_of(idx_ref[t, j], N_CHUNK)
        tile_ref[pl.ds(j, N_CHUNK, stride=TILE_STRIDE), :] = tab_ref[pl.ds(r, N_CHUNK), :]


def _token_pipeline(gather, gather_next, compute, tiles):
    steps = TOK_BLK // TOK_SET

    @pl.when(pl.program_id(0) == 0)
    def _():
        for k in range(TOK_SET):
            gather(tiles[0].at[k], k)

    def step(t, cur, nxt, fill):
        for k in range(TOK_SET):
            compute(cur.at[k], t + k)
            fill(nxt.at[k], k)

    def body(i, carry):
        t = TOK_SET * i
        ahead = lambda tile, k: gather(tile, t + TOK_SET + k)

        @pl.when(i % 2 == 0)
        def _():
            step(t, tiles[0], tiles[1], ahead)

        @pl.when(i % 2 == 1)
        def _():
            step(t, tiles[1], tiles[0], ahead)

        return carry

    lax.fori_loop(0, steps - 1, body, 0)
    step(TOK_BLK - TOK_SET, tiles[1], tiles[0], gather_next)


def _chunk(tile_ref, c):
    return pltpu.bitcast(tile_ref[pl.ds(c * TILE_STRIDE, PEER_ROWS), :], jnp.bfloat16)


def _u_kernel(idx_ref, nxt_ref, x_ref, g_ref, tab_ref, w_ref, *tiles):
    row = lax.broadcasted_iota(jnp.int32, (16, 2 * PEER_ROWS), 0) & 7

    def scores(tile_ref, t):
        x16 = x_ref[t]
        top = jnp.zeros((16, 2 * PEER_ROWS), jnp.float32)
        bot = jnp.zeros((16, 2 * PEER_ROWS), jnp.float32)
        for c in range(N_CHUNK):
            y = lax.dot_general(x16, _chunk(tile_ref, c), (((1,), (1,)), ((), ())),
                                preferred_element_type=jnp.float32)
            top = top + jnp.where(row == c, y, 0.0)
            bot = bot + jnp.where(row == c + N_CHUNK, y, 0.0)
        tot = top + pltpu.roll(bot, 1, axis=1)
        a = jnp.sum(tot, axis=0, keepdims=True)
        gelu = 0.5 * a * (1.0 + lax.erf(a * (2.0 ** -0.5)))
        w_ref[t] = g_ref[t] * gelu

    _token_pipeline(functools.partial(_gather_rows, idx_ref, tab_ref),
                    functools.partial(_gather_rows, nxt_ref, tab_ref), scores, tiles)


def _next_tokens_spec(n):
    per_blk = TOK_BLK // TOK_SET
    return pl.BlockSpec((TOK_SET, PEER_ROWS),
                        lambda i: (jnp.minimum((i + 1) * per_blk, n // TOK_SET - 1), 0),
                        memory_space=pltpu.SMEM)


def peer_scores(idx, x16, g, tab):
    n = idx.shape[0]
    tok = lambda *s: pl.BlockSpec((TOK_BLK,) + s, lambda i: (i,) + (0,) * len(s))
    return pl.pallas_call(
        _u_kernel,
        grid=(n // TOK_BLK,),
        in_specs=[
            pl.BlockSpec((TOK_BLK, PEER_ROWS), lambda i: (i, 0), memory_space=pltpu.SMEM),
            _next_tokens_spec(n),
            tok(16, LANES), tok(1, 2 * PEER_ROWS),
            pl.BlockSpec(tab.shape, lambda i: (0, 0), pipeline_mode=pl.Buffered(1)),
        ],
        out_specs=tok(1, 2 * PEER_ROWS),
        out_shape=jax.ShapeDtypeStruct((n, 1, 2 * PEER_ROWS), jnp.float32),
        scratch_shapes=[pltpu.VMEM((TOK_SET, N_CHUNK * TILE_STRIDE, LANES), jnp.uint32)] * 2,
        compiler_params=_params(("arbitrary",)),
        name="peer_scores",
    )(idx, idx, x16, g, tab)


def _v_kernel(idx_ref, nxt_ref, w_ref, h_ref, g2_ref, tab_ref, o_ref, *tiles):
    row = lax.broadcasted_iota(jnp.int32, (16, 2 * PEER_ROWS), 0)

    def combine(tile_ref, t):
        w = w_ref[t]
        w_hi = w.astype(jnp.bfloat16).astype(jnp.float32)
        w_lo = w - w_hi
        parts = (w_hi, pltpu.roll(w_hi, 2 * PEER_ROWS - 1, axis=1),
                 w_lo, pltpu.roll(w_lo, 2 * PEER_ROWS - 1, axis=1))
        acc = jnp.zeros((16, LANES), jnp.float32)
        for c in range(N_CHUNK):
            lhs = jnp.zeros((16, 2 * PEER_ROWS), jnp.float32)
            for k, part in enumerate(parts):
                lhs = jnp.where(row == c + N_CHUNK * k, part, lhs)
            acc = acc + jnp.dot(lhs.astype(jnp.bfloat16), _chunk(tile_ref, c),
                                preferred_element_type=jnp.float32)
        o_ref[t] = h_ref[t] + g2_ref[0, 0] * (acc[:8] + acc[8:])

    _token_pipeline(functools.partial(_gather_rows, idx_ref, tab_ref),
                    functools.partial(_gather_rows, nxt_ref, tab_ref), combine, tiles)


def peer_combine(idx, w, h8, g2, tab, blocks_per_sample, ctx_blocks):
    n = idx.shape[0]
    tok = lambda *s: pl.BlockSpec((TOK_BLK,) + s, lambda i: (i,) + (0,) * len(s))
    g2_map = lambda i: (i // blocks_per_sample,
                        jnp.where(i % blocks_per_sample < ctx_blocks, 0, 1), 0, 0)
    return pl.pallas_call(
        _v_kernel,
        grid=(n // TOK_BLK,),
        in_specs=[
            pl.BlockSpec((TOK_BLK, PEER_ROWS), lambda i: (i, 0), memory_space=pltpu.SMEM),
            _next_tokens_spec(n),
            tok(1, 2 * PEER_ROWS), tok(8, LANES),
            pl.BlockSpec((1, 1, 8, LANES), g2_map),
            pl.BlockSpec(tab.shape, lambda i: (0, 0), pipeline_mode=pl.Buffered(1)),
        ],
        out_specs=tok(8, LANES),
        out_shape=jax.ShapeDtypeStruct((n, 8, LANES), jnp.float32),
        scratch_shapes=[pltpu.VMEM((TOK_SET, N_CHUNK * TILE_STRIDE, LANES), jnp.uint32)] * 2,
        compiler_params=_params(("arbitrary",)),
        name="peer_combine",
    )(idx, idx, w, h8, g2, tab)


SC_LANES = 16
SC_WORKERS = 32
SC_ROWS = 32
SC_SHARE = 0.5
GROUPS = 1


def peer_combine_sc(idx, w, tab):
    m = idx.shape[0]
    d = tab.shape[1]
    per = m // SC_WORKERS
    mesh = plsc.VectorSubcoreMesh(core_axis_name="c", subcore_axis_name="s")

    quarters = PEER_ROWS // SC_ROWS
    group = 16

    @functools.partial(
        pl.kernel, mesh=mesh,
        out_type=jax.ShapeDtypeStruct((m, d), jnp.float32),
        scratch_types=[
            pltpu.VMEM((2, PEER_ROWS), jnp.int32),
            pltpu.VMEM((2 * PEER_ROWS,), jnp.float32),
            pltpu.VMEM((2, SC_ROWS, d), jnp.float32),
            pltpu.VMEM((d,), jnp.float32),
            pltpu.SemaphoreType.DMA((2,)),
        ],
        compiler_params=pltpu.CompilerParams(needs_layout_passes=False),
        cost_estimate=pl.CostEstimate(flops=2 * m * PEER_ROWS * d, transcendentals=0,
                                      bytes_accessed=4 * m * (PEER_ROWS * d + d + 3 * PEER_ROWS)),
        name="peer_combine_sc",
    )
    def body(idx_hbm, w_hbm, tab_hbm, out_hbm, idx_v, w_v, rows_v, acc_v, sems):
        base = (lax.axis_index("s") * 2 + lax.axis_index("c")) * per

        def gather(slot, q, buf):
            rows = idx_v.at[slot, pl.ds(q * SC_ROWS, SC_ROWS)]
            return pltpu.make_async_copy(tab_hbm.at[rows], rows_v.at[buf], sems.at[buf])

        def accumulate(buf, q):
            for g0 in range(0, d // SC_LANES, group):
                lanes = [pl.ds((g0 + c) * SC_LANES, SC_LANES) for c in range(group)]

                def row(j, accs):
                    wj = plsc.load_gather(w_v, [jnp.full((SC_LANES,), 2 * q * SC_ROWS + 1, jnp.int32) + 2 * j])
                    return tuple(a + wj * rows_v[buf, j, sl] for a, sl in zip(accs, lanes))

                accs = lax.fori_loop(0, SC_ROWS, row, tuple(acc_v[sl] for sl in lanes))
                for sl, a in zip(lanes, accs):
                    acc_v[sl] = a

        pltpu.sync_copy(idx_hbm.at[base], idx_v.at[0])
        gather(0, 0, 0).start()

        @pl.loop(0, per, step=2)
        def _(i):
            for p in range(2):
                t = base + i + p
                pltpu.sync_copy(w_hbm.at[t], w_v)
                for c in range(d // SC_LANES):
                    acc_v[pl.ds(c * SC_LANES, SC_LANES)] = jnp.zeros((SC_LANES,), jnp.float32)
                for q in range(quarters):
                    buf = q % 2
                    if q + 1 < quarters:
                        gather(p, q + 1, 1 - buf).start()
                    else:
                        nxt = jnp.minimum(t + 1, base + per - 1)
                        pltpu.sync_copy(idx_hbm.at[nxt], idx_v.at[1 - p])
                        gather(1 - p, 0, 1 - buf).start()
                    gather(p, q, buf).wait()
                    accumulate(buf, q)
                pltpu.sync_copy(acc_v, out_hbm.at[t])

        gather(0, 0, 0).wait()

    return body(idx, w, tab)


def _residual_kernel(h_ref, g_ref, a_ref, o_ref):
    o_ref[...] = h_ref[...] + g_ref[0, 0] * a_ref[...]


def gated_residual(h, g2, a, first_row, rows_per_sample, ctx_rows):
    m, d = h.shape
    rb = 256
    blk = pl.BlockSpec((rb, d), lambda i: (i, 0))
    per, ctx, off = rows_per_sample // rb, ctx_rows // rb, first_row // rb
    g_map = lambda i: ((i + off) // per, jnp.where((i + off) % per < ctx, 0, 1), 0, 0)
    return pl.pallas_call(
        _residual_kernel, grid=(m // rb,),
        in_specs=[blk, pl.BlockSpec((1, 1, 1, d), g_map), blk],
        out_specs=blk, out_shape=jax.ShapeDtypeStruct((m, d), jnp.float32),
        compiler_params=_params(("parallel",)), name="gated_residual",
    )(h, g2, a)


def _slots(w, width, offset=0):
    lead = w.shape[:-1]
    n = w.shape[-1] // width
    w = w.reshape(lead + (n, width))
    w = jnp.pad(w, [(0, 0)] * len(lead) + [(0, 0), (offset, LANES - width - offset)])
    return w.reshape(lead + (n * LANES,))


def _rope_tables(rot_dim, lane0, t, c):
    s = t - c
    q = rot_dim // 4
    pos = jnp.arange(s, dtype=jnp.float32)
    rows = jnp.floor(pos / GRID_W)
    cols = pos - rows * GRID_W
    inv = ROPE_BASE ** (-jnp.arange(q, dtype=jnp.float32) / q)
    ar = rows[:, None] * inv
    ac = cols[:, None] * inv
    zero = jnp.zeros_like(ar)
    cos = jnp.cos(jnp.concatenate([ar, ar, ac, ac], axis=-1))
    up = jnp.concatenate([-jnp.sin(ar), zero, -jnp.sin(ac), zero], axis=-1)
    dn = jnp.concatenate([zero, jnp.sin(ar), zero, jnp.sin(ac)], axis=-1)
    pad = lambda a, fill: jnp.pad(
        jnp.pad(a, ((0, 0), (lane0, LANES - lane0 - rot_dim)), constant_values=fill),
        ((c, 0), (0, 0)), constant_values=fill)
    cos = jnp.pad(jnp.pad(cos, ((0, 0), (lane0, LANES - lane0 - rot_dim)), constant_values=1.0),
                  ((c, 0), (0, 0)), constant_values=1.0)
    return jnp.stack([cos, pad(up, 0.0), pad(dn, 0.0)])


def kernel(x, c, ctx, c_ctx, ada_w, ada_b, norm1_g, norm2_g, w_in, mla_qa_g, mla_wuq, mla_kva_g, mla_wukv, mla_qn_g, mla_kn_g, swa_qn_g, swa_kn_g, swa_sink, w_out, peer_wq, peer_keys, peer_u, peer_v):
    b, s, d = x.shape
    nctx = ctx.shape[1]
    t = nctx + s
    depth = ada_w.shape[0]
    r = min(256, nctx)
    bf = jnp.bfloat16

    cond = jnp.zeros((16, d), jnp.float32).at[:b].set(c).at[b].set(c_ctx)
    mod_all = modulation(cond, ada_w, ada_b).reshape(depth, 16, N_MOD, d)
    rope_m = _rope_tables(MLA_ROPE, MLA_NOPE, t, nctx)
    rope_s = _rope_tables(SWA_DIM, 0, t, nctx)

    bg = b // GROUPS
    hs = [jnp.concatenate([ctx[g * bg:(g + 1) * bg], x[g * bg:(g + 1) * bg]], axis=1) for g in range(GROUPS)]
    for l in range(depth):
        last = l == depth - 1
        with_ctx = not last
        wi = w_in[l]
        kv0 = Q_COLS + MLA_KV_RANK
        sk0 = kv0 + MLA_ROPE
        sv0 = sk0 + SWA_KV_HEADS * SWA_DIM
        sv = wi[:, sv0:].reshape(d, SWA_KV_HEADS, 1, SWA_DIM)
        win = jnp.concatenate([
            wi[:, :MLA_Q_RANK], wi[:, Q_COLS:kv0],
            _slots(wi[:, kv0:sk0], MLA_ROPE, MLA_NOPE),
            _slots(wi[:, MLA_Q_RANK:Q_COLS], SWA_DIM),
            _slots(wi[:, sk0:sv0], SWA_DIM),
            jnp.broadcast_to(sv, (d, SWA_KV_HEADS, 2, SWA_DIM)).reshape(d, SWA_KV_HEADS * LANES),
        ], axis=1).astype(bf)
        wuq = _slots(mla_wuq[l], MLA_QK).astype(bf)
        wukv = mla_wukv[l].reshape(MLA_KV_RANK, MLA_HEADS, MLA_NOPE + MLA_V)
        wuk = _slots(wukv[:, :, :MLA_NOPE].reshape(MLA_KV_RANK, -1), MLA_NOPE).astype(bf)
        wv = wukv[:, :, MLA_NOPE:].reshape(MLA_KV_RANK, MLA_HEADS // 2, 2, MLA_V)
        zero = jnp.zeros_like(wv[:, :, 0])
        wuv = jnp.stack([jnp.concatenate([wv[:, :, 0], zero], axis=-1),
                         jnp.concatenate([zero, wv[:, :, 1]], axis=-1)], axis=2)
        wuv = wuv.reshape(MLA_KV_RANK, MLA_HEADS * LANES).astype(bf)
        row = lambda g: g.reshape(1, -1)
        wo, wq, keys = w_out[l].astype(bf), peer_wq[l].astype(bf), peer_keys[l].astype(bf)
        utab, vtab = pack_table(peer_u, l), pack_table(peer_v, l)
        vrows = peer_v.reshape(-1, d)

        for g in range(GROUPS):
            h = hs[g]
            mod = jnp.concatenate([mod_all[l, g * bg:(g + 1) * bg], mod_all[l, b:b + 1]], axis=0)
            qm, km, vm, qs, ks, vlo, vhi = projections(
                h, mod, row(norm1_g[l]), win, row(mla_qa_g[l]), wuq, row(mla_kva_g[l]), wuk, wuv,
                row(_slots(mla_qn_g[l], MLA_QK)), row(_slots(mla_kn_g[l], MLA_QK)),
                row(_slots(swa_qn_g[l], SWA_DIM)), row(_slots(swa_kn_g[l], SWA_DIM)),
                rope_m, rope_s, r, nctx)
            om = mla_attention(qm, km, vm, r, nctx, with_ctx)
            osw = swa_attention(swa_sink[l], qs, ks, vlo, vhi, nctx, with_ctx)
            h1, x2, q16 = route(h, om, osw, wo, mod, row(norm2_g[l]), wq, r, nctx, with_ctx)
            idx, gate = peer_topk(q16, keys)

            tl = h1.shape[1]
            n = bg * tl
            x16 = x2.reshape(n, 16, LANES)
            gate = jnp.stack([jnp.zeros_like(gate), gate], axis=-1).reshape(n, 1, 2 * PEER_ROWS)
            g2 = jnp.stack([jnp.broadcast_to(mod[bg, 5], (bg, d)), mod[:bg, 5]], axis=1)
            ctx_rows = nctx if with_ctx else 0
            h1f = h1.reshape(n, d)
            n_sc = int(n * SC_SHARE) // 256 * 256
            n_tc = n - n_sc
            w_sc = peer_scores(idx[n_tc:], x16[n_tc:], gate[n_tc:], utab).reshape(n_sc, 2 * PEER_ROWS)
            experts = (idx[n_tc:] >> 2) + l * peer_v.shape[1]
            acc_sc = peer_combine_sc(experts, w_sc, vrows)
            h_sc = gated_residual(h1f[n_tc:], g2.reshape(bg, 2, 1, d), acc_sc, n_tc, tl, ctx_rows)
            if n_tc:
                w = peer_scores(idx[:n_tc], x16[:n_tc], gate[:n_tc], utab)
                h_tc = peer_combine(idx[:n_tc], w, h1f[:n_tc].reshape(n_tc, 8, LANES),
                                    g2.reshape(bg, 2, 8, LANES), vtab, tl // TOK_BLK,
                                    ctx_rows // TOK_BLK).reshape(n_tc, d)
                h_sc = jnp.concatenate([h_tc, h_sc], axis=0)
            hs[g] = h_sc.reshape(bg, tl, d)
    return jnp.concatenate(hs, axis=0)
```
